```python
import math
import jax, jax.numpy as jnp
from jax import lax
import numpy as np

D_MODEL = 1024
BATCH = 8
SEQ = 2048
DEPTH = 4

N_MIXERS = 2
N_POOL_LAYERS = (DEPTH + 1) // 2
N_ATTN_LAYERS = DEPTH // 2

POOL_WINDOWS = (2, 4, 8, 16)
POOL_GROUPS = len(POOL_WINDOWS)
POOL_CH = D_MODEL // POOL_GROUPS

ATTN_PATTERNS = ((128, 1), (512, 4), (2048, 16))
N_ATTN_GROUPS = len(ATTN_PATTERNS)
HEAD_DIM = 64
N_HEADS = D_MODEL // HEAD_DIM
Q_BLOCK = 128

N_EXPERTS = 64
TOP_K = 8
N_EXPERT_GROUPS = 8
TOPK_GROUPS = 4
D_EXPERT = 256
D_SHARED = 256
ROUTED_SCALE = 2.5
EXPERT_ROWS = 256

ALPHA = (2 * DEPTH) ** 0.25
BETA = (8 * DEPTH) ** -0.25
LN_EPS = 1e-5

kernel_name = "hybrid_pool_dilated_attn_moe_deepnorm"

F32 = jnp.float32


def layer_norm(x, g, b):
    xf = x.astype(F32)
    mu = jnp.mean(xf, axis=-1, keepdims=True)
    var = jnp.mean(jnp.square(xf - mu), axis=-1, keepdims=True)
    return ((xf - mu) * lax.rsqrt(var + LN_EPS) * g.astype(F32) + b.astype(F32)).astype(x.dtype)


def alibi_slopes(n):
    return 2.0 ** (-8.0 * jnp.arange(1, n + 1, dtype=F32) / n)


def multiscale_pool_mixer(h, w_in, w_grp, ch_scale, w_out):
    b, s, _ = h.shape
    u = (h @ w_in).reshape(b, s, POOL_GROUPS, POOL_CH)
    csum = jnp.cumsum(u.astype(F32), axis=1)
    csum = jnp.pad(csum, ((0, 0), (1, 0), (0, 0), (0, 0)))
    t = jnp.arange(s)
    parts = []
    for g, w in enumerate(POOL_WINDOWS):
        lo = jnp.maximum(t + 1 - w, 0)
        win_sum = csum[:, 1:, g] - csum[:, lo, g]
        cnt = jnp.minimum(t + 1, w).astype(F32)[None, :, None]
        parts.append(win_sum / cnt - u[:, :, g].astype(F32))
    z = jnp.stack(parts, axis=2).astype(h.dtype)
    y = jnp.einsum('bsgc,gcf->bsgf', z, w_grp).reshape(b, s, D_MODEL) * ch_scale
    return y @ w_out


def dilated_group_attention(q, k, v, dilation, steps, slopes):
    b, s, h, dh = q.shape
    L = s // dilation
    bq = min(Q_BLOCK, L)
    nb = L // bq
    qs = q.reshape(b, nb, bq, dilation, h, dh)
    pad = ((0, 0), (steps, 0), (0, 0), (0, 0), (0, 0))
    kp = jnp.pad(k.reshape(b, L, dilation, h, dh), pad)
    vp = jnp.pad(v.reshape(b, L, dilation, h, dh), pad)
    key_idx = jnp.arange(nb)[:, None] * bq + jnp.arange(bq + steps)[None, :]
    kb = kp[:, key_idx]
    vb = vp[:, key_idx]
    sc = jnp.einsum('bnidhe,bnjdhe->bdhnij', qs, kb, preferred_element_type=F32) * (dh ** -0.5)
    qpos = jnp.arange(nb)[:, None] * bq + jnp.arange(bq)[None, :]
    kpos = key_idx - steps
    dist = qpos[:, :, None] - kpos[:, None, :]
    valid = (dist >= 0) & (dist <= steps) & (kpos[:, None, :] >= 0)
    bias = -slopes[:, None, None, None] * (dist * dilation).astype(F32)
    sc = jnp.where(valid, sc + bias, -jnp.inf)
    m = jnp.max(sc, axis=-1, keepdims=True)
    p = jnp.exp(sc - m)
    den = jnp.sum(p, axis=-1, keepdims=True)
    o = jnp.einsum('bdhnij,bnjdhe->bdhnie', p, vb, preferred_element_type=F32) / den
    lse = (m + jnp.log(den))[..., 0]
    o = o.transpose(0, 3, 4, 1, 2, 5).reshape(b, s, h, dh)
    lse = lse.transpose(0, 3, 4, 1, 2).reshape(b, s, h)
    return o, lse


def dilated_attention_mixer(h, w_in, w_out):
    b, s, _ = h.shape
    qkv = (h @ w_in).reshape(b, s, N_ATTN_GROUPS, 3, N_HEADS, HEAD_DIM)
    slopes = alibi_slopes(N_ATTN_GROUPS * N_HEADS).reshape(N_ATTN_GROUPS, N_HEADS)
    outs, lses = [], []
    for g, (window, dil) in enumerate(ATTN_PATTERNS):
        o, l = dilated_group_attention(qkv[:, :, g, 0], qkv[:, :, g, 1], qkv[:, :, g, 2],
                                       dil, window // dil, slopes[g])
        outs.append(o)
        lses.append(l)
    wts = jax.nn.softmax(jnp.stack(lses, axis=0), axis=0)
    mixed = jnp.sum(wts[..., None] * jnp.stack(outs, axis=0), axis=0)
    return mixed.astype(h.dtype).reshape(b, s, D_MODEL) @ w_out


def moe_ffn(h, w_router, router_b, w_exp_gu, w_exp_down, w_sh_gu, w_sh_down):
    b, s, d = h.shape
    hf = h.reshape(-1, d)
    T = hf.shape[0]
    scores = jax.nn.sigmoid(jnp.dot(hf, w_router, preferred_element_type=F32))
    sel = scores + router_b.astype(F32)
    grp_score = lax.top_k(sel.reshape(T, N_EXPERT_GROUPS, -1), 2)[0].sum(-1)
    _, grp_idx = lax.top_k(grp_score, TOPK_GROUPS)
    grp_mask = jax.nn.one_hot(grp_idx, N_EXPERT_GROUPS, dtype=F32).sum(1)
    exp_mask = jnp.repeat(grp_mask, N_EXPERTS // N_EXPERT_GROUPS, axis=1) > 0
    _, top_idx = lax.top_k(jnp.where(exp_mask, sel, -jnp.inf), TOP_K)
    top_w = jnp.take_along_axis(scores, top_idx, axis=1)
    top_w = top_w / jnp.sum(top_w, axis=-1, keepdims=True) * ROUTED_SCALE
    C = EXPERT_ROWS
    TK = T * TOP_K
    P = ((TK + N_EXPERTS * (C - 1) + C - 1) // C) * C
    NB = P // C
    flat_e = top_idx.reshape(-1)
    order = jnp.argsort(flat_e)
    sorted_e = flat_e[order]
    tok_sorted = (order // TOP_K).astype(jnp.int32)
    counts = jnp.bincount(flat_e, length=N_EXPERTS).astype(jnp.int32)
    padded = ((counts + C - 1) // C) * C
    starts = jnp.cumsum(counts) - counts
    padded_ends = jnp.cumsum(padded)
    padded_starts = padded_ends - padded
    dest = padded_starts[sorted_e] + (jnp.arange(TK, dtype=jnp.int32) - starts[sorted_e])
    buf_tok = jnp.full((P,), T, jnp.int32).at[dest].set(tok_sorted)
    buf_w = jnp.zeros((P,), F32).at[dest].set(top_w.reshape(-1)[order])
    group_e = jnp.clip(jnp.searchsorted(padded_ends, jnp.arange(NB) * C, side='right'),
                       0, N_EXPERTS - 1)
    hf_pad = jnp.concatenate([hf, jnp.zeros((1, d), hf.dtype)], axis=0)
    xg = hf_pad[buf_tok].reshape(NB, C, d)

    def expert_rows(args):
        xr, e = args
        gu = xr @ w_exp_gu[e]
        return (jax.nn.silu(gu[:, :D_EXPERT]) * gu[:, D_EXPERT:]) @ w_exp_down[e]

    yg = lax.map(expert_rows, (xg, group_e)).reshape(P, d)
    yg = yg * buf_w[:, None].astype(yg.dtype)
    routed = jnp.zeros((T + 1, d), hf.dtype).at[buf_tok].add(yg)[:T]
    sh = hf @ w_sh_gu
    shared = (jax.nn.silu(sh[:, :D_SHARED]) * sh[:, D_SHARED:]) @ w_sh_down
    return (routed + shared).reshape(b, s, d)


def setup_inputs(seed: int = 0) -> dict:
    key = jax.random.key(seed)
    ks = jax.random.split(key, 22)
    D = D_MODEL

    def nrm(k, shape, std):
        return jax.random.normal(k, shape, F32) * std

    return {
        "x": nrm(ks[0], (BATCH, SEQ, D), 1.0),
        "c": nrm(ks[1], (BATCH, D), 1.0),
        "ada_w": nrm(ks[2], (DEPTH, D, 6 * D), 0.2 * D ** -0.5),
        "ada_b": nrm(ks[3], (DEPTH, 6 * D), 0.02),
        "pool_w_in": nrm(ks[4], (N_POOL_LAYERS, D, D), D ** -0.5),
        "pool_w_grp": nrm(ks[5], (N_POOL_LAYERS, POOL_GROUPS, POOL_CH, POOL_CH), POOL_CH ** -0.5),
        "pool_scale": 1.0 + nrm(ks[6], (N_POOL_LAYERS, D), 0.1),
        "pool_w_out": nrm(ks[7], (N_POOL_LAYERS, D, D), BETA * D ** -0.5),
        "attn_w_in": nrm(ks[8], (N_ATTN_LAYERS, D, N_ATTN_GROUPS * 3 * D), D ** -0.5),
        "attn_w_out": nrm(ks[9], (N_ATTN_LAYERS, D, D), BETA * D ** -0.5),
        "ln1_g": 1.0 + nrm(ks[10], (DEPTH, D), 0.05),
        "ln1_b": nrm(ks[11], (DEPTH, D), 0.02),
        "router_w": nrm(ks[12], (DEPTH, D, N_EXPERTS), D ** -0.5),
        "router_b": nrm(ks[13], (DEPTH, N_EXPERTS), 0.01),
        "exp_w_gu": nrm(ks[14], (DEPTH, N_EXPERTS, D, 2 * D_EXPERT), D ** -0.5),
        "exp_w_down": nrm(ks[15], (DEPTH, N_EXPERTS, D_EXPERT, D), BETA * D_EXPERT ** -0.5),
        "sh_w_gu": nrm(ks[16], (DEPTH, D, 2 * D_SHARED), D ** -0.5),
        "sh_w_down": nrm(ks[17], (DEPTH, D_SHARED, D), BETA * D_SHARED ** -0.5),
        "ln2_g": 1.0 + nrm(ks[18], (DEPTH, D), 0.05),
        "ln2_b": nrm(ks[19], (DEPTH, D), 0.02),
    }


def reference(x, c, ada_w, ada_b, pool_w_in, pool_w_grp, pool_scale, pool_w_out,
              attn_w_in, attn_w_out, ln1_g, ln1_b, router_w, router_b, exp_w_gu,
              exp_w_down, sh_w_gu, sh_w_down, ln2_g, ln2_b):
    cs = jax.nn.silu(c)
    for i in range(DEPTH):
        mod = (cs @ ada_w[i] + ada_b[i])[:, None, :]
        sh1, sc1, g1, sh2, sc2, g2 = jnp.split(mod, 6, axis=-1)
        hm = x * (1.0 + sc1) + sh1
        j = i // N_MIXERS
        if i % N_MIXERS == 0:
            y = multiscale_pool_mixer(hm, pool_w_in[j], pool_w_grp[j], pool_scale[j], pool_w_out[j])
        else:
            y = dilated_attention_mixer(hm, attn_w_in[j], attn_w_out[j])
        x = layer_norm(ALPHA * x + (1.0 + g1) * y, ln1_g[i], ln1_b[i])
        hf = x * (1.0 + sc2) + sh2
        y = moe_ffn(hf, router_w[i], router_b[i], exp_w_gu[i], exp_w_down[i], sh_w_gu[i], sh_w_down[i])
        x = layer_norm(ALPHA * x + (1.0 + g2) * y, ln2_g[i], ln2_b[i])
    return x
```

```python
import functools
import math

import jax
import jax.numpy as jnp
from jax import lax
from jax.experimental import pallas as pl
from jax.experimental.pallas import tpu as pltpu

F32 = jnp.float32
BF16 = jnp.bfloat16
U32 = jnp.uint32
I32 = jnp.int32

POOL_WINDOWS = (2, 4, 8, 16)
ATTN_PATTERNS = ((128, 1), (512, 4), (2048, 16))
HEAD_DIM = 64
N_HEADS = 16
Q_BLOCK = 128
N_EXPERTS = 64
TOP_K = 8
N_EXPERT_GROUPS = 8
TOPK_GROUPS = 4
ROUTED_SCALE = 2.5
EXPERT_ROWS = 256
DEPTH = 4
ALPHA = (2 * DEPTH) ** 0.25
LN_EPS = 1e-5

PERM_TILE = 256
POOL_HALO = 16
TOK_TILE = 256
LSE_LANES = 128
VMEM_LIMIT = 56 * 1024 * 1024
NEG_BIG = -1e30


def _cparams(sem):
    return pltpu.CompilerParams(dimension_semantics=sem, vmem_limit_bytes=VMEM_LIMIT)


def _layer_norm(v, g, b):
    mu = jnp.mean(v, axis=-1, keepdims=True)
    c = v - mu
    var = jnp.mean(c * c, axis=-1, keepdims=True)
    return c * lax.rsqrt(var + LN_EPS) * g + b


def _silu(v):
    return v * (1.0 / (1.0 + jnp.exp(-v)))


def _pack_bf16_pairs(v):
    n = v.shape[1] // 2
    hi = lax.bitcast_convert_type(v[:, :n].astype(BF16).astype(F32), U32)
    lo = lax.bitcast_convert_type(v[:, n:].astype(BF16).astype(F32), U32)
    return hi | (lo >> 16)


def _unpack_bf16_pairs(p):
    hi = lax.bitcast_convert_type(p & jnp.uint32(0xFFFF0000), F32)
    lo = lax.bitcast_convert_type(p << 16, F32)
    return hi, lo


def _mod_kernel(c_ref, w_ref, b_ref, o_ref):
    cs = _silu(c_ref[...])
    o_ref[...] = jnp.dot(cs, w_ref[...], preferred_element_type=F32) + b_ref[...]


def _modulation(c, ada_w, ada_b):
    depth, d, n6 = ada_w.shape
    b = c.shape[0]
    tn = 1536
    return pl.pallas_call(
        _mod_kernel,
        grid=(depth, n6 // tn),
        in_specs=[
            pl.BlockSpec((b, d), lambda i, n: (0, 0)),
            pl.BlockSpec((None, d, tn), lambda i, n: (i, 0, n)),
            pl.BlockSpec((None, 1, tn), lambda i, n: (i, 0, n)),
        ],
        out_specs=pl.BlockSpec((None, b, tn), lambda i, n: (i, 0, n)),
        out_shape=jax.ShapeDtypeStruct((depth, b, n6), F32),
        compiler_params=_cparams(("arbitrary", "arbitrary")),
    )(c, ada_w, ada_b.reshape(depth, 1, n6))


def _mod_spec(chunk, d, tiles_per_batch):
    return pl.BlockSpec((None, 1, d), lambda *idx: (idx[0] // tiles_per_batch, 0, chunk))


def _pool_kernel(x_ref, sh_ref, sc_ref, g_ref, win_ref, wgrp_ref, cs_ref, wout_ref,
                 lng_ref, lnb_ref, o_ref, ext_ref, *, tiles_per_batch):
    tm, d = x_ref.shape
    s_idx = pl.program_id(0) % tiles_per_batch
    x = x_ref[...]
    h = (x * (1.0 + sc_ref[...]) + sh_ref[...]).astype(BF16)
    u = jnp.dot(h, win_ref[...], preferred_element_type=F32)

    @pl.when(s_idx == 0)
    def _():
        ext_ref[0:POOL_HALO, :] = jnp.zeros((POOL_HALO, d), F32)

    @pl.when(s_idx != 0)
    def _():
        ext_ref[0:POOL_HALO, :] = ext_ref[tm:tm + POOL_HALO, :]

    ext_ref[POOL_HALO:POOL_HALO + tm, :] = u

    pos = s_idx * tm + lax.broadcasted_iota(I32, (tm, 1), 0) + 1
    gc = d // len(POOL_WINDOWS)
    ys = []
    for g, w in enumerate(POOL_WINDOWS):
        cols = slice(g * gc, (g + 1) * gc)
        acc = u[:, cols]
        for j in range(1, w):
            acc = acc + ext_ref[POOL_HALO - j:POOL_HALO - j + tm, cols]
        cnt = jnp.minimum(pos, w).astype(F32)
        z = (acc / cnt - u[:, cols]).astype(BF16)
        ys.append(jnp.dot(z, wgrp_ref[g], preferred_element_type=F32))
    y = (jnp.concatenate(ys, axis=1) * cs_ref[...]).astype(BF16)
    out = jnp.dot(y, wout_ref[...], preferred_element_type=F32)
    v = ALPHA * x + (1.0 + g_ref[...]) * out
    o_ref[...] = _layer_norm(v, lng_ref[...], lnb_ref[...])


def _pool_layer(x2, mod3, w_in, w_grp, ch_scale, w_out, ln_g, ln_b, seq):
    t, d = x2.shape
    tm = 512
    tpb = seq // tm
    full = lambda shape: pl.BlockSpec(shape, lambda i: (0,) * len(shape))
    return pl.pallas_call(
        functools.partial(_pool_kernel, tiles_per_batch=tpb),
        grid=(t // tm,),
        in_specs=[
            pl.BlockSpec((tm, d), lambda i: (i, 0)),
            _mod_spec(0, d, tpb), _mod_spec(1, d, tpb), _mod_spec(2, d, tpb),
            full((d, d)), full(w_grp.shape), full((1, d)), full((d, d)),
            full((1, d)), full((1, d)),
        ],
        out_specs=pl.BlockSpec((tm, d), lambda i: (i, 0)),
        out_shape=jax.ShapeDtypeStruct((t, d), F32),
        scratch_shapes=[pltpu.VMEM((tm + POOL_HALO, d), F32)],
        compiler_params=_cparams(("arbitrary",)),
    )(x2, mod3, mod3, mod3, w_in, w_grp, ch_scale.reshape(1, d), w_out,
      ln_g.reshape(1, d), ln_b.reshape(1, d))


def _perm_matrix(dil):
    p = jnp.arange(PERM_TILE)
    chunk = PERM_TILE // dil
    src = (p % chunk) * dil + p // chunk
    return (src[:, None] == jnp.arange(PERM_TILE)[None, :]).astype(BF16)


def _qkv_kernel(x_ref, sh_ref, sc_ref, p_ref, w_ref, o_ref, h_ref):
    tm = x_ref.shape[0]
    g = pl.program_id(1)
    part = pl.program_id(2)

    @pl.when((g == 0) & (part == 0))
    def _():
        h = (x_ref[...] * (1.0 + sc_ref[...]) + sh_ref[...]).astype(BF16)
        h_ref[0] = h
        for gi in range(1, len(ATTN_PATTERNS)):
            for s in range(tm // PERM_TILE):
                rows = slice(s * PERM_TILE, (s + 1) * PERM_TILE)
                h_ref[gi, rows, :] = jnp.dot(
                    p_ref[gi - 1], h[rows, :], preferred_element_type=F32).astype(BF16)

    o_ref[...] = jnp.dot(h_ref[g], w_ref[...], preferred_element_type=F32).astype(BF16)


def _qkv_proj(x2, mod3, perms, w_in, seq):
    t, d = x2.shape
    ng = len(ATTN_PATTERNS)
    tm = 1024
    tpb = seq // tm
    return pl.pallas_call(
        _qkv_kernel,
        grid=(t // tm, ng, 3),
        in_specs=[
            pl.BlockSpec((tm, d), lambda m, g, p: (m, 0)),
            _mod_spec(0, d, tpb), _mod_spec(1, d, tpb),
            pl.BlockSpec(perms.shape, lambda m, g, p: (0, 0, 0)),
            pl.BlockSpec((d, d), lambda m, g, p: (0, g * 3 + p)),
        ],
        out_specs=pl.BlockSpec((tm, d), lambda m, g, p: (m, g * 3 + p)),
        out_shape=jax.ShapeDtypeStruct((t, ng * 3 * d), BF16),
        scratch_shapes=[pltpu.VMEM((ng, tm, d), BF16)],
        compiler_params=_cparams(("arbitrary", "arbitrary", "arbitrary")),
    )(x2, mod3, mod3, perms, w_in)


def _attn_kernel(q_ref, kp_ref, kc_ref, vp_ref, vc_ref, o_ref, lse_ref, *, group, dil):
    bq = Q_BLOCK
    d = N_HEADS * HEAD_DIM
    j = pl.program_id(2)
    q = q_ref[...].reshape(bq, d)
    kp = kp_ref[...].reshape(bq, d)
    kc = kc_ref[...].reshape(bq, d)
    vp = vp_ref[...].reshape(bq, d)
    vc = vc_ref[...].reshape(bq, d)

    qi = lax.broadcasted_iota(I32, (bq, 2 * bq), 0)
    kj = lax.broadcasted_iota(I32, (bq, 2 * bq), 1)
    dist = qi + bq - kj
    steps = ATTN_PATTERNS[group][0] // dil
    valid = (dist >= 0) & (dist <= steps) & ((kj >= bq) | (j > 0))
    distf = dist.astype(F32)
    lane = lax.broadcasted_iota(I32, (bq, LSE_LANES), 1)
    lse_tile = jnp.zeros((bq, LSE_LANES), F32)
    n_tot = len(ATTN_PATTERNS) * N_HEADS
    outs = []
    for h in range(N_HEADS):
        cols = slice(h * HEAD_DIM, (h + 1) * HEAD_DIM)
        slope = 2.0 ** (-8.0 * (group * N_HEADS + h + 1) / n_tot)
        kh = jnp.concatenate([kp[:, cols], kc[:, cols]], axis=0)
        vh = jnp.concatenate([vp[:, cols], vc[:, cols]], axis=0)
        s = lax.dot_general(q[:, cols], kh, (((1,), (1,)), ((), ())),
                            preferred_element_type=F32)
        s = s * (HEAD_DIM ** -0.5) - (slope * dil) * distf
        s = jnp.where(valid, s, NEG_BIG)
        m = jnp.max(s, axis=-1, keepdims=True)
        p = jnp.exp(s - m)
        den = jnp.sum(p, axis=-1, keepdims=True)
        o = jnp.dot(p.astype(BF16), vh, preferred_element_type=F32) / den
        outs.append(o)
        lse_tile = jnp.where(lane == h, m + jnp.log(den), lse_tile)
    o_ref[...] = jnp.concatenate(outs, axis=1).astype(BF16).reshape(o_ref.shape)
    lse_ref[...] = lse_tile.reshape(lse_ref.shape)


def _attention_group(qkv, group, batch, seq):
    dil = ATTN_PATTERNS[group][1]
    d = N_HEADS * HEAD_DIM
    t = qkv.shape[0]
    sub = seq // dil
    nb = sub // Q_BLOCK
    rows = Q_BLOCK if dil == 1 else PERM_TILE // dil
    chunks = Q_BLOCK // rows
    u = seq // (rows * dil)
    view = lambda a, c: a.reshape(batch, u, dil, rows, c)
    blk = lambda c: (None, chunks, None, rows, c)
    col0 = group * 3
    q_spec = pl.BlockSpec(blk(d), lambda b, r, j: (b, j, r, 0, col0))
    kc_spec = pl.BlockSpec(blk(d), lambda b, r, j: (b, j, r, 0, col0 + 1))
    kp_spec = pl.BlockSpec(blk(d), lambda b, r, j: (b, jnp.maximum(j - 1, 0), r, 0, col0 + 1))
    vc_spec = pl.BlockSpec(blk(d), lambda b, r, j: (b, j, r, 0, col0 + 2))
    vp_spec = pl.BlockSpec(blk(d), lambda b, r, j: (b, jnp.maximum(j - 1, 0), r, 0, col0 + 2))
    qkv5 = view(qkv, qkv.shape[1])
    o, lse = pl.pallas_call(
        functools.partial(_attn_kernel, group=group, dil=dil),
        grid=(batch, dil, nb),
        in_specs=[q_spec, kp_spec, kc_spec, vp_spec, vc_spec],
        out_specs=[
            pl.BlockSpec(blk(d), lambda b, r, j: (b, j, r, 0, 0)),
            pl.BlockSpec(blk(LSE_LANES), lambda b, r, j: (b, j, r, 0, 0)),
        ],
        out_shape=[
            jax.ShapeDtypeStruct((batch, u, dil, rows, d), BF16),
            jax.ShapeDtypeStruct((batch, u, dil, rows, LSE_LANES), F32),
        ],
        compiler_params=_cparams(("arbitrary", "arbitrary", "arbitrary")),
    )(qkv5, qkv5, qkv5, qkv5, qkv5)
    return o.reshape(t, d), lse.reshape(t, LSE_LANES)


def _split3(v):
    a = v.astype(BF16)
    r = v - a.astype(F32)
    b = r.astype(BF16)
    c = (r - b.astype(F32)).astype(BF16)
    return a, b, c


def _attn_out_kernel(x_ref, g_ref, o0_ref, o1_ref, o2_ref, l0_ref, l1_ref, l2_ref,
                     pt_ref, e_ref, wout_ref, lng_ref, lnb_ref, out_ref):
    tm, d = x_ref.shape
    o_refs = (o0_ref, o1_ref, o2_ref)
    l_refs = (l0_ref, l1_ref, l2_ref)
    n_sub = tm // PERM_TILE

    def unperm(gi, val_bf16):
        if gi == 0:
            return val_bf16.astype(F32)
        parts = [jnp.dot(pt_ref[gi - 1], val_bf16[s * PERM_TILE:(s + 1) * PERM_TILE, :],
                         preferred_element_type=F32) for s in range(n_sub)]
        return jnp.concatenate(parts, axis=0)

    lses = []
    for gi in range(3):
        l = l_refs[gi][...]
        if gi == 0:
            lses.append(l)
        else:
            a, b, c = _split3(l)
            lses.append(unperm(gi, a) + unperm(gi, b) + unperm(gi, c))
    mx = jnp.maximum(jnp.maximum(lses[0], lses[1]), lses[2])
    es = [jnp.exp(l - mx) for l in lses]
    tot = es[0] + es[1] + es[2]
    mixed = jnp.zeros((tm, d), F32)
    for gi in range(3):
        w = es[gi] / tot
        a, b, c = _split3(w)
        wide = (jnp.dot(a, e_ref[...], preferred_element_type=F32)
                + jnp.dot(b, e_ref[...], preferred_element_type=F32)
                + jnp.dot(c, e_ref[...], preferred_element_type=F32))
        mixed = mixed + wide * unperm(gi, o_refs[gi][...])
    y = jnp.dot(mixed.astype(BF16), wout_ref[...], preferred_element_type=F32)
    v = ALPHA * x_ref[...] + (1.0 + g_ref[...]) * y
    out_ref[...] = _layer_norm(v, lng_ref[...], lnb_ref[...])


def _attn_out(x2, mod3, os_, lses, perms_t, expand, w_out, ln_g, ln_b, seq):
    t, d = x2.shape
    tm = 512
    tpb = seq // tm
    row = lambda c: pl.BlockSpec((tm, c), lambda i: (i, 0))
    full = lambda shape: pl.BlockSpec(shape, lambda i: (0,) * len(shape))
    return pl.pallas_call(
        _attn_out_kernel,
        grid=(t // tm,),
        in_specs=[row(d), _mod_spec(2, d, tpb), row(d), row(d), row(d),
                  row(LSE_LANES), row(LSE_LANES), row(LSE_LANES),
                  full(perms_t.shape), full(expand.shape), full((d, d)),
                  full((1, d)), full((1, d))],
        out_specs=row(d),
        out_shape=jax.ShapeDtypeStruct((t, d), F32),
        compiler_params=_cparams(("arbitrary",)),
    )(x2, mod3, *os_, *lses, perms_t, expand, w_out, ln_g.reshape(1, d), ln_b.reshape(1, d))


def _router_kernel(x_ref, sh_ref, sc_ref, wr_ref, hp_ref, lg_ref):
    hf = x_ref[...] * (1.0 + sc_ref[...]) + sh_ref[...]
    hp_ref[...] = _pack_bf16_pairs(hf)
    lg_ref[...] = jnp.dot(hf, wr_ref[...], preferred_element_type=F32,
                          precision=lax.Precision.HIGHEST)


def _router(x2, mod3, w_router, seq):
    t, d = x2.shape
    tm = 512
    tpb = seq // tm
    ne = w_router.shape[1]
    return pl.pallas_call(
        _router_kernel,
        grid=(t // tm,),
        in_specs=[pl.BlockSpec((tm, d), lambda i: (i, 0)),
                  _mod_spec(3, d, tpb), _mod_spec(4, d, tpb),
                  pl.BlockSpec((d, ne), lambda i: (0, 0))],
        out_specs=[pl.BlockSpec((tm, d // 2), lambda i: (i, 0)),
                   pl.BlockSpec((tm, ne), lambda i: (i, 0))],
        out_shape=[jax.ShapeDtypeStruct((t, d // 2), U32),
                   jax.ShapeDtypeStruct((t, ne), F32)],
        compiler_params=_cparams(("arbitrary",)),
    )(x2, mod3, mod3, w_router)


def _first_index_of_max(v, iota, size):
    m = jnp.max(v, axis=0, keepdims=True)
    idx = jnp.min(jnp.where(v == m, iota, float(size)), axis=0, keepdims=True)
    return m, idx


def _topk_kernel(lg_ref, b_ref, tri_ref, idx_ref, w_ref, rank_ref, cnt_ref, carry_ref):
    ne, tr = lg_ref.shape
    gsz = ne // N_EXPERT_GROUPS

    @pl.when(pl.program_id(0) == 0)
    def _():
        carry_ref[...] = jnp.zeros_like(carry_ref)

    scores = 1.0 / (1.0 + jnp.exp(-lg_ref[...]))
    sel = scores + b_ref[...]
    iota_g = lax.broadcasted_iota(I32, (gsz, tr), 0).astype(F32)
    iota_n = lax.broadcasted_iota(I32, (N_EXPERT_GROUPS, tr), 0).astype(F32)
    gs = jnp.zeros((N_EXPERT_GROUPS, tr), F32)
    for g in range(N_EXPERT_GROUPS):
        blk = sel[g * gsz:(g + 1) * gsz, :]
        m1, i1 = _first_index_of_max(blk, iota_g, gsz)
        m2 = jnp.max(jnp.where(iota_g == i1, -jnp.inf, blk), axis=0, keepdims=True)
        gs = jnp.where(iota_n == float(g), m1 + m2, gs)
    gmask = jnp.zeros((N_EXPERT_GROUPS, tr), F32)
    for _ in range(TOPK_GROUPS):
        _, gi = _first_index_of_max(gs, iota_n, N_EXPERT_GROUPS)
        hit = iota_n == gi
        gmask = jnp.where(hit, 1.0, gmask)
        gs = jnp.where(hit, -jnp.inf, gs)
    masked_rows = []
    for g in range(N_EXPERT_GROUPS):
        keep = jnp.broadcast_to(gmask[g:g + 1, :], (gsz, tr)) > 0.5
        masked_rows.append(jnp.where(keep, sel[g * gsz:(g + 1) * gsz, :], -jnp.inf))
    cur = jnp.concatenate(masked_rows, axis=0)
    iota_e = lax.broadcasted_iota(I32, (ne, tr), 0).astype(F32)
    chosen = jnp.zeros((ne, tr), F32)
    idxs, tops = [], []
    for _ in range(TOP_K):
        _, ei = _first_index_of_max(cur, iota_e, ne)
        hit = iota_e == ei
        cur = jnp.where(hit, -jnp.inf, cur)
        chosen = jnp.where(hit, 1.0, chosen)
        idxs.append(ei)
        tops.append(jnp.sum(jnp.where(hit, scores, 0.0), axis=0, keepdims=True))
    wsum = tops[0]
    for k in range(1, TOP_K):
        wsum = wsum + tops[k]
    before = jnp.dot(chosen.astype(BF16), tri_ref[...], preferred_element_type=F32)
    rank_all = before + carry_ref[...]
    carry_ref[...] = carry_ref[...] + jnp.sum(chosen, axis=1, keepdims=True)
    for k in range(TOP_K):
        hit = iota_e == idxs[k]
        idx_ref[k:k + 1, :] = idxs[k].astype(I32)
        w_ref[k:k + 1, :] = tops[k] / wsum * ROUTED_SCALE
        rank_ref[k:k + 1, :] = jnp.sum(jnp.where(hit, rank_all, 0.0), axis=0,
                                       keepdims=True).astype(I32)
    cnt_ref[...] = jnp.broadcast_to(carry_ref[...], cnt_ref.shape).astype(I32)


def _topk_route(logits_t, router_b):
    ne, t = logits_t.shape
    tr = TOK_TILE
    tri = (jnp.arange(tr)[:, None] < jnp.arange(tr)[None, :]).astype(BF16)
    out = lambda dt: jax.ShapeDtypeStruct((TOP_K, t), dt)
    row = pl.BlockSpec((TOP_K, tr), lambda i: (0, i))
    return pl.pallas_call(
        _topk_kernel,
        grid=(t // tr,),
        in_specs=[pl.BlockSpec((ne, tr), lambda i: (0, i)),
                  pl.BlockSpec((ne, 1), lambda i: (0, 0)),
                  pl.BlockSpec((tr, tr), lambda i: (0, 0))],
        out_specs=[row, row, row, pl.BlockSpec((ne, 128), lambda i: (0, 0))],
        out_shape=[out(I32), out(F32), out(I32), jax.ShapeDtypeStruct((ne, 128), I32)],
        scratch_shapes=[pltpu.VMEM((ne, 1), F32)],
        compiler_params=_cparams(("arbitrary",)),
    )(logits_t, router_b.reshape(ne, 1), tri)


def _dest_kernel(start_ref, idx_ref, rank_ref, dest_ref):
    idx = idx_ref[...]
    acc = jnp.zeros(idx.shape, I32)
    for e in range(N_EXPERTS):
        acc = jnp.where(idx == e, start_ref[e], acc)
    dest = acc + rank_ref[...]
    tr = idx.shape[1]
    for k in range(TOP_K):
        dest_ref[:, k * tr:(k + 1) * tr] = dest[k:k + 1, :]


def _dest_slots(padded_start, idx, rank):
    t = idx.shape[1]
    tr = TOK_TILE
    row = pl.BlockSpec((TOP_K, tr), lambda i, s: (0, i))
    return pl.pallas_call(
        _dest_kernel,
        grid_spec=pltpu.PrefetchScalarGridSpec(
            num_scalar_prefetch=1,
            grid=(t // tr,),
            in_specs=[row, row],
            out_specs=pl.BlockSpec((None, 1, TOP_K * tr), lambda i, s: (i, 0, 0)),
        ),
        out_shape=jax.ShapeDtypeStruct((t // tr, 1, TOP_K * tr), I32),
        compiler_params=_cparams(("arbitrary",)),
    )(padded_start, idx, rank)


def _dispatch_kernel(dest_hbm, hp_ref, xg_in, xg_out, dest_smem, sem_idx, sem_rows):
    del xg_in
    tr = hp_ref.shape[0]
    i = pl.program_id(0)
    idx_copy = pltpu.make_async_copy(dest_hbm.at[i], dest_smem, sem_idx)
    idx_copy.start()
    idx_copy.wait()

    def row_copy(tok, slot):
        return pltpu.make_async_copy(hp_ref.at[pl.ds(tok, 1), :],
                                     xg_out.at[pl.ds(slot, 1), :], sem_rows)

    def issue(tok, carry):
        for k in range(TOP_K):
            row_copy(tok, dest_smem[0, k * tr + tok]).start()
        return carry

    lax.fori_loop(0, tr, issue, 0)

    def drain(tok, carry):
        for k in range(TOP_K):
            row_copy(tok, 0).wait()
        return carry

    lax.fori_loop(0, tr, drain, 0)


def _dispatch(dest_tiles, hp, n_slots):
    t, half = hp.shape
    tr = TOK_TILE
    xg0 = jnp.zeros((n_slots, half), U32)
    return pl.pallas_call(
        _dispatch_kernel,
        grid=(t // tr,),
        in_specs=[pl.BlockSpec(memory_space=pl.ANY),
                  pl.BlockSpec((tr, half), lambda i: (i, 0)),
                  pl.BlockSpec(memory_space=pl.ANY)],
        out_specs=pl.BlockSpec(memory_space=pl.ANY),
        out_shape=jax.ShapeDtypeStruct((n_slots, half), U32),
        scratch_shapes=[pltpu.SMEM((1, TOP_K * tr), I32),
                        pltpu.SemaphoreType.DMA, pltpu.SemaphoreType.DMA],
        input_output_aliases={2: 0},
        compiler_params=_cparams(("arbitrary",)),
    )(dest_tiles, hp, xg0)


def _expert_kernel(ge_ref, nu_ref, x_ref, wgu_ref, wdn_ref, y_ref, wgu_bf, wdn_bf):
    j = pl.program_id(0)
    used = j < nu_ref[0]
    prev = ge_ref[jnp.maximum(j - 1, 0)]
    fresh = (j == 0) | (ge_ref[j] != prev)

    @pl.when(used & fresh)
    def _():
        wgu_bf[...] = wgu_ref[...].astype(BF16)
        wdn_bf[...] = wdn_ref[...].astype(BF16)

    @pl.when(used)
    def _():
        half = x_ref.shape[1]
        f = wdn_bf.shape[0]
        hi, lo = _unpack_bf16_pairs(x_ref[...])
        gu = (jnp.dot(hi.astype(BF16), wgu_bf[0:half, :], preferred_element_type=F32)
              + jnp.dot(lo.astype(BF16), wgu_bf[half:, :], preferred_element_type=F32))
        act = (_silu(gu[:, :f]) * gu[:, f:]).astype(BF16)
        y_ref[...] = _pack_bf16_pairs(jnp.dot(act, wdn_bf[...], preferred_element_type=F32))

    @pl.when(jnp.logical_not(used))
    def _():
        y_ref[...] = jnp.zeros_like(y_ref)


def _expert_gemm(group_e, n_used, xg, w_gu, w_down):
    n_slots, half = xg.shape
    ne, d, f2 = w_gu.shape
    f = w_down.shape[1]
    nb = n_slots // EXPERT_ROWS
    return pl.pallas_call(
        _expert_kernel,
        grid_spec=pltpu.PrefetchScalarGridSpec(
            num_scalar_prefetch=2,
            grid=(nb,),
            in_specs=[
                pl.BlockSpec((EXPERT_ROWS, half),
                             lambda j, ge, nu: (jnp.minimum(j, nu[0] - 1), 0)),
                pl.BlockSpec((None, d, f2), lambda j, ge, nu: (ge[j], 0, 0)),
                pl.BlockSpec((None, f, d), lambda j, ge, nu: (ge[j], 0, 0)),
            ],
            out_specs=pl.BlockSpec((EXPERT_ROWS, half), lambda j, ge, nu: (j, 0)),
            scratch_shapes=[pltpu.VMEM((d, f2), BF16), pltpu.VMEM((f, d), BF16)],
        ),
        out_shape=jax.ShapeDtypeStruct((n_slots, half), U32),
        compiler_params=_cparams(("arbitrary",)),
    )(group_e, n_used, xg, w_gu, w_down)


def _combine_kernel(dest_hbm, x_ref, sh_ref, sc_ref, g_ref, wt_ref, wsgu_ref, wsdn_ref,
                    lng_ref, lnb_ref, yg_hbm, o_ref, dest_smem, buf_ref, sem_idx, sem_rows):
    tr, d = x_ref.shape
    i = pl.program_id(0)
    idx_copy = pltpu.make_async_copy(dest_hbm.at[i], dest_smem, sem_idx)
    idx_copy.start()
    idx_copy.wait()

    def row_copy(tok, k, slot):
        return pltpu.make_async_copy(yg_hbm.at[pl.ds(slot, 1), :],
                                     buf_ref.at[k, pl.ds(tok, 1), :], sem_rows)

    def issue(tok, carry):
        for k in range(TOP_K):
            row_copy(tok, k, dest_smem[0, k * tr + tok]).start()
        return carry

    lax.fori_loop(0, tr, issue, 0)

    x = x_ref[...]
    hf = (x * (1.0 + sc_ref[...]) + sh_ref[...]).astype(BF16)
    f = wsdn_ref.shape[0]
    su = jnp.dot(hf, wsgu_ref[...], preferred_element_type=F32)
    act = (_silu(su[:, :f]) * su[:, f:]).astype(BF16)
    shared = jnp.dot(act, wsdn_ref[...], preferred_element_type=F32)

    def drain(tok, carry):
        for k in range(TOP_K):
            row_copy(tok, k, 0).wait()
        return carry

    lax.fori_loop(0, tr, drain, 0)

    half = d // 2
    acc_hi = jnp.zeros((tr, half), F32)
    acc_lo = jnp.zeros((tr, half), F32)
    for k in range(TOP_K):
        hi, lo = _unpack_bf16_pairs(buf_ref[k])
        wk = wt_ref[:, k:k + 1]
        acc_hi = acc_hi + wk * hi
        acc_lo = acc_lo + wk * lo
    routed = jnp.concatenate([acc_hi, acc_lo], axis=1)
    v = ALPHA * x + (1.0 + g_ref[...]) * (routed + shared)
    o_ref[...] = _layer_norm(v, lng_ref[...], lnb_ref[...])


def _combine(dest_tiles, x2, mod3, w_tok, w_sh_gu, w_sh_down, ln_g, ln_b, yg, seq):
    t, d = x2.shape
    tr = TOK_TILE
    tpb = seq // tr
    full = lambda shape: pl.BlockSpec(shape, lambda i: (0,) * len(shape))
    return pl.pallas_call(
        _combine_kernel,
        grid=(t // tr,),
        in_specs=[pl.BlockSpec(memory_space=pl.ANY),
                  pl.BlockSpec((tr, d), lambda i: (i, 0)),
                  _mod_spec(3, d, tpb), _mod_spec(4, d, tpb), _mod_spec(5, d, tpb),
                  pl.BlockSpec((tr, TOP_K), lambda i: (i, 0)),
                  full(w_sh_gu.shape), full(w_sh_down.shape), full((1, d)), full((1, d)),
                  pl.BlockSpec(memory_space=pl.ANY)],
        out_specs=pl.BlockSpec((tr, d), lambda i: (i, 0)),
        out_shape=jax.ShapeDtypeStruct((t, d), F32),
        scratch_shapes=[pltpu.SMEM((1, TOP_K * tr), I32),
                        pltpu.VMEM((TOP_K, tr, d // 2), U32),
                        pltpu.SemaphoreType.DMA, pltpu.SemaphoreType.DMA],
        compiler_params=_cparams(("arbitrary",)),
    )(dest_tiles, x2, mod3, mod3, mod3, w_tok, w_sh_gu, w_sh_down,
      ln_g.reshape(1, d), ln_b.reshape(1, d), yg)


def _moe_layer(x2, mod3, w_router, router_b, w_gu, w_down, w_sh_gu, w_sh_down,
               ln_g, ln_b, seq):
    t, d = x2.shape
    hp, logits = _router(x2, mod3, w_router, seq)
    idx, w_top, rank, counts = _topk_route(logits.T, router_b)
    c = EXPERT_ROWS
    counts = counts[:, 0]
    padded = ((counts + c - 1) // c) * c
    padded_end = jnp.cumsum(padded)
    padded_start = (padded_end - padded).astype(I32)
    n_slots = ((t * TOP_K + N_EXPERTS * (c - 1) + c - 1) // c) * c
    nb = n_slots // c
    group_e = jnp.clip(jnp.searchsorted(padded_end, jnp.arange(nb) * c, side='right'),
                       0, N_EXPERTS - 1).astype(I32)
    n_used = (padded_end[-1:] // c).astype(I32)
    dest_tiles = _dest_slots(padded_start, idx, rank)
    xg = _dispatch(dest_tiles, hp, n_slots)
    yg = _expert_gemm(group_e, n_used, xg, w_gu, w_down)
    return _combine(dest_tiles, x2, mod3, w_top.T, w_sh_gu, w_sh_down, ln_g, ln_b, yg, seq)


def kernel(x, c, ada_w, ada_b, pool_w_in, pool_w_grp, pool_scale, pool_w_out, attn_w_in, attn_w_out, ln1_g, ln1_b, router_w, router_b, exp_w_gu, exp_w_down, sh_w_gu, sh_w_down, ln2_g, ln2_b):
    batch, seq, d = x.shape
    depth = ada_w.shape[0]
    t = batch * seq
    mod = _modulation(c, ada_w, ada_b).reshape(depth, batch, 1, 6 * d)
    perms = jnp.stack([_perm_matrix(dil) for _, dil in ATTN_PATTERNS[1:]])
    perms_t = jnp.swapaxes(perms, 1, 2)
    expand = (jnp.arange(LSE_LANES)[:, None] == (jnp.arange(d)[None, :] // HEAD_DIM)).astype(BF16)
    x2 = x.reshape(t, d)
    for i in range(depth):
        mod3 = mod[i]
        j = i // 2
        if i % 2 == 0:
            x2 = _pool_layer(x2, mod3, pool_w_in[j].astype(BF16), pool_w_grp[j].astype(BF16),
                             pool_scale[j], pool_w_out[j].astype(BF16), ln1_g[i], ln1_b[i], seq)
        else:
            qkv = _qkv_proj(x2, mod3, perms, attn_w_in[j].astype(BF16), seq)
            res = [_attention_group(qkv, g, batch, seq) for g in range(len(ATTN_PATTERNS))]
            x2 = _attn_out(x2, mod3, [r[0] for r in res], [r[1] for r in res], perms_t, expand,
                           attn_w_out[j].astype(BF16), ln1_g[i], ln1_b[i], seq)
        x2 = _moe_layer(x2, mod3, router_w[i], router_b[i], exp_w_gu[i], exp_w_down[i],
                        sh_w_gu[i].astype(BF16), sh_w_down[i].astype(BF16),
                        ln2_g[i], ln2_b[i], seq)
    return x2.reshape(batch, seq, d)
```

```python
import functools
import math

import jax
import jax.numpy as jnp
from jax import lax
from jax.experimental import pallas as pl
from jax.experimental.pallas import tpu as pltpu

F32 = jnp.float32
BF16 = jnp.bfloat16
U32 = jnp.uint32
I32 = jnp.int32

POOL_WINDOWS = (2, 4, 8, 16)
ATTN_PATTERNS = ((128, 1), (512, 4), (2048, 16))
HEAD_DIM = 64
N_HEADS = 16
Q_BLOCK = 128
N_EXPERTS = 64
TOP_K = 8
N_EXPERT_GROUPS = 8
TOPK_GROUPS = 4
ROUTED_SCALE = 2.5
EXPERT_ROWS = 256
DEPTH = 4
ALPHA = (2 * DEPTH) ** 0.25
LN_EPS = 1e-5

PERM_TILE = 256
POOL_HALO = 16
TOK_TILE = 256
LSE_LANES = 128
VMEM_LIMIT = 56 * 1024 * 1024
NEG_BIG = -1e30


def _cparams(sem):
    return pltpu.CompilerParams(dimension_semantics=sem, vmem_limit_bytes=VMEM_LIMIT)


def _layer_norm(v, g, b):
    mu = jnp.mean(v, axis=-1, keepdims=True)
    c = v - mu
    var = jnp.mean(c * c, axis=-1, keepdims=True)
    return c * lax.rsqrt(var + LN_EPS) * g + b


def _silu(v):
    return v * (1.0 / (1.0 + jnp.exp(-v)))


def _pack_bf16_pairs(v):
    n = v.shape[1] // 2
    hi = lax.bitcast_convert_type(v[:, :n].astype(BF16).astype(F32), U32)
    lo = lax.bitcast_convert_type(v[:, n:].astype(BF16).astype(F32), U32)
    return hi | (lo >> 16)


def _unpack_bf16_pairs(p):
    hi = lax.bitcast_convert_type(p & jnp.uint32(0xFFFF0000), F32)
    lo = lax.bitcast_convert_type(p << 16, F32)
    return hi, lo


def _mod_kernel(c_ref, w_ref, b_ref, o_ref):
    cs = _silu(c_ref[...])
    o_ref[...] = jnp.dot(cs, w_ref[...], preferred_element_type=F32) + b_ref[...]


def _modulation(c, ada_w, ada_b):
    depth, d, n6 = ada_w.shape
    b = c.shape[0]
    tn = 1536
    return pl.pallas_call(
        _mod_kernel,
        grid=(depth, n6 // tn),
        in_specs=[
            pl.BlockSpec((b, d), lambda i, n: (0, 0)),
            pl.BlockSpec((None, d, tn), lambda i, n: (i, 0, n)),
            pl.BlockSpec((None, 1, tn), lambda i, n: (i, 0, n)),
        ],
        out_specs=pl.BlockSpec((None, b, tn), lambda i, n: (i, 0, n)),
        out_shape=jax.ShapeDtypeStruct((depth, b, n6), F32),
        compiler_params=_cparams(("arbitrary", "arbitrary")),
    )(c, ada_w, ada_b.reshape(depth, 1, n6))


def _mod_spec(chunk, d, tiles_per_batch):
    return pl.BlockSpec((None, 1, d), lambda *idx: (idx[0] // tiles_per_batch, 0, chunk))


def _pool_kernel(x_ref, sh_ref, sc_ref, g_ref, win_ref, wgrp_ref, cs_ref, wout_ref,
                 lng_ref, lnb_ref, o_ref, ext_ref, *, tiles_per_batch):
    tm, d = x_ref.shape
    s_idx = pl.program_id(0) % tiles_per_batch
    x = x_ref[...]
    h = (x * (1.0 + sc_ref[...]) + sh_ref[...]).astype(BF16)
    u = jnp.dot(h, win_ref[...], preferred_element_type=F32)

    @pl.when(s_idx == 0)
    def _():
        ext_ref[0:POOL_HALO, :] = jnp.zeros((POOL_HALO, d), F32)

    @pl.when(s_idx != 0)
    def _():
        ext_ref[0:POOL_HALO, :] = ext_ref[tm:tm + POOL_HALO, :]

    ext_ref[POOL_HALO:POOL_HALO + tm, :] = u

    pos = s_idx * tm + lax.broadcasted_iota(I32, (tm, 1), 0) + 1
    gc = d // len(POOL_WINDOWS)
    ys = []
    for g, w in enumerate(POOL_WINDOWS):
        cols = slice(g * gc, (g + 1) * gc)
        acc = u[:, cols]
        for j in range(1, w):
            acc = acc + ext_ref[POOL_HALO - j:POOL_HALO - j + tm, cols]
        cnt = jnp.minimum(pos, w).astype(F32)
        z = (acc / cnt - u[:, cols]).astype(BF16)
        ys.append(jnp.dot(z, wgrp_ref[g], preferred_element_type=F32))
    y = (jnp.concatenate(ys, axis=1) * cs_ref[...]).astype(BF16)
    out = jnp.dot(y, wout_ref[...], preferred_element_type=F32)
    v = ALPHA * x + (1.0 + g_ref[...]) * out
    o_ref[...] = _layer_norm(v, lng_ref[...], lnb_ref[...])


def _pool_layer(x2, mod3, w_in, w_grp, ch_scale, w_out, ln_g, ln_b, seq):
    t, d = x2.shape
    tm = 512
    tpb = seq // tm
    full = lambda shape: pl.BlockSpec(shape, lambda i: (0,) * len(shape))
    return pl.pallas_call(
        functools.partial(_pool_kernel, tiles_per_batch=tpb),
        grid=(t // tm,),
        in_specs=[
            pl.BlockSpec((tm, d), lambda i: (i, 0)),
            _mod_spec(0, d, tpb), _mod_spec(1, d, tpb), _mod_spec(2, d, tpb),
            full((d, d)), full(w_grp.shape), full((1, d)), full((d, d)),
            full((1, d)), full((1, d)),
        ],
        out_specs=pl.BlockSpec((tm, d), lambda i: (i, 0)),
        out_shape=jax.ShapeDtypeStruct((t, d), F32),
        scratch_shapes=[pltpu.VMEM((tm + POOL_HALO, d), F32)],
        compiler_params=_cparams(("arbitrary",)),
    )(x2, mod3, mod3, mod3, w_in, w_grp, ch_scale.reshape(1, d), w_out,
      ln_g.reshape(1, d), ln_b.reshape(1, d))


def _perm_matrix(dil):
    p = jnp.arange(PERM_TILE)
    chunk = PERM_TILE // dil
    src = (p % chunk) * dil + p // chunk
    return (src[:, None] == jnp.arange(PERM_TILE)[None, :]).astype(BF16)


def _qkv_kernel(x_ref, sh_ref, sc_ref, p_ref, w_ref, o_ref, h_ref):
    tm = x_ref.shape[0]
    g = pl.program_id(1)
    part = pl.program_id(2)

    @pl.when((g == 0) & (part == 0))
    def _():
        h = (x_ref[...] * (1.0 + sc_ref[...]) + sh_ref[...]).astype(BF16)
        h_ref[0] = h
        for gi in range(1, len(ATTN_PATTERNS)):
            for s in range(tm // PERM_TILE):
                rows = slice(s * PERM_TILE, (s + 1) * PERM_TILE)
                h_ref[gi, rows, :] = jnp.dot(
                    p_ref[gi - 1], h[rows, :], preferred_element_type=F32).astype(BF16)

    o_ref[...] = jnp.dot(h_ref[g], w_ref[...], preferred_element_type=F32).astype(BF16)


def _qkv_proj(x2, mod3, perms, w_in, seq):
    t, d = x2.shape
    ng = len(ATTN_PATTERNS)
    tm = 1024
    tpb = seq // tm
    return pl.pallas_call(
        _qkv_kernel,
        grid=(t // tm, ng, 3),
        in_specs=[
            pl.BlockSpec((tm, d), lambda m, g, p: (m, 0)),
            _mod_spec(0, d, tpb), _mod_spec(1, d, tpb),
            pl.BlockSpec(perms.shape, lambda m, g, p: (0, 0, 0)),
            pl.BlockSpec((d, d), lambda m, g, p: (0, g * 3 + p)),
        ],
        out_specs=pl.BlockSpec((tm, d), lambda m, g, p: (m, g * 3 + p)),
        out_shape=jax.ShapeDtypeStruct((t, ng * 3 * d), BF16),
        scratch_shapes=[pltpu.VMEM((ng, tm, d), BF16)],
        compiler_params=_cparams(("arbitrary", "arbitrary", "arbitrary")),
    )(x2, mod3, mod3, perms, w_in)


def _attn_kernel(q_ref, kp_ref, kc_ref, vp_ref, vc_ref, o_ref, lse_ref, *, group, dil):
    bq = Q_BLOCK
    d = N_HEADS * HEAD_DIM
    j = pl.program_id(2)
    q = q_ref[...].reshape(bq, d)
    kp = kp_ref[...].reshape(bq, d)
    kc = kc_ref[...].reshape(bq, d)
    vp = vp_ref[...].reshape(bq, d)
    vc = vc_ref[...].reshape(bq, d)

    qi = lax.broadcasted_iota(I32, (bq, 2 * bq), 0)
    kj = lax.broadcasted_iota(I32, (bq, 2 * bq), 1)
    dist = qi + bq - kj
    steps = ATTN_PATTERNS[group][0] // dil
    valid = (dist >= 0) & (dist <= steps) & ((kj >= bq) | (j > 0))
    distf = dist.astype(F32)
    lane = lax.broadcasted_iota(I32, (bq, LSE_LANES), 1)
    lse_tile = jnp.zeros((bq, LSE_LANES), F32)
    n_tot = len(ATTN_PATTERNS) * N_HEADS
    outs = []
    for h in range(N_HEADS):
        cols = slice(h * HEAD_DIM, (h + 1) * HEAD_DIM)
        slope = 2.0 ** (-8.0 * (group * N_HEADS + h + 1) / n_tot)
        kh = jnp.concatenate([kp[:, cols], kc[:, cols]], axis=0)
        vh = jnp.concatenate([vp[:, cols], vc[:, cols]], axis=0)
        s = lax.dot_general(q[:, cols], kh, (((1,), (1,)), ((), ())),
                            preferred_element_type=F32)
        s = s * (HEAD_DIM ** -0.5) - (slope * dil) * distf
        s = jnp.where(valid, s, NEG_BIG)
        m = jnp.max(s, axis=-1, keepdims=True)
        p = jnp.exp(s - m)
        den = jnp.sum(p, axis=-1, keepdims=True)
        o = jnp.dot(p.astype(BF16), vh, preferred_element_type=F32) / den
        outs.append(o)
        lse_tile = jnp.where(lane == h, m + jnp.log(den), lse_tile)
    o_ref[...] = jnp.concatenate(outs, axis=1).astype(BF16).reshape(o_ref.shape)
    lse_ref[...] = lse_tile.reshape(lse_ref.shape)


def _attention_group(qkv, group, batch, seq):
    dil = ATTN_PATTERNS[group][1]
    d = N_HEADS * HEAD_DIM
    t = qkv.shape[0]
    sub = seq // dil
    nb = sub // Q_BLOCK
    rows = Q_BLOCK if dil == 1 else PERM_TILE // dil
    chunks = Q_BLOCK // rows
    u = seq // (rows * dil)
    view = lambda a, c: a.reshape(batch, u, dil, rows, c)
    blk = lambda c: (None, chunks, None, rows, c)
    col0 = group * 3
    q_spec = pl.BlockSpec(blk(d), lambda b, r, j: (b, j, r, 0, col0))
    kc_spec = pl.BlockSpec(blk(d), lambda b, r, j: (b, j, r, 0, col0 + 1))
    kp_spec = pl.BlockSpec(blk(d), lambda b, r, j: (b, jnp.maximum(j - 1, 0), r, 0, col0 + 1))
    vc_spec = pl.BlockSpec(blk(d), lambda b, r, j: (b, j, r, 0, col0 + 2))
    vp_spec = pl.BlockSpec(blk(d), lambda b, r, j: (b, jnp.maximum(j - 1, 0), r, 0, col0 + 2))
    qkv5 = view(qkv, qkv.shape[1])
    o, lse = pl.pallas_call(
        functools.partial(_attn_kernel, group=group, dil=dil),
        grid=(batch, dil, nb),
        in_specs=[q_spec, kp_spec, kc_spec, vp_spec, vc_spec],
        out_specs=[
            pl.BlockSpec(blk(d), lambda b, r, j: (b, j, r, 0, 0)),
            pl.BlockSpec(blk(LSE_LANES), lambda b, r, j: (b, j, r, 0, 0)),
        ],
        out_shape=[
            jax.ShapeDtypeStruct((batch, u, dil, rows, d), BF16),
            jax.ShapeDtypeStruct((batch, u, dil, rows, LSE_LANES), F32),
        ],
        compiler_params=_cparams(("arbitrary", "arbitrary", "arbitrary")),
    )(qkv5, qkv5, qkv5, qkv5, qkv5)
    return o.reshape(t, d), lse.reshape(t, LSE_LANES)


def _split3(v):
    a = v.astype(BF16)
    r = v - a.astype(F32)
    b = r.astype(BF16)
    c = (r - b.astype(F32)).astype(BF16)
    return a, b, c


def _attn_out_kernel(x_ref, g_ref, o0_ref, o1_ref, o2_ref, l0_ref, l1_ref, l2_ref,
                     pt_ref, e_ref, wout_ref, lng_ref, lnb_ref, out_ref):
    tm, d = x_ref.shape
    o_refs = (o0_ref, o1_ref, o2_ref)
    l_refs = (l0_ref, l1_ref, l2_ref)
    n_sub = tm // PERM_TILE

    def unperm(gi, val_bf16):
        if gi == 0:
            return val_bf16.astype(F32)
        parts = [jnp.dot(pt_ref[gi - 1], val_bf16[s * PERM_TILE:(s + 1) * PERM_TILE, :],
                         preferred_element_type=F32) for s in range(n_sub)]
        return jnp.concatenate(parts, axis=0)

    lses = []
    for gi in range(3):
        l = l_refs[gi][...]
        if gi == 0:
            lses.append(l)
        else:
            a, b, c = _split3(l)
            lses.append(unperm(gi, a) + unperm(gi, b) + unperm(gi, c))
    mx = jnp.maximum(jnp.maximum(lses[0], lses[1]), lses[2])
    es = [jnp.exp(l - mx) for l in lses]
    tot = es[0] + es[1] + es[2]
    mixed = jnp.zeros((tm, d), F32)
    for gi in range(3):
        w = es[gi] / tot
        a, b, c = _split3(w)
        wide = (jnp.dot(a, e_ref[...], preferred_element_type=F32)
                + jnp.dot(b, e_ref[...], preferred_element_type=F32)
                + jnp.dot(c, e_ref[...], preferred_element_type=F32))
        mixed = mixed + wide * unperm(gi, o_refs[gi][...])
    y = jnp.dot(mixed.astype(BF16), wout_ref[...], preferred_element_type=F32)
    v = ALPHA * x_ref[...] + (1.0 + g_ref[...]) * y
    out_ref[...] = _layer_norm(v, lng_ref[...], lnb_ref[...])


def _attn_out(x2, mod3, os_, lses, perms_t, expand, w_out, ln_g, ln_b, seq):
    t, d = x2.shape
    tm = 512
    tpb = seq // tm
    row = lambda c: pl.BlockSpec((tm, c), lambda i: (i, 0))
    full = lambda shape: pl.BlockSpec(shape, lambda i: (0,) * len(shape))
    return pl.pallas_call(
        _attn_out_kernel,
        grid=(t // tm,),
        in_specs=[row(d), _mod_spec(2, d, tpb), row(d), row(d), row(d),
                  row(LSE_LANES), row(LSE_LANES), row(LSE_LANES),
                  full(perms_t.shape), full(expand.shape), full((d, d)),
                  full((1, d)), full((1, d))],
        out_specs=row(d),
        out_shape=jax.ShapeDtypeStruct((t, d), F32),
        compiler_params=_cparams(("arbitrary",)),
    )(x2, mod3, *os_, *lses, perms_t, expand, w_out, ln_g.reshape(1, d), ln_b.reshape(1, d))


def _router_kernel(x_ref, sh_ref, sc_ref, wr_ref, hp_ref, lg_ref):
    hf = x_ref[...] * (1.0 + sc_ref[...]) + sh_ref[...]
    hp_ref[...] = _pack_bf16_pairs(hf)
    lg_ref[...] = jnp.dot(hf, wr_ref[...], preferred_element_type=F32,
                          precision=lax.Precision.HIGHEST)


def _router(x2, mod3, w_router, seq):
    t, d = x2.shape
    tm = 512
    tpb = seq // tm
    ne = w_router.shape[1]
    return pl.pallas_call(
        _router_kernel,
        grid=(t // tm,),
        in_specs=[pl.BlockSpec((tm, d), lambda i: (i, 0)),
                  _mod_spec(3, d, tpb), _mod_spec(4, d, tpb),
                  pl.BlockSpec((d, ne), lambda i: (0, 0))],
        out_specs=[pl.BlockSpec((tm, d // 2), lambda i: (i, 0)),
                   pl.BlockSpec((tm, ne), lambda i: (i, 0))],
        out_shape=[jax.ShapeDtypeStruct((t, d // 2), U32),
                   jax.ShapeDtypeStruct((t, ne), F32)],
        compiler_params=_cparams(("arbitrary",)),
    )(x2, mod3, mod3, w_router)


def _first_index_of_max(v, iota, size):
    m = jnp.max(v, axis=0, keepdims=True)
    idx = jnp.min(jnp.where(v == m, iota, float(size)), axis=0, keepdims=True)
    return m, idx


def _topk_kernel(lg_ref, b_ref, tri_ref, idx_ref, w_ref, rank_ref, cnt_ref, carry_ref):
    ne, tr = lg_ref.shape
    gsz = ne // N_EXPERT_GROUPS

    @pl.when(pl.program_id(0) == 0)
    def _():
        carry_ref[...] = jnp.zeros_like(carry_ref)

    scores = 1.0 / (1.0 + jnp.exp(-lg_ref[...]))
    sel = scores + b_ref[...]
    iota_g = lax.broadcasted_iota(I32, (gsz, tr), 0).astype(F32)
    iota_n = lax.broadcasted_iota(I32, (N_EXPERT_GROUPS, tr), 0).astype(F32)
    gs = jnp.zeros((N_EXPERT_GROUPS, tr), F32)
    for g in range(N_EXPERT_GROUPS):
        blk = sel[g * gsz:(g + 1) * gsz, :]
        m1, i1 = _first_index_of_max(blk, iota_g, gsz)
        m2 = jnp.max(jnp.where(iota_g == i1, -jnp.inf, blk), axis=0, keepdims=True)
        gs = jnp.where(iota_n == float(g), m1 + m2, gs)
    gmask = jnp.zeros((N_EXPERT_GROUPS, tr), F32)
    for _ in range(TOPK_GROUPS):
        _, gi = _first_index_of_max(gs, iota_n, N_EXPERT_GROUPS)
        hit = iota_n == gi
        gmask = jnp.where(hit, 1.0, gmask)
        gs = jnp.where(hit, -jnp.inf, gs)
    masked_rows = []
    for g in range(N_EXPERT_GROUPS):
        keep = jnp.broadcast_to(gmask[g:g + 1, :], (gsz, tr)) > 0.5
        masked_rows.append(jnp.where(keep, sel[g * gsz:(g + 1) * gsz, :], -jnp.inf))
    cur = jnp.concatenate(masked_rows, axis=0)
    iota_e = lax.broadcasted_iota(I32, (ne, tr), 0).astype(F32)
    chosen = jnp.zeros((ne, tr), F32)
    idxs, tops = [], []
    for _ in range(TOP_K):
        _, ei = _first_index_of_max(cur, iota_e, ne)
        hit = iota_e == ei
        cur = jnp.where(hit, -jnp.inf, cur)
        chosen = jnp.where(hit, 1.0, chosen)
        idxs.append(ei)
        tops.append(jnp.sum(jnp.where(hit, scores, 0.0), axis=0, keepdims=True))
    wsum = tops[0]
    for k in range(1, TOP_K):
        wsum = wsum + tops[k]
    before = jnp.dot(chosen.astype(BF16), tri_ref[...], preferred_element_type=F32)
    rank_all = before + carry_ref[...]
    carry_ref[...] = carry_ref[...] + jnp.sum(chosen, axis=1, keepdims=True)
    for k in range(TOP_K):
        hit = iota_e == idxs[k]
        idx_ref[k:k + 1, :] = idxs[k].astype(I32)
        w_ref[k:k + 1, :] = tops[k] / wsum * ROUTED_SCALE
        rank_ref[k:k + 1, :] = jnp.sum(jnp.where(hit, rank_all, 0.0), axis=0,
                                       keepdims=True).astype(I32)
    cnt_ref[...] = jnp.broadcast_to(carry_ref[...], cnt_ref.shape).astype(I32)


def _topk_route(logits_t, router_b):
    ne, t = logits_t.shape
    tr = TOK_TILE
    tri = (jnp.arange(tr)[:, None] < jnp.arange(tr)[None, :]).astype(BF16)
    out = lambda dt: jax.ShapeDtypeStruct((TOP_K, t), dt)
    row = pl.BlockSpec((TOP_K, tr), lambda i: (0, i))
    return pl.pallas_call(
        _topk_kernel,
        grid=(t // tr,),
        in_specs=[pl.BlockSpec((ne, tr), lambda i: (0, i)),
                  pl.BlockSpec((ne, 1), lambda i: (0, 0)),
                  pl.BlockSpec((tr, tr), lambda i: (0, 0))],
        out_specs=[row, row, row, pl.BlockSpec((ne, 128), lambda i: (0, 0))],
        out_shape=[out(I32), out(F32), out(I32), jax.ShapeDtypeStruct((ne, 128), I32)],
        scratch_shapes=[pltpu.VMEM((ne, 1), F32)],
        compiler_params=_cparams(("arbitrary",)),
    )(logits_t, router_b.reshape(ne, 1), tri)


def _dest_kernel(start_ref, idx_ref, rank_ref, dest_ref):
    idx = idx_ref[...]
    acc = jnp.zeros(idx.shape, I32)
    for e in range(N_EXPERTS):
        acc = jnp.where(idx == e, start_ref[e], acc)
    dest = acc + rank_ref[...]
    tr = idx.shape[1]
    for k in range(TOP_K):
        dest_ref[:, k * tr:(k + 1) * tr] = dest[k:k + 1, :]


def _dest_slots(padded_start, idx, rank):
    t = idx.shape[1]
    tr = TOK_TILE
    row = pl.BlockSpec((TOP_K, tr), lambda i, s: (0, i))
    return pl.pallas_call(
        _dest_kernel,
        grid_spec=pltpu.PrefetchScalarGridSpec(
            num_scalar_prefetch=1,
            grid=(t // tr,),
            in_specs=[row, row],
            out_specs=pl.BlockSpec((None, 1, TOP_K * tr), lambda i, s: (i, 0, 0)),
        ),
        out_shape=jax.ShapeDtypeStruct((t // tr, 1, TOP_K * tr), I32),
        compiler_params=_cparams(("arbitrary",)),
    )(padded_start, idx, rank)


def _dispatch_kernel(dest_hbm, hp_ref, xg_in, xg_out, dest_smem, sem_idx, sem_rows):
    del xg_in
    tr = hp_ref.shape[0]
    i = pl.program_id(0)
    idx_copy = pltpu.make_async_copy(dest_hbm.at[i], dest_smem, sem_idx)
    idx_copy.start()
    idx_copy.wait()

    def row_copy(tok, slot):
        return pltpu.make_async_copy(hp_ref.at[pl.ds(tok, 1), :],
                                     xg_out.at[pl.ds(slot, 1), :], sem_rows)

    def issue(tok, carry):
        for k in range(TOP_K):
            row_copy(tok, dest_smem[0, k * tr + tok]).start()
        return carry

    lax.fori_loop(0, tr, issue, 0)

    def drain(tok, carry):
        for k in range(TOP_K):
            row_copy(tok, 0).wait()
        return carry

    lax.fori_loop(0, tr, drain, 0)


def _dispatch(dest_tiles, hp, n_slots):
    t, half = hp.shape
    tr = TOK_TILE
    xg0 = jnp.zeros((n_slots, half), U32)
    return pl.pallas_call(
        _dispatch_kernel,
        grid=(t // tr,),
        in_specs=[pl.BlockSpec(memory_space=pl.ANY),
                  pl.BlockSpec((tr, half), lambda i: (i, 0)),
                  pl.BlockSpec(memory_space=pl.ANY)],
        out_specs=pl.BlockSpec(memory_space=pl.ANY),
        out_shape=jax.ShapeDtypeStruct((n_slots, half), U32),
        scratch_shapes=[pltpu.SMEM((1, TOP_K * tr), I32),
                        pltpu.SemaphoreType.DMA, pltpu.SemaphoreType.DMA],
        input_output_aliases={2: 0},
        compiler_params=_cparams(("arbitrary",)),
    )(dest_tiles, hp, xg0)


def _expert_kernel(ge_ref, nu_ref, x_ref, wgu_ref, wdn_ref, y_ref, wgu_bf, wdn_bf):
    j = pl.program_id(0)
    used = j < nu_ref[0]
    prev = ge_ref[jnp.maximum(j - 1, 0)]
    fresh = (j == 0) | (ge_ref[j] != prev)

    @pl.when(used & fresh)
    def _():
        wgu_bf[...] = wgu_ref[...].astype(BF16)
        wdn_bf[...] = wdn_ref[...].astype(BF16)

    @pl.when(used)
    def _():
        half = x_ref.shape[1]
        f = wdn_bf.shape[0]
        hi, lo = _unpack_bf16_pairs(x_ref[...])
        gu = (jnp.dot(hi.astype(BF16), wgu_bf[0:half, :], preferred_element_type=F32)
              + jnp.dot(lo.astype(BF16), wgu_bf[half:, :], preferred_element_type=F32))
        act = (_silu(gu[:, :f]) * gu[:, f:]).astype(BF16)
        y_ref[...] = _pack_bf16_pairs(jnp.dot(act, wdn_bf[...], preferred_element_type=F32))

    @pl.when(jnp.logical_not(used))
    def _():
        y_ref[...] = jnp.zeros_like(y_ref)


def _expert_gemm(group_e, n_used, xg, w_gu, w_down, layer):
    n_slots, half = xg.shape
    _, ne, d, f2 = w_gu.shape
    f = w_down.shape[2]
    nb = n_slots // EXPERT_ROWS
    return pl.pallas_call(
        _expert_kernel,
        grid_spec=pltpu.PrefetchScalarGridSpec(
            num_scalar_prefetch=2,
            grid=(nb,),
            in_specs=[
                pl.BlockSpec((EXPERT_ROWS, half),
                             lambda j, ge, nu: (jnp.minimum(j, nu[0] - 1), 0)),
                pl.BlockSpec((None, None, d, f2), lambda j, ge, nu: (layer, ge[j], 0, 0)),
                pl.BlockSpec((None, None, f, d), lambda j, ge, nu: (layer, ge[j], 0, 0)),
            ],
            out_specs=pl.BlockSpec((EXPERT_ROWS, half), lambda j, ge, nu: (j, 0)),
            scratch_shapes=[pltpu.VMEM((d, f2), BF16), pltpu.VMEM((f, d), BF16)],
        ),
        out_shape=jax.ShapeDtypeStruct((n_slots, half), U32),
        compiler_params=_cparams(("arbitrary",)),
    )(group_e, n_used, xg, w_gu, w_down)


def _combine_kernel(dest_hbm, x_ref, sh_ref, sc_ref, g_ref, wt_ref, wsgu_ref, wsdn_ref,
                    lng_ref, lnb_ref, yg_hbm, o_ref, dest_smem, buf_ref, sem_idx, sem_rows):
    tr, d = x_ref.shape
    i = pl.program_id(0)
    idx_copy = pltpu.make_async_copy(dest_hbm.at[i], dest_smem, sem_idx)
    idx_copy.start()
    idx_copy.wait()

    def row_copy(tok, k, slot):
        return pltpu.make_async_copy(yg_hbm.at[pl.ds(slot, 1), :],
                                     buf_ref.at[k, pl.ds(tok, 1), :], sem_rows)

    def issue(tok, carry):
        for k in range(TOP_K):
            row_copy(tok, k, dest_smem[0, k * tr + tok]).start()
        return carry

    lax.fori_loop(0, tr, issue, 0)

    x = x_ref[...]
    hf = (x * (1.0 + sc_ref[...]) + sh_ref[...]).astype(BF16)
    f = wsdn_ref.shape[0]
    su = jnp.dot(hf, wsgu_ref[...], preferred_element_type=F32)
    act = (_silu(su[:, :f]) * su[:, f:]).astype(BF16)
    shared = jnp.dot(act, wsdn_ref[...], preferred_element_type=F32)

    def drain(tok, carry):
        for k in range(TOP_K):
            row_copy(tok, k, 0).wait()
        return carry

    lax.fori_loop(0, tr, drain, 0)

    half = d // 2
    acc_hi = jnp.zeros((tr, half), F32)
    acc_lo = jnp.zeros((tr, half), F32)
    for k in range(TOP_K):
        hi, lo = _unpack_bf16_pairs(buf_ref[k])
        wk = wt_ref[:, k:k + 1]
        acc_hi = acc_hi + wk * hi
        acc_lo = acc_lo + wk * lo
    routed = jnp.concatenate([acc_hi, acc_lo], axis=1)
    v = ALPHA * x + (1.0 + g_ref[...]) * (routed + shared)
    o_ref[...] = _layer_norm(v, lng_ref[...], lnb_ref[...])


def _combine(dest_tiles, x2, mod3, w_tok, w_sh_gu, w_sh_down, ln_g, ln_b, yg, seq):
    t, d = x2.shape
    tr = TOK_TILE
    tpb = seq // tr
    full = lambda shape: pl.BlockSpec(shape, lambda i: (0,) * len(shape))
    return pl.pallas_call(
        _combine_kernel,
        grid=(t // tr,),
        in_specs=[pl.BlockSpec(memory_space=pl.ANY),
                  pl.BlockSpec((tr, d), lambda i: (i, 0)),
                  _mod_spec(3, d, tpb), _mod_spec(4, d, tpb), _mod_spec(5, d, tpb),
                  pl.BlockSpec((tr, TOP_K), lambda i: (i, 0)),
                  full(w_sh_gu.shape), full(w_sh_down.shape), full((1, d)), full((1, d)),
                  pl.BlockSpec(memory_space=pl.ANY)],
        out_specs=pl.BlockSpec((tr, d), lambda i: (i, 0)),
        out_shape=jax.ShapeDtypeStruct((t, d), F32),
        scratch_shapes=[pltpu.SMEM((1, TOP_K * tr), I32),
                        pltpu.VMEM((TOP_K, tr, d // 2), U32),
                        pltpu.SemaphoreType.DMA, pltpu.SemaphoreType.DMA],
        compiler_params=_cparams(("arbitrary",)),
    )(dest_tiles, x2, mod3, mod3, mod3, w_tok, w_sh_gu, w_sh_down,
      ln_g.reshape(1, d), ln_b.reshape(1, d), yg)


def _moe_layer(x2, mod3, w_router, router_b, w_gu, w_down, layer, w_sh_gu, w_sh_down,
               ln_g, ln_b, seq):
    t, d = x2.shape
    hp, logits = _router(x2, mod3, w_router, seq)
    idx, w_top, rank, counts = _topk_route(logits.T, router_b)
    c = EXPERT_ROWS
    counts = counts[:, 0]
    padded = ((counts + c - 1) // c) * c
    padded_end = jnp.cumsum(padded)
    padded_start = (padded_end - padded).astype(I32)
    n_slots = ((t * TOP_K + N_EXPERTS * (c - 1) + c - 1) // c) * c
    nb = n_slots // c
    block_row = jnp.arange(nb, dtype=I32) * c
    group_e = jnp.minimum(jnp.sum((padded_end[None, :] <= block_row[:, None]).astype(I32), axis=1),
                          N_EXPERTS - 1)
    n_used = (padded_end[-1:] // c).astype(I32)
    dest_tiles = _dest_slots(padded_start, idx, rank)
    xg = _dispatch(dest_tiles, hp, n_slots)
    yg = _expert_gemm(group_e, n_used, xg, w_gu, w_down, layer)
    return _combine(dest_tiles, x2, mod3, w_top.T, w_sh_gu, w_sh_down, ln_g, ln_b, yg, seq)


def kernel(x, c, ada_w, ada_b, pool_w_in, pool_w_grp, pool_scale, pool_w_out, attn_w_in, attn_w_out, ln1_g, ln1_b, router_w, router_b, exp_w_gu, exp_w_down, sh_w_gu, sh_w_down, ln2_g, ln2_b):
    batch, seq, d = x.shape
    depth = ada_w.shape[0]
    t = batch * seq
    mod = _modulation(c, ada_w, ada_b).reshape(depth, batch, 1, 6 * d)
    perms = jnp.stack([_perm_matrix(dil) for _, dil in ATTN_PATTERNS[1:]])
    perms_t = jnp.swapaxes(perms, 1, 2)
    expand = (jnp.arange(LSE_LANES)[:, None] == (jnp.arange(d)[None, :] // HEAD_DIM)).astype(BF16)
    x2 = x.reshape(t, d)
    for i in range(depth):
        mod3 = mod[i]
        j = i // 2
        if i % 2 == 0:
            x2 = _pool_layer(x2, mod3, pool_w_in[j].astype(BF16), pool_w_grp[j].astype(BF16),
                             pool_scale[j], pool_w_out[j].astype(BF16), ln1_g[i], ln1_b[i], seq)
        else:
            qkv = _qkv_proj(x2, mod3, perms, attn_w_in[j].astype(BF16), seq)
            res = [_attention_group(qkv, g, batch, seq) for g in range(len(ATTN_PATTERNS))]
            x2 = _attn_out(x2, mod3, [r[0] for r in res], [r[1] for r in res], perms_t, expand,
                           attn_w_out[j].astype(BF16), ln1_g[i], ln1_b[i], seq)
        x2 = _moe_layer(x2, mod3, router_w[i], router_b[i], exp_w_gu, exp_w_down, i,
                        sh_w_gu[i].astype(BF16), sh_w_down[i].astype(BF16),
                        ln2_g[i], ln2_b[i], seq)
    return x2.reshape(batch, seq, d)
```

```python
import functools
import math

import jax
import jax.numpy as jnp
from jax import lax
from jax.experimental import pallas as pl
from jax.experimental.pallas import tpu as pltpu

F32 = jnp.float32
BF16 = jnp.bfloat16
U32 = jnp.uint32
I32 = jnp.int32

POOL_WINDOWS = (2, 4, 8, 16)
ATTN_PATTERNS = ((128, 1), (512, 4), (2048, 16))
HEAD_DIM = 64
N_HEADS = 16
Q_BLOCK = 128
N_EXPERTS = 64
TOP_K = 8
N_EXPERT_GROUPS = 8
TOPK_GROUPS = 4
ROUTED_SCALE = 2.5
EXPERT_ROWS = 256
DEPTH = 4
ALPHA = (2 * DEPTH) ** 0.25
LN_EPS = 1e-5

PERM_TILE = 256
POOL_HALO = 16
TOK_TILE = 256
CHUNK = 8
LOCAL_ROWS = -(-(TOK_TILE * TOP_K + N_EXPERTS * (CHUNK - 1)) // 256) * 256
LSE_LANES = 128
VMEM_LIMIT = 56 * 1024 * 1024
NEG_BIG = -1e30


def _cparams(sem):
    return pltpu.CompilerParams(dimension_semantics=sem, vmem_limit_bytes=VMEM_LIMIT)


def _layer_norm(v, g, b):
    mu = jnp.mean(v, axis=-1, keepdims=True)
    c = v - mu
    var = jnp.mean(c * c, axis=-1, keepdims=True)
    return c * lax.rsqrt(var + LN_EPS) * g + b


def _silu(v):
    return v * (1.0 / (1.0 + jnp.exp(-v)))


def _pack_bf16_pairs(v):
    n = v.shape[1] // 2
    hi = lax.bitcast_convert_type(v[:, :n].astype(BF16).astype(F32), U32)
    lo = lax.bitcast_convert_type(v[:, n:].astype(BF16).astype(F32), U32)
    return hi | (lo >> 16)


def _unpack_bf16_pairs(p):
    hi = lax.bitcast_convert_type(p & jnp.uint32(0xFFFF0000), F32)
    lo = lax.bitcast_convert_type(p << 16, F32)
    return hi, lo


def _mod_kernel(c_ref, w_ref, b_ref, o_ref):
    cs = _silu(c_ref[...])
    o_ref[...] = jnp.dot(cs, w_ref[...], preferred_element_type=F32) + b_ref[...]


def _modulation(c, ada_w, ada_b):
    depth, d, n6 = ada_w.shape
    b = c.shape[0]
    tn = 1536
    return pl.pallas_call(
        _mod_kernel,
        grid=(depth, n6 // tn),
        in_specs=[
            pl.BlockSpec((b, d), lambda i, n: (0, 0)),
            pl.BlockSpec((None, d, tn), lambda i, n: (i, 0, n)),
            pl.BlockSpec((None, 1, tn), lambda i, n: (i, 0, n)),
        ],
        out_specs=pl.BlockSpec((None, b, tn), lambda i, n: (i, 0, n)),
        out_shape=jax.ShapeDtypeStruct((depth, b, n6), F32),
        compiler_params=_cparams(("arbitrary", "arbitrary")),
    )(c, ada_w, ada_b.reshape(depth, 1, n6))


def _mod_spec(chunk, d, tiles_per_batch):
    return pl.BlockSpec((None, 1, d), lambda *idx: (idx[0] // tiles_per_batch, 0, chunk))


def _pool_kernel(x_ref, sh_ref, sc_ref, g_ref, win_ref, wgrp_ref, cs_ref, wout_ref,
                 lng_ref, lnb_ref, o_ref, ext_ref, *, tiles_per_batch):
    tm, d = x_ref.shape
    s_idx = pl.program_id(0) % tiles_per_batch
    x = x_ref[...]
    h = (x * (1.0 + sc_ref[...]) + sh_ref[...]).astype(BF16)
    u = jnp.dot(h, win_ref[...], preferred_element_type=F32)

    @pl.when(s_idx == 0)
    def _():
        ext_ref[0:POOL_HALO, :] = jnp.zeros((POOL_HALO, d), F32)

    @pl.when(s_idx != 0)
    def _():
        ext_ref[0:POOL_HALO, :] = ext_ref[tm:tm + POOL_HALO, :]

    ext_ref[POOL_HALO:POOL_HALO + tm, :] = u

    pos = s_idx * tm + lax.broadcasted_iota(I32, (tm, 1), 0) + 1
    gc = d // len(POOL_WINDOWS)
    ys = []
    for g, w in enumerate(POOL_WINDOWS):
        cols = slice(g * gc, (g + 1) * gc)
        acc = u[:, cols]
        for j in range(1, w):
            acc = acc + ext_ref[POOL_HALO - j:POOL_HALO - j + tm, cols]
        cnt = jnp.minimum(pos, w).astype(F32)
        z = (acc / cnt - u[:, cols]).astype(BF16)
        ys.append(jnp.dot(z, wgrp_ref[g], preferred_element_type=F32))
    y = (jnp.concatenate(ys, axis=1) * cs_ref[...]).astype(BF16)
    out = jnp.dot(y, wout_ref[...], preferred_element_type=F32)
    v = ALPHA * x + (1.0 + g_ref[...]) * out
    o_ref[...] = _layer_norm(v, lng_ref[...], lnb_ref[...])


def _pool_layer(x2, mod3, w_in, w_grp, ch_scale, w_out, ln_g, ln_b, seq):
    t, d = x2.shape
    tm = 512
    tpb = seq // tm
    full = lambda shape: pl.BlockSpec(shape, lambda i: (0,) * len(shape))
    return pl.pallas_call(
        functools.partial(_pool_kernel, tiles_per_batch=tpb),
        grid=(t // tm,),
        in_specs=[
            pl.BlockSpec((tm, d), lambda i: (i, 0)),
            _mod_spec(0, d, tpb), _mod_spec(1, d, tpb), _mod_spec(2, d, tpb),
            full((d, d)), full(w_grp.shape), full((1, d)), full((d, d)),
            full((1, d)), full((1, d)),
        ],
        out_specs=pl.BlockSpec((tm, d), lambda i: (i, 0)),
        out_shape=jax.ShapeDtypeStruct((t, d), F32),
        scratch_shapes=[pltpu.VMEM((tm + POOL_HALO, d), F32)],
        compiler_params=_cparams(("arbitrary",)),
    )(x2, mod3, mod3, mod3, w_in, w_grp, ch_scale.reshape(1, d), w_out,
      ln_g.reshape(1, d), ln_b.reshape(1, d))


def _perm_matrix(dil):
    p = jnp.arange(PERM_TILE)
    chunk = PERM_TILE // dil
    src = (p % chunk) * dil + p // chunk
    return (src[:, None] == jnp.arange(PERM_TILE)[None, :]).astype(BF16)


def _qkv_kernel(x_ref, sh_ref, sc_ref, p_ref, w_ref, o_ref, h_ref):
    tm = x_ref.shape[0]
    g = pl.program_id(1)
    part = pl.program_id(2)

    @pl.when((g == 0) & (part == 0))
    def _():
        h = (x_ref[...] * (1.0 + sc_ref[...]) + sh_ref[...]).astype(BF16)
        h_ref[0] = h
        for gi in range(1, len(ATTN_PATTERNS)):
            for s in range(tm // PERM_TILE):
                rows = slice(s * PERM_TILE, (s + 1) * PERM_TILE)
                h_ref[gi, rows, :] = jnp.dot(
                    p_ref[gi - 1], h[rows, :], preferred_element_type=F32).astype(BF16)

    o_ref[...] = jnp.dot(h_ref[g], w_ref[...], preferred_element_type=F32).astype(BF16)


def _qkv_proj(x2, mod3, perms, w_in, seq):
    t, d = x2.shape
    ng = len(ATTN_PATTERNS)
    tm = 1024
    tpb = seq // tm
    return pl.pallas_call(
        _qkv_kernel,
        grid=(t // tm, ng, 3),
        in_specs=[
            pl.BlockSpec((tm, d), lambda m, g, p: (m, 0)),
            _mod_spec(0, d, tpb), _mod_spec(1, d, tpb),
            pl.BlockSpec(perms.shape, lambda m, g, p: (0, 0, 0)),
            pl.BlockSpec((d, d), lambda m, g, p: (0, g * 3 + p)),
        ],
        out_specs=pl.BlockSpec((tm, d), lambda m, g, p: (m, g * 3 + p)),
        out_shape=jax.ShapeDtypeStruct((t, ng * 3 * d), BF16),
        scratch_shapes=[pltpu.VMEM((ng, tm, d), BF16)],
        compiler_params=_cparams(("arbitrary", "arbitrary", "arbitrary")),
    )(x2, mod3, mod3, perms, w_in)


def _attn_kernel(q_ref, kp_ref, kc_ref, vp_ref, vc_ref, o_ref, lse_ref, *, group, dil):
    bq = Q_BLOCK
    d = N_HEADS * HEAD_DIM
    j = pl.program_id(2)
    q = q_ref[...].reshape(bq, d)
    kp = kp_ref[...].reshape(bq, d)
    kc = kc_ref[...].reshape(bq, d)
    vp = vp_ref[...].reshape(bq, d)
    vc = vc_ref[...].reshape(bq, d)

    qi = lax.broadcasted_iota(I32, (bq, 2 * bq), 0)
    kj = lax.broadcasted_iota(I32, (bq, 2 * bq), 1)
    dist = qi + bq - kj
    steps = ATTN_PATTERNS[group][0] // dil
    valid = (dist >= 0) & (dist <= steps) & ((kj >= bq) | (j > 0))
    distf = dist.astype(F32)
    lane = lax.broadcasted_iota(I32, (bq, LSE_LANES), 1)
    lse_tile = jnp.zeros((bq, LSE_LANES), F32)
    n_tot = len(ATTN_PATTERNS) * N_HEADS
    outs = []
    for h in range(N_HEADS):
        cols = slice(h * HEAD_DIM, (h + 1) * HEAD_DIM)
        slope = 2.0 ** (-8.0 * (group * N_HEADS + h + 1) / n_tot)
        kh = jnp.concatenate([kp[:, cols], kc[:, cols]], axis=0)
        vh = jnp.concatenate([vp[:, cols], vc[:, cols]], axis=0)
        s = lax.dot_general(q[:, cols], kh, (((1,), (1,)), ((), ())),
                            preferred_element_type=F32)
        s = s * (HEAD_DIM ** -0.5) - (slope * dil) * distf
        s = jnp.where(valid, s, NEG_BIG)
        m = jnp.max(s, axis=-1, keepdims=True)
        p = jnp.exp(s - m)
        den = jnp.sum(p, axis=-1, keepdims=True)
        o = jnp.dot(p.astype(BF16), vh, preferred_element_type=F32) / den
        outs.append(o)
        lse_tile = jnp.where(lane == h, m + jnp.log(den), lse_tile)
    o_ref[...] = jnp.concatenate(outs, axis=1).astype(BF16).reshape(o_ref.shape)
    lse_ref[...] = lse_tile.reshape(lse_ref.shape)


def _attention_group(qkv, group, batch, seq):
    dil = ATTN_PATTERNS[group][1]
    d = N_HEADS * HEAD_DIM
    t = qkv.shape[0]
    sub = seq // dil
    nb = sub // Q_BLOCK
    rows = Q_BLOCK if dil == 1 else PERM_TILE // dil
    chunks = Q_BLOCK // rows
    u = seq // (rows * dil)
    view = lambda a, c: a.reshape(batch, u, dil, rows, c)
    blk = lambda c: (None, chunks, None, rows, c)
    col0 = group * 3
    q_spec = pl.BlockSpec(blk(d), lambda b, r, j: (b, j, r, 0, col0))
    kc_spec = pl.BlockSpec(blk(d), lambda b, r, j: (b, j, r, 0, col0 + 1))
    kp_spec = pl.BlockSpec(blk(d), lambda b, r, j: (b, jnp.maximum(j - 1, 0), r, 0, col0 + 1))
    vc_spec = pl.BlockSpec(blk(d), lambda b, r, j: (b, j, r, 0, col0 + 2))
    vp_spec = pl.BlockSpec(blk(d), lambda b, r, j: (b, jnp.maximum(j - 1, 0), r, 0, col0 + 2))
    qkv5 = view(qkv, qkv.shape[1])
    o, lse = pl.pallas_call(
        functools.partial(_attn_kernel, group=group, dil=dil),
        grid=(batch, dil, nb),
        in_specs=[q_spec, kp_spec, kc_spec, vp_spec, vc_spec],
        out_specs=[
            pl.BlockSpec(blk(d), lambda b, r, j: (b, j, r, 0, 0)),
            pl.BlockSpec(blk(LSE_LANES), lambda b, r, j: (b, j, r, 0, 0)),
        ],
        out_shape=[
            jax.ShapeDtypeStruct((batch, u, dil, rows, d), BF16),
            jax.ShapeDtypeStruct((batch, u, dil, rows, LSE_LANES), F32),
        ],
        compiler_params=_cparams(("arbitrary", "arbitrary", "arbitrary")),
    )(qkv5, qkv5, qkv5, qkv5, qkv5)
    return o.reshape(t, d), lse.reshape(t, LSE_LANES)


def _split3(v):
    a = v.astype(BF16)
    r = v - a.astype(F32)
    b = r.astype(BF16)
    c = (r - b.astype(F32)).astype(BF16)
    return a, b, c


def _attn_out_kernel(x_ref, g_ref, o0_ref, o1_ref, o2_ref, l0_ref, l1_ref, l2_ref,
                     pt_ref, e_ref, wout_ref, lng_ref, lnb_ref, out_ref):
    tm, d = x_ref.shape
    o_refs = (o0_ref, o1_ref, o2_ref)
    l_refs = (l0_ref, l1_ref, l2_ref)
    n_sub = tm // PERM_TILE

    def unperm(gi, val_bf16):
        if gi == 0:
            return val_bf16.astype(F32)
        parts = [jnp.dot(pt_ref[gi - 1], val_bf16[s * PERM_TILE:(s + 1) * PERM_TILE, :],
                         preferred_element_type=F32) for s in range(n_sub)]
        return jnp.concatenate(parts, axis=0)

    lses = []
    for gi in range(3):
        l = l_refs[gi][...]
        if gi == 0:
            lses.append(l)
        else:
            a, b, c = _split3(l)
            lses.append(unperm(gi, a) + unperm(gi, b) + unperm(gi, c))
    mx = jnp.maximum(jnp.maximum(lses[0], lses[1]), lses[2])
    es = [jnp.exp(l - mx) for l in lses]
    tot = es[0] + es[1] + es[2]
    mixed = jnp.zeros((tm, d), F32)
    for gi in range(3):
        w = es[gi] / tot
        a, b, c = _split3(w)
        wide = (jnp.dot(a, e_ref[...], preferred_element_type=F32)
                + jnp.dot(b, e_ref[...], preferred_element_type=F32)
                + jnp.dot(c, e_ref[...], preferred_element_type=F32))
        mixed = mixed + wide * unperm(gi, o_refs[gi][...])
    y = jnp.dot(mixed.astype(BF16), wout_ref[...], preferred_element_type=F32)
    v = ALPHA * x_ref[...] + (1.0 + g_ref[...]) * y
    out_ref[...] = _layer_norm(v, lng_ref[...], lnb_ref[...])


def _attn_out(x2, mod3, os_, lses, perms_t, expand, w_out, ln_g, ln_b, seq):
    t, d = x2.shape
    tm = 512
    tpb = seq // tm
    row = lambda c: pl.BlockSpec((tm, c), lambda i: (i, 0))
    full = lambda shape: pl.BlockSpec(shape, lambda i: (0,) * len(shape))
    return pl.pallas_call(
        _attn_out_kernel,
        grid=(t // tm,),
        in_specs=[row(d), _mod_spec(2, d, tpb), row(d), row(d), row(d),
                  row(LSE_LANES), row(LSE_LANES), row(LSE_LANES),
                  full(perms_t.shape), full(expand.shape), full((d, d)),
                  full((1, d)), full((1, d))],
        out_specs=row(d),
        out_shape=jax.ShapeDtypeStruct((t, d), F32),
        compiler_params=_cparams(("arbitrary",)),
    )(x2, mod3, *os_, *lses, perms_t, expand, w_out, ln_g.reshape(1, d), ln_b.reshape(1, d))


def _router_kernel(x_ref, sh_ref, sc_ref, wr_ref, lg_ref):
    hf = x_ref[...] * (1.0 + sc_ref[...]) + sh_ref[...]
    lg_ref[...] = jnp.dot(hf, wr_ref[...], preferred_element_type=F32,
                          precision=lax.Precision.HIGHEST)


def _router(x2, mod3, w_router, seq):
    t, d = x2.shape
    tm = 512
    tpb = seq // tm
    ne = w_router.shape[1]
    return pl.pallas_call(
        _router_kernel,
        grid=(t // tm,),
        in_specs=[pl.BlockSpec((tm, d), lambda i: (i, 0)),
                  _mod_spec(3, d, tpb), _mod_spec(4, d, tpb),
                  pl.BlockSpec((d, ne), lambda i: (0, 0))],
        out_specs=pl.BlockSpec((tm, ne), lambda i: (i, 0)),
        out_shape=jax.ShapeDtypeStruct((t, ne), F32),
        compiler_params=_cparams(("arbitrary",)),
    )(x2, mod3, mod3, w_router)


def _first_index_of_max(v, iota, size):
    m = jnp.max(v, axis=0, keepdims=True)
    idx = jnp.min(jnp.where(v == m, iota, float(size)), axis=0, keepdims=True)
    return m, idx


def _topk_kernel(lg_ref, b_ref, tri_ref, ltri_ref, w_ref, pos_ref, cnt_ref):
    ne, tr = lg_ref.shape
    gsz = ne // N_EXPERT_GROUPS
    tile = pl.program_id(0)

    scores = 1.0 / (1.0 + jnp.exp(-lg_ref[...]))
    sel = scores + b_ref[...]
    iota_g = lax.broadcasted_iota(I32, (gsz, tr), 0).astype(F32)
    iota_n = lax.broadcasted_iota(I32, (N_EXPERT_GROUPS, tr), 0).astype(F32)
    gs = jnp.zeros((N_EXPERT_GROUPS, tr), F32)
    for g in range(N_EXPERT_GROUPS):
        blk = sel[g * gsz:(g + 1) * gsz, :]
        m1, i1 = _first_index_of_max(blk, iota_g, gsz)
        m2 = jnp.max(jnp.where(iota_g == i1, -jnp.inf, blk), axis=0, keepdims=True)
        gs = jnp.where(iota_n == float(g), m1 + m2, gs)
    gmask = jnp.zeros((N_EXPERT_GROUPS, tr), F32)
    for _ in range(TOPK_GROUPS):
        _, gi = _first_index_of_max(gs, iota_n, N_EXPERT_GROUPS)
        hit = iota_n == gi
        gmask = jnp.where(hit, 1.0, gmask)
        gs = jnp.where(hit, -jnp.inf, gs)
    masked_rows = []
    for g in range(N_EXPERT_GROUPS):
        keep = jnp.broadcast_to(gmask[g:g + 1, :], (gsz, tr)) > 0.5
        masked_rows.append(jnp.where(keep, sel[g * gsz:(g + 1) * gsz, :], -jnp.inf))
    cur = jnp.concatenate(masked_rows, axis=0)
    iota_e = lax.broadcasted_iota(I32, (ne, tr), 0).astype(F32)
    chosen = jnp.zeros((ne, tr), F32)
    idxs, tops = [], []
    for _ in range(TOP_K):
        _, ei = _first_index_of_max(cur, iota_e, ne)
        hit = iota_e == ei
        cur = jnp.where(hit, -jnp.inf, cur)
        chosen = jnp.where(hit, 1.0, chosen)
        idxs.append(ei)
        tops.append(jnp.sum(jnp.where(hit, scores, 0.0), axis=0, keepdims=True))
    wsum = tops[0]
    for k in range(1, TOP_K):
        wsum = wsum + tops[k]
    before = jnp.dot(chosen.astype(BF16), tri_ref[...], preferred_element_type=F32)
    n = jnp.sum(chosen, axis=1, keepdims=True)
    n_chunks = jnp.floor((n + (CHUNK - 1)) * (1.0 / CHUNK))
    run_off = jnp.dot(ltri_ref[...], jnp.broadcast_to(n_chunks, (ne, 128)).astype(BF16),
                      preferred_element_type=F32)[:, 0:1] * CHUNK
    pos_all = before + run_off
    for k in range(TOP_K):
        hit = iota_e == idxs[k]
        w_ref[k:k + 1, :] = tops[k] / wsum * ROUTED_SCALE
        pos_ref[k:k + 1, :] = jnp.sum(jnp.where(hit, pos_all, 0.0), axis=0,
                                      keepdims=True).astype(I32)

    @pl.when(tile == 0)
    def _():
        cnt_ref[...] = jnp.zeros_like(cnt_ref)

    lane = lax.broadcasted_iota(I32, cnt_ref.shape, 1)
    cnt_ref[...] = jnp.where(lane == tile, jnp.broadcast_to(n, cnt_ref.shape).astype(I32),
                             cnt_ref[...])


def _topk_route(logits_t, router_b):
    ne, t = logits_t.shape
    tr = TOK_TILE
    assert t // tr <= 128
    tri = (jnp.arange(tr)[:, None] < jnp.arange(tr)[None, :]).astype(BF16)
    ltri = (jnp.arange(ne)[None, :] < jnp.arange(ne)[:, None]).astype(BF16)
    out = lambda dt: jax.ShapeDtypeStruct((TOP_K, t), dt)
    row = pl.BlockSpec((TOP_K, tr), lambda i: (0, i))
    return pl.pallas_call(
        _topk_kernel,
        grid=(t // tr,),
        in_specs=[pl.BlockSpec((ne, tr), lambda i: (0, i)),
                  pl.BlockSpec((ne, 1), lambda i: (0, 0)),
                  pl.BlockSpec((tr, tr), lambda i: (0, 0)),
                  pl.BlockSpec((ne, ne), lambda i: (0, 0))],
        out_specs=[row, row, pl.BlockSpec((ne, 128), lambda i: (0, 0))],
        out_shape=[out(F32), out(I32), jax.ShapeDtypeStruct((ne, 128), I32)],
        compiler_params=_cparams(("arbitrary",)),
    )(logits_t, router_b.reshape(ne, 1), tri, ltri)


def _slot_plan(counts, n_tiles, n_blocks):
    n = counts[:, :n_tiles]
    nch = (n + (CHUNK - 1)) // CHUNK
    rows = jnp.sum(nch, axis=1) * CHUNK
    region = ((rows + EXPERT_ROWS - 1) // EXPERT_ROWS) * EXPERT_ROWS
    region_end = jnp.cumsum(region)
    region_start = region_end - region
    run_chunk = region_start[:, None] // CHUNK + jnp.cumsum(nch, axis=1) - nch
    local_chunk = jnp.cumsum(nch, axis=0) - nch
    per_tile = lambda a: a.T.reshape(-1).astype(I32)
    block_row = jnp.arange(n_blocks, dtype=I32) * EXPERT_ROWS
    group_e = jnp.minimum(jnp.sum((region_end[None, :] <= block_row[:, None]).astype(I32), axis=1),
                          N_EXPERTS - 1)
    return dict(
        local_chunk=per_tile(local_chunk), n_chunks=per_tile(nch), run_chunk=per_tile(run_chunk),
        tile_chunks=jnp.sum(nch, axis=0).astype(I32),
        tail_chunk=((region_start + rows) // CHUNK).astype(I32),
        tail_chunks=((region - rows) // CHUNK).astype(I32),
        group_e=group_e, n_used=(region_end[-1:] // EXPERT_ROWS).astype(I32))


def _chunk_rows(ref, chunk_index):
    start = chunk_index * CHUNK
    if not isinstance(start, int):
        start = pl.multiple_of(start, CHUNK)
    return ref.at[pl.ds(start, CHUNK), :]


def _dispatch_kernel(local_ref, nch_ref, run_ref, tile_ref, tail_ref, tailn_ref, used_ref,
                     x_ref, sh_ref, sc_ref, pos_ref, xg_hbm, stage_ref, zero_ref, sem, sem_blk):
    tr = x_ref.shape[0]
    i = pl.program_id(0)
    n_blocks = xg_hbm.shape[0] // EXPERT_ROWS

    def drain(count):
        def body(c, carry):
            pltpu.make_async_copy(_chunk_rows(stage_ref, 0), _chunk_rows(xg_hbm, 0), sem).wait()
            return carry
        lax.fori_loop(0, count, body, 0)

    @pl.when(i == 0)
    def _():
        zero_ref[...] = jnp.zeros_like(zero_ref)

        def block_copy(j):
            rows = pl.ds(pl.multiple_of(j * EXPERT_ROWS, EXPERT_ROWS), EXPERT_ROWS)
            return pltpu.make_async_copy(zero_ref, xg_hbm.at[rows, :], sem_blk)

        def start_block(j, carry):
            block_copy(j).start()
            return carry

        def wait_block(j, carry):
            block_copy(j).wait()
            return carry

        lax.fori_loop(used_ref[0], n_blocks, start_block, 0)

        def per_expert(e, total):
            def per_chunk(c, carry):
                pltpu.make_async_copy(_chunk_rows(zero_ref, 0),
                                      _chunk_rows(xg_hbm, tail_ref[e] + c), sem).start()
                return carry
            lax.fori_loop(0, tailn_ref[e], per_chunk, 0)
            return total + tailn_ref[e]

        drain(lax.fori_loop(0, N_EXPERTS, per_expert, 0))
        lax.fori_loop(used_ref[0], n_blocks, wait_block, 0)

    hf = (x_ref[...] * (1.0 + sc_ref[...]) + sh_ref[...]).astype(BF16)
    slot = lax.broadcasted_iota(I32, (LOCAL_ROWS, tr), 0)
    perm = jnp.zeros((LOCAL_ROWS, tr), F32)
    for k in range(TOP_K):
        perm = jnp.where(slot == pos_ref[k:k + 1, :], 1.0, perm)
    rows = jnp.dot(perm.astype(BF16), hf, preferred_element_type=F32)
    half = rows.shape[1] // 2
    stage_ref[...] = (lax.bitcast_convert_type(rows[:, :half], U32)
                      | (lax.bitcast_convert_type(rows[:, half:], U32) >> 16))

    def per_expert(e, carry):
        base = i * N_EXPERTS + e

        def per_chunk(c, inner):
            pltpu.make_async_copy(_chunk_rows(stage_ref, local_ref[base] + c),
                                  _chunk_rows(xg_hbm, run_ref[base] + c), sem).start()
            return inner
        lax.fori_loop(0, nch_ref[base], per_chunk, 0)
        return carry

    lax.fori_loop(0, N_EXPERTS, per_expert, 0)
    drain(tile_ref[i])


def _dispatch(plan, x2, mod3, pos, n_slots, seq):
    t, d = x2.shape
    tr = TOK_TILE
    tpb = seq // tr
    return pl.pallas_call(
        _dispatch_kernel,
        grid_spec=pltpu.PrefetchScalarGridSpec(
            num_scalar_prefetch=7,
            grid=(t // tr,),
            in_specs=[pl.BlockSpec((tr, d), lambda i, *_: (i, 0)),
                      _mod_spec(3, d, tpb), _mod_spec(4, d, tpb),
                      pl.BlockSpec((TOP_K, tr), lambda i, *_: (0, i))],
            out_specs=pl.BlockSpec(memory_space=pl.ANY),
            scratch_shapes=[pltpu.VMEM((LOCAL_ROWS, d // 2), U32),
                            pltpu.VMEM((EXPERT_ROWS, d // 2), U32),
                            pltpu.SemaphoreType.DMA, pltpu.SemaphoreType.DMA],
        ),
        out_shape=jax.ShapeDtypeStruct((n_slots, d // 2), U32),
        compiler_params=_cparams(("arbitrary",)),
    )(plan["local_chunk"], plan["n_chunks"], plan["run_chunk"], plan["tile_chunks"],
      plan["tail_chunk"], plan["tail_chunks"], plan["n_used"], x2, mod3, mod3, pos)


def _expert_kernel(ge_ref, nu_ref, x_ref, wgu_ref, wdn_ref, y_ref, wgu_bf, wdn_bf):
    j = pl.program_id(0)
    used = j < nu_ref[0]
    prev = ge_ref[jnp.maximum(j - 1, 0)]
    fresh = (j == 0) | (ge_ref[j] != prev)

    @pl.when(used & fresh)
    def _():
        wgu_bf[...] = wgu_ref[...].astype(BF16)
        wdn_bf[...] = wdn_ref[...].astype(BF16)

    @pl.when(used)
    def _():
        half = x_ref.shape[1]
        f = wdn_bf.shape[0]
        hi, lo = _unpack_bf16_pairs(x_ref[...])
        gu = (jnp.dot(hi.astype(BF16), wgu_bf[0:half, :], preferred_element_type=F32)
              + jnp.dot(lo.astype(BF16), wgu_bf[half:, :], preferred_element_type=F32))
        act = (_silu(gu[:, :f]) * gu[:, f:]).astype(BF16)
        y_ref[...] = _pack_bf16_pairs(jnp.dot(act, wdn_bf[...], preferred_element_type=F32))

    @pl.when(jnp.logical_not(used))
    def _():
        y_ref[...] = jnp.zeros_like(y_ref)


def _expert_gemm(group_e, n_used, xg, w_gu, w_down, layer):
    n_slots, half = xg.shape
    _, ne, d, f2 = w_gu.shape
    f = w_down.shape[2]
    nb = n_slots // EXPERT_ROWS
    return pl.pallas_call(
        _expert_kernel,
        grid_spec=pltpu.PrefetchScalarGridSpec(
            num_scalar_prefetch=2,
            grid=(nb,),
            in_specs=[
                pl.BlockSpec((EXPERT_ROWS, half),
                             lambda j, ge, nu: (jnp.minimum(j, nu[0] - 1), 0)),
                pl.BlockSpec((None, None, d, f2), lambda j, ge, nu: (layer, ge[j], 0, 0)),
                pl.BlockSpec((None, None, f, d), lambda j, ge, nu: (layer, ge[j], 0, 0)),
            ],
            out_specs=pl.BlockSpec((EXPERT_ROWS, half), lambda j, ge, nu: (j, 0)),
            scratch_shapes=[pltpu.VMEM((d, f2), BF16), pltpu.VMEM((f, d), BF16)],
        ),
        out_shape=jax.ShapeDtypeStruct((n_slots, half), U32),
        compiler_params=_cparams(("arbitrary",)),
    )(group_e, n_used, xg, w_gu, w_down)


def _combine_kernel(local_ref, nch_ref, run_ref, tile_ref,
                    x_ref, sh_ref, sc_ref, g_ref, pos_ref, wt_ref, wsgu_ref, wsdn_ref,
                    lng_ref, lnb_ref, yg_hbm, o_ref, stage_ref, sem):
    tr, d = x_ref.shape
    i = pl.program_id(0)

    @pl.when(i == 0)
    def _():
        stage_ref[...] = jnp.zeros_like(stage_ref)

    def per_expert(e, carry):
        base = i * N_EXPERTS + e

        def per_chunk(c, inner):
            pltpu.make_async_copy(_chunk_rows(yg_hbm, run_ref[base] + c),
                                  _chunk_rows(stage_ref, local_ref[base] + c), sem).start()
            return inner
        lax.fori_loop(0, nch_ref[base], per_chunk, 0)
        return carry

    lax.fori_loop(0, N_EXPERTS, per_expert, 0)

    x = x_ref[...]
    hf = (x * (1.0 + sc_ref[...]) + sh_ref[...]).astype(BF16)
    f = wsdn_ref.shape[0]
    su = jnp.dot(hf, wsgu_ref[...], preferred_element_type=F32)
    act = (_silu(su[:, :f]) * su[:, f:]).astype(BF16)
    shared = jnp.dot(act, wsdn_ref[...], preferred_element_type=F32)

    lane = lax.broadcasted_iota(I32, (tr, LOCAL_ROWS), 1)
    weights = jnp.zeros((tr, LOCAL_ROWS), F32)
    for k in range(TOP_K):
        weights = jnp.where(lane == pos_ref[:, k:k + 1], wt_ref[:, k:k + 1], weights)
    w_hi = weights.astype(BF16)
    w_lo = (weights - w_hi.astype(F32)).astype(BF16)

    def drain(c, carry):
        pltpu.make_async_copy(_chunk_rows(yg_hbm, 0), _chunk_rows(stage_ref, 0), sem).wait()
        return carry

    lax.fori_loop(0, tile_ref[i], drain, 0)

    y_hi, y_lo = _unpack_bf16_pairs(stage_ref[...])
    y_hi = y_hi.astype(BF16)
    y_lo = y_lo.astype(BF16)
    gather = lambda y: (jnp.dot(w_hi, y, preferred_element_type=F32)
                        + jnp.dot(w_lo, y, preferred_element_type=F32))
    routed = jnp.concatenate([gather(y_hi), gather(y_lo)], axis=1)
    v = ALPHA * x + (1.0 + g_ref[...]) * (routed + shared)
    o_ref[...] = _layer_norm(v, lng_ref[...], lnb_ref[...])


def _combine(plan, x2, mod3, pos_tok, w_tok, w_sh_gu, w_sh_down, ln_g, ln_b, yg, seq):
    t, d = x2.shape
    tr = TOK_TILE
    tpb = seq // tr
    full = lambda shape: pl.BlockSpec(shape, lambda i, *_: (0,) * len(shape))
    return pl.pallas_call(
        _combine_kernel,
        grid_spec=pltpu.PrefetchScalarGridSpec(
            num_scalar_prefetch=4,
            grid=(t // tr,),
            in_specs=[pl.BlockSpec((tr, d), lambda i, *_: (i, 0)),
                      _mod_spec(3, d, tpb), _mod_spec(4, d, tpb), _mod_spec(5, d, tpb),
                      pl.BlockSpec((tr, TOP_K), lambda i, *_: (i, 0)),
                      pl.BlockSpec((tr, TOP_K), lambda i, *_: (i, 0)),
                      full(w_sh_gu.shape), full(w_sh_down.shape), full((1, d)), full((1, d)),
                      pl.BlockSpec(memory_space=pl.ANY)],
            out_specs=pl.BlockSpec((tr, d), lambda i, *_: (i, 0)),
            scratch_shapes=[pltpu.VMEM((LOCAL_ROWS, d // 2), U32), pltpu.SemaphoreType.DMA],
        ),
        out_shape=jax.ShapeDtypeStruct((t, d), F32),
        compiler_params=_cparams(("arbitrary",)),
    )(plan["local_chunk"], plan["n_chunks"], plan["run_chunk"], plan["tile_chunks"],
      x2, mod3, mod3, mod3, pos_tok, w_tok, w_sh_gu, w_sh_down,
      ln_g.reshape(1, d), ln_b.reshape(1, d), yg)


def _moe_layer(x2, mod3, w_router, router_b, w_gu, w_down, layer, w_sh_gu, w_sh_down,
               ln_g, ln_b, seq):
    t, d = x2.shape
    n_tiles = t // TOK_TILE
    logits = _router(x2, mod3, w_router, seq)
    w_top, pos, counts = _topk_route(logits.T, router_b)
    bound = t * TOP_K + n_tiles * N_EXPERTS * (CHUNK - 1) + N_EXPERTS * (EXPERT_ROWS - 1)
    n_blocks = -(-bound // EXPERT_ROWS)
    plan = _slot_plan(counts, n_tiles, n_blocks)
    xg = _dispatch(plan, x2, mod3, pos, n_blocks * EXPERT_ROWS, seq)
    yg = _expert_gemm(plan["group_e"], plan["n_used"], xg, w_gu, w_down, layer)
    return _combine(plan, x2, mod3, pos.T, w_top.T, w_sh_gu, w_sh_down, ln_g, ln_b, yg, seq)


def kernel(x, c, ada_w, ada_b, pool_w_in, pool_w_grp, pool_scale, pool_w_out, attn_w_in, attn_w_out, ln1_g, ln1_b, router_w, router_b, exp_w_gu, exp_w_down, sh_w_gu, sh_w_down, ln2_g, ln2_b):
    batch, seq, d = x.shape
    depth = ada_w.shape[0]
    t = batch * seq
    mod = _modulation(c, ada_w, ada_b).reshape(depth, batch, 1, 6 * d)
    perms = jnp.stack([_perm_matrix(dil) for _, dil in ATTN_PATTERNS[1:]])
    perms_t = jnp.swapaxes(perms, 1, 2)
    expand = (jnp.arange(LSE_LANES)[:, None] == (jnp.arange(d)[None, :] // HEAD_DIM)).astype(BF16)
    x2 = x.reshape(t, d)
    for i in range(depth):
        mod3 = mod[i]
        j = i // 2
        if i % 2 == 0:
            x2 = _pool_layer(x2, mod3, pool_w_in[j].astype(BF16), pool_w_grp[j].astype(BF16),
                             pool_scale[j], pool_w_out[j].astype(BF16), ln1_g[i], ln1_b[i], seq)
        else:
            qkv = _qkv_proj(x2, mod3, perms, attn_w_in[j].astype(BF16), seq)
            res = [_attention_group(qkv, g, batch, seq) for g in range(len(ATTN_PATTERNS))]
            x2 = _attn_out(x2, mod3, [r[0] for r in res], [r[1] for r in res], perms_t, expand,
                           attn_w_out[j].astype(BF16), ln1_g[i], ln1_b[i], seq)
        x2 = _moe_layer(x2, mod3, router_w[i], router_b[i], exp_w_gu, exp_w_down, i,
                        sh_w_gu[i].astype(BF16), sh_w_down[i].astype(BF16),
                        ln2_g[i], ln2_b[i], seq)
    return x2.reshape(batch, seq, d)
```

```python
import functools
import math

import jax
import jax.numpy as jnp
from jax import lax
from jax.experimental import pallas as pl
from jax.experimental.pallas import tpu as pltpu

F32 = jnp.float32
BF16 = jnp.bfloat16
U32 = jnp.uint32
I32 = jnp.int32

POOL_WINDOWS = (2, 4, 8, 16)
ATTN_PATTERNS = ((128, 1), (512, 4), (2048, 16))
HEAD_DIM = 64
N_HEADS = 16
Q_BLOCK = 128
N_EXPERTS = 64
TOP_K = 8
N_EXPERT_GROUPS = 8
TOPK_GROUPS = 4
ROUTED_SCALE = 2.5
EXPERT_ROWS = 256
DEPTH = 4
ALPHA = (2 * DEPTH) ** 0.25
LN_EPS = 1e-5

PERM_TILE = 256
POOL_HALO = 16
TOK_TILE = 256
CHUNK = 8
LOCAL_ROWS = -(-(TOK_TILE * TOP_K + N_EXPERTS * (CHUNK - 1)) // 256) * 256
LSE_LANES = 128
VMEM_LIMIT = 56 * 1024 * 1024
NEG_BIG = -1e30


def _cparams(sem):
    return pltpu.CompilerParams(dimension_semantics=sem, vmem_limit_bytes=VMEM_LIMIT)


def _layer_norm(v, g, b):
    mu = jnp.mean(v, axis=-1, keepdims=True)
    c = v - mu
    var = jnp.mean(c * c, axis=-1, keepdims=True)
    return c * lax.rsqrt(var + LN_EPS) * g + b


def _silu(v):
    return v * (1.0 / (1.0 + jnp.exp(-v)))


def _pack_bf16_pairs(v):
    n = v.shape[1] // 2
    hi = lax.bitcast_convert_type(v[:, :n].astype(BF16).astype(F32), U32)
    lo = lax.bitcast_convert_type(v[:, n:].astype(BF16).astype(F32), U32)
    return hi | (lo >> 16)


def _unpack_bf16_pairs(p):
    hi = lax.bitcast_convert_type(p & jnp.uint32(0xFFFF0000), F32)
    lo = lax.bitcast_convert_type(p << 16, F32)
    return hi, lo


def _mod_kernel(c_ref, w_ref, b_ref, o_ref):
    cs = _silu(c_ref[...])
    o_ref[...] = jnp.dot(cs, w_ref[...], preferred_element_type=F32) + b_ref[...]


def _modulation(c, ada_w, ada_b):
    depth, d, n6 = ada_w.shape
    b = c.shape[0]
    tn = 1536
    return pl.pallas_call(
        _mod_kernel,
        grid=(depth, n6 // tn),
        in_specs=[
            pl.BlockSpec((b, d), lambda i, n: (0, 0)),
            pl.BlockSpec((None, d, tn), lambda i, n: (i, 0, n)),
            pl.BlockSpec((None, 1, tn), lambda i, n: (i, 0, n)),
        ],
        out_specs=pl.BlockSpec((None, b, tn), lambda i, n: (i, 0, n)),
        out_shape=jax.ShapeDtypeStruct((depth, b, n6), F32),
        compiler_params=_cparams(("arbitrary", "arbitrary")),
    )(c, ada_w, ada_b.reshape(depth, 1, n6))


def _mod_spec(chunk, d, tiles_per_batch):
    return pl.BlockSpec((None, 1, d), lambda *idx: (idx[0] // tiles_per_batch, 0, chunk))


def _pool_kernel(x_ref, sh_ref, sc_ref, g_ref, win_ref, wgrp_ref, cs_ref, wout_ref,
                 lng_ref, lnb_ref, o_ref, ext_ref, *, tiles_per_batch):
    tm, d = x_ref.shape
    s_idx = pl.program_id(0) % tiles_per_batch
    x = x_ref[...]
    h = (x * (1.0 + sc_ref[...]) + sh_ref[...]).astype(BF16)
    u = jnp.dot(h, win_ref[...], preferred_element_type=F32)

    @pl.when(s_idx == 0)
    def _():
        ext_ref[0:POOL_HALO, :] = jnp.zeros((POOL_HALO, d), F32)

    @pl.when(s_idx != 0)
    def _():
        ext_ref[0:POOL_HALO, :] = ext_ref[tm:tm + POOL_HALO, :]

    ext_ref[POOL_HALO:POOL_HALO + tm, :] = u

    pos = s_idx * tm + lax.broadcasted_iota(I32, (tm, 1), 0) + 1
    gc = d // len(POOL_WINDOWS)
    ys = []
    for g, w in enumerate(POOL_WINDOWS):
        cols = slice(g * gc, (g + 1) * gc)
        acc = u[:, cols]
        for j in range(1, w):
            acc = acc + ext_ref[POOL_HALO - j:POOL_HALO - j + tm, cols]
        cnt = jnp.minimum(pos, w).astype(F32)
        z = (acc / cnt - u[:, cols]).astype(BF16)
        ys.append(jnp.dot(z, wgrp_ref[g], preferred_element_type=F32))
    y = (jnp.concatenate(ys, axis=1) * cs_ref[...]).astype(BF16)
    out = jnp.dot(y, wout_ref[...], preferred_element_type=F32)
    v = ALPHA * x + (1.0 + g_ref[...]) * out
    o_ref[...] = _layer_norm(v, lng_ref[...], lnb_ref[...])


def _pool_layer(x2, mod3, w_in, w_grp, ch_scale, w_out, ln_g, ln_b, seq):
    t, d = x2.shape
    tm = 512
    tpb = seq // tm
    full = lambda shape: pl.BlockSpec(shape, lambda i: (0,) * len(shape))
    return pl.pallas_call(
        functools.partial(_pool_kernel, tiles_per_batch=tpb),
        grid=(t // tm,),
        in_specs=[
            pl.BlockSpec((tm, d), lambda i: (i, 0)),
            _mod_spec(0, d, tpb), _mod_spec(1, d, tpb), _mod_spec(2, d, tpb),
            full((d, d)), full(w_grp.shape), full((1, d)), full((d, d)),
            full((1, d)), full((1, d)),
        ],
        out_specs=pl.BlockSpec((tm, d), lambda i: (i, 0)),
        out_shape=jax.ShapeDtypeStruct((t, d), F32),
        scratch_shapes=[pltpu.VMEM((tm + POOL_HALO, d), F32)],
        compiler_params=_cparams(("arbitrary",)),
    )(x2, mod3, mod3, mod3, w_in, w_grp, ch_scale.reshape(1, d), w_out,
      ln_g.reshape(1, d), ln_b.reshape(1, d))


def _perm_matrix(dil):
    p = jnp.arange(PERM_TILE)
    chunk = PERM_TILE // dil
    src = (p % chunk) * dil + p // chunk
    return (src[:, None] == jnp.arange(PERM_TILE)[None, :]).astype(BF16)


def _qkv_kernel(x_ref, sh_ref, sc_ref, p_ref, w_ref, o_ref, h_ref):
    tm = x_ref.shape[0]
    g = pl.program_id(1)
    part = pl.program_id(2)

    @pl.when((g == 0) & (part == 0))
    def _():
        h = (x_ref[...] * (1.0 + sc_ref[...]) + sh_ref[...]).astype(BF16)
        h_ref[0] = h
        for gi in range(1, len(ATTN_PATTERNS)):
            for s in range(tm // PERM_TILE):
                rows = slice(s * PERM_TILE, (s + 1) * PERM_TILE)
                h_ref[gi, rows, :] = jnp.dot(
                    p_ref[gi - 1], h[rows, :], preferred_element_type=F32).astype(BF16)

    o_ref[...] = jnp.dot(h_ref[g], w_ref[...], preferred_element_type=F32).astype(BF16)


def _qkv_proj(x2, mod3, perms, w_in, seq):
    t, d = x2.shape
    ng = len(ATTN_PATTERNS)
    tm = 1024
    tpb = seq // tm
    return pl.pallas_call(
        _qkv_kernel,
        grid=(t // tm, ng, 3),
        in_specs=[
            pl.BlockSpec((tm, d), lambda m, g, p: (m, 0)),
            _mod_spec(0, d, tpb), _mod_spec(1, d, tpb),
            pl.BlockSpec(perms.shape, lambda m, g, p: (0, 0, 0)),
            pl.BlockSpec((d, d), lambda m, g, p: (0, g * 3 + p)),
        ],
        out_specs=pl.BlockSpec((tm, d), lambda m, g, p: (m, g * 3 + p)),
        out_shape=jax.ShapeDtypeStruct((t, ng * 3 * d), BF16),
        scratch_shapes=[pltpu.VMEM((ng, tm, d), BF16)],
        compiler_params=_cparams(("arbitrary", "arbitrary", "arbitrary")),
    )(x2, mod3, mod3, perms, w_in)


def _attn_kernel(q_ref, kp_ref, kc_ref, vp_ref, vc_ref, o_ref, lse_ref, *, group, dil):
    bq = Q_BLOCK
    d = N_HEADS * HEAD_DIM
    j = pl.program_id(2)
    q = q_ref[...].reshape(bq, d)
    kp = kp_ref[...].reshape(bq, d)
    kc = kc_ref[...].reshape(bq, d)
    vp = vp_ref[...].reshape(bq, d)
    vc = vc_ref[...].reshape(bq, d)

    qi = lax.broadcasted_iota(I32, (bq, 2 * bq), 0)
    kj = lax.broadcasted_iota(I32, (bq, 2 * bq), 1)
    dist = qi + bq - kj
    steps = ATTN_PATTERNS[group][0] // dil
    valid = (dist >= 0) & (dist <= steps) & ((kj >= bq) | (j > 0))
    distf = dist.astype(F32)
    lane = lax.broadcasted_iota(I32, (bq, LSE_LANES), 1)
    lse_tile = jnp.zeros((bq, LSE_LANES), F32)
    n_tot = len(ATTN_PATTERNS) * N_HEADS
    outs = []
    for h in range(N_HEADS):
        cols = slice(h * HEAD_DIM, (h + 1) * HEAD_DIM)
        slope = 2.0 ** (-8.0 * (group * N_HEADS + h + 1) / n_tot)
        kh = jnp.concatenate([kp[:, cols], kc[:, cols]], axis=0)
        vh = jnp.concatenate([vp[:, cols], vc[:, cols]], axis=0)
        s = lax.dot_general(q[:, cols], kh, (((1,), (1,)), ((), ())),
                            preferred_element_type=F32)
        s = s * (HEAD_DIM ** -0.5) - (slope * dil) * distf
        s = jnp.where(valid, s, NEG_BIG)
        m = jnp.max(s, axis=-1, keepdims=True)
        p = jnp.exp(s - m)
        den = jnp.sum(p, axis=-1, keepdims=True)
        o = jnp.dot(p.astype(BF16), vh, preferred_element_type=F32) / den
        outs.append(o)
        lse_tile = jnp.where(lane == h, m + jnp.log(den), lse_tile)
    o_ref[...] = jnp.concatenate(outs, axis=1).astype(BF16).reshape(o_ref.shape)
    lse_ref[...] = lse_tile.reshape(lse_ref.shape)


def _attention_group(qkv, group, batch, seq):
    dil = ATTN_PATTERNS[group][1]
    d = N_HEADS * HEAD_DIM
    t = qkv.shape[0]
    sub = seq // dil
    nb = sub // Q_BLOCK
    rows = Q_BLOCK if dil == 1 else PERM_TILE // dil
    chunks = Q_BLOCK // rows
    u = seq // (rows * dil)
    view = lambda a, c: a.reshape(batch, u, dil, rows, c)
    blk = lambda c: (None, chunks, None, rows, c)
    col0 = group * 3
    q_spec = pl.BlockSpec(blk(d), lambda b, r, j: (b, j, r, 0, col0))
    kc_spec = pl.BlockSpec(blk(d), lambda b, r, j: (b, j, r, 0, col0 + 1))
    kp_spec = pl.BlockSpec(blk(d), lambda b, r, j: (b, jnp.maximum(j - 1, 0), r, 0, col0 + 1))
    vc_spec = pl.BlockSpec(blk(d), lambda b, r, j: (b, j, r, 0, col0 + 2))
    vp_spec = pl.BlockSpec(blk(d), lambda b, r, j: (b, jnp.maximum(j - 1, 0), r, 0, col0 + 2))
    qkv5 = view(qkv, qkv.shape[1])
    o, lse = pl.pallas_call(
        functools.partial(_attn_kernel, group=group, dil=dil),
        grid=(batch, dil, nb),
        in_specs=[q_spec, kp_spec, kc_spec, vp_spec, vc_spec],
        out_specs=[
            pl.BlockSpec(blk(d), lambda b, r, j: (b, j, r, 0, 0)),
            pl.BlockSpec(blk(LSE_LANES), lambda b, r, j: (b, j, r, 0, 0)),
        ],
        out_shape=[
            jax.ShapeDtypeStruct((batch, u, dil, rows, d), BF16),
            jax.ShapeDtypeStruct((batch, u, dil, rows, LSE_LANES), F32),
        ],
        compiler_params=_cparams(("arbitrary", "arbitrary", "arbitrary")),
    )(qkv5, qkv5, qkv5, qkv5, qkv5)
    return o.reshape(t, d), lse.reshape(t, LSE_LANES)


def _split3(v):
    a = v.astype(BF16)
    r = v - a.astype(F32)
    b = r.astype(BF16)
    c = (r - b.astype(F32)).astype(BF16)
    return a, b, c


def _attn_out_kernel(x_ref, g_ref, o0_ref, o1_ref, o2_ref, l0_ref, l1_ref, l2_ref,
                     pt_ref, e_ref, wout_ref, lng_ref, lnb_ref, out_ref):
    tm, d = x_ref.shape
    o_refs = (o0_ref, o1_ref, o2_ref)
    l_refs = (l0_ref, l1_ref, l2_ref)
    n_sub = tm // PERM_TILE

    def unperm(gi, val_bf16):
        if gi == 0:
            return val_bf16.astype(F32)
        parts = [jnp.dot(pt_ref[gi - 1], val_bf16[s * PERM_TILE:(s + 1) * PERM_TILE, :],
                         preferred_element_type=F32) for s in range(n_sub)]
        return jnp.concatenate(parts, axis=0)

    lses = []
    for gi in range(3):
        l = l_refs[gi][...]
        if gi == 0:
            lses.append(l)
        else:
            a, b, c = _split3(l)
            lses.append(unperm(gi, a) + unperm(gi, b) + unperm(gi, c))
    mx = jnp.maximum(jnp.maximum(lses[0], lses[1]), lses[2])
    es = [jnp.exp(l - mx) for l in lses]
    tot = es[0] + es[1] + es[2]
    mixed = jnp.zeros((tm, d), F32)
    for gi in range(3):
        w = es[gi] / tot
        a, b, c = _split3(w)
        wide = (jnp.dot(a, e_ref[...], preferred_element_type=F32)
                + jnp.dot(b, e_ref[...], preferred_element_type=F32)
                + jnp.dot(c, e_ref[...], preferred_element_type=F32))
        mixed = mixed + wide * unperm(gi, o_refs[gi][...])
    y = jnp.dot(mixed.astype(BF16), wout_ref[...], preferred_element_type=F32)
    v = ALPHA * x_ref[...] + (1.0 + g_ref[...]) * y
    out_ref[...] = _layer_norm(v, lng_ref[...], lnb_ref[...])


def _attn_out(x2, mod3, os_, lses, perms_t, expand, w_out, ln_g, ln_b, seq):
    t, d = x2.shape
    tm = 512
    tpb = seq // tm
    row = lambda c: pl.BlockSpec((tm, c), lambda i: (i, 0))
    full = lambda shape: pl.BlockSpec(shape, lambda i: (0,) * len(shape))
    return pl.pallas_call(
        _attn_out_kernel,
        grid=(t // tm,),
        in_specs=[row(d), _mod_spec(2, d, tpb), row(d), row(d), row(d),
                  row(LSE_LANES), row(LSE_LANES), row(LSE_LANES),
                  full(perms_t.shape), full(expand.shape), full((d, d)),
                  full((1, d)), full((1, d))],
        out_specs=row(d),
        out_shape=jax.ShapeDtypeStruct((t, d), F32),
        compiler_params=_cparams(("arbitrary",)),
    )(x2, mod3, *os_, *lses, perms_t, expand, w_out, ln_g.reshape(1, d), ln_b.reshape(1, d))


def _router_kernel(x_ref, sh_ref, sc_ref, wr_ref, lg_ref):
    hf = x_ref[...] * (1.0 + sc_ref[...]) + sh_ref[...]
    lg_ref[...] = jnp.dot(hf, wr_ref[...], preferred_element_type=F32,
                          precision=lax.Precision.HIGHEST)


def _router(x2, mod3, w_router, seq):
    t, d = x2.shape
    tm = 512
    tpb = seq // tm
    ne = w_router.shape[1]
    return pl.pallas_call(
        _router_kernel,
        grid=(t // tm,),
        in_specs=[pl.BlockSpec((tm, d), lambda i: (i, 0)),
                  _mod_spec(3, d, tpb), _mod_spec(4, d, tpb),
                  pl.BlockSpec((d, ne), lambda i: (0, 0))],
        out_specs=pl.BlockSpec((tm, ne), lambda i: (i, 0)),
        out_shape=jax.ShapeDtypeStruct((t, ne), F32),
        compiler_params=_cparams(("arbitrary",)),
    )(x2, mod3, mod3, w_router)


def _first_index_of_max(v, iota, size):
    m = jnp.max(v, axis=0, keepdims=True)
    idx = jnp.min(jnp.where(v == m, iota, float(size)), axis=0, keepdims=True)
    return m, idx


def _topk_kernel(lg_ref, b_ref, tri_ref, ltri_ref, w_ref, pos_ref, cnt_ref):
    ne, tr = lg_ref.shape
    gsz = ne // N_EXPERT_GROUPS
    tile = pl.program_id(0)

    scores = 1.0 / (1.0 + jnp.exp(-lg_ref[...]))
    sel = scores + b_ref[...]
    iota_g = lax.broadcasted_iota(I32, (gsz, tr), 0).astype(F32)
    iota_n = lax.broadcasted_iota(I32, (N_EXPERT_GROUPS, tr), 0).astype(F32)
    gs = jnp.zeros((N_EXPERT_GROUPS, tr), F32)
    for g in range(N_EXPERT_GROUPS):
        blk = sel[g * gsz:(g + 1) * gsz, :]
        m1, i1 = _first_index_of_max(blk, iota_g, gsz)
        m2 = jnp.max(jnp.where(iota_g == i1, -jnp.inf, blk), axis=0, keepdims=True)
        gs = jnp.where(iota_n == float(g), m1 + m2, gs)
    gmask = jnp.zeros((N_EXPERT_GROUPS, tr), F32)
    for _ in range(TOPK_GROUPS):
        _, gi = _first_index_of_max(gs, iota_n, N_EXPERT_GROUPS)
        hit = iota_n == gi
        gmask = jnp.where(hit, 1.0, gmask)
        gs = jnp.where(hit, -jnp.inf, gs)
    masked_rows = []
    for g in range(N_EXPERT_GROUPS):
        keep = jnp.broadcast_to(gmask[g:g + 1, :], (gsz, tr)) > 0.5
        masked_rows.append(jnp.where(keep, sel[g * gsz:(g + 1) * gsz, :], -jnp.inf))
    cur = jnp.concatenate(masked_rows, axis=0)
    iota_e = lax.broadcasted_iota(I32, (ne, tr), 0).astype(F32)
    chosen = jnp.zeros((ne, tr), F32)
    idxs, tops = [], []
    for _ in range(TOP_K):
        _, ei = _first_index_of_max(cur, iota_e, ne)
        hit = iota_e == ei
        cur = jnp.where(hit, -jnp.inf, cur)
        chosen = jnp.where(hit, 1.0, chosen)
        idxs.append(ei)
        tops.append(jnp.sum(jnp.where(hit, scores, 0.0), axis=0, keepdims=True))
    wsum = tops[0]
    for k in range(1, TOP_K):
        wsum = wsum + tops[k]
    before = jnp.dot(chosen.astype(BF16), tri_ref[...], preferred_element_type=F32)
    n = jnp.sum(chosen, axis=1, keepdims=True)
    n_chunks = jnp.floor((n + (CHUNK - 1)) * (1.0 / CHUNK))
    run_off = jnp.dot(ltri_ref[...], jnp.broadcast_to(n_chunks, (ne, 128)).astype(BF16),
                      preferred_element_type=F32)[:, 0:1] * CHUNK
    pos_all = before + run_off
    for k in range(TOP_K):
        hit = iota_e == idxs[k]
        w_ref[k:k + 1, :] = tops[k] / wsum * ROUTED_SCALE
        pos_ref[k:k + 1, :] = jnp.sum(jnp.where(hit, pos_all, 0.0), axis=0,
                                      keepdims=True).astype(I32)

    @pl.when(tile == 0)
    def _():
        cnt_ref[...] = jnp.zeros_like(cnt_ref)

    lane = lax.broadcasted_iota(I32, cnt_ref.shape, 1)
    cnt_ref[...] = jnp.where(lane == tile, jnp.broadcast_to(n, cnt_ref.shape).astype(I32),
                             cnt_ref[...])


def _topk_route(logits_t, router_b):
    ne, t = logits_t.shape
    tr = TOK_TILE
    assert t // tr <= 128
    tri = (jnp.arange(tr)[:, None] < jnp.arange(tr)[None, :]).astype(BF16)
    ltri = (jnp.arange(ne)[None, :] < jnp.arange(ne)[:, None]).astype(BF16)
    out = lambda dt: jax.ShapeDtypeStruct((TOP_K, t), dt)
    row = pl.BlockSpec((TOP_K, tr), lambda i: (0, i))
    return pl.pallas_call(
        _topk_kernel,
        grid=(t // tr,),
        in_specs=[pl.BlockSpec((ne, tr), lambda i: (0, i)),
                  pl.BlockSpec((ne, 1), lambda i: (0, 0)),
                  pl.BlockSpec((tr, tr), lambda i: (0, 0)),
                  pl.BlockSpec((ne, ne), lambda i: (0, 0))],
        out_specs=[row, row, pl.BlockSpec((ne, 128), lambda i: (0, 0))],
        out_shape=[out(F32), out(I32), jax.ShapeDtypeStruct((ne, 128), I32)],
        compiler_params=_cparams(("arbitrary",)),
    )(logits_t, router_b.reshape(ne, 1), tri, ltri)


def _slot_plan(counts, n_tiles, n_blocks):
    n = counts[:, :n_tiles]
    nch = (n + (CHUNK - 1)) // CHUNK
    rows = jnp.sum(nch, axis=1) * CHUNK
    region = ((rows + EXPERT_ROWS - 1) // EXPERT_ROWS) * EXPERT_ROWS
    region_end = jnp.cumsum(region)
    region_start = region_end - region
    run_chunk = region_start[:, None] // CHUNK + jnp.cumsum(nch, axis=1) - nch
    local_end = jnp.cumsum(nch, axis=0)
    local_chunk = local_end - nch
    c = jnp.arange(LOCAL_ROWS // CHUNK, dtype=I32)
    expert_of_c = jnp.sum((local_end[:, :, None] <= c[None, None, :]).astype(I32), axis=0)
    owner = expert_of_c[None] == jnp.arange(N_EXPERTS, dtype=I32)[:, None, None]
    slot_chunk = jnp.sum(jnp.where(owner, (run_chunk - local_chunk)[:, :, None], 0), axis=0) + c
    block_row = jnp.arange(n_blocks, dtype=I32) * EXPERT_ROWS
    group_e = jnp.minimum(jnp.sum((region_end[None, :] <= block_row[:, None]).astype(I32), axis=1),
                          N_EXPERTS - 1)
    return dict(
        slot_chunk=slot_chunk.reshape(-1).astype(I32),
        tile_chunks=jnp.sum(nch, axis=0).astype(I32),
        tail_chunk=((region_start + rows) // CHUNK).astype(I32),
        tail_chunks=((region - rows) // CHUNK).astype(I32),
        group_e=group_e, n_used=(region_end[-1:] // EXPERT_ROWS).astype(I32))


def _chunk_rows(ref, chunk_index):
    start = chunk_index * CHUNK
    if not isinstance(start, int):
        start = pl.multiple_of(start, CHUNK)
    return ref.at[pl.ds(start, CHUNK), :]


def _dispatch_kernel(slot_ref, tile_ref, tail_ref, tailn_ref, used_ref,
                     x_ref, sh_ref, sc_ref, pos_ref, xg_hbm, stage_ref, zero_ref, sem, sem_blk):
    tr = x_ref.shape[0]
    i = pl.program_id(0)
    n_blocks = xg_hbm.shape[0] // EXPERT_ROWS
    chunks_per_tile = LOCAL_ROWS // CHUNK

    def drain(count):
        def body(c, carry):
            pltpu.make_async_copy(_chunk_rows(zero_ref, 0), _chunk_rows(xg_hbm, 0), sem).wait()
            return carry
        lax.fori_loop(0, count, body, 0)

    @pl.when(i == 0)
    def _():
        zero_ref[...] = jnp.zeros_like(zero_ref)

        def block_copy(j):
            rows = pl.ds(pl.multiple_of(j * EXPERT_ROWS, EXPERT_ROWS), EXPERT_ROWS)
            return pltpu.make_async_copy(zero_ref, xg_hbm.at[rows, :], sem_blk)

        def start_block(j, carry):
            block_copy(j).start()
            return carry

        def wait_block(j, carry):
            block_copy(j).wait()
            return carry

        lax.fori_loop(used_ref[0], n_blocks, start_block, 0)

        def per_expert(e, total):
            def per_chunk(c, carry):
                pltpu.make_async_copy(_chunk_rows(zero_ref, 0),
                                      _chunk_rows(xg_hbm, tail_ref[e] + c), sem).start()
                return carry
            lax.fori_loop(0, tailn_ref[e], per_chunk, 0)
            return total + tailn_ref[e]

        drain(lax.fori_loop(0, N_EXPERTS, per_expert, 0))
        lax.fori_loop(used_ref[0], n_blocks, wait_block, 0)

    hf = (x_ref[...] * (1.0 + sc_ref[...]) + sh_ref[...]).astype(BF16)
    slot = lax.broadcasted_iota(I32, (LOCAL_ROWS, tr), 0)
    perm = jnp.zeros((LOCAL_ROWS, tr), F32)
    for k in range(TOP_K):
        perm = jnp.where(slot == pos_ref[k:k + 1, :], 1.0, perm)
    rows = jnp.dot(perm.astype(BF16), hf, preferred_element_type=F32)
    half = rows.shape[1] // 2
    buf = stage_ref.at[i % 2]
    buf[...] = (lax.bitcast_convert_type(rows[:, :half], U32)
                | (lax.bitcast_convert_type(rows[:, half:], U32) >> 16))

    @pl.when(i > 0)
    def _():
        drain(tile_ref[jnp.maximum(i - 1, 0)])

    def send(c, carry):
        pltpu.make_async_copy(_chunk_rows(buf, c),
                              _chunk_rows(xg_hbm, slot_ref[i * chunks_per_tile + c]), sem).start()
        return carry

    lax.fori_loop(0, tile_ref[i], send, 0)

    @pl.when(i == pl.num_programs(0) - 1)
    def _():
        drain(tile_ref[i])


def _dispatch(plan, x2, mod3, pos, n_slots, seq):
    t, d = x2.shape
    tr = TOK_TILE
    tpb = seq // tr
    return pl.pallas_call(
        _dispatch_kernel,
        grid_spec=pltpu.PrefetchScalarGridSpec(
            num_scalar_prefetch=5,
            grid=(t // tr,),
            in_specs=[pl.BlockSpec((tr, d), lambda i, *_: (i, 0)),
                      _mod_spec(3, d, tpb), _mod_spec(4, d, tpb),
                      pl.BlockSpec((TOP_K, tr), lambda i, *_: (0, i))],
            out_specs=pl.BlockSpec(memory_space=pl.ANY),
            scratch_shapes=[pltpu.VMEM((2, LOCAL_ROWS, d // 2), U32),
                            pltpu.VMEM((EXPERT_ROWS, d // 2), U32),
                            pltpu.SemaphoreType.DMA, pltpu.SemaphoreType.DMA],
        ),
        out_shape=jax.ShapeDtypeStruct((n_slots, d // 2), U32),
        compiler_params=_cparams(("arbitrary",)),
    )(plan["slot_chunk"], plan["tile_chunks"], plan["tail_chunk"], plan["tail_chunks"],
      plan["n_used"], x2, mod3, mod3, pos)


def _expert_kernel(ge_ref, nu_ref, x_ref, wgu_ref, wdn_ref, y_ref, wgu_bf, wdn_bf):
    j = pl.program_id(0)
    used = j < nu_ref[0]
    prev = ge_ref[jnp.maximum(j - 1, 0)]
    fresh = (j == 0) | (ge_ref[j] != prev)

    @pl.when(used & fresh)
    def _():
        wgu_bf[...] = wgu_ref[...].astype(BF16)
        wdn_bf[...] = wdn_ref[...].astype(BF16)

    @pl.when(used)
    def _():
        half = x_ref.shape[1]
        f = wdn_bf.shape[0]
        hi, lo = _unpack_bf16_pairs(x_ref[...])
        gu = (jnp.dot(hi.astype(BF16), wgu_bf[0:half, :], preferred_element_type=F32)
              + jnp.dot(lo.astype(BF16), wgu_bf[half:, :], preferred_element_type=F32))
        act = (_silu(gu[:, :f]) * gu[:, f:]).astype(BF16)
        y_ref[...] = _pack_bf16_pairs(jnp.dot(act, wdn_bf[...], preferred_element_type=F32))

    @pl.when(jnp.logical_not(used))
    def _():
        y_ref[...] = jnp.zeros_like(y_ref)


def _expert_gemm(group_e, n_used, xg, w_gu, w_down, layer):
    n_slots, half = xg.shape
    _, ne, d, f2 = w_gu.shape
    f = w_down.shape[2]
    nb = n_slots // EXPERT_ROWS
    return pl.pallas_call(
        _expert_kernel,
        grid_spec=pltpu.PrefetchScalarGridSpec(
            num_scalar_prefetch=2,
            grid=(nb,),
            in_specs=[
                pl.BlockSpec((EXPERT_ROWS, half),
                             lambda j, ge, nu: (jnp.minimum(j, nu[0] - 1), 0)),
                pl.BlockSpec((None, None, d, f2), lambda j, ge, nu: (layer, ge[j], 0, 0)),
                pl.BlockSpec((None, None, f, d), lambda j, ge, nu: (layer, ge[j], 0, 0)),
            ],
            out_specs=pl.BlockSpec((EXPERT_ROWS, half), lambda j, ge, nu: (j, 0)),
            scratch_shapes=[pltpu.VMEM((d, f2), BF16), pltpu.VMEM((f, d), BF16)],
        ),
        out_shape=jax.ShapeDtypeStruct((n_slots, half), U32),
        compiler_params=_cparams(("arbitrary",)),
    )(group_e, n_used, xg, w_gu, w_down)


def _combine_kernel(slot_ref, tile_ref,
                    x_ref, sh_ref, sc_ref, g_ref, pos_ref, wt_ref, wsgu_ref, wsdn_ref,
                    lng_ref, lnb_ref, yg_hbm, o_ref, stage_ref, sems):
    tr, d = x_ref.shape
    i = pl.program_id(0)
    n_tiles = pl.num_programs(0)
    chunks_per_tile = LOCAL_ROWS // CHUNK

    def fetch(tile):
        def body(c, carry):
            pltpu.make_async_copy(_chunk_rows(yg_hbm, slot_ref[tile * chunks_per_tile + c]),
                                  _chunk_rows(stage_ref.at[tile % 2], c),
                                  sems.at[tile % 2]).start()
            return carry
        lax.fori_loop(0, tile_ref[tile], body, 0)

    @pl.when(i == 0)
    def _():
        stage_ref[...] = jnp.zeros_like(stage_ref)
        fetch(i)

    @pl.when(i + 1 < n_tiles)
    def _():
        fetch(jnp.minimum(i + 1, n_tiles - 1))

    x = x_ref[...]
    hf = (x * (1.0 + sc_ref[...]) + sh_ref[...]).astype(BF16)
    f = wsdn_ref.shape[0]
    su = jnp.dot(hf, wsgu_ref[...], preferred_element_type=F32)
    act = (_silu(su[:, :f]) * su[:, f:]).astype(BF16)
    shared = jnp.dot(act, wsdn_ref[...], preferred_element_type=F32)

    lane = lax.broadcasted_iota(I32, (tr, LOCAL_ROWS), 1)
    weights = jnp.zeros((tr, LOCAL_ROWS), F32)
    for k in range(TOP_K):
        weights = jnp.where(lane == pos_ref[:, k:k + 1], wt_ref[:, k:k + 1], weights)
    w_hi = weights.astype(BF16)
    w_lo = (weights - w_hi.astype(F32)).astype(BF16)
    w_both = jnp.concatenate([w_hi, w_lo], axis=0)

    buf = stage_ref.at[i % 2]

    def drain(c, carry):
        pltpu.make_async_copy(_chunk_rows(yg_hbm, 0), _chunk_rows(buf, 0), sems.at[i % 2]).wait()
        return carry

    lax.fori_loop(0, tile_ref[i], drain, 0)

    y_hi, y_lo = _unpack_bf16_pairs(buf[...])
    y = jnp.concatenate([y_hi.astype(BF16), y_lo.astype(BF16)], axis=1)
    both = jnp.dot(w_both, y, preferred_element_type=F32)
    routed = both[:tr, :] + both[tr:, :]
    v = ALPHA * x + (1.0 + g_ref[...]) * (routed + shared)
    o_ref[...] = _layer_norm(v, lng_ref[...], lnb_ref[...])


def _combine(plan, x2, mod3, pos_tok, w_tok, w_sh_gu, w_sh_down, ln_g, ln_b, yg, seq):
    t, d = x2.shape
    tr = TOK_TILE
    tpb = seq // tr
    full = lambda shape: pl.BlockSpec(shape, lambda i, *_: (0,) * len(shape))
    return pl.pallas_call(
        _combine_kernel,
        grid_spec=pltpu.PrefetchScalarGridSpec(
            num_scalar_prefetch=2,
            grid=(t // tr,),
            in_specs=[pl.BlockSpec((tr, d), lambda i, *_: (i, 0)),
                      _mod_spec(3, d, tpb), _mod_spec(4, d, tpb), _mod_spec(5, d, tpb),
                      pl.BlockSpec((tr, TOP_K), lambda i, *_: (i, 0)),
                      pl.BlockSpec((tr, TOP_K), lambda i, *_: (i, 0)),
                      full(w_sh_gu.shape), full(w_sh_down.shape), full((1, d)), full((1, d)),
                      pl.BlockSpec(memory_space=pl.ANY)],
            out_specs=pl.BlockSpec((tr, d), lambda i, *_: (i, 0)),
            scratch_shapes=[pltpu.VMEM((2, LOCAL_ROWS, d // 2), U32),
                            pltpu.SemaphoreType.DMA((2,))],
        ),
        out_shape=jax.ShapeDtypeStruct((t, d), F32),
        compiler_params=_cparams(("arbitrary",)),
    )(plan["slot_chunk"], plan["tile_chunks"],
      x2, mod3, mod3, mod3, pos_tok, w_tok, w_sh_gu, w_sh_down,
      ln_g.reshape(1, d), ln_b.reshape(1, d), yg)


def _moe_layer(x2, mod3, w_router, router_b, w_gu, w_down, layer, w_sh_gu, w_sh_down,
               ln_g, ln_b, seq):
    t, d = x2.shape
    n_tiles = t // TOK_TILE
    logits = _router(x2, mod3, w_router, seq)
    w_top, pos, counts = _topk_route(logits.T, router_b)
    bound = t * TOP_K + n_tiles * N_EXPERTS * (CHUNK - 1) + N_EXPERTS * (EXPERT_ROWS - 1)
    n_blocks = -(-bound // EXPERT_ROWS)
    plan = _slot_plan(counts, n_tiles, n_blocks)
    xg = _dispatch(plan, x2, mod3, pos, n_blocks * EXPERT_ROWS, seq)
    yg = _expert_gemm(plan["group_e"], plan["n_used"], xg, w_gu, w_down, layer)
    return _combine(plan, x2, mod3, pos.T, w_top.T, w_sh_gu, w_sh_down, ln_g, ln_b, yg, seq)


def kernel(x, c, ada_w, ada_b, pool_w_in, pool_w_grp, pool_scale, pool_w_out, attn_w_in, attn_w_out, ln1_g, ln1_b, router_w, router_b, exp_w_gu, exp_w_down, sh_w_gu, sh_w_down, ln2_g, ln2_b):
    batch, seq, d = x.shape
    depth = ada_w.shape[0]
    t = batch * seq
    mod = _modulation(c, ada_w, ada_b).reshape(depth, batch, 1, 6 * d)
    perms = jnp.stack([_perm_matrix(dil) for _, dil in ATTN_PATTERNS[1:]])
    perms_t = jnp.swapaxes(perms, 1, 2)
    expand = (jnp.arange(LSE_LANES)[:, None] == (jnp.arange(d)[None, :] // HEAD_DIM)).astype(BF16)
    x2 = x.reshape(t, d)
    for i in range(depth):
        mod3 = mod[i]
        j = i // 2
        if i % 2 == 0:
            x2 = _pool_layer(x2, mod3, pool_w_in[j].astype(BF16), pool_w_grp[j].astype(BF16),
                             pool_scale[j], pool_w_out[j].astype(BF16), ln1_g[i], ln1_b[i], seq)
        else:
            qkv = _qkv_proj(x2, mod3, perms, attn_w_in[j].astype(BF16), seq)
            res = [_attention_group(qkv, g, batch, seq) for g in range(len(ATTN_PATTERNS))]
            x2 = _attn_out(x2, mod3, [r[0] for r in res], [r[1] for r in res], perms_t, expand,
                           attn_w_out[j].astype(BF16), ln1_g[i], ln1_b[i], seq)
        x2 = _moe_layer(x2, mod3, router_w[i], router_b[i], exp_w_gu, exp_w_down, i,
                        sh_w_gu[i].astype(BF16), sh_w_down[i].astype(BF16),
                        ln2_g[i], ln2_b[i], seq)
    return x2.reshape(batch, seq, d)
```

```python
import functools
import math

import jax
import jax.numpy as jnp
from jax import lax
from jax.experimental import pallas as pl
from jax.experimental.pallas import tpu as pltpu

F32 = jnp.float32
BF16 = jnp.bfloat16
U32 = jnp.uint32
I32 = jnp.int32

POOL_WINDOWS = (2, 4, 8, 16)
ATTN_PATTERNS = ((128, 1), (512, 4), (2048, 16))
HEAD_DIM = 64
N_HEADS = 16
Q_BLOCK = 128
N_EXPERTS = 64
TOP_K = 8
N_EXPERT_GROUPS = 8
TOPK_GROUPS = 4
ROUTED_SCALE = 2.5
EXPERT_ROWS = 256
DEPTH = 4
ALPHA = (2 * DEPTH) ** 0.25
LN_EPS = 1e-5

PERM_TILE = 256
POOL_HALO = 16
TOK_TILE = 256
CHUNK = 8
LOCAL_ROWS = -(-(TOK_TILE * TOP_K + N_EXPERTS * (CHUNK - 1)) // 256) * 256
LSE_LANES = 128
VMEM_LIMIT = 56 * 1024 * 1024
NEG_BIG = -1e30


def _cparams(sem):
    return pltpu.CompilerParams(dimension_semantics=sem, vmem_limit_bytes=VMEM_LIMIT)


def _layer_norm(v, g, b):
    mu = jnp.mean(v, axis=-1, keepdims=True)
    c = v - mu
    var = jnp.mean(c * c, axis=-1, keepdims=True)
    return c * lax.rsqrt(var + LN_EPS) * g + b


def _silu(v):
    return v * (1.0 / (1.0 + jnp.exp(-v)))


def _pack_bf16_pairs(v):
    n = v.shape[1] // 2
    hi = lax.bitcast_convert_type(v[:, :n].astype(BF16).astype(F32), U32)
    lo = lax.bitcast_convert_type(v[:, n:].astype(BF16).astype(F32), U32)
    return hi | (lo >> 16)


def _unpack_bf16_pairs(p):
    hi = lax.bitcast_convert_type(p & jnp.uint32(0xFFFF0000), F32)
    lo = lax.bitcast_convert_type(p << 16, F32)
    return hi, lo


def _mod_kernel(c_ref, w_ref, b_ref, o_ref):
    cs = _silu(c_ref[...])
    o_ref[...] = jnp.dot(cs, w_ref[...], preferred_element_type=F32) + b_ref[...]


def _modulation(c, ada_w, ada_b):
    depth, d, n6 = ada_w.shape
    b = c.shape[0]
    tn = 1536
    return pl.pallas_call(
        _mod_kernel,
        grid=(depth, n6 // tn),
        in_specs=[
            pl.BlockSpec((b, d), lambda i, n: (0, 0)),
            pl.BlockSpec((None, d, tn), lambda i, n: (i, 0, n)),
            pl.BlockSpec((None, 1, tn), lambda i, n: (i, 0, n)),
        ],
        out_specs=pl.BlockSpec((None, b, tn), lambda i, n: (i, 0, n)),
        out_shape=jax.ShapeDtypeStruct((depth, b, n6), F32),
        compiler_params=_cparams(("arbitrary", "arbitrary")),
    )(c, ada_w, ada_b.reshape(depth, 1, n6))


def _mod_spec(chunk, d, tiles_per_batch):
    return pl.BlockSpec((None, 1, d), lambda *idx: (idx[0] // tiles_per_batch, 0, chunk))


def _pool_kernel(x_ref, sh_ref, sc_ref, g_ref, win_ref, wgrp_ref, cs_ref, wout_ref,
                 lng_ref, lnb_ref, o_ref, ext_ref, *, tiles_per_batch):
    tm, d = x_ref.shape
    s_idx = pl.program_id(0) % tiles_per_batch
    x = x_ref[...]
    h = (x * (1.0 + sc_ref[...]) + sh_ref[...]).astype(BF16)
    u = jnp.dot(h, win_ref[...], preferred_element_type=F32)

    @pl.when(s_idx == 0)
    def _():
        ext_ref[0:POOL_HALO, :] = jnp.zeros((POOL_HALO, d), F32)

    @pl.when(s_idx != 0)
    def _():
        ext_ref[0:POOL_HALO, :] = ext_ref[tm:tm + POOL_HALO, :]

    ext_ref[POOL_HALO:POOL_HALO + tm, :] = u

    pos = s_idx * tm + lax.broadcasted_iota(I32, (tm, 1), 0) + 1
    gc = d // len(POOL_WINDOWS)
    ys = []
    for g, w in enumerate(POOL_WINDOWS):
        cols = slice(g * gc, (g + 1) * gc)
        acc = u[:, cols]
        for j in range(1, w):
            acc = acc + ext_ref[POOL_HALO - j:POOL_HALO - j + tm, cols]
        cnt = jnp.minimum(pos, w).astype(F32)
        z = (acc / cnt - u[:, cols]).astype(BF16)
        ys.append(jnp.dot(z, wgrp_ref[g], preferred_element_type=F32))
    y = (jnp.concatenate(ys, axis=1) * cs_ref[...]).astype(BF16)
    out = jnp.dot(y, wout_ref[...], preferred_element_type=F32)
    v = ALPHA * x + (1.0 + g_ref[...]) * out
    o_ref[...] = _layer_norm(v, lng_ref[...], lnb_ref[...])


def _pool_layer(x2, mod3, w_in, w_grp, ch_scale, w_out, ln_g, ln_b, seq):
    t, d = x2.shape
    tm = 512
    tpb = seq // tm
    full = lambda shape: pl.BlockSpec(shape, lambda i: (0,) * len(shape))
    return pl.pallas_call(
        functools.partial(_pool_kernel, tiles_per_batch=tpb),
        grid=(t // tm,),
        in_specs=[
            pl.BlockSpec((tm, d), lambda i: (i, 0)),
            _mod_spec(0, d, tpb), _mod_spec(1, d, tpb), _mod_spec(2, d, tpb),
            full((d, d)), full(w_grp.shape), full((1, d)), full((d, d)),
            full((1, d)), full((1, d)),
        ],
        out_specs=pl.BlockSpec((tm, d), lambda i: (i, 0)),
        out_shape=jax.ShapeDtypeStruct((t, d), F32),
        scratch_shapes=[pltpu.VMEM((tm + POOL_HALO, d), F32)],
        compiler_params=_cparams(("arbitrary",)),
    )(x2, mod3, mod3, mod3, w_in, w_grp, ch_scale.reshape(1, d), w_out,
      ln_g.reshape(1, d), ln_b.reshape(1, d))


def _perm_matrix(dil):
    p = jnp.arange(PERM_TILE)
    chunk = PERM_TILE // dil
    src = (p % chunk) * dil + p // chunk
    return (src[:, None] == jnp.arange(PERM_TILE)[None, :]).astype(BF16)


def _qkv_kernel(x_ref, sh_ref, sc_ref, p_ref, w_ref, o_ref, h_ref):
    tm = x_ref.shape[0]
    g = pl.program_id(1)
    part = pl.program_id(2)

    @pl.when((g == 0) & (part == 0))
    def _():
        h = (x_ref[...] * (1.0 + sc_ref[...]) + sh_ref[...]).astype(BF16)
        h_ref[0] = h
        for gi in range(1, len(ATTN_PATTERNS)):
            for s in range(tm // PERM_TILE):
                rows = slice(s * PERM_TILE, (s + 1) * PERM_TILE)
                h_ref[gi, rows, :] = jnp.dot(
                    p_ref[gi - 1], h[rows, :], preferred_element_type=F32).astype(BF16)

    o_ref[...] = jnp.dot(h_ref[g], w_ref[...], preferred_element_type=F32).astype(BF16)


def _qkv_proj(x2, mod3, perms, w_in, seq):
    t, d = x2.shape
    ng = len(ATTN_PATTERNS)
    tm = 1024
    tpb = seq // tm
    return pl.pallas_call(
        _qkv_kernel,
        grid=(t // tm, ng, 3),
        in_specs=[
            pl.BlockSpec((tm, d), lambda m, g, p: (m, 0)),
            _mod_spec(0, d, tpb), _mod_spec(1, d, tpb),
            pl.BlockSpec(perms.shape, lambda m, g, p: (0, 0, 0)),
            pl.BlockSpec((d, d), lambda m, g, p: (0, g * 3 + p)),
        ],
        out_specs=pl.BlockSpec((tm, d), lambda m, g, p: (m, g * 3 + p)),
        out_shape=jax.ShapeDtypeStruct((t, ng * 3 * d), BF16),
        scratch_shapes=[pltpu.VMEM((ng, tm, d), BF16)],
        compiler_params=_cparams(("arbitrary", "arbitrary", "arbitrary")),
    )(x2, mod3, mod3, perms, w_in)


def _attn_kernel(q_ref, kp_ref, kc_ref, vp_ref, vc_ref, o_ref, lse_ref, *, group, dil):
    bq = Q_BLOCK
    d = N_HEADS * HEAD_DIM
    j = pl.program_id(2)
    q = q_ref[...].reshape(bq, d)
    kp = kp_ref[...].reshape(bq, d)
    kc = kc_ref[...].reshape(bq, d)
    vp = vp_ref[...].reshape(bq, d)
    vc = vc_ref[...].reshape(bq, d)

    qi = lax.broadcasted_iota(I32, (bq, 2 * bq), 0)
    kj = lax.broadcasted_iota(I32, (bq, 2 * bq), 1)
    dist = qi + bq - kj
    steps = ATTN_PATTERNS[group][0] // dil
    valid = (dist >= 0) & (dist <= steps) & ((kj >= bq) | (j > 0))
    distf = dist.astype(F32)
    lane = lax.broadcasted_iota(I32, (bq, LSE_LANES), 1)
    lse_tile = jnp.zeros((bq, LSE_LANES), F32)
    n_tot = len(ATTN_PATTERNS) * N_HEADS
    outs = []
    for h in range(N_HEADS):
        cols = slice(h * HEAD_DIM, (h + 1) * HEAD_DIM)
        slope = 2.0 ** (-8.0 * (group * N_HEADS + h + 1) / n_tot)
        kh = jnp.concatenate([kp[:, cols], kc[:, cols]], axis=0)
        vh = jnp.concatenate([vp[:, cols], vc[:, cols]], axis=0)
        s = lax.dot_general(q[:, cols], kh, (((1,), (1,)), ((), ())),
                            preferred_element_type=F32)
        s = s * (HEAD_DIM ** -0.5) - (slope * dil) * distf
        s = jnp.where(valid, s, NEG_BIG)
        m = jnp.max(s, axis=-1, keepdims=True)
        p = jnp.exp(s - m)
        den = jnp.sum(p, axis=-1, keepdims=True)
        o = jnp.dot(p.astype(BF16), vh, preferred_element_type=F32) / den
        outs.append(o)
        lse_tile = jnp.where(lane == h, m + jnp.log(den), lse_tile)
    o_ref[...] = jnp.concatenate(outs, axis=1).astype(BF16).reshape(o_ref.shape)
    lse_ref[...] = lse_tile.reshape(lse_ref.shape)


def _attention_group(qkv, group, batch, seq):
    dil = ATTN_PATTERNS[group][1]
    d = N_HEADS * HEAD_DIM
    t = qkv.shape[0]
    sub = seq // dil
    nb = sub // Q_BLOCK
    rows = Q_BLOCK if dil == 1 else PERM_TILE // dil
    chunks = Q_BLOCK // rows
    u = seq // (rows * dil)
    view = lambda a, c: a.reshape(batch, u, dil, rows, c)
    blk = lambda c: (None, chunks, None, rows, c)
    col0 = group * 3
    q_spec = pl.BlockSpec(blk(d), lambda b, r, j: (b, j, r, 0, col0))
    kc_spec = pl.BlockSpec(blk(d), lambda b, r, j: (b, j, r, 0, col0 + 1))
    kp_spec = pl.BlockSpec(blk(d), lambda b, r, j: (b, jnp.maximum(j - 1, 0), r, 0, col0 + 1))
    vc_spec = pl.BlockSpec(blk(d), lambda b, r, j: (b, j, r, 0, col0 + 2))
    vp_spec = pl.BlockSpec(blk(d), lambda b, r, j: (b, jnp.maximum(j - 1, 0), r, 0, col0 + 2))
    qkv5 = view(qkv, qkv.shape[1])
    o, lse = pl.pallas_call(
        functools.partial(_attn_kernel, group=group, dil=dil),
        grid=(batch, dil, nb),
        in_specs=[q_spec, kp_spec, kc_spec, vp_spec, vc_spec],
        out_specs=[
            pl.BlockSpec(blk(d), lambda b, r, j: (b, j, r, 0, 0)),
            pl.BlockSpec(blk(LSE_LANES), lambda b, r, j: (b, j, r, 0, 0)),
        ],
        out_shape=[
            jax.ShapeDtypeStruct((batch, u, dil, rows, d), BF16),
            jax.ShapeDtypeStruct((batch, u, dil, rows, LSE_LANES), F32),
        ],
        compiler_params=_cparams(("arbitrary", "arbitrary", "arbitrary")),
    )(qkv5, qkv5, qkv5, qkv5, qkv5)
    return o.reshape(t, d), lse.reshape(t, LSE_LANES)


def _split3(v):
    a = v.astype(BF16)
    r = v - a.astype(F32)
    b = r.astype(BF16)
    c = (r - b.astype(F32)).astype(BF16)
    return a, b, c


def _attn_out_kernel(x_ref, g_ref, o0_ref, o1_ref, o2_ref, l0_ref, l1_ref, l2_ref,
                     pt_ref, e_ref, wout_ref, lng_ref, lnb_ref, out_ref):
    tm, d = x_ref.shape
    o_refs = (o0_ref, o1_ref, o2_ref)
    l_refs = (l0_ref, l1_ref, l2_ref)
    n_sub = tm // PERM_TILE

    def unperm(gi, val_bf16):
        if gi == 0:
            return val_bf16.astype(F32)
        parts = [jnp.dot(pt_ref[gi - 1], val_bf16[s * PERM_TILE:(s + 1) * PERM_TILE, :],
                         preferred_element_type=F32) for s in range(n_sub)]
        return jnp.concatenate(parts, axis=0)

    lses = []
    for gi in range(3):
        l = l_refs[gi][...]
        if gi == 0:
            lses.append(l)
        else:
            a, b, c = _split3(l)
            lses.append(unperm(gi, a) + unperm(gi, b) + unperm(gi, c))
    mx = jnp.maximum(jnp.maximum(lses[0], lses[1]), lses[2])
    es = [jnp.exp(l - mx) for l in lses]
    tot = es[0] + es[1] + es[2]
    mixed = jnp.zeros((tm, d), F32)
    for gi in range(3):
        w = es[gi] / tot
        a, b, c = _split3(w)
        wide = (jnp.dot(a, e_ref[...], preferred_element_type=F32)
                + jnp.dot(b, e_ref[...], preferred_element_type=F32)
                + jnp.dot(c, e_ref[...], preferred_element_type=F32))
        mixed = mixed + wide * unperm(gi, o_refs[gi][...])
    y = jnp.dot(mixed.astype(BF16), wout_ref[...], preferred_element_type=F32)
    v = ALPHA * x_ref[...] + (1.0 + g_ref[...]) * y
    out_ref[...] = _layer_norm(v, lng_ref[...], lnb_ref[...])


def _attn_out(x2, mod3, os_, lses, perms_t, expand, w_out, ln_g, ln_b, seq):
    t, d = x2.shape
    tm = 512
    tpb = seq // tm
    row = lambda c: pl.BlockSpec((tm, c), lambda i: (i, 0))
    full = lambda shape: pl.BlockSpec(shape, lambda i: (0,) * len(shape))
    return pl.pallas_call(
        _attn_out_kernel,
        grid=(t // tm,),
        in_specs=[row(d), _mod_spec(2, d, tpb), row(d), row(d), row(d),
                  row(LSE_LANES), row(LSE_LANES), row(LSE_LANES),
                  full(perms_t.shape), full(expand.shape), full((d, d)),
                  full((1, d)), full((1, d))],
        out_specs=row(d),
        out_shape=jax.ShapeDtypeStruct((t, d), F32),
        compiler_params=_cparams(("arbitrary",)),
    )(x2, mod3, *os_, *lses, perms_t, expand, w_out, ln_g.reshape(1, d), ln_b.reshape(1, d))


def _router_kernel(x_ref, sh_ref, sc_ref, wr_ref, lg_ref):
    hf = x_ref[...] * (1.0 + sc_ref[...]) + sh_ref[...]
    lg_ref[...] = jnp.dot(hf, wr_ref[...], preferred_element_type=F32,
                          precision=lax.Precision.HIGHEST)


def _router(x2, mod3, w_router, seq):
    t, d = x2.shape
    tm = 512
    tpb = seq // tm
    ne = w_router.shape[1]
    return pl.pallas_call(
        _router_kernel,
        grid=(t // tm,),
        in_specs=[pl.BlockSpec((tm, d), lambda i: (i, 0)),
                  _mod_spec(3, d, tpb), _mod_spec(4, d, tpb),
                  pl.BlockSpec((d, ne), lambda i: (0, 0))],
        out_specs=pl.BlockSpec((tm, ne), lambda i: (i, 0)),
        out_shape=jax.ShapeDtypeStruct((t, ne), F32),
        compiler_params=_cparams(("arbitrary",)),
    )(x2, mod3, mod3, w_router)


def _first_index_of_max(v, iota, size):
    m = jnp.max(v, axis=0, keepdims=True)
    idx = jnp.min(jnp.where(v == m, iota, float(size)), axis=0, keepdims=True)
    return m, idx


def _topk_kernel(lg_ref, b_ref, tri_ref, ltri_ref, w_ref, pos_ref, cnt_ref):
    ne, tr = lg_ref.shape
    gsz = ne // N_EXPERT_GROUPS
    tile = pl.program_id(0)

    scores = 1.0 / (1.0 + jnp.exp(-lg_ref[...]))
    sel = scores + b_ref[...]
    iota_g = lax.broadcasted_iota(I32, (gsz, tr), 0).astype(F32)
    iota_n = lax.broadcasted_iota(I32, (N_EXPERT_GROUPS, tr), 0).astype(F32)
    gs = jnp.zeros((N_EXPERT_GROUPS, tr), F32)
    for g in range(N_EXPERT_GROUPS):
        blk = sel[g * gsz:(g + 1) * gsz, :]
        m1, i1 = _first_index_of_max(blk, iota_g, gsz)
        m2 = jnp.max(jnp.where(iota_g == i1, -jnp.inf, blk), axis=0, keepdims=True)
        gs = jnp.where(iota_n == float(g), m1 + m2, gs)
    gmask = jnp.zeros((N_EXPERT_GROUPS, tr), F32)
    for _ in range(TOPK_GROUPS):
        _, gi = _first_index_of_max(gs, iota_n, N_EXPERT_GROUPS)
        hit = iota_n == gi
        gmask = jnp.where(hit, 1.0, gmask)
        gs = jnp.where(hit, -jnp.inf, gs)
    masked_rows = []
    for g in range(N_EXPERT_GROUPS):
        keep = jnp.broadcast_to(gmask[g:g + 1, :], (gsz, tr)) > 0.5
        masked_rows.append(jnp.where(keep, sel[g * gsz:(g + 1) * gsz, :], -jnp.inf))
    cur = jnp.concatenate(masked_rows, axis=0)
    iota_e = lax.broadcasted_iota(I32, (ne, tr), 0).astype(F32)
    chosen = jnp.zeros((ne, tr), F32)
    idxs, tops = [], []
    for _ in range(TOP_K):
        _, ei = _first_index_of_max(cur, iota_e, ne)
        hit = iota_e == ei
        cur = jnp.where(hit, -jnp.inf, cur)
        chosen = jnp.where(hit, 1.0, chosen)
        idxs.append(ei)
        tops.append(jnp.sum(jnp.where(hit, scores, 0.0), axis=0, keepdims=True))
    wsum = tops[0]
    for k in range(1, TOP_K):
        wsum = wsum + tops[k]
    before = jnp.dot(chosen.astype(BF16), tri_ref[...], preferred_element_type=F32)
    n = jnp.sum(chosen, axis=1, keepdims=True)
    n_chunks = jnp.floor((n + (CHUNK - 1)) * (1.0 / CHUNK))
    run_off = jnp.dot(ltri_ref[...], jnp.broadcast_to(n_chunks, (ne, 128)).astype(BF16),
                      preferred_element_type=F32)[:, 0:1] * CHUNK
    pos_all = before + run_off
    for k in range(TOP_K):
        hit = iota_e == idxs[k]
        w_ref[k:k + 1, :] = tops[k] / wsum * ROUTED_SCALE
        pos_ref[k:k + 1, :] = jnp.sum(jnp.where(hit, pos_all, 0.0), axis=0,
                                      keepdims=True).astype(I32)

    @pl.when(tile == 0)
    def _():
        cnt_ref[...] = jnp.zeros_like(cnt_ref)

    lane = lax.broadcasted_iota(I32, cnt_ref.shape, 1)
    cnt_ref[...] = jnp.where(lane == tile, jnp.broadcast_to(n, cnt_ref.shape).astype(I32),
                             cnt_ref[...])


def _topk_route(logits_t, router_b):
    ne, t = logits_t.shape
    tr = TOK_TILE
    assert t // tr <= 128
    tri = (jnp.arange(tr)[:, None] < jnp.arange(tr)[None, :]).astype(BF16)
    ltri = (jnp.arange(ne)[None, :] < jnp.arange(ne)[:, None]).astype(BF16)
    out = lambda dt: jax.ShapeDtypeStruct((TOP_K, t), dt)
    row = pl.BlockSpec((TOP_K, tr), lambda i: (0, i))
    return pl.pallas_call(
        _topk_kernel,
        grid=(t // tr,),
        in_specs=[pl.BlockSpec((ne, tr), lambda i: (0, i)),
                  pl.BlockSpec((ne, 1), lambda i: (0, 0)),
                  pl.BlockSpec((tr, tr), lambda i: (0, 0)),
                  pl.BlockSpec((ne, ne), lambda i: (0, 0))],
        out_specs=[row, row, pl.BlockSpec((ne, 128), lambda i: (0, 0))],
        out_shape=[out(F32), out(I32), jax.ShapeDtypeStruct((ne, 128), I32)],
        compiler_params=_cparams(("arbitrary",)),
    )(logits_t, router_b.reshape(ne, 1), tri, ltri)


def _slot_plan(counts, n_tiles, n_blocks):
    n = counts[:, :n_tiles]
    nch = (n + (CHUNK - 1)) // CHUNK
    rows = jnp.sum(nch, axis=1) * CHUNK
    region = ((rows + EXPERT_ROWS - 1) // EXPERT_ROWS) * EXPERT_ROWS
    region_end = jnp.cumsum(region)
    region_start = region_end - region
    run_chunk = region_start[:, None] // CHUNK + jnp.cumsum(nch, axis=1) - nch
    local_end = jnp.cumsum(nch, axis=0)
    local_chunk = local_end - nch
    c = jnp.arange(LOCAL_ROWS // CHUNK, dtype=I32)
    expert_of_c = jnp.sum((local_end[:, :, None] <= c[None, None, :]).astype(I32), axis=0)
    owner = expert_of_c[None] == jnp.arange(N_EXPERTS, dtype=I32)[:, None, None]
    slot_chunk = jnp.sum(jnp.where(owner, (run_chunk - local_chunk)[:, :, None], 0), axis=0) + c
    block_row = jnp.arange(n_blocks, dtype=I32) * EXPERT_ROWS
    group_e = jnp.minimum(jnp.sum((region_end[None, :] <= block_row[:, None]).astype(I32), axis=1),
                          N_EXPERTS - 1)
    n_used = region_end[-1] // EXPERT_ROWS
    nonempty = region > 0
    run_of_expert = jnp.cumsum(nonempty.astype(I32)) - 1
    experts = jnp.arange(N_EXPERTS, dtype=I32)
    run_expert = jnp.sum(jnp.where((run_of_expert[None, :] == experts[:, None]) & nonempty[None, :],
                                   experts[None, :], 0), axis=1)
    blocks = jnp.arange(n_blocks, dtype=I32)
    first = ((blocks == 0) | (group_e != jnp.roll(group_e, 1))) & (blocks < n_used)
    return dict(
        block_run=(jnp.cumsum(first.astype(I32)) - 1).astype(I32), block_first=first.astype(I32),
        run_expert=run_expert.astype(I32), n_runs=jnp.sum(nonempty.astype(I32)).reshape(1),
        slot_chunk=slot_chunk.reshape(-1).astype(I32),
        tile_chunks=jnp.sum(nch, axis=0).astype(I32),
        tail_chunk=((region_start + rows) // CHUNK).astype(I32),
        tail_chunks=((region - rows) // CHUNK).astype(I32),
        group_e=group_e, n_used=(region_end[-1:] // EXPERT_ROWS).astype(I32))


def _chunk_rows(ref, chunk_index):
    start = chunk_index * CHUNK
    if not isinstance(start, int):
        start = pl.multiple_of(start, CHUNK)
    return ref.at[pl.ds(start, CHUNK), :]


def _dispatch_kernel(slot_ref, tile_ref, tail_ref, tailn_ref, used_ref,
                     x_ref, sh_ref, sc_ref, pos_ref, xg_hbm, stage_ref, zero_ref, sem, sem_blk):
    tr = x_ref.shape[0]
    i = pl.program_id(0)
    n_blocks = xg_hbm.shape[0] // EXPERT_ROWS
    chunks_per_tile = LOCAL_ROWS // CHUNK

    def drain(count):
        def body(c, carry):
            pltpu.make_async_copy(_chunk_rows(zero_ref, 0), _chunk_rows(xg_hbm, 0), sem).wait()
            return carry
        lax.fori_loop(0, count, body, 0)

    @pl.when(i == 0)
    def _():
        zero_ref[...] = jnp.zeros_like(zero_ref)

        def block_copy(j):
            rows = pl.ds(pl.multiple_of(j * EXPERT_ROWS, EXPERT_ROWS), EXPERT_ROWS)
            return pltpu.make_async_copy(zero_ref, xg_hbm.at[rows, :], sem_blk)

        def start_block(j, carry):
            block_copy(j).start()
            return carry

        def wait_block(j, carry):
            block_copy(j).wait()
            return carry

        lax.fori_loop(used_ref[0], n_blocks, start_block, 0)

        def per_expert(e, total):
            def per_chunk(c, carry):
                pltpu.make_async_copy(_chunk_rows(zero_ref, 0),
                                      _chunk_rows(xg_hbm, tail_ref[e] + c), sem).start()
                return carry
            lax.fori_loop(0, tailn_ref[e], per_chunk, 0)
            return total + tailn_ref[e]

        drain(lax.fori_loop(0, N_EXPERTS, per_expert, 0))
        lax.fori_loop(used_ref[0], n_blocks, wait_block, 0)

    hf = (x_ref[...] * (1.0 + sc_ref[...]) + sh_ref[...]).astype(BF16)
    slot = lax.broadcasted_iota(I32, (LOCAL_ROWS, tr), 0)
    perm = jnp.zeros((LOCAL_ROWS, tr), F32)
    for k in range(TOP_K):
        perm = jnp.where(slot == pos_ref[k:k + 1, :], 1.0, perm)
    rows = jnp.dot(perm.astype(BF16), hf, preferred_element_type=F32)
    half = rows.shape[1] // 2
    buf = stage_ref.at[i % 2]
    buf[...] = (lax.bitcast_convert_type(rows[:, :half], U32)
                | (lax.bitcast_convert_type(rows[:, half:], U32) >> 16))

    @pl.when(i > 0)
    def _():
        drain(tile_ref[jnp.maximum(i - 1, 0)])

    def send(c, carry):
        pltpu.make_async_copy(_chunk_rows(buf, c),
                              _chunk_rows(xg_hbm, slot_ref[i * chunks_per_tile + c]), sem).start()
        return carry

    lax.fori_loop(0, tile_ref[i], send, 0)

    @pl.when(i == pl.num_programs(0) - 1)
    def _():
        drain(tile_ref[i])


def _dispatch(plan, x2, mod3, pos, n_slots, seq):
    t, d = x2.shape
    tr = TOK_TILE
    tpb = seq // tr
    return pl.pallas_call(
        _dispatch_kernel,
        grid_spec=pltpu.PrefetchScalarGridSpec(
            num_scalar_prefetch=5,
            grid=(t // tr,),
            in_specs=[pl.BlockSpec((tr, d), lambda i, *_: (i, 0)),
                      _mod_spec(3, d, tpb), _mod_spec(4, d, tpb),
                      pl.BlockSpec((TOP_K, tr), lambda i, *_: (0, i))],
            out_specs=pl.BlockSpec(memory_space=pl.ANY),
            scratch_shapes=[pltpu.VMEM((2, LOCAL_ROWS, d // 2), U32),
                            pltpu.VMEM((EXPERT_ROWS, d // 2), U32),
                            pltpu.SemaphoreType.DMA, pltpu.SemaphoreType.DMA],
        ),
        out_shape=jax.ShapeDtypeStruct((n_slots, d // 2), U32),
        compiler_params=_cparams(("arbitrary",)),
    )(plan["slot_chunk"], plan["tile_chunks"], plan["tail_chunk"], plan["tail_chunks"],
      plan["n_used"], x2, mod3, mod3, pos)


def _expert_kernel(run_ref, first_ref, rexp_ref, nrun_ref, nu_ref,
                   xg_hbm, wgu_hbm, wdn_hbm, yg_hbm,
                   xbuf, ybuf, wgu_f32, wdn_f32, wgu_bf, wdn_bf, x_sem, y_sem, w_sem, z_sem,
                   *, layer):
    n_used = nu_ref[0]
    n_blocks = yg_hbm.shape[0] // EXPERT_ROWS
    half = xbuf.shape[2]
    f = wdn_bf.shape[0]

    def rows(j):
        return pl.ds(pl.multiple_of(j * EXPERT_ROWS, EXPERT_ROWS), EXPERT_ROWS)

    def x_copy(j):
        return pltpu.make_async_copy(xg_hbm.at[rows(j), :], xbuf.at[j % 2], x_sem.at[j % 2])

    def y_copy(j):
        return pltpu.make_async_copy(ybuf.at[j % 2], yg_hbm.at[rows(j), :], y_sem.at[j % 2])

    def w_copies(r):
        e = rexp_ref[r]
        return (pltpu.make_async_copy(wgu_hbm.at[layer, e], wgu_f32.at[r % 2], w_sem.at[r % 2]),
                pltpu.make_async_copy(wdn_hbm.at[layer, e], wdn_f32.at[r % 2], w_sem.at[r % 2]))

    def zero_copy(j):
        return pltpu.make_async_copy(ybuf.at[0], yg_hbm.at[rows(j), :], z_sem)

    ybuf[0] = jnp.zeros(ybuf.shape[1:], U32)

    def zero_start(j, carry):
        zero_copy(j).start()
        return carry

    def zero_wait(j, carry):
        zero_copy(j).wait()
        return carry

    lax.fori_loop(n_used, n_blocks, zero_start, 0)
    lax.fori_loop(n_used, n_blocks, zero_wait, 0)

    x_copy(0).start()
    for cp in w_copies(0):
        cp.start()

    def block(j, carry):
        r = run_ref[j]

        @pl.when(first_ref[j] == 1)
        def _():
            for cp in w_copies(r):
                cp.wait()

            @pl.when(r + 1 < nrun_ref[0])
            def _():
                for cp in w_copies(r + 1):
                    cp.start()

            wgu_bf[...] = wgu_f32[r % 2].astype(BF16)
            wdn_bf[...] = wdn_f32[r % 2].astype(BF16)

        x_copy(j).wait()

        @pl.when(j + 1 < n_used)
        def _():
            x_copy(j + 1).start()

        @pl.when(j >= 2)
        def _():
            y_copy(j - 2).wait()

        hi, lo = _unpack_bf16_pairs(xbuf[j % 2])
        gu = (jnp.dot(hi.astype(BF16), wgu_bf[0:half, :], preferred_element_type=F32)
              + jnp.dot(lo.astype(BF16), wgu_bf[half:, :], preferred_element_type=F32))
        act = (_silu(gu[:, :f]) * gu[:, f:]).astype(BF16)
        ybuf[j % 2] = _pack_bf16_pairs(jnp.dot(act, wdn_bf[...], preferred_element_type=F32))
        y_copy(j).start()
        return carry

    lax.fori_loop(0, n_used, block, 0)

    @pl.when(n_used >= 2)
    def _():
        y_copy(n_used - 2).wait()

    y_copy(n_used - 1).wait()


def _expert_gemm(plan, xg, w_gu, w_down, layer):
    n_slots, half = xg.shape
    _, ne, d, f2 = w_gu.shape
    f = w_down.shape[2]
    hbm = pl.BlockSpec(memory_space=pl.ANY)
    return pl.pallas_call(
        functools.partial(_expert_kernel, layer=layer),
        grid_spec=pltpu.PrefetchScalarGridSpec(
            num_scalar_prefetch=5,
            grid=(1,),
            in_specs=[hbm, hbm, hbm],
            out_specs=hbm,
            scratch_shapes=[pltpu.VMEM((2, EXPERT_ROWS, half), U32),
                            pltpu.VMEM((2, EXPERT_ROWS, half), U32),
                            pltpu.VMEM((2, d, f2), F32), pltpu.VMEM((2, f, d), F32),
                            pltpu.VMEM((d, f2), BF16), pltpu.VMEM((f, d), BF16),
                            pltpu.SemaphoreType.DMA((2,)), pltpu.SemaphoreType.DMA((2,)),
                            pltpu.SemaphoreType.DMA((2,)), pltpu.SemaphoreType.DMA],
        ),
        out_shape=jax.ShapeDtypeStruct((n_slots, half), U32),
        compiler_params=_cparams(("arbitrary",)),
    )(plan["block_run"], plan["block_first"], plan["run_expert"], plan["n_runs"], plan["n_used"],
      xg, w_gu, w_down)


def _combine_kernel(slot_ref, tile_ref,
                    x_ref, sh_ref, sc_ref, g_ref, pos_ref, wt_ref, wsgu_ref, wsdn_ref,
                    lng_ref, lnb_ref, yg_hbm, o_ref, stage_ref, sems):
    tr, d = x_ref.shape
    i = pl.program_id(0)
    n_tiles = pl.num_programs(0)
    chunks_per_tile = LOCAL_ROWS // CHUNK

    def fetch(tile):
        def body(c, carry):
            pltpu.make_async_copy(_chunk_rows(yg_hbm, slot_ref[tile * chunks_per_tile + c]),
                                  _chunk_rows(stage_ref.at[tile % 2], c),
                                  sems.at[tile % 2]).start()
            return carry
        lax.fori_loop(0, tile_ref[tile], body, 0)

    @pl.when(i == 0)
    def _():
        stage_ref[...] = jnp.zeros_like(stage_ref)
        fetch(i)

    @pl.when(i + 1 < n_tiles)
    def _():
        fetch(jnp.minimum(i + 1, n_tiles - 1))

    x = x_ref[...]
    hf = (x * (1.0 + sc_ref[...]) + sh_ref[...]).astype(BF16)
    f = wsdn_ref.shape[0]
    su = jnp.dot(hf, wsgu_ref[...], preferred_element_type=F32)
    act = (_silu(su[:, :f]) * su[:, f:]).astype(BF16)
    shared = jnp.dot(act, wsdn_ref[...], preferred_element_type=F32)

    lane = lax.broadcasted_iota(I32, (tr, LOCAL_ROWS), 1)
    weights = jnp.zeros((tr, LOCAL_ROWS), F32)
    for k in range(TOP_K):
        weights = jnp.where(lane == pos_ref[:, k:k + 1], wt_ref[:, k:k + 1], weights)
    w_hi = weights.astype(BF16)
    w_lo = (weights - w_hi.astype(F32)).astype(BF16)
    w_both = jnp.concatenate([w_hi, w_lo], axis=0)

    buf = stage_ref.at[i % 2]

    def drain(c, carry):
        pltpu.make_async_copy(_chunk_rows(yg_hbm, 0), _chunk_rows(buf, 0), sems.at[i % 2]).wait()
        return carry

    lax.fori_loop(0, tile_ref[i], drain, 0)

    y_hi, y_lo = _unpack_bf16_pairs(buf[...])
    y = jnp.concatenate([y_hi.astype(BF16), y_lo.astype(BF16)], axis=1)
    both = jnp.dot(w_both, y, preferred_element_type=F32)
    routed = both[:tr, :] + both[tr:, :]
    v = ALPHA * x + (1.0 + g_ref[...]) * (routed + shared)
    o_ref[...] = _layer_norm(v, lng_ref[...], lnb_ref[...])


def _combine(plan, x2, mod3, pos_tok, w_tok, w_sh_gu, w_sh_down, ln_g, ln_b, yg, seq):
    t, d = x2.shape
    tr = TOK_TILE
    tpb = seq // tr
    full = lambda shape: pl.BlockSpec(shape, lambda i, *_: (0,) * len(shape))
    return pl.pallas_call(
        _combine_kernel,
        grid_spec=pltpu.PrefetchScalarGridSpec(
            num_scalar_prefetch=2,
            grid=(t // tr,),
            in_specs=[pl.BlockSpec((tr, d), lambda i, *_: (i, 0)),
                      _mod_spec(3, d, tpb), _mod_spec(4, d, tpb), _mod_spec(5, d, tpb),
                      pl.BlockSpec((tr, TOP_K), lambda i, *_: (i, 0)),
                      pl.BlockSpec((tr, TOP_K), lambda i, *_: (i, 0)),
                      full(w_sh_gu.shape), full(w_sh_down.shape), full((1, d)), full((1, d)),
                      pl.BlockSpec(memory_space=pl.ANY)],
            out_specs=pl.BlockSpec((tr, d), lambda i, *_: (i, 0)),
            scratch_shapes=[pltpu.VMEM((2, LOCAL_ROWS, d // 2), U32),
                            pltpu.SemaphoreType.DMA((2,))],
        ),
        out_shape=jax.ShapeDtypeStruct((t, d), F32),
        compiler_params=_cparams(("arbitrary",)),
    )(plan["slot_chunk"], plan["tile_chunks"],
      x2, mod3, mod3, mod3, pos_tok, w_tok, w_sh_gu, w_sh_down,
      ln_g.reshape(1, d), ln_b.reshape(1, d), yg)


def _moe_layer(x2, mod3, w_router, router_b, w_gu, w_down, layer, w_sh_gu, w_sh_down,
               ln_g, ln_b, seq):
    t, d = x2.shape
    n_tiles = t // TOK_TILE
    logits = _router(x2, mod3, w_router, seq)
    w_top, pos, counts = _topk_route(logits.T, router_b)
    bound = t * TOP_K + n_tiles * N_EXPERTS * (CHUNK - 1) + N_EXPERTS * (EXPERT_ROWS - 1)
    n_blocks = -(-bound // EXPERT_ROWS)
    plan = _slot_plan(counts, n_tiles, n_blocks)
    xg = _dispatch(plan, x2, mod3, pos, n_blocks * EXPERT_ROWS, seq)
    yg = _expert_gemm(plan, xg, w_gu, w_down, layer)
    return _combine(plan, x2, mod3, pos.T, w_top.T, w_sh_gu, w_sh_down, ln_g, ln_b, yg, seq)


def kernel(x, c, ada_w, ada_b, pool_w_in, pool_w_grp, pool_scale, pool_w_out, attn_w_in, attn_w_out, ln1_g, ln1_b, router_w, router_b, exp_w_gu, exp_w_down, sh_w_gu, sh_w_down, ln2_g, ln2_b):
    batch, seq, d = x.shape
    depth = ada_w.shape[0]
    t = batch * seq
    mod = _modulation(c, ada_w, ada_b).reshape(depth, batch, 1, 6 * d)
    perms = jnp.stack([_perm_matrix(dil) for _, dil in ATTN_PATTERNS[1:]])
    perms_t = jnp.swapaxes(perms, 1, 2)
    expand = (jnp.arange(LSE_LANES)[:, None] == (jnp.arange(d)[None, :] // HEAD_DIM)).astype(BF16)
    x2 = x.reshape(t, d)
    for i in range(depth):
        mod3 = mod[i]
        j = i // 2
        if i % 2 == 0:
            x2 = _pool_layer(x2, mod3, pool_w_in[j].astype(BF16), pool_w_grp[j].astype(BF16),
                             pool_scale[j], pool_w_out[j].astype(BF16), ln1_g[i], ln1_b[i], seq)
        else:
            qkv = _qkv_proj(x2, mod3, perms, attn_w_in[j].astype(BF16), seq)
            res = [_attention_group(qkv, g, batch, seq) for g in range(len(ATTN_PATTERNS))]
            x2 = _attn_out(x2, mod3, [r[0] for r in res], [r[1] for r in res], perms_t, expand,
                           attn_w_out[j].astype(BF16), ln1_g[i], ln1_b[i], seq)
        x2 = _moe_layer(x2, mod3, router_w[i], router_b[i], exp_w_gu, exp_w_down, i,
                        sh_w_gu[i].astype(BF16), sh_w_down[i].astype(BF16),
                        ln2_g[i], ln2_b[i], seq)
    return x2.reshape(batch, seq, d)
```

```python
import functools
import math

import jax
import jax.numpy as jnp
from jax import lax
from jax.experimental import pallas as pl
from jax.experimental.pallas import tpu as pltpu

F32 = jnp.float32
BF16 = jnp.bfloat16
U32 = jnp.uint32
I32 = jnp.int32

POOL_WINDOWS = (2, 4, 8, 16)
ATTN_PATTERNS = ((128, 1), (512, 4), (2048, 16))
HEAD_DIM = 64
N_HEADS = 16
Q_BLOCK = 128
N_EXPERTS = 64
TOP_K = 8
N_EXPERT_GROUPS = 8
TOPK_GROUPS = 4
ROUTED_SCALE = 2.5
EXPERT_ROWS = 256
DEPTH = 4
ALPHA = (2 * DEPTH) ** 0.25
LN_EPS = 1e-5

PERM_TILE = 256
POOL_HALO = 16
TOK_TILE = 256
CHUNK = 8
LOCAL_ROWS = -(-(TOK_TILE * TOP_K + N_EXPERTS * (CHUNK - 1)) // 256) * 256
GEMM_DEPTH = 4
LSE_LANES = 128
VMEM_LIMIT = 56 * 1024 * 1024
NEG_BIG = -1e30


def _cparams(sem):
    return pltpu.CompilerParams(dimension_semantics=sem, vmem_limit_bytes=VMEM_LIMIT)


def _layer_norm(v, g, b):
    mu = jnp.mean(v, axis=-1, keepdims=True)
    c = v - mu
    var = jnp.mean(c * c, axis=-1, keepdims=True)
    return c * lax.rsqrt(var + LN_EPS) * g + b


def _silu(v):
    return v * (1.0 / (1.0 + jnp.exp(-v)))


def _pack_bf16_pairs(v):
    n = v.shape[1] // 2
    hi = lax.bitcast_convert_type(v[:, :n].astype(BF16).astype(F32), U32)
    lo = lax.bitcast_convert_type(v[:, n:].astype(BF16).astype(F32), U32)
    return hi | (lo >> 16)


def _unpack_bf16_pairs(p):
    hi = lax.bitcast_convert_type(p & jnp.uint32(0xFFFF0000), F32)
    lo = lax.bitcast_convert_type(p << 16, F32)
    return hi, lo


def _mod_kernel(c_ref, w_ref, b_ref, o_ref):
    cs = _silu(c_ref[...])
    o_ref[...] = jnp.dot(cs, w_ref[...], preferred_element_type=F32) + b_ref[...]


def _modulation(c, ada_w, ada_b):
    depth, d, n6 = ada_w.shape
    b = c.shape[0]
    tn = 1536
    return pl.pallas_call(
        _mod_kernel,
        grid=(depth, n6 // tn),
        in_specs=[
            pl.BlockSpec((b, d), lambda i, n: (0, 0)),
            pl.BlockSpec((None, d, tn), lambda i, n: (i, 0, n)),
            pl.BlockSpec((None, 1, tn), lambda i, n: (i, 0, n)),
        ],
        out_specs=pl.BlockSpec((None, b, tn), lambda i, n: (i, 0, n)),
        out_shape=jax.ShapeDtypeStruct((depth, b, n6), F32),
        compiler_params=_cparams(("arbitrary", "arbitrary")),
    )(c, ada_w, ada_b.reshape(depth, 1, n6))


def _mod_spec(chunk, d, tiles_per_batch):
    return pl.BlockSpec((None, 1, d), lambda *idx: (idx[0] // tiles_per_batch, 0, chunk))


def _pool_kernel(x_ref, sh_ref, sc_ref, g_ref, win_ref, wgrp_ref, cs_ref, wout_ref,
                 lng_ref, lnb_ref, o_ref, ext_ref, *, tiles_per_batch):
    tm, d = x_ref.shape
    s_idx = pl.program_id(0) % tiles_per_batch
    x = x_ref[...]
    h = (x * (1.0 + sc_ref[...]) + sh_ref[...]).astype(BF16)
    u = jnp.dot(h, win_ref[...], preferred_element_type=F32)

    @pl.when(s_idx == 0)
    def _():
        ext_ref[0:POOL_HALO, :] = jnp.zeros((POOL_HALO, d), F32)

    @pl.when(s_idx != 0)
    def _():
        ext_ref[0:POOL_HALO, :] = ext_ref[tm:tm + POOL_HALO, :]

    ext_ref[POOL_HALO:POOL_HALO + tm, :] = u

    pos = s_idx * tm + lax.broadcasted_iota(I32, (tm, 1), 0) + 1
    gc = d // len(POOL_WINDOWS)
    ys = []
    for g, w in enumerate(POOL_WINDOWS):
        cols = slice(g * gc, (g + 1) * gc)
        acc = u[:, cols]
        for j in range(1, w):
            acc = acc + ext_ref[POOL_HALO - j:POOL_HALO - j + tm, cols]
        cnt = jnp.minimum(pos, w).astype(F32)
        z = (acc / cnt - u[:, cols]).astype(BF16)
        ys.append(jnp.dot(z, wgrp_ref[g], preferred_element_type=F32))
    y = (jnp.concatenate(ys, axis=1) * cs_ref[...]).astype(BF16)
    out = jnp.dot(y, wout_ref[...], preferred_element_type=F32)
    v = ALPHA * x + (1.0 + g_ref[...]) * out
    o_ref[...] = _layer_norm(v, lng_ref[...], lnb_ref[...])


def _pool_layer(x2, mod3, w_in, w_grp, ch_scale, w_out, ln_g, ln_b, seq):
    t, d = x2.shape
    tm = 512
    tpb = seq // tm
    full = lambda shape: pl.BlockSpec(shape, lambda i: (0,) * len(shape))
    return pl.pallas_call(
        functools.partial(_pool_kernel, tiles_per_batch=tpb),
        grid=(t // tm,),
        in_specs=[
            pl.BlockSpec((tm, d), lambda i: (i, 0)),
            _mod_spec(0, d, tpb), _mod_spec(1, d, tpb), _mod_spec(2, d, tpb),
            full((d, d)), full(w_grp.shape), full((1, d)), full((d, d)),
            full((1, d)), full((1, d)),
        ],
        out_specs=pl.BlockSpec((tm, d), lambda i: (i, 0)),
        out_shape=jax.ShapeDtypeStruct((t, d), F32),
        scratch_shapes=[pltpu.VMEM((tm + POOL_HALO, d), F32)],
        compiler_params=_cparams(("arbitrary",)),
    )(x2, mod3, mod3, mod3, w_in, w_grp, ch_scale.reshape(1, d), w_out,
      ln_g.reshape(1, d), ln_b.reshape(1, d))


def _perm_matrix(dil):
    p = jnp.arange(PERM_TILE)
    chunk = PERM_TILE // dil
    src = (p % chunk) * dil + p // chunk
    return (src[:, None] == jnp.arange(PERM_TILE)[None, :]).astype(BF16)


def _qkv_kernel(x_ref, sh_ref, sc_ref, p_ref, w_ref, o_ref, h_ref):
    tm = x_ref.shape[0]
    g = pl.program_id(1)
    part = pl.program_id(2)

    @pl.when((g == 0) & (part == 0))
    def _():
        h = (x_ref[...] * (1.0 + sc_ref[...]) + sh_ref[...]).astype(BF16)
        h_ref[0] = h
        for gi in range(1, len(ATTN_PATTERNS)):
            for s in range(tm // PERM_TILE):
                rows = slice(s * PERM_TILE, (s + 1) * PERM_TILE)
                h_ref[gi, rows, :] = jnp.dot(
                    p_ref[gi - 1], h[rows, :], preferred_element_type=F32).astype(BF16)

    o_ref[...] = jnp.dot(h_ref[g], w_ref[...], preferred_element_type=F32).astype(BF16)


def _qkv_proj(x2, mod3, perms, w_in, seq):
    t, d = x2.shape
    ng = len(ATTN_PATTERNS)
    tm = 1024
    tpb = seq // tm
    return pl.pallas_call(
        _qkv_kernel,
        grid=(t // tm, ng, 3),
        in_specs=[
            pl.BlockSpec((tm, d), lambda m, g, p: (m, 0)),
            _mod_spec(0, d, tpb), _mod_spec(1, d, tpb),
            pl.BlockSpec(perms.shape, lambda m, g, p: (0, 0, 0)),
            pl.BlockSpec((d, d), lambda m, g, p: (0, g * 3 + p)),
        ],
        out_specs=pl.BlockSpec((tm, d), lambda m, g, p: (m, g * 3 + p)),
        out_shape=jax.ShapeDtypeStruct((t, ng * 3 * d), BF16),
        scratch_shapes=[pltpu.VMEM((ng, tm, d), BF16)],
        compiler_params=_cparams(("arbitrary", "arbitrary", "arbitrary")),
    )(x2, mod3, mod3, perms, w_in)


def _attn_kernel(q_ref, kp_ref, kc_ref, vp_ref, vc_ref, o_ref, lse_ref, *, group, dil):
    bq = Q_BLOCK
    d = N_HEADS * HEAD_DIM
    j = pl.program_id(2)
    q = q_ref[...].reshape(bq, d)
    kp = kp_ref[...].reshape(bq, d)
    kc = kc_ref[...].reshape(bq, d)
    vp = vp_ref[...].reshape(bq, d)
    vc = vc_ref[...].reshape(bq, d)

    qi = lax.broadcasted_iota(I32, (bq, 2 * bq), 0)
    kj = lax.broadcasted_iota(I32, (bq, 2 * bq), 1)
    dist = qi + bq - kj
    steps = ATTN_PATTERNS[group][0] // dil
    valid = (dist >= 0) & (dist <= steps) & ((kj >= bq) | (j > 0))
    distf = dist.astype(F32)
    lane = lax.broadcasted_iota(I32, (bq, LSE_LANES), 1)
    lse_tile = jnp.zeros((bq, LSE_LANES), F32)
    n_tot = len(ATTN_PATTERNS) * N_HEADS
    outs = []
    for h in range(N_HEADS):
        cols = slice(h * HEAD_DIM, (h + 1) * HEAD_DIM)
        slope = 2.0 ** (-8.0 * (group * N_HEADS + h + 1) / n_tot)
        kh = jnp.concatenate([kp[:, cols], kc[:, cols]], axis=0)
        vh = jnp.concatenate([vp[:, cols], vc[:, cols]], axis=0)
        s = lax.dot_general(q[:, cols], kh, (((1,), (1,)), ((), ())),
                            preferred_element_type=F32)
        s = s * (HEAD_DIM ** -0.5) - (slope * dil) * distf
        s = jnp.where(valid, s, NEG_BIG)
        m = jnp.max(s, axis=-1, keepdims=True)
        p = jnp.exp(s - m)
        den = jnp.sum(p, axis=-1, keepdims=True)
        o = jnp.dot(p.astype(BF16), vh, preferred_element_type=F32) / den
        outs.append(o)
        lse_tile = jnp.where(lane == h, m + jnp.log(den), lse_tile)
    o_ref[...] = jnp.concatenate(outs, axis=1).astype(BF16).reshape(o_ref.shape)
    lse_ref[...] = lse_tile.reshape(lse_ref.shape)


def _attention_group(qkv, group, batch, seq):
    dil = ATTN_PATTERNS[group][1]
    d = N_HEADS * HEAD_DIM
    t = qkv.shape[0]
    sub = seq // dil
    nb = sub // Q_BLOCK
    rows = Q_BLOCK if dil == 1 else PERM_TILE // dil
    chunks = Q_BLOCK // rows
    u = seq // (rows * dil)
    view = lambda a, c: a.reshape(batch, u, dil, rows, c)
    blk = lambda c: (None, chunks, None, rows, c)
    col0 = group * 3
    q_spec = pl.BlockSpec(blk(d), lambda b, r, j: (b, j, r, 0, col0))
    kc_spec = pl.BlockSpec(blk(d), lambda b, r, j: (b, j, r, 0, col0 + 1))
    kp_spec = pl.BlockSpec(blk(d), lambda b, r, j: (b, jnp.maximum(j - 1, 0), r, 0, col0 + 1))
    vc_spec = pl.BlockSpec(blk(d), lambda b, r, j: (b, j, r, 0, col0 + 2))
    vp_spec = pl.BlockSpec(blk(d), lambda b, r, j: (b, jnp.maximum(j - 1, 0), r, 0, col0 + 2))
    qkv5 = view(qkv, qkv.shape[1])
    o, lse = pl.pallas_call(
        functools.partial(_attn_kernel, group=group, dil=dil),
        grid=(batch, dil, nb),
        in_specs=[q_spec, kp_spec, kc_spec, vp_spec, vc_spec],
        out_specs=[
            pl.BlockSpec(blk(d), lambda b, r, j: (b, j, r, 0, 0)),
            pl.BlockSpec(blk(LSE_LANES), lambda b, r, j: (b, j, r, 0, 0)),
        ],
        out_shape=[
            jax.ShapeDtypeStruct((batch, u, dil, rows, d), BF16),
            jax.ShapeDtypeStruct((batch, u, dil, rows, LSE_LANES), F32),
        ],
        compiler_params=_cparams(("arbitrary", "arbitrary", "arbitrary")),
    )(qkv5, qkv5, qkv5, qkv5, qkv5)
    return o.reshape(t, d), lse.reshape(t, LSE_LANES)


def _split3(v):
    a = v.astype(BF16)
    r = v - a.astype(F32)
    b = r.astype(BF16)
    c = (r - b.astype(F32)).astype(BF16)
    return a, b, c


def _attn_out_kernel(x_ref, g_ref, o0_ref, o1_ref, o2_ref, l0_ref, l1_ref, l2_ref,
                     pt_ref, e_ref, wout_ref, lng_ref, lnb_ref, out_ref):
    tm, d = x_ref.shape
    o_refs = (o0_ref, o1_ref, o2_ref)
    l_refs = (l0_ref, l1_ref, l2_ref)
    n_sub = tm // PERM_TILE

    def unperm(gi, val_bf16):
        if gi == 0:
            return val_bf16.astype(F32)
        parts = [jnp.dot(pt_ref[gi - 1], val_bf16[s * PERM_TILE:(s + 1) * PERM_TILE, :],
                         preferred_element_type=F32) for s in range(n_sub)]
        return jnp.concatenate(parts, axis=0)

    lses = []
    for gi in range(3):
        l = l_refs[gi][...]
        if gi == 0:
            lses.append(l)
        else:
            a, b, c = _split3(l)
            lses.append(unperm(gi, a) + unperm(gi, b) + unperm(gi, c))
    mx = jnp.maximum(jnp.maximum(lses[0], lses[1]), lses[2])
    es = [jnp.exp(l - mx) for l in lses]
    tot = es[0] + es[1] + es[2]
    mixed = jnp.zeros((tm, d), F32)
    for gi in range(3):
        w = es[gi] / tot
        a, b, c = _split3(w)
        wide = (jnp.dot(a, e_ref[...], preferred_element_type=F32)
                + jnp.dot(b, e_ref[...], preferred_element_type=F32)
                + jnp.dot(c, e_ref[...], preferred_element_type=F32))
        mixed = mixed + wide * unperm(gi, o_refs[gi][...])
    y = jnp.dot(mixed.astype(BF16), wout_ref[...], preferred_element_type=F32)
    v = ALPHA * x_ref[...] + (1.0 + g_ref[...]) * y
    out_ref[...] = _layer_norm(v, lng_ref[...], lnb_ref[...])


def _attn_out(x2, mod3, os_, lses, perms_t, expand, w_out, ln_g, ln_b, seq):
    t, d = x2.shape
    tm = 512
    tpb = seq // tm
    row = lambda c: pl.BlockSpec((tm, c), lambda i: (i, 0))
    full = lambda shape: pl.BlockSpec(shape, lambda i: (0,) * len(shape))
    return pl.pallas_call(
        _attn_out_kernel,
        grid=(t // tm,),
        in_specs=[row(d), _mod_spec(2, d, tpb), row(d), row(d), row(d),
                  row(LSE_LANES), row(LSE_LANES), row(LSE_LANES),
                  full(perms_t.shape), full(expand.shape), full((d, d)),
                  full((1, d)), full((1, d))],
        out_specs=row(d),
        out_shape=jax.ShapeDtypeStruct((t, d), F32),
        compiler_params=_cparams(("arbitrary",)),
    )(x2, mod3, *os_, *lses, perms_t, expand, w_out, ln_g.reshape(1, d), ln_b.reshape(1, d))


def _router_kernel(x_ref, sh_ref, sc_ref, wr_ref, lg_ref):
    hf = x_ref[...] * (1.0 + sc_ref[...]) + sh_ref[...]
    lg_ref[...] = jnp.dot(hf, wr_ref[...], preferred_element_type=F32,
                          precision=lax.Precision.HIGHEST)


def _router(x2, mod3, w_router, seq):
    t, d = x2.shape
    tm = 512
    tpb = seq // tm
    ne = w_router.shape[1]
    return pl.pallas_call(
        _router_kernel,
        grid=(t // tm,),
        in_specs=[pl.BlockSpec((tm, d), lambda i: (i, 0)),
                  _mod_spec(3, d, tpb), _mod_spec(4, d, tpb),
                  pl.BlockSpec((d, ne), lambda i: (0, 0))],
        out_specs=pl.BlockSpec((tm, ne), lambda i: (i, 0)),
        out_shape=jax.ShapeDtypeStruct((t, ne), F32),
        compiler_params=_cparams(("arbitrary",)),
    )(x2, mod3, mod3, w_router)


def _first_index_of_max(v, iota, size):
    m = jnp.max(v, axis=0, keepdims=True)
    idx = jnp.min(jnp.where(v == m, iota, float(size)), axis=0, keepdims=True)
    return m, idx


def _topk_kernel(lg_ref, b_ref, tri_ref, ltri_ref, w_ref, pos_ref, cnt_ref):
    ne, tr = lg_ref.shape
    gsz = ne // N_EXPERT_GROUPS
    tile = pl.program_id(0)

    scores = 1.0 / (1.0 + jnp.exp(-lg_ref[...]))
    sel = scores + b_ref[...]
    iota_g = lax.broadcasted_iota(I32, (gsz, tr), 0).astype(F32)
    iota_n = lax.broadcasted_iota(I32, (N_EXPERT_GROUPS, tr), 0).astype(F32)
    gs = jnp.zeros((N_EXPERT_GROUPS, tr), F32)
    for g in range(N_EXPERT_GROUPS):
        blk = sel[g * gsz:(g + 1) * gsz, :]
        m1, i1 = _first_index_of_max(blk, iota_g, gsz)
        m2 = jnp.max(jnp.where(iota_g == i1, -jnp.inf, blk), axis=0, keepdims=True)
        gs = jnp.where(iota_n == float(g), m1 + m2, gs)
    gmask = jnp.zeros((N_EXPERT_GROUPS, tr), F32)
    for _ in range(TOPK_GROUPS):
        _, gi = _first_index_of_max(gs, iota_n, N_EXPERT_GROUPS)
        hit = iota_n == gi
        gmask = jnp.where(hit, 1.0, gmask)
        gs = jnp.where(hit, -jnp.inf, gs)
    masked_rows = []
    for g in range(N_EXPERT_GROUPS):
        keep = jnp.broadcast_to(gmask[g:g + 1, :], (gsz, tr)) > 0.5
        masked_rows.append(jnp.where(keep, sel[g * gsz:(g + 1) * gsz, :], -jnp.inf))
    cur = jnp.concatenate(masked_rows, axis=0)
    iota_e = lax.broadcasted_iota(I32, (ne, tr), 0).astype(F32)
    chosen = jnp.zeros((ne, tr), F32)
    idxs, tops = [], []
    for _ in range(TOP_K):
        _, ei = _first_index_of_max(cur, iota_e, ne)
        hit = iota_e == ei
        cur = jnp.where(hit, -jnp.inf, cur)
        chosen = jnp.where(hit, 1.0, chosen)
        idxs.append(ei)
        tops.append(jnp.sum(jnp.where(hit, scores, 0.0), axis=0, keepdims=True))
    wsum = tops[0]
    for k in range(1, TOP_K):
        wsum = wsum + tops[k]
    before = jnp.dot(chosen.astype(BF16), tri_ref[...], preferred_element_type=F32)
    n = jnp.sum(chosen, axis=1, keepdims=True)
    n_chunks = jnp.floor((n + (CHUNK - 1)) * (1.0 / CHUNK))
    run_off = jnp.dot(ltri_ref[...], jnp.broadcast_to(n_chunks, (ne, 128)).astype(BF16),
                      preferred_element_type=F32)[:, 0:1] * CHUNK
    pos_all = before + run_off
    for k in range(TOP_K):
        hit = iota_e == idxs[k]
        w_ref[k:k + 1, :] = tops[k] / wsum * ROUTED_SCALE
        pos_ref[k:k + 1, :] = jnp.sum(jnp.where(hit, pos_all, 0.0), axis=0,
                                      keepdims=True).astype(I32)

    @pl.when(tile == 0)
    def _():
        cnt_ref[...] = jnp.zeros_like(cnt_ref)

    lane = lax.broadcasted_iota(I32, cnt_ref.shape, 1)
    cnt_ref[...] = jnp.where(lane == tile, jnp.broadcast_to(n, cnt_ref.shape).astype(I32),
                             cnt_ref[...])


def _topk_route(logits_t, router_b):
    ne, t = logits_t.shape
    tr = TOK_TILE
    assert t // tr <= 128
    tri = (jnp.arange(tr)[:, None] < jnp.arange(tr)[None, :]).astype(BF16)
    ltri = (jnp.arange(ne)[None, :] < jnp.arange(ne)[:, None]).astype(BF16)
    out = lambda dt: jax.ShapeDtypeStruct((TOP_K, t), dt)
    row = pl.BlockSpec((TOP_K, tr), lambda i: (0, i))
    return pl.pallas_call(
        _topk_kernel,
        grid=(t // tr,),
        in_specs=[pl.BlockSpec((ne, tr), lambda i: (0, i)),
                  pl.BlockSpec((ne, 1), lambda i: (0, 0)),
                  pl.BlockSpec((tr, tr), lambda i: (0, 0)),
                  pl.BlockSpec((ne, ne), lambda i: (0, 0))],
        out_specs=[row, row, pl.BlockSpec((ne, 128), lambda i: (0, 0))],
        out_shape=[out(F32), out(I32), jax.ShapeDtypeStruct((ne, 128), I32)],
        compiler_params=_cparams(("arbitrary",)),
    )(logits_t, router_b.reshape(ne, 1), tri, ltri)


def _slot_plan(counts, n_tiles, n_blocks):
    n = counts[:, :n_tiles]
    nch = (n + (CHUNK - 1)) // CHUNK
    rows = jnp.sum(nch, axis=1) * CHUNK
    region = ((rows + EXPERT_ROWS - 1) // EXPERT_ROWS) * EXPERT_ROWS
    region_end = jnp.cumsum(region)
    region_start = region_end - region
    run_chunk = region_start[:, None] // CHUNK + jnp.cumsum(nch, axis=1) - nch
    local_end = jnp.cumsum(nch, axis=0)
    local_chunk = local_end - nch
    c = jnp.arange(LOCAL_ROWS // CHUNK, dtype=I32)
    expert_of_c = jnp.sum((local_end[:, :, None] <= c[None, None, :]).astype(I32), axis=0)
    owner = expert_of_c[None] == jnp.arange(N_EXPERTS, dtype=I32)[:, None, None]
    slot_chunk = jnp.sum(jnp.where(owner, (run_chunk - local_chunk)[:, :, None], 0), axis=0) + c
    block_row = jnp.arange(n_blocks, dtype=I32) * EXPERT_ROWS
    group_e = jnp.minimum(jnp.sum((region_end[None, :] <= block_row[:, None]).astype(I32), axis=1),
                          N_EXPERTS - 1)
    n_used = region_end[-1] // EXPERT_ROWS
    nonempty = region > 0
    run_of_expert = jnp.cumsum(nonempty.astype(I32)) - 1
    experts = jnp.arange(N_EXPERTS, dtype=I32)
    run_expert = jnp.sum(jnp.where((run_of_expert[None, :] == experts[:, None]) & nonempty[None, :],
                                   experts[None, :], 0), axis=1)
    blocks = jnp.arange(n_blocks, dtype=I32)
    first = ((blocks == 0) | (group_e != jnp.roll(group_e, 1))) & (blocks < n_used)
    return dict(
        block_run=(jnp.cumsum(first.astype(I32)) - 1).astype(I32), block_first=first.astype(I32),
        run_expert=run_expert.astype(I32), n_runs=jnp.sum(nonempty.astype(I32)).reshape(1),
        slot_chunk=slot_chunk.reshape(-1).astype(I32),
        tile_chunks=jnp.sum(nch, axis=0).astype(I32),
        tail_chunk=((region_start + rows) // CHUNK).astype(I32),
        tail_chunks=((region - rows) // CHUNK).astype(I32),
        group_e=group_e, n_used=(region_end[-1:] // EXPERT_ROWS).astype(I32))


def _chunk_rows(ref, chunk_index):
    start = chunk_index * CHUNK
    if not isinstance(start, int):
        start = pl.multiple_of(start, CHUNK)
    return ref.at[pl.ds(start, CHUNK), :]


def _dispatch_kernel(slot_ref, tile_ref, tail_ref, tailn_ref, used_ref,
                     x_ref, sh_ref, sc_ref, pos_ref, xg_hbm, stage_ref, zero_ref, sems, sem_blk):
    tr = x_ref.shape[0]
    i = pl.program_id(0)
    n_blocks = xg_hbm.shape[0] // EXPERT_ROWS
    chunks_per_tile = LOCAL_ROWS // CHUNK

    def drain(count, sem):
        def body(c, carry):
            pltpu.make_async_copy(_chunk_rows(zero_ref, 0), _chunk_rows(xg_hbm, 0), sem).wait()
            return carry
        lax.fori_loop(0, count, body, 0)

    @pl.when(i == 0)
    def _():
        zero_ref[...] = jnp.zeros_like(zero_ref)

        def block_copy(j):
            rows = pl.ds(pl.multiple_of(j * EXPERT_ROWS, EXPERT_ROWS), EXPERT_ROWS)
            return pltpu.make_async_copy(zero_ref, xg_hbm.at[rows, :], sem_blk)

        def start_block(j, carry):
            block_copy(j).start()
            return carry

        def wait_block(j, carry):
            block_copy(j).wait()
            return carry

        lax.fori_loop(used_ref[0], n_blocks, start_block, 0)

        def per_expert(e, total):
            def per_chunk(c, carry):
                pltpu.make_async_copy(_chunk_rows(zero_ref, 0),
                                      _chunk_rows(xg_hbm, tail_ref[e] + c), sems.at[0]).start()
                return carry
            lax.fori_loop(0, tailn_ref[e], per_chunk, 0)
            return total + tailn_ref[e]

        drain(lax.fori_loop(0, N_EXPERTS, per_expert, 0), sems.at[0])
        lax.fori_loop(used_ref[0], n_blocks, wait_block, 0)

    hf = (x_ref[...] * (1.0 + sc_ref[...]) + sh_ref[...]).astype(BF16)
    slot = lax.broadcasted_iota(I32, (LOCAL_ROWS, tr), 0)
    perm = jnp.zeros((LOCAL_ROWS, tr), F32)
    for k in range(TOP_K):
        perm = jnp.where(slot == pos_ref[k:k + 1, :], 1.0, perm)
    rows = jnp.dot(perm.astype(BF16), hf, preferred_element_type=F32)
    half = rows.shape[1] // 2
    buf = stage_ref.at[i % 2]
    buf[...] = (lax.bitcast_convert_type(rows[:, :half], U32)
                | (lax.bitcast_convert_type(rows[:, half:], U32) >> 16))

    def send(c, carry):
        pltpu.make_async_copy(_chunk_rows(buf, c),
                              _chunk_rows(xg_hbm, slot_ref[i * chunks_per_tile + c]),
                              sems.at[i % 2]).start()
        return carry

    lax.fori_loop(0, tile_ref[i], send, 0)

    @pl.when(i > 0)
    def _():
        drain(tile_ref[jnp.maximum(i - 1, 0)], sems.at[(i + 1) % 2])

    @pl.when(i == pl.num_programs(0) - 1)
    def _():
        drain(tile_ref[i], sems.at[i % 2])


def _dispatch(plan, x2, mod3, pos, n_slots, seq):
    t, d = x2.shape
    tr = TOK_TILE
    tpb = seq // tr
    return pl.pallas_call(
        _dispatch_kernel,
        grid_spec=pltpu.PrefetchScalarGridSpec(
            num_scalar_prefetch=5,
            grid=(t // tr,),
            in_specs=[pl.BlockSpec((tr, d), lambda i, *_: (i, 0)),
                      _mod_spec(3, d, tpb), _mod_spec(4, d, tpb),
                      pl.BlockSpec((TOP_K, tr), lambda i, *_: (0, i))],
            out_specs=pl.BlockSpec(memory_space=pl.ANY),
            scratch_shapes=[pltpu.VMEM((2, LOCAL_ROWS, d // 2), U32),
                            pltpu.VMEM((EXPERT_ROWS, d // 2), U32),
                            pltpu.SemaphoreType.DMA((2,)), pltpu.SemaphoreType.DMA],
        ),
        out_shape=jax.ShapeDtypeStruct((n_slots, d // 2), U32),
        compiler_params=_cparams(("arbitrary",)),
    )(plan["slot_chunk"], plan["tile_chunks"], plan["tail_chunk"], plan["tail_chunks"],
      plan["n_used"], x2, mod3, mod3, pos)


def _expert_kernel(run_ref, first_ref, rexp_ref, nrun_ref, nu_ref,
                   xg_hbm, wgu_hbm, wdn_hbm, yg_hbm,
                   xbuf, ybuf, wgu_f32, wdn_f32, wgu_bf, wdn_bf, x_sem, y_sem, w_sem, z_sem,
                   *, layer):
    n_used = nu_ref[0]
    n_blocks = yg_hbm.shape[0] // EXPERT_ROWS
    depth, _, half = xbuf.shape
    f = wdn_bf.shape[0]

    def rows(j):
        return pl.ds(pl.multiple_of(j * EXPERT_ROWS, EXPERT_ROWS), EXPERT_ROWS)

    def x_copy(j):
        return pltpu.make_async_copy(xg_hbm.at[rows(j), :], xbuf.at[j % depth],
                                     x_sem.at[j % depth])

    def y_copy(j):
        return pltpu.make_async_copy(ybuf.at[j % depth], yg_hbm.at[rows(j), :],
                                     y_sem.at[j % depth])

    def w_copies(r):
        e = rexp_ref[r]
        return (pltpu.make_async_copy(wgu_hbm.at[layer, e], wgu_f32.at[r % 2], w_sem.at[r % 2]),
                pltpu.make_async_copy(wdn_hbm.at[layer, e], wdn_f32.at[r % 2], w_sem.at[r % 2]))

    def zero_copy(j):
        return pltpu.make_async_copy(ybuf.at[0], yg_hbm.at[rows(j), :], z_sem)

    ybuf[0] = jnp.zeros(ybuf.shape[1:], U32)

    def zero_start(j, carry):
        zero_copy(j).start()
        return carry

    def zero_wait(j, carry):
        zero_copy(j).wait()
        return carry

    lax.fori_loop(n_used, n_blocks, zero_start, 0)
    lax.fori_loop(n_used, n_blocks, zero_wait, 0)

    for cp in w_copies(0):
        cp.start()
    for a in range(depth - 1):
        @pl.when(a < n_used)
        def _():
            x_copy(a).start()

    def block(j, carry):
        r = run_ref[j]

        @pl.when(first_ref[j] == 1)
        def _():
            for cp in w_copies(r):
                cp.wait()

            @pl.when(r + 1 < nrun_ref[0])
            def _():
                for cp in w_copies(r + 1):
                    cp.start()

            wgu_bf[...] = wgu_f32[r % 2].astype(BF16)
            wdn_bf[...] = wdn_f32[r % 2].astype(BF16)

        x_copy(j).wait()

        @pl.when(j + (depth - 1) < n_used)
        def _():
            x_copy(j + (depth - 1)).start()

        @pl.when(j >= depth)
        def _():
            y_copy(j - depth).wait()

        hi, lo = _unpack_bf16_pairs(xbuf[j % depth])
        gu = (jnp.dot(hi.astype(BF16), wgu_bf[0:half, :], preferred_element_type=F32)
              + jnp.dot(lo.astype(BF16), wgu_bf[half:, :], preferred_element_type=F32))
        act = (_silu(gu[:, :f]) * gu[:, f:]).astype(BF16)
        ybuf[j % depth] = _pack_bf16_pairs(jnp.dot(act, wdn_bf[...], preferred_element_type=F32))
        y_copy(j).start()
        return carry

    lax.fori_loop(0, n_used, block, 0)

    for a in range(depth, 0, -1):
        @pl.when(n_used >= a)
        def _():
            y_copy(n_used - a).wait()


def _expert_gemm(plan, xg, w_gu, w_down, layer):
    n_slots, half = xg.shape
    _, ne, d, f2 = w_gu.shape
    f = w_down.shape[2]
    hbm = pl.BlockSpec(memory_space=pl.ANY)
    return pl.pallas_call(
        functools.partial(_expert_kernel, layer=layer),
        grid_spec=pltpu.PrefetchScalarGridSpec(
            num_scalar_prefetch=5,
            grid=(1,),
            in_specs=[hbm, hbm, hbm],
            out_specs=hbm,
            scratch_shapes=[pltpu.VMEM((GEMM_DEPTH, EXPERT_ROWS, half), U32),
                            pltpu.VMEM((GEMM_DEPTH, EXPERT_ROWS, half), U32),
                            pltpu.VMEM((2, d, f2), F32), pltpu.VMEM((2, f, d), F32),
                            pltpu.VMEM((d, f2), BF16), pltpu.VMEM((f, d), BF16),
                            pltpu.SemaphoreType.DMA((GEMM_DEPTH,)),
                            pltpu.SemaphoreType.DMA((GEMM_DEPTH,)),
                            pltpu.SemaphoreType.DMA((2,)), pltpu.SemaphoreType.DMA],
        ),
        out_shape=jax.ShapeDtypeStruct((n_slots, half), U32),
        compiler_params=_cparams(("arbitrary",)),
    )(plan["block_run"], plan["block_first"], plan["run_expert"], plan["n_runs"], plan["n_used"],
      xg, w_gu, w_down)


def _combine_kernel(slot_ref, tile_ref,
                    x_ref, sh_ref, sc_ref, g_ref, pos_ref, wt_ref, wsgu_ref, wsdn_ref,
                    lng_ref, lnb_ref, yg_hbm, o_ref, stage_ref, sems):
    tr, d = x_ref.shape
    i = pl.program_id(0)
    n_tiles = pl.num_programs(0)
    chunks_per_tile = LOCAL_ROWS // CHUNK

    def fetch(tile):
        def body(c, carry):
            pltpu.make_async_copy(_chunk_rows(yg_hbm, slot_ref[tile * chunks_per_tile + c]),
                                  _chunk_rows(stage_ref.at[tile % 2], c),
                                  sems.at[tile % 2]).start()
            return carry
        lax.fori_loop(0, tile_ref[tile], body, 0)

    @pl.when(i == 0)
    def _():
        stage_ref[...] = jnp.zeros_like(stage_ref)
        fetch(i)

    @pl.when(i + 1 < n_tiles)
    def _():
        fetch(jnp.minimum(i + 1, n_tiles - 1))

    x = x_ref[...]
    hf = (x * (1.0 + sc_ref[...]) + sh_ref[...]).astype(BF16)
    f = wsdn_ref.shape[0]
    su = jnp.dot(hf, wsgu_ref[...], preferred_element_type=F32)
    act = (_silu(su[:, :f]) * su[:, f:]).astype(BF16)
    shared = jnp.dot(act, wsdn_ref[...], preferred_element_type=F32)

    lane = lax.broadcasted_iota(I32, (tr, LOCAL_ROWS), 1)
    weights = jnp.zeros((tr, LOCAL_ROWS), F32)
    for k in range(TOP_K):
        weights = jnp.where(lane == pos_ref[:, k:k + 1], wt_ref[:, k:k + 1], weights)
    w_hi = weights.astype(BF16)
    w_lo = (weights - w_hi.astype(F32)).astype(BF16)
    w_both = jnp.concatenate([w_hi, w_lo], axis=0)

    buf = stage_ref.at[i % 2]

    def drain(c, carry):
        pltpu.make_async_copy(_chunk_rows(yg_hbm, 0), _chunk_rows(buf, 0), sems.at[i % 2]).wait()
        return carry

    lax.fori_loop(0, tile_ref[i], drain, 0)

    y_hi, y_lo = _unpack_bf16_pairs(buf[...])
    y = jnp.concatenate([y_hi.astype(BF16), y_lo.astype(BF16)], axis=1)
    both = jnp.dot(w_both, y, preferred_element_type=F32)
    routed = both[:tr, :] + both[tr:, :]
    v = ALPHA * x + (1.0 + g_ref[...]) * (routed + shared)
    o_ref[...] = _layer_norm(v, lng_ref[...], lnb_ref[...])


def _combine(plan, x2, mod3, pos_tok, w_tok, w_sh_gu, w_sh_down, ln_g, ln_b, yg, seq):
    t, d = x2.shape
    tr = TOK_TILE
    tpb = seq // tr
    full = lambda shape: pl.BlockSpec(shape, lambda i, *_: (0,) * len(shape))
    return pl.pallas_call(
        _combine_kernel,
        grid_spec=pltpu.PrefetchScalarGridSpec(
            num_scalar_prefetch=2,
            grid=(t // tr,),
            in_specs=[pl.BlockSpec((tr, d), lambda i, *_: (i, 0)),
                      _mod_spec(3, d, tpb), _mod_spec(4, d, tpb), _mod_spec(5, d, tpb),
                      pl.BlockSpec((tr, TOP_K), lambda i, *_: (i, 0)),
                      pl.BlockSpec((tr, TOP_K), lambda i, *_: (i, 0)),
                      full(w_sh_gu.shape), full(w_sh_down.shape), full((1, d)), full((1, d)),
                      pl.BlockSpec(memory_space=pl.ANY)],
            out_specs=pl.BlockSpec((tr, d), lambda i, *_: (i, 0)),
            scratch_shapes=[pltpu.VMEM((2, LOCAL_ROWS, d // 2), U32),
                            pltpu.SemaphoreType.DMA((2,))],
        ),
        out_shape=jax.ShapeDtypeStruct((t, d), F32),
        compiler_params=_cparams(("arbitrary",)),
    )(plan["slot_chunk"], plan["tile_chunks"],
      x2, mod3, mod3, mod3, pos_tok, w_tok, w_sh_gu, w_sh_down,
      ln_g.reshape(1, d), ln_b.reshape(1, d), yg)


def _moe_layer(x2, mod3, w_router, router_b, w_gu, w_down, layer, w_sh_gu, w_sh_down,
               ln_g, ln_b, seq):
    t, d = x2.shape
    n_tiles = t // TOK_TILE
    logits = _router(x2, mod3, w_router, seq)
    w_top, pos, counts = _topk_route(logits.T, router_b)
    bound = t * TOP_K + n_tiles * N_EXPERTS * (CHUNK - 1) + N_EXPERTS * (EXPERT_ROWS - 1)
    n_blocks = -(-bound // EXPERT_ROWS)
    plan = _slot_plan(counts, n_tiles, n_blocks)
    xg = _dispatch(plan, x2, mod3, pos, n_blocks * EXPERT_ROWS, seq)
    yg = _expert_gemm(plan, xg, w_gu, w_down, layer)
    return _combine(plan, x2, mod3, pos.T, w_top.T, w_sh_gu, w_sh_down, ln_g, ln_b, yg, seq)


def kernel(x, c, ada_w, ada_b, pool_w_in, pool_w_grp, pool_scale, pool_w_out, attn_w_in, attn_w_out, ln1_g, ln1_b, router_w, router_b, exp_w_gu, exp_w_down, sh_w_gu, sh_w_down, ln2_g, ln2_b):
    batch, seq, d = x.shape
    depth = ada_w.shape[0]
    t = batch * seq
    mod = _modulation(c, ada_w, ada_b).reshape(depth, batch, 1, 6 * d)
    perms = jnp.stack([_perm_matrix(dil) for _, dil in ATTN_PATTERNS[1:]])
    perms_t = jnp.swapaxes(perms, 1, 2)
    expand = (jnp.arange(LSE_LANES)[:, None] == (jnp.arange(d)[None, :] // HEAD_DIM)).astype(BF16)
    x2 = x.reshape(t, d)
    for i in range(depth):
        mod3 = mod[i]
        j = i // 2
        if i % 2 == 0:
            x2 = _pool_layer(x2, mod3, pool_w_in[j].astype(BF16), pool_w_grp[j].astype(BF16),
                             pool_scale[j], pool_w_out[j].astype(BF16), ln1_g[i], ln1_b[i], seq)
        else:
            qkv = _qkv_proj(x2, mod3, perms, attn_w_in[j].astype(BF16), seq)
            res = [_attention_group(qkv, g, batch, seq) for g in range(len(ATTN_PATTERNS))]
            x2 = _attn_out(x2, mod3, [r[0] for r in res], [r[1] for r in res], perms_t, expand,
                           attn_w_out[j].astype(BF16), ln1_g[i], ln1_b[i], seq)
        x2 = _moe_layer(x2, mod3, router_w[i], router_b[i], exp_w_gu, exp_w_down, i,
                        sh_w_gu[i].astype(BF16), sh_w_down[i].astype(BF16),
                        ln2_g[i], ln2_b[i], seq)
    return x2.reshape(batch, seq, d)
```

```python
import functools
import math

import jax
import jax.numpy as jnp
from jax import lax
from jax.experimental import pallas as pl
from jax.experimental.pallas import tpu as pltpu

F32 = jnp.float32
BF16 = jnp.bfloat16
U32 = jnp.uint32
I32 = jnp.int32

POOL_WINDOWS = (2, 4, 8, 16)
ATTN_PATTERNS = ((128, 1), (512, 4), (2048, 16))
HEAD_DIM = 64
N_HEADS = 16
Q_BLOCK = 128
N_EXPERTS = 64
TOP_K = 8
N_EXPERT_GROUPS = 8
TOPK_GROUPS = 4
ROUTED_SCALE = 2.5
EXPERT_ROWS = 256
DEPTH = 4
ALPHA = (2 * DEPTH) ** 0.25
LN_EPS = 1e-5

PERM_TILE = 256
POOL_HALO = 16
TOK_TILE = 256
CHUNK = 8
LOCAL_ROWS = -(-(TOK_TILE * TOP_K + N_EXPERTS * (CHUNK - 1)) // 256) * 256
WAIT_GROUP = 32
GEMM_DEPTH = 4
LSE_LANES = 128
VMEM_LIMIT = 56 * 1024 * 1024
NEG_BIG = -1e30


def _cparams(sem):
    return pltpu.CompilerParams(dimension_semantics=sem, vmem_limit_bytes=VMEM_LIMIT)


def _layer_norm(v, g, b):
    mu = jnp.mean(v, axis=-1, keepdims=True)
    c = v - mu
    var = jnp.mean(c * c, axis=-1, keepdims=True)
    return c * lax.rsqrt(var + LN_EPS) * g + b


def _silu(v):
    return v * (1.0 / (1.0 + jnp.exp(-v)))


def _pack_bf16_pairs(v):
    n = v.shape[1] // 2
    hi = lax.bitcast_convert_type(v[:, :n].astype(BF16).astype(F32), U32)
    lo = lax.bitcast_convert_type(v[:, n:].astype(BF16).astype(F32), U32)
    return hi | (lo >> 16)


def _unpack_bf16_pairs(p):
    hi = lax.bitcast_convert_type(p & jnp.uint32(0xFFFF0000), F32)
    lo = lax.bitcast_convert_type(p << 16, F32)
    return hi, lo


def _mod_kernel(c_ref, w_ref, b_ref, o_ref):
    cs = _silu(c_ref[...])
    o_ref[...] = jnp.dot(cs, w_ref[...], preferred_element_type=F32) + b_ref[...]


def _modulation(c, ada_w, ada_b):
    depth, d, n6 = ada_w.shape
    b = c.shape[0]
    tn = 1536
    return pl.pallas_call(
        _mod_kernel,
        grid=(depth, n6 // tn),
        in_specs=[
            pl.BlockSpec((b, d), lambda i, n: (0, 0)),
            pl.BlockSpec((None, d, tn), lambda i, n: (i, 0, n)),
            pl.BlockSpec((None, 1, tn), lambda i, n: (i, 0, n)),
        ],
        out_specs=pl.BlockSpec((None, b, tn), lambda i, n: (i, 0, n)),
        out_shape=jax.ShapeDtypeStruct((depth, b, n6), F32),
        compiler_params=_cparams(("arbitrary", "arbitrary")),
    )(c, ada_w, ada_b.reshape(depth, 1, n6))


def _mod_spec(chunk, d, tiles_per_batch):
    return pl.BlockSpec((None, 1, d), lambda *idx: (idx[0] // tiles_per_batch, 0, chunk))


def _pool_kernel(x_ref, sh_ref, sc_ref, g_ref, win_ref, wgrp_ref, cs_ref, wout_ref,
                 lng_ref, lnb_ref, o_ref, ext_ref, *, tiles_per_batch):
    tm, d = x_ref.shape
    s_idx = pl.program_id(0) % tiles_per_batch
    x = x_ref[...]
    h = (x * (1.0 + sc_ref[...]) + sh_ref[...]).astype(BF16)
    u = jnp.dot(h, win_ref[...], preferred_element_type=F32)

    @pl.when(s_idx == 0)
    def _():
        ext_ref[0:POOL_HALO, :] = jnp.zeros((POOL_HALO, d), F32)

    @pl.when(s_idx != 0)
    def _():
        ext_ref[0:POOL_HALO, :] = ext_ref[tm:tm + POOL_HALO, :]

    ext_ref[POOL_HALO:POOL_HALO + tm, :] = u

    pos = s_idx * tm + lax.broadcasted_iota(I32, (tm, 1), 0) + 1
    gc = d // len(POOL_WINDOWS)
    ys = []
    for g, w in enumerate(POOL_WINDOWS):
        cols = slice(g * gc, (g + 1) * gc)
        acc = u[:, cols]
        for j in range(1, w):
            acc = acc + ext_ref[POOL_HALO - j:POOL_HALO - j + tm, cols]
        cnt = jnp.minimum(pos, w).astype(F32)
        z = (acc / cnt - u[:, cols]).astype(BF16)
        ys.append(jnp.dot(z, wgrp_ref[g], preferred_element_type=F32))
    y = (jnp.concatenate(ys, axis=1) * cs_ref[...]).astype(BF16)
    out = jnp.dot(y, wout_ref[...], preferred_element_type=F32)
    v = ALPHA * x + (1.0 + g_ref[...]) * out
    o_ref[...] = _layer_norm(v, lng_ref[...], lnb_ref[...])


def _pool_layer(x2, mod3, w_in, w_grp, ch_scale, w_out, ln_g, ln_b, seq):
    t, d = x2.shape
    tm = 512
    tpb = seq // tm
    full = lambda shape: pl.BlockSpec(shape, lambda i: (0,) * len(shape))
    return pl.pallas_call(
        functools.partial(_pool_kernel, tiles_per_batch=tpb),
        grid=(t // tm,),
        in_specs=[
            pl.BlockSpec((tm, d), lambda i: (i, 0)),
            _mod_spec(0, d, tpb), _mod_spec(1, d, tpb), _mod_spec(2, d, tpb),
            full((d, d)), full(w_grp.shape), full((1, d)), full((d, d)),
            full((1, d)), full((1, d)),
        ],
        out_specs=pl.BlockSpec((tm, d), lambda i: (i, 0)),
        out_shape=jax.ShapeDtypeStruct((t, d), F32),
        scratch_shapes=[pltpu.VMEM((tm + POOL_HALO, d), F32)],
        compiler_params=_cparams(("arbitrary",)),
    )(x2, mod3, mod3, mod3, w_in, w_grp, ch_scale.reshape(1, d), w_out,
      ln_g.reshape(1, d), ln_b.reshape(1, d))


def _perm_matrix(dil):
    p = jnp.arange(PERM_TILE)
    chunk = PERM_TILE // dil
    src = (p % chunk) * dil + p // chunk
    return (src[:, None] == jnp.arange(PERM_TILE)[None, :]).astype(BF16)


def _qkv_kernel(x_ref, sh_ref, sc_ref, p_ref, w_ref, o_ref, h_ref):
    tm = x_ref.shape[0]
    g = pl.program_id(1)
    part = pl.program_id(2)

    @pl.when((g == 0) & (part == 0))
    def _():
        h = (x_ref[...] * (1.0 + sc_ref[...]) + sh_ref[...]).astype(BF16)
        h_ref[0] = h
        for gi in range(1, len(ATTN_PATTERNS)):
            for s in range(tm // PERM_TILE):
                rows = slice(s * PERM_TILE, (s + 1) * PERM_TILE)
                h_ref[gi, rows, :] = jnp.dot(
                    p_ref[gi - 1], h[rows, :], preferred_element_type=F32).astype(BF16)

    o_ref[...] = jnp.dot(h_ref[g], w_ref[...], preferred_element_type=F32).astype(BF16)


def _qkv_proj(x2, mod3, perms, w_in, seq):
    t, d = x2.shape
    ng = len(ATTN_PATTERNS)
    tm = 1024
    tpb = seq // tm
    return pl.pallas_call(
        _qkv_kernel,
        grid=(t // tm, ng, 3),
        in_specs=[
            pl.BlockSpec((tm, d), lambda m, g, p: (m, 0)),
            _mod_spec(0, d, tpb), _mod_spec(1, d, tpb),
            pl.BlockSpec(perms.shape, lambda m, g, p: (0, 0, 0)),
            pl.BlockSpec((d, d), lambda m, g, p: (0, g * 3 + p)),
        ],
        out_specs=pl.BlockSpec((tm, d), lambda m, g, p: (m, g * 3 + p)),
        out_shape=jax.ShapeDtypeStruct((t, ng * 3 * d), BF16),
        scratch_shapes=[pltpu.VMEM((ng, tm, d), BF16)],
        compiler_params=_cparams(("arbitrary", "arbitrary", "arbitrary")),
    )(x2, mod3, mod3, perms, w_in)


def _attn_kernel(q_ref, kp_ref, kc_ref, vp_ref, vc_ref, o_ref, lse_ref, *, group, dil):
    bq = Q_BLOCK
    d = N_HEADS * HEAD_DIM
    j = pl.program_id(2)
    q = q_ref[...].reshape(bq, d)
    kp = kp_ref[...].reshape(bq, d)
    kc = kc_ref[...].reshape(bq, d)
    vp = vp_ref[...].reshape(bq, d)
    vc = vc_ref[...].reshape(bq, d)

    qi = lax.broadcasted_iota(I32, (bq, 2 * bq), 0)
    kj = lax.broadcasted_iota(I32, (bq, 2 * bq), 1)
    dist = qi + bq - kj
    steps = ATTN_PATTERNS[group][0] // dil
    valid = (dist >= 0) & (dist <= steps) & ((kj >= bq) | (j > 0))
    distf = dist.astype(F32)
    lane = lax.broadcasted_iota(I32, (bq, LSE_LANES), 1)
    lse_tile = jnp.zeros((bq, LSE_LANES), F32)
    n_tot = len(ATTN_PATTERNS) * N_HEADS
    outs = []
    for h in range(N_HEADS):
        cols = slice(h * HEAD_DIM, (h + 1) * HEAD_DIM)
        slope = 2.0 ** (-8.0 * (group * N_HEADS + h + 1) / n_tot)
        kh = jnp.concatenate([kp[:, cols], kc[:, cols]], axis=0)
        vh = jnp.concatenate([vp[:, cols], vc[:, cols]], axis=0)
        s = lax.dot_general(q[:, cols], kh, (((1,), (1,)), ((), ())),
                            preferred_element_type=F32)
        s = s * (HEAD_DIM ** -0.5) - (slope * dil) * distf
        s = jnp.where(valid, s, NEG_BIG)
        m = jnp.max(s, axis=-1, keepdims=True)
        p = jnp.exp(s - m)
        den = jnp.sum(p, axis=-1, keepdims=True)
        o = jnp.dot(p.astype(BF16), vh, preferred_element_type=F32) / den
        outs.append(o)
        lse_tile = jnp.where(lane == h, m + jnp.log(den), lse_tile)
    o_ref[...] = jnp.concatenate(outs, axis=1).astype(BF16).reshape(o_ref.shape)
    lse_ref[...] = lse_tile.reshape(lse_ref.shape)


def _attention_group(qkv, group, batch, seq):
    dil = ATTN_PATTERNS[group][1]
    d = N_HEADS * HEAD_DIM
    t = qkv.shape[0]
    sub = seq // dil
    nb = sub // Q_BLOCK
    rows = Q_BLOCK if dil == 1 else PERM_TILE // dil
    chunks = Q_BLOCK // rows
    u = seq // (rows * dil)
    view = lambda a, c: a.reshape(batch, u, dil, rows, c)
    blk = lambda c: (None, chunks, None, rows, c)
    col0 = group * 3
    q_spec = pl.BlockSpec(blk(d), lambda b, r, j: (b, j, r, 0, col0))
    kc_spec = pl.BlockSpec(blk(d), lambda b, r, j: (b, j, r, 0, col0 + 1))
    kp_spec = pl.BlockSpec(blk(d), lambda b, r, j: (b, jnp.maximum(j - 1, 0), r, 0, col0 + 1))
    vc_spec = pl.BlockSpec(blk(d), lambda b, r, j: (b, j, r, 0, col0 + 2))
    vp_spec = pl.BlockSpec(blk(d), lambda b, r, j: (b, jnp.maximum(j - 1, 0), r, 0, col0 + 2))
    qkv5 = view(qkv, qkv.shape[1])
    o, lse = pl.pallas_call(
        functools.partial(_attn_kernel, group=group, dil=dil),
        grid=(batch, dil, nb),
        in_specs=[q_spec, kp_spec, kc_spec, vp_spec, vc_spec],
        out_specs=[
            pl.BlockSpec(blk(d), lambda b, r, j: (b, j, r, 0, 0)),
            pl.BlockSpec(blk(LSE_LANES), lambda b, r, j: (b, j, r, 0, 0)),
        ],
        out_shape=[
            jax.ShapeDtypeStruct((batch, u, dil, rows, d), BF16),
            jax.ShapeDtypeStruct((batch, u, dil, rows, LSE_LANES), F32),
        ],
        compiler_params=_cparams(("arbitrary", "arbitrary", "arbitrary")),
    )(qkv5, qkv5, qkv5, qkv5, qkv5)
    return o.reshape(t, d), lse.reshape(t, LSE_LANES)


def _split3(v):
    a = v.astype(BF16)
    r = v - a.astype(F32)
    b = r.astype(BF16)
    c = (r - b.astype(F32)).astype(BF16)
    return a, b, c


def _attn_out_kernel(x_ref, g_ref, o0_ref, o1_ref, o2_ref, l0_ref, l1_ref, l2_ref,
                     pt_ref, e_ref, wout_ref, lng_ref, lnb_ref, out_ref):
    tm, d = x_ref.shape
    o_refs = (o0_ref, o1_ref, o2_ref)
    l_refs = (l0_ref, l1_ref, l2_ref)
    n_sub = tm // PERM_TILE

    def unperm(gi, val_bf16):
        if gi == 0:
            return val_bf16.astype(F32)
        parts = [jnp.dot(pt_ref[gi - 1], val_bf16[s * PERM_TILE:(s + 1) * PERM_TILE, :],
                         preferred_element_type=F32) for s in range(n_sub)]
        return jnp.concatenate(parts, axis=0)

    lses = []
    for gi in range(3):
        l = l_refs[gi][...]
        if gi == 0:
            lses.append(l)
        else:
            a, b, c = _split3(l)
            lses.append(unperm(gi, a) + unperm(gi, b) + unperm(gi, c))
    mx = jnp.maximum(jnp.maximum(lses[0], lses[1]), lses[2])
    es = [jnp.exp(l - mx) for l in lses]
    tot = es[0] + es[1] + es[2]
    mixed = jnp.zeros((tm, d), F32)
    for gi in range(3):
        w = es[gi] / tot
        a, b, c = _split3(w)
        wide = (jnp.dot(a, e_ref[...], preferred_element_type=F32)
                + jnp.dot(b, e_ref[...], preferred_element_type=F32)
                + jnp.dot(c, e_ref[...], preferred_element_type=F32))
        mixed = mixed + wide * unperm(gi, o_refs[gi][...])
    y = jnp.dot(mixed.astype(BF16), wout_ref[...], preferred_element_type=F32)
    v = ALPHA * x_ref[...] + (1.0 + g_ref[...]) * y
    out_ref[...] = _layer_norm(v, lng_ref[...], lnb_ref[...])


def _attn_out(x2, mod3, os_, lses, perms_t, expand, w_out, ln_g, ln_b, seq):
    t, d = x2.shape
    tm = 512
    tpb = seq // tm
    row = lambda c: pl.BlockSpec((tm, c), lambda i: (i, 0))
    full = lambda shape: pl.BlockSpec(shape, lambda i: (0,) * len(shape))
    return pl.pallas_call(
        _attn_out_kernel,
        grid=(t // tm,),
        in_specs=[row(d), _mod_spec(2, d, tpb), row(d), row(d), row(d),
                  row(LSE_LANES), row(LSE_LANES), row(LSE_LANES),
                  full(perms_t.shape), full(expand.shape), full((d, d)),
                  full((1, d)), full((1, d))],
        out_specs=row(d),
        out_shape=jax.ShapeDtypeStruct((t, d), F32),
        compiler_params=_cparams(("arbitrary",)),
    )(x2, mod3, *os_, *lses, perms_t, expand, w_out, ln_g.reshape(1, d), ln_b.reshape(1, d))


def _router_kernel(x_ref, sh_ref, sc_ref, wr_ref, lg_ref):
    hf = x_ref[...] * (1.0 + sc_ref[...]) + sh_ref[...]
    lg_ref[...] = jnp.dot(hf, wr_ref[...], preferred_element_type=F32,
                          precision=lax.Precision.HIGHEST)


def _router(x2, mod3, w_router, seq):
    t, d = x2.shape
    tm = 512
    tpb = seq // tm
    ne = w_router.shape[1]
    return pl.pallas_call(
        _router_kernel,
        grid=(t // tm,),
        in_specs=[pl.BlockSpec((tm, d), lambda i: (i, 0)),
                  _mod_spec(3, d, tpb), _mod_spec(4, d, tpb),
                  pl.BlockSpec((d, ne), lambda i: (0, 0))],
        out_specs=pl.BlockSpec((tm, ne), lambda i: (i, 0)),
        out_shape=jax.ShapeDtypeStruct((t, ne), F32),
        compiler_params=_cparams(("arbitrary",)),
    )(x2, mod3, mod3, w_router)


def _first_index_of_max(v, iota, size):
    m = jnp.max(v, axis=0, keepdims=True)
    idx = jnp.min(jnp.where(v == m, iota, float(size)), axis=0, keepdims=True)
    return m, idx


def _topk_kernel(lg_ref, b_ref, tri_ref, ltri_ref, w_ref, pos_ref, cnt_ref):
    ne, tr = lg_ref.shape
    gsz = ne // N_EXPERT_GROUPS
    tile = pl.program_id(0)

    scores = 1.0 / (1.0 + jnp.exp(-lg_ref[...]))
    sel = scores + b_ref[...]
    iota_g = lax.broadcasted_iota(I32, (gsz, tr), 0).astype(F32)
    iota_n = lax.broadcasted_iota(I32, (N_EXPERT_GROUPS, tr), 0).astype(F32)
    gs = jnp.zeros((N_EXPERT_GROUPS, tr), F32)
    for g in range(N_EXPERT_GROUPS):
        blk = sel[g * gsz:(g + 1) * gsz, :]
        m1, i1 = _first_index_of_max(blk, iota_g, gsz)
        m2 = jnp.max(jnp.where(iota_g == i1, -jnp.inf, blk), axis=0, keepdims=True)
        gs = jnp.where(iota_n == float(g), m1 + m2, gs)
    gmask = jnp.zeros((N_EXPERT_GROUPS, tr), F32)
    for _ in range(TOPK_GROUPS):
        _, gi = _first_index_of_max(gs, iota_n, N_EXPERT_GROUPS)
        hit = iota_n == gi
        gmask = jnp.where(hit, 1.0, gmask)
        gs = jnp.where(hit, -jnp.inf, gs)
    masked_rows = []
    for g in range(N_EXPERT_GROUPS):
        keep = jnp.broadcast_to(gmask[g:g + 1, :], (gsz, tr)) > 0.5
        masked_rows.append(jnp.where(keep, sel[g * gsz:(g + 1) * gsz, :], -jnp.inf))
    cur = jnp.concatenate(masked_rows, axis=0)
    iota_e = lax.broadcasted_iota(I32, (ne, tr), 0).astype(F32)
    chosen = jnp.zeros((ne, tr), F32)
    idxs, tops = [], []
    for _ in range(TOP_K):
        _, ei = _first_index_of_max(cur, iota_e, ne)
        hit = iota_e == ei
        cur = jnp.where(hit, -jnp.inf, cur)
        chosen = jnp.where(hit, 1.0, chosen)
        idxs.append(ei)
        tops.append(jnp.sum(jnp.where(hit, scores, 0.0), axis=0, keepdims=True))
    wsum = tops[0]
    for k in range(1, TOP_K):
        wsum = wsum + tops[k]
    before = jnp.dot(chosen.astype(BF16), tri_ref[...], preferred_element_type=F32)
    n = jnp.sum(chosen, axis=1, keepdims=True)
    n_chunks = jnp.floor((n + (CHUNK - 1)) * (1.0 / CHUNK))
    run_off = jnp.dot(ltri_ref[...], jnp.broadcast_to(n_chunks, (ne, 128)).astype(BF16),
                      preferred_element_type=F32)[:, 0:1] * CHUNK
    pos_all = before + run_off
    for k in range(TOP_K):
        hit = iota_e == idxs[k]
        w_ref[k:k + 1, :] = tops[k] / wsum * ROUTED_SCALE
        pos_ref[k:k + 1, :] = jnp.sum(jnp.where(hit, pos_all, 0.0), axis=0,
                                      keepdims=True).astype(I32)

    @pl.when(tile == 0)
    def _():
        cnt_ref[...] = jnp.zeros_like(cnt_ref)

    lane = lax.broadcasted_iota(I32, cnt_ref.shape, 1)
    cnt_ref[...] = jnp.where(lane == tile, jnp.broadcast_to(n, cnt_ref.shape).astype(I32),
                             cnt_ref[...])


def _topk_route(logits_t, router_b):
    ne, t = logits_t.shape
    tr = TOK_TILE
    assert t // tr <= 128
    tri = (jnp.arange(tr)[:, None] < jnp.arange(tr)[None, :]).astype(BF16)
    ltri = (jnp.arange(ne)[None, :] < jnp.arange(ne)[:, None]).astype(BF16)
    out = lambda dt: jax.ShapeDtypeStruct((TOP_K, t), dt)
    row = pl.BlockSpec((TOP_K, tr), lambda i: (0, i))
    return pl.pallas_call(
        _topk_kernel,
        grid=(t // tr,),
        in_specs=[pl.BlockSpec((ne, tr), lambda i: (0, i)),
                  pl.BlockSpec((ne, 1), lambda i: (0, 0)),
                  pl.BlockSpec((tr, tr), lambda i: (0, 0)),
                  pl.BlockSpec((ne, ne), lambda i: (0, 0))],
        out_specs=[row, row, pl.BlockSpec((ne, 128), lambda i: (0, 0))],
        out_shape=[out(F32), out(I32), jax.ShapeDtypeStruct((ne, 128), I32)],
        compiler_params=_cparams(("arbitrary",)),
    )(logits_t, router_b.reshape(ne, 1), tri, ltri)


def _slot_plan(counts, n_tiles, n_blocks):
    n = counts[:, :n_tiles]
    nch = (n + (CHUNK - 1)) // CHUNK
    rows = jnp.sum(nch, axis=1) * CHUNK
    region = ((rows + EXPERT_ROWS - 1) // EXPERT_ROWS) * EXPERT_ROWS
    region_end = jnp.cumsum(region)
    region_start = region_end - region
    run_chunk = region_start[:, None] // CHUNK + jnp.cumsum(nch, axis=1) - nch
    local_end = jnp.cumsum(nch, axis=0)
    local_chunk = local_end - nch
    c = jnp.arange(LOCAL_ROWS // CHUNK, dtype=I32)
    expert_of_c = jnp.sum((local_end[:, :, None] <= c[None, None, :]).astype(I32), axis=0)
    owner = expert_of_c[None] == jnp.arange(N_EXPERTS, dtype=I32)[:, None, None]
    slot_chunk = jnp.sum(jnp.where(owner, (run_chunk - local_chunk)[:, :, None], 0), axis=0) + c
    block_row = jnp.arange(n_blocks, dtype=I32) * EXPERT_ROWS
    group_e = jnp.minimum(jnp.sum((region_end[None, :] <= block_row[:, None]).astype(I32), axis=1),
                          N_EXPERTS - 1)
    n_used = region_end[-1] // EXPERT_ROWS
    nonempty = region > 0
    run_of_expert = jnp.cumsum(nonempty.astype(I32)) - 1
    experts = jnp.arange(N_EXPERTS, dtype=I32)
    run_expert = jnp.sum(jnp.where((run_of_expert[None, :] == experts[:, None]) & nonempty[None, :],
                                   experts[None, :], 0), axis=1)
    blocks = jnp.arange(n_blocks, dtype=I32)
    first = ((blocks == 0) | (group_e != jnp.roll(group_e, 1))) & (blocks < n_used)
    return dict(
        block_run=(jnp.cumsum(first.astype(I32)) - 1).astype(I32), block_first=first.astype(I32),
        run_expert=run_expert.astype(I32), n_runs=jnp.sum(nonempty.astype(I32)).reshape(1),
        slot_chunk=slot_chunk.reshape(-1).astype(I32),
        tile_chunks=jnp.sum(nch, axis=0).astype(I32),
        tail_chunk=((region_start + rows) // CHUNK).astype(I32),
        tail_chunks=((region - rows) // CHUNK).astype(I32),
        group_e=group_e, n_used=(region_end[-1:] // EXPERT_ROWS).astype(I32))


def _chunk_rows(ref, chunk_index):
    start = chunk_index * CHUNK
    if not isinstance(start, int):
        start = pl.multiple_of(start, CHUNK)
    return ref.at[pl.ds(start, CHUNK), :]


def _wait_chunks(count, src_ref, dst_ref, sem):
    group_rows = pl.ds(0, WAIT_GROUP * CHUNK)

    def wait_group(c, carry):
        pltpu.make_async_copy(src_ref.at[group_rows, :], dst_ref.at[group_rows, :], sem).wait()
        return carry

    def wait_one(c, carry):
        pltpu.make_async_copy(_chunk_rows(src_ref, 0), _chunk_rows(dst_ref, 0), sem).wait()
        return carry

    lax.fori_loop(0, lax.shift_right_logical(count, WAIT_GROUP.bit_length() - 1), wait_group, 0)
    lax.fori_loop(0, count & (WAIT_GROUP - 1), wait_one, 0)


def _dispatch_kernel(slot_ref, tile_ref, tail_ref, tailn_ref, used_ref,
                     x_ref, sh_ref, sc_ref, pos_ref, xg_hbm, stage_ref, zero_ref, sems, sem_blk):
    tr = x_ref.shape[0]
    i = pl.program_id(0)
    n_blocks = xg_hbm.shape[0] // EXPERT_ROWS
    chunks_per_tile = LOCAL_ROWS // CHUNK

    def drain(count, sem):
        _wait_chunks(count, stage_ref.at[0], xg_hbm, sem)

    @pl.when(i == 0)
    def _():
        zero_ref[...] = jnp.zeros_like(zero_ref)

        def block_copy(j):
            rows = pl.ds(pl.multiple_of(j * EXPERT_ROWS, EXPERT_ROWS), EXPERT_ROWS)
            return pltpu.make_async_copy(zero_ref, xg_hbm.at[rows, :], sem_blk)

        def start_block(j, carry):
            block_copy(j).start()
            return carry

        def wait_block(j, carry):
            block_copy(j).wait()
            return carry

        lax.fori_loop(used_ref[0], n_blocks, start_block, 0)

        def per_expert(e, total):
            def per_chunk(c, carry):
                pltpu.make_async_copy(_chunk_rows(zero_ref, 0),
                                      _chunk_rows(xg_hbm, tail_ref[e] + c), sems.at[0]).start()
                return carry
            lax.fori_loop(0, tailn_ref[e], per_chunk, 0)
            return total + tailn_ref[e]

        drain(lax.fori_loop(0, N_EXPERTS, per_expert, 0), sems.at[0])
        lax.fori_loop(used_ref[0], n_blocks, wait_block, 0)

    hf = (x_ref[...] * (1.0 + sc_ref[...]) + sh_ref[...]).astype(BF16)
    slot = lax.broadcasted_iota(I32, (LOCAL_ROWS, tr), 0)
    perm = jnp.zeros((LOCAL_ROWS, tr), F32)
    for k in range(TOP_K):
        perm = jnp.where(slot == pos_ref[k:k + 1, :], 1.0, perm)
    rows = jnp.dot(perm.astype(BF16), hf, preferred_element_type=F32)
    half = rows.shape[1] // 2
    buf = stage_ref.at[i % 2]
    buf[...] = (lax.bitcast_convert_type(rows[:, :half], U32)
                | (lax.bitcast_convert_type(rows[:, half:], U32) >> 16))

    def send(c, carry):
        pltpu.make_async_copy(_chunk_rows(buf, c),
                              _chunk_rows(xg_hbm, slot_ref[i * chunks_per_tile + c]),
                              sems.at[i % 2]).start()
        return carry

    lax.fori_loop(0, tile_ref[i], send, 0)

    @pl.when(i > 0)
    def _():
        drain(tile_ref[jnp.maximum(i - 1, 0)], sems.at[(i + 1) % 2])

    @pl.when(i == pl.num_programs(0) - 1)
    def _():
        drain(tile_ref[i], sems.at[i % 2])


def _dispatch(plan, x2, mod3, pos, n_slots, seq):
    t, d = x2.shape
    tr = TOK_TILE
    tpb = seq // tr
    return pl.pallas_call(
        _dispatch_kernel,
        grid_spec=pltpu.PrefetchScalarGridSpec(
            num_scalar_prefetch=5,
            grid=(t // tr,),
            in_specs=[pl.BlockSpec((tr, d), lambda i, *_: (i, 0)),
                      _mod_spec(3, d, tpb), _mod_spec(4, d, tpb),
                      pl.BlockSpec((TOP_K, tr), lambda i, *_: (0, i))],
            out_specs=pl.BlockSpec(memory_space=pl.ANY),
            scratch_shapes=[pltpu.VMEM((2, LOCAL_ROWS, d // 2), U32),
                            pltpu.VMEM((EXPERT_ROWS, d // 2), U32),
                            pltpu.SemaphoreType.DMA((2,)), pltpu.SemaphoreType.DMA],
        ),
        out_shape=jax.ShapeDtypeStruct((n_slots, d // 2), U32),
        compiler_params=_cparams(("arbitrary",)),
    )(plan["slot_chunk"], plan["tile_chunks"], plan["tail_chunk"], plan["tail_chunks"],
      plan["n_used"], x2, mod3, mod3, pos)


def _expert_kernel(run_ref, first_ref, rexp_ref, nrun_ref, nu_ref,
                   xg_hbm, wgu_hbm, wdn_hbm, yg_hbm,
                   xbuf, ybuf, wgu_f32, wdn_f32, wgu_bf, wdn_bf, x_sem, y_sem, w_sem, z_sem,
                   *, layer):
    n_used = nu_ref[0]
    n_blocks = yg_hbm.shape[0] // EXPERT_ROWS
    depth, _, half = xbuf.shape
    f = wdn_bf.shape[0]

    def rows(j):
        return pl.ds(pl.multiple_of(j * EXPERT_ROWS, EXPERT_ROWS), EXPERT_ROWS)

    def x_copy(j):
        return pltpu.make_async_copy(xg_hbm.at[rows(j), :], xbuf.at[j % depth],
                                     x_sem.at[j % depth])

    def y_copy(j):
        return pltpu.make_async_copy(ybuf.at[j % depth], yg_hbm.at[rows(j), :],
                                     y_sem.at[j % depth])

    def w_copies(r):
        e = rexp_ref[r]
        return (pltpu.make_async_copy(wgu_hbm.at[layer, e], wgu_f32.at[r % 2], w_sem.at[r % 2]),
                pltpu.make_async_copy(wdn_hbm.at[layer, e], wdn_f32.at[r % 2], w_sem.at[r % 2]))

    def zero_copy(j):
        return pltpu.make_async_copy(ybuf.at[0], yg_hbm.at[rows(j), :], z_sem)

    ybuf[0] = jnp.zeros(ybuf.shape[1:], U32)

    def zero_start(j, carry):
        zero_copy(j).start()
        return carry

    def zero_wait(j, carry):
        zero_copy(j).wait()
        return carry

    lax.fori_loop(n_used, n_blocks, zero_start, 0)
    lax.fori_loop(n_used, n_blocks, zero_wait, 0)

    for cp in w_copies(0):
        cp.start()
    for a in range(depth - 1):
        @pl.when(a < n_used)
        def _():
            x_copy(a).start()

    def block(j, carry):
        r = run_ref[j]

        @pl.when(first_ref[j] == 1)
        def _():
            for cp in w_copies(r):
                cp.wait()

            @pl.when(r + 1 < nrun_ref[0])
            def _():
                for cp in w_copies(r + 1):
                    cp.start()

            wgu_bf[...] = wgu_f32[r % 2].astype(BF16)
            wdn_bf[...] = wdn_f32[r % 2].astype(BF16)

        x_copy(j).wait()

        @pl.when(j + (depth - 1) < n_used)
        def _():
            x_copy(j + (depth - 1)).start()

        @pl.when(j >= depth)
        def _():
            y_copy(j - depth).wait()

        hi, lo = _unpack_bf16_pairs(xbuf[j % depth])
        gu = (jnp.dot(hi.astype(BF16), wgu_bf[0:half, :], preferred_element_type=F32)
              + jnp.dot(lo.astype(BF16), wgu_bf[half:, :], preferred_element_type=F32))
        act = (_silu(gu[:, :f]) * gu[:, f:]).astype(BF16)
        ybuf[j % depth] = _pack_bf16_pairs(jnp.dot(act, wdn_bf[...], preferred_element_type=F32))
        y_copy(j).start()
        return carry

    lax.fori_loop(0, n_used, block, 0)

    for a in range(depth, 0, -1):
        @pl.when(n_used >= a)
        def _():
            y_copy(n_used - a).wait()


def _expert_gemm(plan, xg, w_gu, w_down, layer):
    n_slots, half = xg.shape
    _, ne, d, f2 = w_gu.shape
    f = w_down.shape[2]
    hbm = pl.BlockSpec(memory_space=pl.ANY)
    return pl.pallas_call(
        functools.partial(_expert_kernel, layer=layer),
        grid_spec=pltpu.PrefetchScalarGridSpec(
            num_scalar_prefetch=5,
            grid=(1,),
            in_specs=[hbm, hbm, hbm],
            out_specs=hbm,
            scratch_shapes=[pltpu.VMEM((GEMM_DEPTH, EXPERT_ROWS, half), U32),
                            pltpu.VMEM((GEMM_DEPTH, EXPERT_ROWS, half), U32),
                            pltpu.VMEM((2, d, f2), F32), pltpu.VMEM((2, f, d), F32),
                            pltpu.VMEM((d, f2), BF16), pltpu.VMEM((f, d), BF16),
                            pltpu.SemaphoreType.DMA((GEMM_DEPTH,)),
                            pltpu.SemaphoreType.DMA((GEMM_DEPTH,)),
                            pltpu.SemaphoreType.DMA((2,)), pltpu.SemaphoreType.DMA],
        ),
        out_shape=jax.ShapeDtypeStruct((n_slots, half), U32),
        compiler_params=_cparams(("arbitrary",)),
    )(plan["block_run"], plan["block_first"], plan["run_expert"], plan["n_runs"], plan["n_used"],
      xg, w_gu, w_down)


def _combine_kernel(slot_ref, tile_ref,
                    x_ref, sh_ref, sc_ref, g_ref, pos_ref, wt_ref, wsgu_ref, wsdn_ref,
                    lng_ref, lnb_ref, yg_hbm, o_ref, stage_ref, sems):
    tr, d = x_ref.shape
    i = pl.program_id(0)
    n_tiles = pl.num_programs(0)
    chunks_per_tile = LOCAL_ROWS // CHUNK

    def fetch(tile, buffer):
        def body(c, carry):
            pltpu.make_async_copy(_chunk_rows(yg_hbm, slot_ref[tile * chunks_per_tile + c]),
                                  _chunk_rows(stage_ref.at[buffer], c), sems.at[buffer]).start()
            return carry
        lax.fori_loop(0, tile_ref[tile], body, 0)

    @pl.when(i == 0)
    def _():
        stage_ref[...] = jnp.zeros_like(stage_ref)
        fetch(0, 0)

    nxt = jnp.minimum(i + 1, n_tiles - 1)
    for buffer in range(2):
        @pl.when((i + 1 < n_tiles) & (nxt % 2 == buffer))
        def _():
            fetch(nxt, buffer)

    x = x_ref[...]
    hf = (x * (1.0 + sc_ref[...]) + sh_ref[...]).astype(BF16)
    f = wsdn_ref.shape[0]
    su = jnp.dot(hf, wsgu_ref[...], preferred_element_type=F32)
    act = (_silu(su[:, :f]) * su[:, f:]).astype(BF16)
    shared = jnp.dot(act, wsdn_ref[...], preferred_element_type=F32)

    lane = lax.broadcasted_iota(I32, (tr, LOCAL_ROWS), 1)
    weights = jnp.zeros((tr, LOCAL_ROWS), F32)
    for k in range(TOP_K):
        weights = jnp.where(lane == pos_ref[:, k:k + 1], wt_ref[:, k:k + 1], weights)
    w_hi = weights.astype(BF16)
    w_lo = (weights - w_hi.astype(F32)).astype(BF16)
    w_both = jnp.concatenate([w_hi, w_lo], axis=0)

    buf = stage_ref.at[i % 2]
    _wait_chunks(tile_ref[i], yg_hbm, buf, sems.at[i % 2])

    y_hi, y_lo = _unpack_bf16_pairs(buf[...])
    y = jnp.concatenate([y_hi.astype(BF16), y_lo.astype(BF16)], axis=1)
    both = jnp.dot(w_both, y, preferred_element_type=F32)
    routed = both[:tr, :] + both[tr:, :]
    v = ALPHA * x + (1.0 + g_ref[...]) * (routed + shared)
    o_ref[...] = _layer_norm(v, lng_ref[...], lnb_ref[...])


def _combine(plan, x2, mod3, pos_tok, w_tok, w_sh_gu, w_sh_down, ln_g, ln_b, yg, seq):
    t, d = x2.shape
    tr = TOK_TILE
    tpb = seq // tr
    full = lambda shape: pl.BlockSpec(shape, lambda i, *_: (0,) * len(shape))
    return pl.pallas_call(
        _combine_kernel,
        grid_spec=pltpu.PrefetchScalarGridSpec(
            num_scalar_prefetch=2,
            grid=(t // tr,),
            in_specs=[pl.BlockSpec((tr, d), lambda i, *_: (i, 0)),
                      _mod_spec(3, d, tpb), _mod_spec(4, d, tpb), _mod_spec(5, d, tpb),
                      pl.BlockSpec((tr, TOP_K), lambda i, *_: (i, 0)),
                      pl.BlockSpec((tr, TOP_K), lambda i, *_: (i, 0)),
                      full(w_sh_gu.shape), full(w_sh_down.shape), full((1, d)), full((1, d)),
                      pl.BlockSpec(memory_space=pl.ANY)],
            out_specs=pl.BlockSpec((tr, d), lambda i, *_: (i, 0)),
            scratch_shapes=[pltpu.VMEM((2, LOCAL_ROWS, d // 2), U32),
                            pltpu.SemaphoreType.DMA((2,))],
        ),
        out_shape=jax.ShapeDtypeStruct((t, d), F32),
        compiler_params=_cparams(("arbitrary",)),
    )(plan["slot_chunk"], plan["tile_chunks"],
      x2, mod3, mod3, mod3, pos_tok, w_tok, w_sh_gu, w_sh_down,
      ln_g.reshape(1, d), ln_b.reshape(1, d), yg)


def _moe_layer(x2, mod3, w_router, router_b, w_gu, w_down, layer, w_sh_gu, w_sh_down,
               ln_g, ln_b, seq):
    t, d = x2.shape
    n_tiles = t // TOK_TILE
    logits = _router(x2, mod3, w_router, seq)
    w_top, pos, counts = _topk_route(logits.T, router_b)
    bound = t * TOP_K + n_tiles * N_EXPERTS * (CHUNK - 1) + N_EXPERTS * (EXPERT_ROWS - 1)
    n_blocks = -(-bound // EXPERT_ROWS)
    plan = _slot_plan(counts, n_tiles, n_blocks)
    xg = _dispatch(plan, x2, mod3, pos, n_blocks * EXPERT_ROWS, seq)
    yg = _expert_gemm(plan, xg, w_gu, w_down, layer)
    return _combine(plan, x2, mod3, pos.T, w_top.T, w_sh_gu, w_sh_down, ln_g, ln_b, yg, seq)


def kernel(x, c, ada_w, ada_b, pool_w_in, pool_w_grp, pool_scale, pool_w_out, attn_w_in, attn_w_out, ln1_g, ln1_b, router_w, router_b, exp_w_gu, exp_w_down, sh_w_gu, sh_w_down, ln2_g, ln2_b):
    batch, seq, d = x.shape
    depth = ada_w.shape[0]
    t = batch * seq
    mod = _modulation(c, ada_w, ada_b).reshape(depth, batch, 1, 6 * d)
    perms = jnp.stack([_perm_matrix(dil) for _, dil in ATTN_PATTERNS[1:]])
    perms_t = jnp.swapaxes(perms, 1, 2)
    expand = (jnp.arange(LSE_LANES)[:, None] == (jnp.arange(d)[None, :] // HEAD_DIM)).astype(BF16)
    x2 = x.reshape(t, d)
    for i in range(depth):
        mod3 = mod[i]
        j = i // 2
        if i % 2 == 0:
            x2 = _pool_layer(x2, mod3, pool_w_in[j].astype(BF16), pool_w_grp[j].astype(BF16),
                             pool_scale[j], pool_w_out[j].astype(BF16), ln1_g[i], ln1_b[i], seq)
        else:
            qkv = _qkv_proj(x2, mod3, perms, attn_w_in[j].astype(BF16), seq)
            res = [_attention_group(qkv, g, batch, seq) for g in range(len(ATTN_PATTERNS))]
            x2 = _attn_out(x2, mod3, [r[0] for r in res], [r[1] for r in res], perms_t, expand,
                           attn_w_out[j].astype(BF16), ln1_g[i], ln1_b[i], seq)
        x2 = _moe_layer(x2, mod3, router_w[i], router_b[i], exp_w_gu, exp_w_down, i,
                        sh_w_gu[i].astype(BF16), sh_w_down[i].astype(BF16),
                        ln2_g[i], ln2_b[i], seq)
    return x2.reshape(batch, seq, d)
```

```python
import functools
import math

import jax
import jax.numpy as jnp
from jax import lax
from jax.experimental import pallas as pl
from jax.experimental.pallas import tpu as pltpu

F32 = jnp.float32
BF16 = jnp.bfloat16
U32 = jnp.uint32
I32 = jnp.int32

POOL_WINDOWS = (2, 4, 8, 16)
ATTN_PATTERNS = ((128, 1), (512, 4), (2048, 16))
HEAD_DIM = 64
N_HEADS = 16
Q_BLOCK = 128
N_EXPERTS = 64
TOP_K = 8
N_EXPERT_GROUPS = 8
TOPK_GROUPS = 4
ROUTED_SCALE = 2.5
EXPERT_ROWS = 256
DEPTH = 4
ALPHA = (2 * DEPTH) ** 0.25
LN_EPS = 1e-5

PERM_TILE = 256
POOL_HALO = 16
TOK_TILE = 256
CHUNK = 8
LOCAL_ROWS = -(-(TOK_TILE * TOP_K + N_EXPERTS * (CHUNK - 1)) // 256) * 256
NO_SLOT = -64.0
POS_RADIX = 64.0
WAIT_GROUP = 32
GEMM_DEPTH = 4
LSE_LANES = 128
VMEM_LIMIT = 56 * 1024 * 1024
NEG_BIG = -1e30


def _cparams(sem):
    return pltpu.CompilerParams(dimension_semantics=sem, vmem_limit_bytes=VMEM_LIMIT)


def _layer_norm(v, g, b):
    mu = jnp.mean(v, axis=-1, keepdims=True)
    c = v - mu
    var = jnp.mean(c * c, axis=-1, keepdims=True)
    return c * lax.rsqrt(var + LN_EPS) * g + b


def _silu(v):
    return v * (1.0 / (1.0 + jnp.exp(-v)))


def _pack_bf16_pairs(v):
    n = v.shape[1] // 2
    hi = lax.bitcast_convert_type(v[:, :n].astype(BF16).astype(F32), U32)
    lo = lax.bitcast_convert_type(v[:, n:].astype(BF16).astype(F32), U32)
    return hi | (lo >> 16)


def _unpack_bf16_pairs(p):
    hi = lax.bitcast_convert_type(p & jnp.uint32(0xFFFF0000), F32)
    lo = lax.bitcast_convert_type(p << 16, F32)
    return hi, lo


def _mod_kernel(c_ref, w_ref, b_ref, o_ref):
    cs = _silu(c_ref[...])
    o_ref[...] = jnp.dot(cs, w_ref[...], preferred_element_type=F32) + b_ref[...]


def _modulation(c, ada_w, ada_b):
    depth, d, n6 = ada_w.shape
    b = c.shape[0]
    tn = 1536
    return pl.pallas_call(
        _mod_kernel,
        grid=(depth, n6 // tn),
        in_specs=[
            pl.BlockSpec((b, d), lambda i, n: (0, 0)),
            pl.BlockSpec((None, d, tn), lambda i, n: (i, 0, n)),
            pl.BlockSpec((None, 1, tn), lambda i, n: (i, 0, n)),
        ],
        out_specs=pl.BlockSpec((None, b, tn), lambda i, n: (i, 0, n)),
        out_shape=jax.ShapeDtypeStruct((depth, b, n6), F32),
        compiler_params=_cparams(("arbitrary", "arbitrary")),
    )(c, ada_w, ada_b.reshape(depth, 1, n6))


def _mod_spec(chunk, d, tiles_per_batch):
    return pl.BlockSpec((None, 1, d), lambda *idx: (idx[0] // tiles_per_batch, 0, chunk))


def _pool_kernel(x_ref, sh_ref, sc_ref, g_ref, win_ref, wgrp_ref, cs_ref, wout_ref,
                 lng_ref, lnb_ref, o_ref, ext_ref, *, tiles_per_batch):
    tm, d = x_ref.shape
    s_idx = pl.program_id(0) % tiles_per_batch
    x = x_ref[...]
    h = (x * (1.0 + sc_ref[...]) + sh_ref[...]).astype(BF16)
    u = jnp.dot(h, win_ref[...], preferred_element_type=F32)

    @pl.when(s_idx == 0)
    def _():
        ext_ref[0:POOL_HALO, :] = jnp.zeros((POOL_HALO, d), F32)

    @pl.when(s_idx != 0)
    def _():
        ext_ref[0:POOL_HALO, :] = ext_ref[tm:tm + POOL_HALO, :]

    ext_ref[POOL_HALO:POOL_HALO + tm, :] = u

    pos = s_idx * tm + lax.broadcasted_iota(I32, (tm, 1), 0) + 1
    gc = d // len(POOL_WINDOWS)
    ys = []
    for g, w in enumerate(POOL_WINDOWS):
        cols = slice(g * gc, (g + 1) * gc)
        acc = u[:, cols]
        for j in range(1, w):
            acc = acc + ext_ref[POOL_HALO - j:POOL_HALO - j + tm, cols]
        cnt = jnp.minimum(pos, w).astype(F32)
        z = (acc / cnt - u[:, cols]).astype(BF16)
        ys.append(jnp.dot(z, wgrp_ref[g], preferred_element_type=F32))
    y = (jnp.concatenate(ys, axis=1) * cs_ref[...]).astype(BF16)
    out = jnp.dot(y, wout_ref[...], preferred_element_type=F32)
    v = ALPHA * x + (1.0 + g_ref[...]) * out
    o_ref[...] = _layer_norm(v, lng_ref[...], lnb_ref[...])


def _pool_layer(x2, mod3, w_in, w_grp, ch_scale, w_out, ln_g, ln_b, seq):
    t, d = x2.shape
    tm = 512
    tpb = seq // tm
    full = lambda shape: pl.BlockSpec(shape, lambda i: (0,) * len(shape))
    return pl.pallas_call(
        functools.partial(_pool_kernel, tiles_per_batch=tpb),
        grid=(t // tm,),
        in_specs=[
            pl.BlockSpec((tm, d), lambda i: (i, 0)),
            _mod_spec(0, d, tpb), _mod_spec(1, d, tpb), _mod_spec(2, d, tpb),
            full((d, d)), full(w_grp.shape), full((1, d)), full((d, d)),
            full((1, d)), full((1, d)),
        ],
        out_specs=pl.BlockSpec((tm, d), lambda i: (i, 0)),
        out_shape=jax.ShapeDtypeStruct((t, d), F32),
        scratch_shapes=[pltpu.VMEM((tm + POOL_HALO, d), F32)],
        compiler_params=_cparams(("arbitrary",)),
    )(x2, mod3, mod3, mod3, w_in, w_grp, ch_scale.reshape(1, d), w_out,
      ln_g.reshape(1, d), ln_b.reshape(1, d))


def _perm_matrix(dil):
    p = jnp.arange(PERM_TILE)
    chunk = PERM_TILE // dil
    src = (p % chunk) * dil + p // chunk
    return (src[:, None] == jnp.arange(PERM_TILE)[None, :]).astype(BF16)


def _qkv_kernel(x_ref, sh_ref, sc_ref, p_ref, w_ref, o_ref, h_ref):
    tm = x_ref.shape[0]
    g = pl.program_id(1)
    part = pl.program_id(2)

    @pl.when((g == 0) & (part == 0))
    def _():
        h = (x_ref[...] * (1.0 + sc_ref[...]) + sh_ref[...]).astype(BF16)
        h_ref[0] = h
        for gi in range(1, len(ATTN_PATTERNS)):
            for s in range(tm // PERM_TILE):
                rows = slice(s * PERM_TILE, (s + 1) * PERM_TILE)
                h_ref[gi, rows, :] = jnp.dot(
                    p_ref[gi - 1], h[rows, :], preferred_element_type=F32).astype(BF16)

    o_ref[...] = jnp.dot(h_ref[g], w_ref[...], preferred_element_type=F32).astype(BF16)


def _qkv_proj(x2, mod3, perms, w_in, seq):
    t, d = x2.shape
    ng = len(ATTN_PATTERNS)
    tm = 1024
    tpb = seq // tm
    return pl.pallas_call(
        _qkv_kernel,
        grid=(t // tm, ng, 3),
        in_specs=[
            pl.BlockSpec((tm, d), lambda m, g, p: (m, 0)),
            _mod_spec(0, d, tpb), _mod_spec(1, d, tpb),
            pl.BlockSpec(perms.shape, lambda m, g, p: (0, 0, 0)),
            pl.BlockSpec((d, d), lambda m, g, p: (0, g * 3 + p)),
        ],
        out_specs=pl.BlockSpec((tm, d), lambda m, g, p: (m, g * 3 + p)),
        out_shape=jax.ShapeDtypeStruct((t, ng * 3 * d), BF16),
        scratch_shapes=[pltpu.VMEM((ng, tm, d), BF16)],
        compiler_params=_cparams(("arbitrary", "arbitrary", "arbitrary")),
    )(x2, mod3, mod3, perms, w_in)


def _attn_kernel(q_ref, kp_ref, kc_ref, vp_ref, vc_ref, o_ref, lse_ref, *, group, dil):
    bq = Q_BLOCK
    d = N_HEADS * HEAD_DIM
    j = pl.program_id(2)
    q = q_ref[...].reshape(bq, d) * (HEAD_DIM ** -0.5)
    kp = kp_ref[...].reshape(bq, d)
    kc = kc_ref[...].reshape(bq, d)
    vp = vp_ref[...].reshape(bq, d)
    vc = vc_ref[...].reshape(bq, d)

    qi = lax.broadcasted_iota(I32, (bq, 2 * bq), 0)
    kj = lax.broadcasted_iota(I32, (bq, 2 * bq), 1)
    dist = qi + bq - kj
    steps = ATTN_PATTERNS[group][0] // dil
    valid = (dist >= 0) & (dist <= steps) & ((kj >= bq) | (j > 0))
    neg_dist = jnp.where(valid, (dist * -dil).astype(F32), NEG_BIG)
    lane = lax.broadcasted_iota(I32, (bq, LSE_LANES), 1)
    lse_tile = jnp.zeros((bq, LSE_LANES), F32)
    n_tot = len(ATTN_PATTERNS) * N_HEADS
    outs = []
    for h in range(N_HEADS):
        cols = slice(h * HEAD_DIM, (h + 1) * HEAD_DIM)
        slope = 2.0 ** (-8.0 * (group * N_HEADS + h + 1) / n_tot)
        kh = jnp.concatenate([kp[:, cols], kc[:, cols]], axis=0)
        vh = jnp.concatenate([vp[:, cols], vc[:, cols]], axis=0)
        s = lax.dot_general(q[:, cols], kh, (((1,), (1,)), ((), ())),
                            preferred_element_type=F32)
        s = s + slope * neg_dist
        m = jnp.max(s, axis=-1, keepdims=True)
        p = jnp.exp(s - m)
        den = jnp.sum(p, axis=-1, keepdims=True)
        o = jnp.dot(p.astype(BF16), vh, preferred_element_type=F32) / den
        outs.append(o)
        lse_tile = jnp.where(lane == h, m + jnp.log(den), lse_tile)
    o_ref[...] = jnp.concatenate(outs, axis=1).astype(BF16).reshape(o_ref.shape)
    lse_ref[...] = lse_tile.reshape(lse_ref.shape)


def _attention_group(qkv, group, batch, seq):
    dil = ATTN_PATTERNS[group][1]
    d = N_HEADS * HEAD_DIM
    t = qkv.shape[0]
    sub = seq // dil
    nb = sub // Q_BLOCK
    rows = Q_BLOCK if dil == 1 else PERM_TILE // dil
    chunks = Q_BLOCK // rows
    u = seq // (rows * dil)
    view = lambda a, c: a.reshape(batch, u, dil, rows, c)
    blk = lambda c: (None, chunks, None, rows, c)
    col0 = group * 3
    q_spec = pl.BlockSpec(blk(d), lambda b, r, j: (b, j, r, 0, col0))
    kc_spec = pl.BlockSpec(blk(d), lambda b, r, j: (b, j, r, 0, col0 + 1))
    kp_spec = pl.BlockSpec(blk(d), lambda b, r, j: (b, jnp.maximum(j - 1, 0), r, 0, col0 + 1))
    vc_spec = pl.BlockSpec(blk(d), lambda b, r, j: (b, j, r, 0, col0 + 2))
    vp_spec = pl.BlockSpec(blk(d), lambda b, r, j: (b, jnp.maximum(j - 1, 0), r, 0, col0 + 2))
    qkv5 = view(qkv, qkv.shape[1])
    o, lse = pl.pallas_call(
        functools.partial(_attn_kernel, group=group, dil=dil),
        grid=(batch, dil, nb),
        in_specs=[q_spec, kp_spec, kc_spec, vp_spec, vc_spec],
        out_specs=[
            pl.BlockSpec(blk(d), lambda b, r, j: (b, j, r, 0, 0)),
            pl.BlockSpec(blk(LSE_LANES), lambda b, r, j: (b, j, r, 0, 0)),
        ],
        out_shape=[
            jax.ShapeDtypeStruct((batch, u, dil, rows, d), BF16),
            jax.ShapeDtypeStruct((batch, u, dil, rows, LSE_LANES), F32),
        ],
        compiler_params=_cparams(("arbitrary", "arbitrary", "arbitrary")),
    )(qkv5, qkv5, qkv5, qkv5, qkv5)
    return o.reshape(t, d), lse.reshape(t, LSE_LANES)


def _split3(v):
    a = v.astype(BF16)
    r = v - a.astype(F32)
    b = r.astype(BF16)
    c = (r - b.astype(F32)).astype(BF16)
    return a, b, c


def _attn_out_kernel(x_ref, g_ref, o0_ref, o1_ref, o2_ref, l0_ref, l1_ref, l2_ref,
                     pt_ref, e_ref, wout_ref, lng_ref, lnb_ref, out_ref):
    tm, d = x_ref.shape
    o_refs = (o0_ref, o1_ref, o2_ref)
    l_refs = (l0_ref, l1_ref, l2_ref)
    n_sub = tm // PERM_TILE

    def unperm(gi, val_bf16):
        if gi == 0:
            return val_bf16.astype(F32)
        parts = [jnp.dot(pt_ref[gi - 1], val_bf16[s * PERM_TILE:(s + 1) * PERM_TILE, :],
                         preferred_element_type=F32) for s in range(n_sub)]
        return jnp.concatenate(parts, axis=0)

    lses = []
    for gi in range(3):
        l = l_refs[gi][...]
        if gi == 0:
            lses.append(l)
        else:
            a, b, c = _split3(l)
            lses.append(unperm(gi, a) + unperm(gi, b) + unperm(gi, c))
    mx = jnp.maximum(jnp.maximum(lses[0], lses[1]), lses[2])
    es = [jnp.exp(l - mx) for l in lses]
    tot = es[0] + es[1] + es[2]
    mixed = jnp.zeros((tm, d), F32)
    for gi in range(3):
        w = es[gi] / tot
        a, b, c = _split3(w)
        wide = (jnp.dot(a, e_ref[...], preferred_element_type=F32)
                + jnp.dot(b, e_ref[...], preferred_element_type=F32)
                + jnp.dot(c, e_ref[...], preferred_element_type=F32))
        mixed = mixed + wide * unperm(gi, o_refs[gi][...])
    y = jnp.dot(mixed.astype(BF16), wout_ref[...], preferred_element_type=F32)
    v = ALPHA * x_ref[...] + (1.0 + g_ref[...]) * y
    out_ref[...] = _layer_norm(v, lng_ref[...], lnb_ref[...])


def _attn_out(x2, mod3, os_, lses, perms_t, expand, w_out, ln_g, ln_b, seq):
    t, d = x2.shape
    tm = 512
    tpb = seq // tm
    row = lambda c: pl.BlockSpec((tm, c), lambda i: (i, 0))
    full = lambda shape: pl.BlockSpec(shape, lambda i: (0,) * len(shape))
    return pl.pallas_call(
        _attn_out_kernel,
        grid=(t // tm,),
        in_specs=[row(d), _mod_spec(2, d, tpb), row(d), row(d), row(d),
                  row(LSE_LANES), row(LSE_LANES), row(LSE_LANES),
                  full(perms_t.shape), full(expand.shape), full((d, d)),
                  full((1, d)), full((1, d))],
        out_specs=row(d),
        out_shape=jax.ShapeDtypeStruct((t, d), F32),
        compiler_params=_cparams(("arbitrary",)),
    )(x2, mod3, *os_, *lses, perms_t, expand, w_out, ln_g.reshape(1, d), ln_b.reshape(1, d))


def _router_kernel(x_ref, sh_ref, sc_ref, wr_ref, lg_ref):
    hf = x_ref[...] * (1.0 + sc_ref[...]) + sh_ref[...]
    lg_ref[...] = jnp.dot(hf, wr_ref[...], preferred_element_type=F32,
                          precision=lax.Precision.HIGHEST)


def _router(x2, mod3, w_router, seq):
    t, d = x2.shape
    tm = 512
    tpb = seq // tm
    ne = w_router.shape[1]
    return pl.pallas_call(
        _router_kernel,
        grid=(t // tm,),
        in_specs=[pl.BlockSpec((tm, d), lambda i: (i, 0)),
                  _mod_spec(3, d, tpb), _mod_spec(4, d, tpb),
                  pl.BlockSpec((d, ne), lambda i: (0, 0))],
        out_specs=pl.BlockSpec((tm, ne), lambda i: (i, 0)),
        out_shape=jax.ShapeDtypeStruct((t, ne), F32),
        compiler_params=_cparams(("arbitrary",)),
    )(x2, mod3, mod3, w_router)


def _first_index_of_max(v, iota, size):
    m = jnp.max(v, axis=0, keepdims=True)
    idx = jnp.min(jnp.where(v == m, iota, float(size)), axis=0, keepdims=True)
    return m, idx


def _topk_kernel(lg_ref, b_ref, tri_ref, ltri_ref, w_ref, pos_ref, cnt_ref):
    ne, tr = lg_ref.shape
    gsz = ne // N_EXPERT_GROUPS
    tile = pl.program_id(0)

    scores = 1.0 / (1.0 + jnp.exp(-lg_ref[...]))
    sel = scores + b_ref[...]
    iota_g = lax.broadcasted_iota(I32, (gsz, tr), 0).astype(F32)
    iota_n = lax.broadcasted_iota(I32, (N_EXPERT_GROUPS, tr), 0).astype(F32)
    gs = jnp.zeros((N_EXPERT_GROUPS, tr), F32)
    for g in range(N_EXPERT_GROUPS):
        blk = sel[g * gsz:(g + 1) * gsz, :]
        m1, i1 = _first_index_of_max(blk, iota_g, gsz)
        m2 = jnp.max(jnp.where(iota_g == i1, -jnp.inf, blk), axis=0, keepdims=True)
        gs = jnp.where(iota_n == float(g), m1 + m2, gs)
    gmask = jnp.zeros((N_EXPERT_GROUPS, tr), F32)
    for _ in range(TOPK_GROUPS):
        _, gi = _first_index_of_max(gs, iota_n, N_EXPERT_GROUPS)
        hit = iota_n == gi
        gmask = jnp.where(hit, 1.0, gmask)
        gs = jnp.where(hit, -jnp.inf, gs)
    masked_rows = []
    for g in range(N_EXPERT_GROUPS):
        keep = jnp.broadcast_to(gmask[g:g + 1, :], (gsz, tr)) > 0.5
        masked_rows.append(jnp.where(keep, sel[g * gsz:(g + 1) * gsz, :], -jnp.inf))
    cur = jnp.concatenate(masked_rows, axis=0)
    iota_e = lax.broadcasted_iota(I32, (ne, tr), 0).astype(F32)
    chosen = jnp.zeros((ne, tr), F32)
    for _ in range(TOP_K):
        _, ei = _first_index_of_max(cur, iota_e, ne)
        hit = iota_e == ei
        cur = jnp.where(hit, -jnp.inf, cur)
        chosen = jnp.where(hit, 1.0, chosen)
    picked = chosen > 0.5
    top_scores = jnp.where(picked, scores, 0.0)
    wsum = jnp.sum(top_scores, axis=0, keepdims=True)
    w_ref[...] = top_scores / wsum * ROUTED_SCALE
    before = jnp.dot(chosen.astype(BF16), tri_ref[...], preferred_element_type=F32)
    n = jnp.sum(chosen, axis=1, keepdims=True)
    n_chunks = jnp.floor((n + (CHUNK - 1)) * (1.0 / CHUNK))
    run_off = jnp.dot(ltri_ref[...], jnp.broadcast_to(n_chunks, (ne, 128)).astype(BF16),
                      preferred_element_type=F32)[:, 0:1] * CHUNK
    pos_ref[...] = jnp.where(picked, before + run_off, NO_SLOT)

    @pl.when(tile == 0)
    def _():
        cnt_ref[...] = jnp.zeros_like(cnt_ref)

    lane = lax.broadcasted_iota(I32, cnt_ref.shape, 1)
    cnt_ref[...] = jnp.where(lane == tile, jnp.broadcast_to(n, cnt_ref.shape).astype(I32),
                             cnt_ref[...])


def _topk_route(logits_t, router_b):
    ne, t = logits_t.shape
    tr = TOK_TILE
    assert t // tr <= 128
    tri = (jnp.arange(tr)[:, None] < jnp.arange(tr)[None, :]).astype(BF16)
    ltri = (jnp.arange(ne)[None, :] < jnp.arange(ne)[:, None]).astype(BF16)
    out = lambda dt: jax.ShapeDtypeStruct((ne, t), dt)
    row = pl.BlockSpec((ne, tr), lambda i: (0, i))
    return pl.pallas_call(
        _topk_kernel,
        grid=(t // tr,),
        in_specs=[pl.BlockSpec((ne, tr), lambda i: (0, i)),
                  pl.BlockSpec((ne, 1), lambda i: (0, 0)),
                  pl.BlockSpec((tr, tr), lambda i: (0, 0)),
                  pl.BlockSpec((ne, ne), lambda i: (0, 0))],
        out_specs=[row, row, pl.BlockSpec((ne, 128), lambda i: (0, 0))],
        out_shape=[out(F32), out(F32), jax.ShapeDtypeStruct((ne, 128), I32)],
        compiler_params=_cparams(("arbitrary",)),
    )(logits_t, router_b.reshape(ne, 1), tri, ltri)


def _slot_plan(counts, n_tiles, n_blocks):
    n = counts[:, :n_tiles]
    nch = (n + (CHUNK - 1)) // CHUNK
    rows = jnp.sum(nch, axis=1) * CHUNK
    region = ((rows + EXPERT_ROWS - 1) // EXPERT_ROWS) * EXPERT_ROWS
    region_end = jnp.cumsum(region)
    region_start = region_end - region
    run_chunk = region_start[:, None] // CHUNK + jnp.cumsum(nch, axis=1) - nch
    local_end = jnp.cumsum(nch, axis=0)
    local_chunk = local_end - nch
    c = jnp.arange(LOCAL_ROWS // CHUNK, dtype=I32)
    expert_of_c = jnp.sum((local_end[:, :, None] <= c[None, None, :]).astype(I32), axis=0)
    owner = expert_of_c[None] == jnp.arange(N_EXPERTS, dtype=I32)[:, None, None]
    slot_chunk = jnp.sum(jnp.where(owner, (run_chunk - local_chunk)[:, :, None], 0), axis=0) + c
    block_row = jnp.arange(n_blocks, dtype=I32) * EXPERT_ROWS
    group_e = jnp.minimum(jnp.sum((region_end[None, :] <= block_row[:, None]).astype(I32), axis=1),
                          N_EXPERTS - 1)
    n_used = region_end[-1] // EXPERT_ROWS
    nonempty = region > 0
    run_of_expert = jnp.cumsum(nonempty.astype(I32)) - 1
    experts = jnp.arange(N_EXPERTS, dtype=I32)
    run_expert = jnp.sum(jnp.where((run_of_expert[None, :] == experts[:, None]) & nonempty[None, :],
                                   experts[None, :], 0), axis=1)
    blocks = jnp.arange(n_blocks, dtype=I32)
    first = ((blocks == 0) | (group_e != jnp.roll(group_e, 1))) & (blocks < n_used)
    per_tile = lambda a: (a * CHUNK).T.astype(I32)
    return dict(
        run_begin_rows=per_tile(local_chunk)[:, None, :], run_end_rows=per_tile(local_end)[:, None, :],
        run_begin_cols=per_tile(local_chunk)[:, :, None], run_end_cols=per_tile(local_end)[:, :, None],
        block_run=(jnp.cumsum(first.astype(I32)) - 1).astype(I32), block_first=first.astype(I32),
        run_expert=run_expert.astype(I32), n_runs=jnp.sum(nonempty.astype(I32)).reshape(1),
        slot_chunk=slot_chunk.reshape(-1).astype(I32),
        tile_chunks=jnp.sum(nch, axis=0).astype(I32),
        tail_chunk=((region_start + rows) // CHUNK).astype(I32),
        tail_chunks=((region - rows) // CHUNK).astype(I32),
        group_e=group_e, n_used=(region_end[-1:] // EXPERT_ROWS).astype(I32))


def _chunk_rows(ref, chunk_index):
    start = chunk_index * CHUNK
    if not isinstance(start, int):
        start = pl.multiple_of(start, CHUNK)
    return ref.at[pl.ds(start, CHUNK), :]


def _pos_digits(pos):
    hi = jnp.floor(pos * (1.0 / POS_RADIX))
    return hi.astype(BF16), (pos - POS_RADIX * hi).astype(BF16)


def _wait_chunks(count, src_ref, dst_ref, sem):
    group_rows = pl.ds(0, WAIT_GROUP * CHUNK)

    def wait_group(c, carry):
        pltpu.make_async_copy(src_ref.at[group_rows, :], dst_ref.at[group_rows, :], sem).wait()
        return carry

    def wait_one(c, carry):
        pltpu.make_async_copy(_chunk_rows(src_ref, 0), _chunk_rows(dst_ref, 0), sem).wait()
        return carry

    lax.fori_loop(0, lax.shift_right_logical(count, WAIT_GROUP.bit_length() - 1), wait_group, 0)
    lax.fori_loop(0, count & (WAIT_GROUP - 1), wait_one, 0)


def _dispatch_kernel(slot_ref, tile_ref, tail_ref, tailn_ref, used_ref,
                     x_ref, sh_ref, sc_ref, pos_ref, begin_ref, end_ref,
                     xg_hbm, stage_ref, zero_ref, sems, sem_blk):
    tr = x_ref.shape[0]
    i = pl.program_id(0)
    n_blocks = xg_hbm.shape[0] // EXPERT_ROWS
    chunks_per_tile = LOCAL_ROWS // CHUNK

    def drain(count, sem):
        _wait_chunks(count, stage_ref.at[0], xg_hbm, sem)

    @pl.when(i == 0)
    def _():
        zero_ref[...] = jnp.zeros_like(zero_ref)

        def block_copy(j):
            rows = pl.ds(pl.multiple_of(j * EXPERT_ROWS, EXPERT_ROWS), EXPERT_ROWS)
            return pltpu.make_async_copy(zero_ref, xg_hbm.at[rows, :], sem_blk)

        def start_block(j, carry):
            block_copy(j).start()
            return carry

        def wait_block(j, carry):
            block_copy(j).wait()
            return carry

        lax.fori_loop(used_ref[0], n_blocks, start_block, 0)

        def per_expert(e, total):
            def per_chunk(c, carry):
                pltpu.make_async_copy(_chunk_rows(zero_ref, 0),
                                      _chunk_rows(xg_hbm, tail_ref[e] + c), sems.at[0]).start()
                return carry
            lax.fori_loop(0, tailn_ref[e], per_chunk, 0)
            return total + tailn_ref[e]

        drain(lax.fori_loop(0, N_EXPERTS, per_expert, 0), sems.at[0])
        lax.fori_loop(used_ref[0], n_blocks, wait_block, 0)

    hf = (x_ref[...] * (1.0 + sc_ref[...]) + sh_ref[...]).astype(BF16)
    row_e = lax.broadcasted_iota(I32, (LOCAL_ROWS, N_EXPERTS), 0)
    owner = jnp.where((row_e >= begin_ref[...]) & (row_e < end_ref[...]), 1.0, 0.0).astype(BF16)
    hi_digit, lo_digit = _pos_digits(pos_ref[...])
    slot_pos = (POS_RADIX * jnp.dot(owner, hi_digit, preferred_element_type=F32)
                + jnp.dot(owner, lo_digit, preferred_element_type=F32))
    row_t = lax.broadcasted_iota(I32, (LOCAL_ROWS, tr), 0).astype(F32)
    perm = jnp.where(slot_pos == row_t, 1.0, 0.0).astype(BF16)
    rows = jnp.dot(perm, hf, preferred_element_type=F32)
    half = rows.shape[1] // 2
    buf = stage_ref.at[i % 2]
    buf[...] = (lax.bitcast_convert_type(rows[:, :half], U32)
                | (lax.bitcast_convert_type(rows[:, half:], U32) >> 16))

    def send(c, carry):
        pltpu.make_async_copy(_chunk_rows(buf, c),
                              _chunk_rows(xg_hbm, slot_ref[i * chunks_per_tile + c]),
                              sems.at[i % 2]).start()
        return carry

    lax.fori_loop(0, tile_ref[i], send, 0)

    @pl.when(i > 0)
    def _():
        drain(tile_ref[jnp.maximum(i - 1, 0)], sems.at[(i + 1) % 2])

    @pl.when(i == pl.num_programs(0) - 1)
    def _():
        drain(tile_ref[i], sems.at[i % 2])


def _dispatch(plan, x2, mod3, pos, n_slots, seq):
    t, d = x2.shape
    tr = TOK_TILE
    tpb = seq // tr
    return pl.pallas_call(
        _dispatch_kernel,
        grid_spec=pltpu.PrefetchScalarGridSpec(
            num_scalar_prefetch=5,
            grid=(t // tr,),
            in_specs=[pl.BlockSpec((tr, d), lambda i, *_: (i, 0)),
                      _mod_spec(3, d, tpb), _mod_spec(4, d, tpb),
                      pl.BlockSpec((N_EXPERTS, tr), lambda i, *_: (0, i)),
                      pl.BlockSpec((None, 1, N_EXPERTS), lambda i, *_: (i, 0, 0)),
                      pl.BlockSpec((None, 1, N_EXPERTS), lambda i, *_: (i, 0, 0))],
            out_specs=pl.BlockSpec(memory_space=pl.ANY),
            scratch_shapes=[pltpu.VMEM((2, LOCAL_ROWS, d // 2), U32),
                            pltpu.VMEM((EXPERT_ROWS, d // 2), U32),
                            pltpu.SemaphoreType.DMA((2,)), pltpu.SemaphoreType.DMA],
        ),
        out_shape=jax.ShapeDtypeStruct((n_slots, d // 2), U32),
        compiler_params=_cparams(("arbitrary",)),
    )(plan["slot_chunk"], plan["tile_chunks"], plan["tail_chunk"], plan["tail_chunks"],
      plan["n_used"], x2, mod3, mod3, pos, plan["run_begin_rows"], plan["run_end_rows"])


def _expert_kernel(run_ref, first_ref, rexp_ref, nrun_ref, nu_ref,
                   xg_hbm, wgu_hbm, wdn_hbm, yg_hbm,
                   xbuf, ybuf, wgu_f32, wdn_f32, wgu_bf, wdn_bf, x_sem, y_sem, w_sem, z_sem,
                   *, layer):
    n_used = nu_ref[0]
    n_blocks = yg_hbm.shape[0] // EXPERT_ROWS
    depth, _, half = xbuf.shape
    f = wdn_bf.shape[0]

    def rows(j):
        return pl.ds(pl.multiple_of(j * EXPERT_ROWS, EXPERT_ROWS), EXPERT_ROWS)

    def x_copy(j):
        return pltpu.make_async_copy(xg_hbm.at[rows(j), :], xbuf.at[j % depth],
                                     x_sem.at[j % depth])

    def y_copy(j):
        return pltpu.make_async_copy(ybuf.at[j % depth], yg_hbm.at[rows(j), :],
                                     y_sem.at[j % depth])

    def w_copies(r):
        e = rexp_ref[r]
        return (pltpu.make_async_copy(wgu_hbm.at[layer, e], wgu_f32.at[r % 2], w_sem.at[r % 2]),
                pltpu.make_async_copy(wdn_hbm.at[layer, e], wdn_f32.at[r % 2], w_sem.at[r % 2]))

    def zero_copy(j):
        return pltpu.make_async_copy(ybuf.at[0], yg_hbm.at[rows(j), :], z_sem)

    ybuf[0] = jnp.zeros(ybuf.shape[1:], U32)

    def zero_start(j, carry):
        zero_copy(j).start()
        return carry

    def zero_wait(j, carry):
        zero_copy(j).wait()
        return carry

    lax.fori_loop(n_used, n_blocks, zero_start, 0)
    lax.fori_loop(n_used, n_blocks, zero_wait, 0)

    for cp in w_copies(0):
        cp.start()
    for a in range(depth - 1):
        @pl.when(a < n_used)
        def _():
            x_copy(a).start()

    def block(j, carry):
        r = run_ref[j]

        @pl.when(first_ref[j] == 1)
        def _():
            for cp in w_copies(r):
                cp.wait()

            @pl.when(r + 1 < nrun_ref[0])
            def _():
                for cp in w_copies(r + 1):
                    cp.start()

            wgu_bf[...] = wgu_f32[r % 2].astype(BF16)
            wdn_bf[...] = wdn_f32[r % 2].astype(BF16)

        x_copy(j).wait()

        @pl.when(j + (depth - 1) < n_used)
        def _():
            x_copy(j + (depth - 1)).start()

        @pl.when(j >= depth)
        def _():
            y_copy(j - depth).wait()

        hi, lo = _unpack_bf16_pairs(xbuf[j % depth])
        gu = (jnp.dot(hi.astype(BF16), wgu_bf[0:half, :], preferred_element_type=F32)
              + jnp.dot(lo.astype(BF16), wgu_bf[half:, :], preferred_element_type=F32))
        act = (_silu(gu[:, :f]) * gu[:, f:]).astype(BF16)
        ybuf[j % depth] = _pack_bf16_pairs(jnp.dot(act, wdn_bf[...], preferred_element_type=F32))
        y_copy(j).start()
        return carry

    lax.fori_loop(0, n_used, block, 0)

    for a in range(depth, 0, -1):
        @pl.when(n_used >= a)
        def _():
            y_copy(n_used - a).wait()


def _expert_gemm(plan, xg, w_gu, w_down, layer):
    n_slots, half = xg.shape
    _, ne, d, f2 = w_gu.shape
    f = w_down.shape[2]
    hbm = pl.BlockSpec(memory_space=pl.ANY)
    return pl.pallas_call(
        functools.partial(_expert_kernel, layer=layer),
        grid_spec=pltpu.PrefetchScalarGridSpec(
            num_scalar_prefetch=5,
            grid=(1,),
            in_specs=[hbm, hbm, hbm],
            out_specs=hbm,
            scratch_shapes=[pltpu.VMEM((GEMM_DEPTH, EXPERT_ROWS, half), U32),
                            pltpu.VMEM((GEMM_DEPTH, EXPERT_ROWS, half), U32),
                            pltpu.VMEM((2, d, f2), F32), pltpu.VMEM((2, f, d), F32),
                            pltpu.VMEM((d, f2), BF16), pltpu.VMEM((f, d), BF16),
                            pltpu.SemaphoreType.DMA((GEMM_DEPTH,)),
                            pltpu.SemaphoreType.DMA((GEMM_DEPTH,)),
                            pltpu.SemaphoreType.DMA((2,)), pltpu.SemaphoreType.DMA],
        ),
        out_shape=jax.ShapeDtypeStruct((n_slots, half), U32),
        compiler_params=_cparams(("arbitrary",)),
    )(plan["block_run"], plan["block_first"], plan["run_expert"], plan["n_runs"], plan["n_used"],
      xg, w_gu, w_down)


def _combine_kernel(slot_ref, tile_ref,
                    x_ref, sh_ref, sc_ref, g_ref, pos_ref, wt_ref, begin_ref, end_ref,
                    wsgu_ref, wsdn_ref, lng_ref, lnb_ref, yg_hbm, o_ref, stage_ref, sems):
    tr, d = x_ref.shape
    i = pl.program_id(0)
    n_tiles = pl.num_programs(0)
    chunks_per_tile = LOCAL_ROWS // CHUNK

    def fetch(tile, buffer):
        def body(c, carry):
            pltpu.make_async_copy(_chunk_rows(yg_hbm, slot_ref[tile * chunks_per_tile + c]),
                                  _chunk_rows(stage_ref.at[buffer], c), sems.at[buffer]).start()
            return carry
        lax.fori_loop(0, tile_ref[tile], body, 0)

    @pl.when(i == 0)
    def _():
        stage_ref[...] = jnp.zeros_like(stage_ref)
        fetch(0, 0)

    nxt = jnp.minimum(i + 1, n_tiles - 1)
    for buffer in range(2):
        @pl.when((i + 1 < n_tiles) & (nxt % 2 == buffer))
        def _():
            fetch(nxt, buffer)

    x = x_ref[...]
    hf = (x * (1.0 + sc_ref[...]) + sh_ref[...]).astype(BF16)
    f = wsdn_ref.shape[0]
    su = jnp.dot(hf, wsgu_ref[...], preferred_element_type=F32)
    act = (_silu(su[:, :f]) * su[:, f:]).astype(BF16)
    shared = jnp.dot(act, wsdn_ref[...], preferred_element_type=F32)

    col_e = lax.broadcasted_iota(I32, (N_EXPERTS, LOCAL_ROWS), 1)
    owner = jnp.where((col_e >= begin_ref[...]) & (col_e < end_ref[...]), 1.0, 0.0).astype(BF16)
    hi_digit, lo_digit = _pos_digits(pos_ref[...])
    slot_pos = (POS_RADIX * jnp.dot(hi_digit, owner, preferred_element_type=F32)
                + jnp.dot(lo_digit, owner, preferred_element_type=F32))
    holds = slot_pos == lax.broadcasted_iota(I32, (tr, LOCAL_ROWS), 1).astype(F32)
    w_tok = wt_ref[...]
    w_tok_hi = w_tok.astype(BF16)
    w_tok_lo = (w_tok - w_tok_hi.astype(F32)).astype(BF16)
    part = lambda w: jnp.where(holds, jnp.dot(w, owner, preferred_element_type=F32),
                               0.0).astype(BF16)
    w_both = jnp.concatenate([part(w_tok_hi), part(w_tok_lo)], axis=0)

    buf = stage_ref.at[i % 2]
    _wait_chunks(tile_ref[i], yg_hbm, buf, sems.at[i % 2])

    y_hi, y_lo = _unpack_bf16_pairs(buf[...])
    y = jnp.concatenate([y_hi.astype(BF16), y_lo.astype(BF16)], axis=1)
    both = jnp.dot(w_both, y, preferred_element_type=F32)
    routed = both[:tr, :] + both[tr:, :]
    v = ALPHA * x + (1.0 + g_ref[...]) * (routed + shared)
    o_ref[...] = _layer_norm(v, lng_ref[...], lnb_ref[...])


def _combine(plan, x2, mod3, pos_tok, w_tok, w_sh_gu, w_sh_down, ln_g, ln_b, yg, seq):
    t, d = x2.shape
    tr = TOK_TILE
    tpb = seq // tr
    full = lambda shape: pl.BlockSpec(shape, lambda i, *_: (0,) * len(shape))
    return pl.pallas_call(
        _combine_kernel,
        grid_spec=pltpu.PrefetchScalarGridSpec(
            num_scalar_prefetch=2,
            grid=(t // tr,),
            in_specs=[pl.BlockSpec((tr, d), lambda i, *_: (i, 0)),
                      _mod_spec(3, d, tpb), _mod_spec(4, d, tpb), _mod_spec(5, d, tpb),
                      pl.BlockSpec((tr, N_EXPERTS), lambda i, *_: (i, 0)),
                      pl.BlockSpec((tr, N_EXPERTS), lambda i, *_: (i, 0)),
                      pl.BlockSpec((None, N_EXPERTS, 1), lambda i, *_: (i, 0, 0)),
                      pl.BlockSpec((None, N_EXPERTS, 1), lambda i, *_: (i, 0, 0)),
                      full(w_sh_gu.shape), full(w_sh_down.shape), full((1, d)), full((1, d)),
                      pl.BlockSpec(memory_space=pl.ANY)],
            out_specs=pl.BlockSpec((tr, d), lambda i, *_: (i, 0)),
            scratch_shapes=[pltpu.VMEM((2, LOCAL_ROWS, d // 2), U32),
                            pltpu.SemaphoreType.DMA((2,))],
        ),
        out_shape=jax.ShapeDtypeStruct((t, d), F32),
        compiler_params=_cparams(("arbitrary",)),
    )(plan["slot_chunk"], plan["tile_chunks"],
      x2, mod3, mod3, mod3, pos_tok, w_tok, plan["run_begin_cols"], plan["run_end_cols"],
      w_sh_gu, w_sh_down, ln_g.reshape(1, d), ln_b.reshape(1, d), yg)


def _moe_layer(x2, mod3, w_router, router_b, w_gu, w_down, layer, w_sh_gu, w_sh_down,
               ln_g, ln_b, seq):
    t, d = x2.shape
    n_tiles = t // TOK_TILE
    logits = _router(x2, mod3, w_router, seq)
    w_top, pos, counts = _topk_route(logits.T, router_b)
    bound = t * TOP_K + n_tiles * N_EXPERTS * (CHUNK - 1) + N_EXPERTS * (EXPERT_ROWS - 1)
    n_blocks = -(-bound // EXPERT_ROWS)
    plan = _slot_plan(counts, n_tiles, n_blocks)
    xg = _dispatch(plan, x2, mod3, pos, n_blocks * EXPERT_ROWS, seq)
    yg = _expert_gemm(plan, xg, w_gu, w_down, layer)
    return _combine(plan, x2, mod3, pos.T, w_top.T, w_sh_gu, w_sh_down, ln_g, ln_b, yg, seq)


def kernel(x, c, ada_w, ada_b, pool_w_in, pool_w_grp, pool_scale, pool_w_out, attn_w_in, attn_w_out, ln1_g, ln1_b, router_w, router_b, exp_w_gu, exp_w_down, sh_w_gu, sh_w_down, ln2_g, ln2_b):
    batch, seq, d = x.shape
    depth = ada_w.shape[0]
    t = batch * seq
    mod = _modulation(c, ada_w, ada_b).reshape(depth, batch, 1, 6 * d)
    perms = jnp.stack([_perm_matrix(dil) for _, dil in ATTN_PATTERNS[1:]])
    perms_t = jnp.swapaxes(perms, 1, 2)
    expand = (jnp.arange(LSE_LANES)[:, None] == (jnp.arange(d)[None, :] // HEAD_DIM)).astype(BF16)
    x2 = x.reshape(t, d)
    for i in range(depth):
        mod3 = mod[i]
        j = i // 2
        if i % 2 == 0:
            x2 = _pool_layer(x2, mod3, pool_w_in[j].astype(BF16), pool_w_grp[j].astype(BF16),
                             pool_scale[j], pool_w_out[j].astype(BF16), ln1_g[i], ln1_b[i], seq)
        else:
            qkv = _qkv_proj(x2, mod3, perms, attn_w_in[j].astype(BF16), seq)
            res = [_attention_group(qkv, g, batch, seq) for g in range(len(ATTN_PATTERNS))]
            x2 = _attn_out(x2, mod3, [r[0] for r in res], [r[1] for r in res], perms_t, expand,
                           attn_w_out[j].astype(BF16), ln1_g[i], ln1_b[i], seq)
        x2 = _moe_layer(x2, mod3, router_w[i], router_b[i], exp_w_gu, exp_w_down, i,
                        sh_w_gu[i].astype(BF16), sh_w_down[i].astype(BF16),
                        ln2_g[i], ln2_b[i], seq)
    return x2.reshape(batch, seq, d)
```

```python
import functools
import math

import jax
import jax.numpy as jnp
from jax import lax
from jax.experimental import pallas as pl
from jax.experimental.pallas import tpu as pltpu

F32 = jnp.float32
BF16 = jnp.bfloat16
U32 = jnp.uint32
I32 = jnp.int32

POOL_WINDOWS = (2, 4, 8, 16)
ATTN_PATTERNS = ((128, 1), (512, 4), (2048, 16))
HEAD_DIM = 64
N_HEADS = 16
Q_BLOCK = 128
N_EXPERTS = 64
TOP_K = 8
N_EXPERT_GROUPS = 8
TOPK_GROUPS = 4
ROUTED_SCALE = 2.5
EXPERT_ROWS = 256
DEPTH = 4
ALPHA = (2 * DEPTH) ** 0.25
LN_EPS = 1e-5

PERM_TILE = 256
POOL_HALO = 16
TOK_TILE = 256
CHUNK = 16
LOCAL_ROWS = -(-(TOK_TILE * TOP_K + N_EXPERTS * (CHUNK - 1)) // 256) * 256
NO_SLOT = -64.0
POS_RADIX = 64.0
WAIT_GROUP = 32
GEMM_DEPTH = 4
LSE_LANES = 128
VMEM_LIMIT = 56 * 1024 * 1024
NEG_BIG = -1e30


def _cparams(sem):
    return pltpu.CompilerParams(dimension_semantics=sem, vmem_limit_bytes=VMEM_LIMIT)


def _layer_norm(v, g, b):
    mu = jnp.mean(v, axis=-1, keepdims=True)
    c = v - mu
    var = jnp.mean(c * c, axis=-1, keepdims=True)
    return c * lax.rsqrt(var + LN_EPS) * g + b


def _silu(v):
    return v * (1.0 / (1.0 + jnp.exp(-v)))


def _pack_bf16_pairs(v):
    n = v.shape[1] // 2
    hi = lax.bitcast_convert_type(v[:, :n].astype(BF16).astype(F32), U32)
    lo = lax.bitcast_convert_type(v[:, n:].astype(BF16).astype(F32), U32)
    return hi | (lo >> 16)


def _unpack_bf16_pairs(p):
    hi = lax.bitcast_convert_type(p & jnp.uint32(0xFFFF0000), F32)
    lo = lax.bitcast_convert_type(p << 16, F32)
    return hi, lo


def _mod_kernel(c_ref, w_ref, b_ref, o_ref):
    cs = _silu(c_ref[...])
    o_ref[...] = jnp.dot(cs, w_ref[...], preferred_element_type=F32) + b_ref[...]


def _modulation(c, ada_w, ada_b):
    depth, d, n6 = ada_w.shape
    b = c.shape[0]
    tn = 1536
    return pl.pallas_call(
        _mod_kernel,
        grid=(depth, n6 // tn),
        in_specs=[
            pl.BlockSpec((b, d), lambda i, n: (0, 0)),
            pl.BlockSpec((None, d, tn), lambda i, n: (i, 0, n)),
            pl.BlockSpec((None, 1, tn), lambda i, n: (i, 0, n)),
        ],
        out_specs=pl.BlockSpec((None, b, tn), lambda i, n: (i, 0, n)),
        out_shape=jax.ShapeDtypeStruct((depth, b, n6), F32),
        compiler_params=_cparams(("arbitrary", "arbitrary")),
    )(c, ada_w, ada_b.reshape(depth, 1, n6))


def _mod_spec(chunk, d, tiles_per_batch):
    return pl.BlockSpec((None, 1, d), lambda *idx: (idx[0] // tiles_per_batch, 0, chunk))


def _pool_kernel(x_ref, sh_ref, sc_ref, g_ref, win_ref, wgrp_ref, cs_ref, wout_ref,
                 lng_ref, lnb_ref, o_ref, ext_ref, *, tiles_per_batch):
    tm, d = x_ref.shape
    s_idx = pl.program_id(0) % tiles_per_batch
    x = x_ref[...]
    h = (x * (1.0 + sc_ref[...]) + sh_ref[...]).astype(BF16)
    u = jnp.dot(h, win_ref[...], preferred_element_type=F32)

    @pl.when(s_idx == 0)
    def _():
        ext_ref[0:POOL_HALO, :] = jnp.zeros((POOL_HALO, d), F32)

    @pl.when(s_idx != 0)
    def _():
        ext_ref[0:POOL_HALO, :] = ext_ref[tm:tm + POOL_HALO, :]

    ext_ref[POOL_HALO:POOL_HALO + tm, :] = u

    pos = s_idx * tm + lax.broadcasted_iota(I32, (tm, 1), 0) + 1
    gc = d // len(POOL_WINDOWS)
    ys = []
    for g, w in enumerate(POOL_WINDOWS):
        cols = slice(g * gc, (g + 1) * gc)
        acc = u[:, cols]
        for j in range(1, w):
            acc = acc + ext_ref[POOL_HALO - j:POOL_HALO - j + tm, cols]
        cnt = jnp.minimum(pos, w).astype(F32)
        z = (acc / cnt - u[:, cols]).astype(BF16)
        ys.append(jnp.dot(z, wgrp_ref[g], preferred_element_type=F32))
    y = (jnp.concatenate(ys, axis=1) * cs_ref[...]).astype(BF16)
    out = jnp.dot(y, wout_ref[...], preferred_element_type=F32)
    v = ALPHA * x + (1.0 + g_ref[...]) * out
    o_ref[...] = _layer_norm(v, lng_ref[...], lnb_ref[...])


def _pool_layer(x2, mod3, w_in, w_grp, ch_scale, w_out, ln_g, ln_b, seq):
    t, d = x2.shape
    tm = 512
    tpb = seq // tm
    full = lambda shape: pl.BlockSpec(shape, lambda i: (0,) * len(shape))
    return pl.pallas_call(
        functools.partial(_pool_kernel, tiles_per_batch=tpb),
        grid=(t // tm,),
        in_specs=[
            pl.BlockSpec((tm, d), lambda i: (i, 0)),
            _mod_spec(0, d, tpb), _mod_spec(1, d, tpb), _mod_spec(2, d, tpb),
            full((d, d)), full(w_grp.shape), full((1, d)), full((d, d)),
            full((1, d)), full((1, d)),
        ],
        out_specs=pl.BlockSpec((tm, d), lambda i: (i, 0)),
        out_shape=jax.ShapeDtypeStruct((t, d), F32),
        scratch_shapes=[pltpu.VMEM((tm + POOL_HALO, d), F32)],
        compiler_params=_cparams(("arbitrary",)),
    )(x2, mod3, mod3, mod3, w_in, w_grp, ch_scale.reshape(1, d), w_out,
      ln_g.reshape(1, d), ln_b.reshape(1, d))


def _perm_matrix(dil):
    p = jnp.arange(PERM_TILE)
    chunk = PERM_TILE // dil
    src = (p % chunk) * dil + p // chunk
    return (src[:, None] == jnp.arange(PERM_TILE)[None, :]).astype(BF16)


def _qkv_kernel(x_ref, sh_ref, sc_ref, p_ref, w_ref, o_ref, h_ref):
    tm = x_ref.shape[0]
    g = pl.program_id(1)
    part = pl.program_id(2)

    @pl.when((g == 0) & (part == 0))
    def _():
        h = (x_ref[...] * (1.0 + sc_ref[...]) + sh_ref[...]).astype(BF16)
        h_ref[0] = h
        for gi in range(1, len(ATTN_PATTERNS)):
            for s in range(tm // PERM_TILE):
                rows = slice(s * PERM_TILE, (s + 1) * PERM_TILE)
                h_ref[gi, rows, :] = jnp.dot(
                    p_ref[gi - 1], h[rows, :], preferred_element_type=F32).astype(BF16)

    o_ref[...] = jnp.dot(h_ref[g], w_ref[...], preferred_element_type=F32).astype(BF16)


def _qkv_proj(x2, mod3, perms, w_in, seq):
    t, d = x2.shape
    ng = len(ATTN_PATTERNS)
    tm = 1024
    tpb = seq // tm
    return pl.pallas_call(
        _qkv_kernel,
        grid=(t // tm, ng, 3),
        in_specs=[
            pl.BlockSpec((tm, d), lambda m, g, p: (m, 0)),
            _mod_spec(0, d, tpb), _mod_spec(1, d, tpb),
            pl.BlockSpec(perms.shape, lambda m, g, p: (0, 0, 0)),
            pl.BlockSpec((d, d), lambda m, g, p: (0, g * 3 + p)),
        ],
        out_specs=pl.BlockSpec((tm, d), lambda m, g, p: (m, g * 3 + p)),
        out_shape=jax.ShapeDtypeStruct((t, ng * 3 * d), BF16),
        scratch_shapes=[pltpu.VMEM((ng, tm, d), BF16)],
        compiler_params=_cparams(("arbitrary", "arbitrary", "arbitrary")),
    )(x2, mod3, mod3, perms, w_in)


def _attn_kernel(q_ref, kp_ref, kc_ref, vp_ref, vc_ref, o_ref, lse_ref, *, group, dil):
    bq = Q_BLOCK
    d = N_HEADS * HEAD_DIM
    j = pl.program_id(2)
    q = q_ref[...].reshape(bq, d) * (HEAD_DIM ** -0.5)
    kp = kp_ref[...].reshape(bq, d)
    kc = kc_ref[...].reshape(bq, d)
    vp = vp_ref[...].reshape(bq, d)
    vc = vc_ref[...].reshape(bq, d)

    qi = lax.broadcasted_iota(I32, (bq, 2 * bq), 0)
    kj = lax.broadcasted_iota(I32, (bq, 2 * bq), 1)
    dist = qi + bq - kj
    steps = ATTN_PATTERNS[group][0] // dil
    valid = (dist >= 0) & (dist <= steps) & ((kj >= bq) | (j > 0))
    neg_dist = jnp.where(valid, (dist * -dil).astype(F32), NEG_BIG)
    lane = lax.broadcasted_iota(I32, (bq, LSE_LANES), 1)
    lse_tile = jnp.zeros((bq, LSE_LANES), F32)
    n_tot = len(ATTN_PATTERNS) * N_HEADS
    outs = []
    for h in range(N_HEADS):
        cols = slice(h * HEAD_DIM, (h + 1) * HEAD_DIM)
        slope = 2.0 ** (-8.0 * (group * N_HEADS + h + 1) / n_tot)
        kh = jnp.concatenate([kp[:, cols], kc[:, cols]], axis=0)
        vh = jnp.concatenate([vp[:, cols], vc[:, cols]], axis=0)
        s = lax.dot_general(q[:, cols], kh, (((1,), (1,)), ((), ())),
                            preferred_element_type=F32)
        s = s + slope * neg_dist
        m = jnp.max(s, axis=-1, keepdims=True)
        p = jnp.exp(s - m)
        den = jnp.sum(p, axis=-1, keepdims=True)
        o = jnp.dot(p.astype(BF16), vh, preferred_element_type=F32) / den
        outs.append(o)
        lse_tile = jnp.where(lane == h, m + jnp.log(den), lse_tile)
    o_ref[...] = jnp.concatenate(outs, axis=1).astype(BF16).reshape(o_ref.shape)
    lse_ref[...] = lse_tile.reshape(lse_ref.shape)


def _attention_group(qkv, group, batch, seq):
    dil = ATTN_PATTERNS[group][1]
    d = N_HEADS * HEAD_DIM
    t = qkv.shape[0]
    sub = seq // dil
    nb = sub // Q_BLOCK
    rows = Q_BLOCK if dil == 1 else PERM_TILE // dil
    chunks = Q_BLOCK // rows
    u = seq // (rows * dil)
    view = lambda a, c: a.reshape(batch, u, dil, rows, c)
    blk = lambda c: (None, chunks, None, rows, c)
    col0 = group * 3
    q_spec = pl.BlockSpec(blk(d), lambda b, r, j: (b, j, r, 0, col0))
    kc_spec = pl.BlockSpec(blk(d), lambda b, r, j: (b, j, r, 0, col0 + 1))
    kp_spec = pl.BlockSpec(blk(d), lambda b, r, j: (b, jnp.maximum(j - 1, 0), r, 0, col0 + 1))
    vc_spec = pl.BlockSpec(blk(d), lambda b, r, j: (b, j, r, 0, col0 + 2))
    vp_spec = pl.BlockSpec(blk(d), lambda b, r, j: (b, jnp.maximum(j - 1, 0), r, 0, col0 + 2))
    qkv5 = view(qkv, qkv.shape[1])
    o, lse = pl.pallas_call(
        functools.partial(_attn_kernel, group=group, dil=dil),
        grid=(batch, dil, nb),
        in_specs=[q_spec, kp_spec, kc_spec, vp_spec, vc_spec],
        out_specs=[
            pl.BlockSpec(blk(d), lambda b, r, j: (b, j, r, 0, 0)),
            pl.BlockSpec(blk(LSE_LANES), lambda b, r, j: (b, j, r, 0, 0)),
        ],
        out_shape=[
            jax.ShapeDtypeStruct((batch, u, dil, rows, d), BF16),
            jax.ShapeDtypeStruct((batch, u, dil, rows, LSE_LANES), F32),
        ],
        compiler_params=_cparams(("arbitrary", "arbitrary", "arbitrary")),
    )(qkv5, qkv5, qkv5, qkv5, qkv5)
    return o.reshape(t, d), lse.reshape(t, LSE_LANES)


def _split3(v):
    a = v.astype(BF16)
    r = v - a.astype(F32)
    b = r.astype(BF16)
    c = (r - b.astype(F32)).astype(BF16)
    return a, b, c


def _attn_out_kernel(x_ref, g_ref, o0_ref, o1_ref, o2_ref, l0_ref, l1_ref, l2_ref,
                     pt_ref, e_ref, wout_ref, lng_ref, lnb_ref, out_ref):
    tm, d = x_ref.shape
    o_refs = (o0_ref, o1_ref, o2_ref)
    l_refs = (l0_ref, l1_ref, l2_ref)
    n_sub = tm // PERM_TILE

    def unperm(gi, val_bf16):
        if gi == 0:
            return val_bf16.astype(F32)
        parts = [jnp.dot(pt_ref[gi - 1], val_bf16[s * PERM_TILE:(s + 1) * PERM_TILE, :],
                         preferred_element_type=F32) for s in range(n_sub)]
        return jnp.concatenate(parts, axis=0)

    lses = []
    for gi in range(3):
        l = l_refs[gi][...]
        if gi == 0:
            lses.append(l)
        else:
            a, b, c = _split3(l)
            lses.append(unperm(gi, a) + unperm(gi, b) + unperm(gi, c))
    mx = jnp.maximum(jnp.maximum(lses[0], lses[1]), lses[2])
    es = [jnp.exp(l - mx) for l in lses]
    tot = es[0] + es[1] + es[2]
    mixed = jnp.zeros((tm, d), F32)
    for gi in range(3):
        w = es[gi] / tot
        a, b, c = _split3(w)
        wide = (jnp.dot(a, e_ref[...], preferred_element_type=F32)
                + jnp.dot(b, e_ref[...], preferred_element_type=F32)
                + jnp.dot(c, e_ref[...], preferred_element_type=F32))
        mixed = mixed + wide * unperm(gi, o_refs[gi][...])
    y = jnp.dot(mixed.astype(BF16), wout_ref[...], preferred_element_type=F32)
    v = ALPHA * x_ref[...] + (1.0 + g_ref[...]) * y
    out_ref[...] = _layer_norm(v, lng_ref[...], lnb_ref[...])


def _attn_out(x2, mod3, os_, lses, perms_t, expand, w_out, ln_g, ln_b, seq):
    t, d = x2.shape
    tm = 512
    tpb = seq // tm
    row = lambda c: pl.BlockSpec((tm, c), lambda i: (i, 0))
    full = lambda shape: pl.BlockSpec(shape, lambda i: (0,) * len(shape))
    return pl.pallas_call(
        _attn_out_kernel,
        grid=(t // tm,),
        in_specs=[row(d), _mod_spec(2, d, tpb), row(d), row(d), row(d),
                  row(LSE_LANES), row(LSE_LANES), row(LSE_LANES),
                  full(perms_t.shape), full(expand.shape), full((d, d)),
                  full((1, d)), full((1, d))],
        out_specs=row(d),
        out_shape=jax.ShapeDtypeStruct((t, d), F32),
        compiler_params=_cparams(("arbitrary",)),
    )(x2, mod3, *os_, *lses, perms_t, expand, w_out, ln_g.reshape(1, d), ln_b.reshape(1, d))


def _router_kernel(x_ref, sh_ref, sc_ref, wr_ref, lg_ref):
    hf = x_ref[...] * (1.0 + sc_ref[...]) + sh_ref[...]
    lg_ref[...] = jnp.dot(hf, wr_ref[...], preferred_element_type=F32,
                          precision=lax.Precision.HIGHEST)


def _router(x2, mod3, w_router, seq):
    t, d = x2.shape
    tm = 512
    tpb = seq // tm
    ne = w_router.shape[1]
    return pl.pallas_call(
        _router_kernel,
        grid=(t // tm,),
        in_specs=[pl.BlockSpec((tm, d), lambda i: (i, 0)),
                  _mod_spec(3, d, tpb), _mod_spec(4, d, tpb),
                  pl.BlockSpec((d, ne), lambda i: (0, 0))],
        out_specs=pl.BlockSpec((tm, ne), lambda i: (i, 0)),
        out_shape=jax.ShapeDtypeStruct((t, ne), F32),
        compiler_params=_cparams(("arbitrary",)),
    )(x2, mod3, mod3, w_router)


def _first_index_of_max(v, iota, size):
    m = jnp.max(v, axis=0, keepdims=True)
    idx = jnp.min(jnp.where(v == m, iota, float(size)), axis=0, keepdims=True)
    return m, idx


def _topk_kernel(lg_ref, b_ref, tri_ref, ltri_ref, w_ref, pos_ref, cnt_ref):
    ne, tr = lg_ref.shape
    gsz = ne // N_EXPERT_GROUPS
    tile = pl.program_id(0)

    scores = 1.0 / (1.0 + jnp.exp(-lg_ref[...]))
    sel = scores + b_ref[...]
    iota_g = lax.broadcasted_iota(I32, (gsz, tr), 0).astype(F32)
    iota_n = lax.broadcasted_iota(I32, (N_EXPERT_GROUPS, tr), 0).astype(F32)
    gs = jnp.zeros((N_EXPERT_GROUPS, tr), F32)
    for g in range(N_EXPERT_GROUPS):
        blk = sel[g * gsz:(g + 1) * gsz, :]
        m1, i1 = _first_index_of_max(blk, iota_g, gsz)
        m2 = jnp.max(jnp.where(iota_g == i1, -jnp.inf, blk), axis=0, keepdims=True)
        gs = jnp.where(iota_n == float(g), m1 + m2, gs)
    gmask = jnp.zeros((N_EXPERT_GROUPS, tr), F32)
    for _ in range(TOPK_GROUPS):
        _, gi = _first_index_of_max(gs, iota_n, N_EXPERT_GROUPS)
        hit = iota_n == gi
        gmask = jnp.where(hit, 1.0, gmask)
        gs = jnp.where(hit, -jnp.inf, gs)
    masked_rows = []
    for g in range(N_EXPERT_GROUPS):
        keep = jnp.broadcast_to(gmask[g:g + 1, :], (gsz, tr)) > 0.5
        masked_rows.append(jnp.where(keep, sel[g * gsz:(g + 1) * gsz, :], -jnp.inf))
    cur = jnp.concatenate(masked_rows, axis=0)
    iota_e = lax.broadcasted_iota(I32, (ne, tr), 0).astype(F32)
    chosen = jnp.zeros((ne, tr), F32)
    for _ in range(TOP_K):
        _, ei = _first_index_of_max(cur, iota_e, ne)
        hit = iota_e == ei
        cur = jnp.where(hit, -jnp.inf, cur)
        chosen = jnp.where(hit, 1.0, chosen)
    picked = chosen > 0.5
    top_scores = jnp.where(picked, scores, 0.0)
    wsum = jnp.sum(top_scores, axis=0, keepdims=True)
    w_ref[...] = top_scores / wsum * ROUTED_SCALE
    before = jnp.dot(chosen.astype(BF16), tri_ref[...], preferred_element_type=F32)
    n = jnp.sum(chosen, axis=1, keepdims=True)
    n_chunks = jnp.floor((n + (CHUNK - 1)) * (1.0 / CHUNK))
    run_off = jnp.dot(ltri_ref[...], jnp.broadcast_to(n_chunks, (ne, 128)).astype(BF16),
                      preferred_element_type=F32)[:, 0:1] * CHUNK
    pos_ref[...] = jnp.where(picked, before + run_off, NO_SLOT)

    @pl.when(tile == 0)
    def _():
        cnt_ref[...] = jnp.zeros_like(cnt_ref)

    lane = lax.broadcasted_iota(I32, cnt_ref.shape, 1)
    cnt_ref[...] = jnp.where(lane == tile, jnp.broadcast_to(n, cnt_ref.shape).astype(I32),
                             cnt_ref[...])


def _topk_route(logits_t, router_b):
    ne, t = logits_t.shape
    tr = TOK_TILE
    assert t // tr <= 128
    tri = (jnp.arange(tr)[:, None] < jnp.arange(tr)[None, :]).astype(BF16)
    ltri = (jnp.arange(ne)[None, :] < jnp.arange(ne)[:, None]).astype(BF16)
    out = lambda dt: jax.ShapeDtypeStruct((ne, t), dt)
    row = pl.BlockSpec((ne, tr), lambda i: (0, i))
    return pl.pallas_call(
        _topk_kernel,
        grid=(t // tr,),
        in_specs=[pl.BlockSpec((ne, tr), lambda i: (0, i)),
                  pl.BlockSpec((ne, 1), lambda i: (0, 0)),
                  pl.BlockSpec((tr, tr), lambda i: (0, 0)),
                  pl.BlockSpec((ne, ne), lambda i: (0, 0))],
        out_specs=[row, row, pl.BlockSpec((ne, 128), lambda i: (0, 0))],
        out_shape=[out(F32), out(F32), jax.ShapeDtypeStruct((ne, 128), I32)],
        compiler_params=_cparams(("arbitrary",)),
    )(logits_t, router_b.reshape(ne, 1), tri, ltri)


def _slot_plan(counts, n_tiles, n_blocks):
    n = counts[:, :n_tiles]
    nch = (n + (CHUNK - 1)) // CHUNK
    rows = jnp.sum(nch, axis=1) * CHUNK
    region = ((rows + EXPERT_ROWS - 1) // EXPERT_ROWS) * EXPERT_ROWS
    region_end = jnp.cumsum(region)
    region_start = region_end - region
    run_chunk = region_start[:, None] // CHUNK + jnp.cumsum(nch, axis=1) - nch
    local_end = jnp.cumsum(nch, axis=0)
    local_chunk = local_end - nch
    c = jnp.arange(LOCAL_ROWS // CHUNK, dtype=I32)
    expert_of_c = jnp.sum((local_end[:, :, None] <= c[None, None, :]).astype(I32), axis=0)
    owner = expert_of_c[None] == jnp.arange(N_EXPERTS, dtype=I32)[:, None, None]
    slot_chunk = jnp.sum(jnp.where(owner, (run_chunk - local_chunk)[:, :, None], 0), axis=0) + c
    block_row = jnp.arange(n_blocks, dtype=I32) * EXPERT_ROWS
    group_e = jnp.minimum(jnp.sum((region_end[None, :] <= block_row[:, None]).astype(I32), axis=1),
                          N_EXPERTS - 1)
    n_used = region_end[-1] // EXPERT_ROWS
    nonempty = region > 0
    run_of_expert = jnp.cumsum(nonempty.astype(I32)) - 1
    experts = jnp.arange(N_EXPERTS, dtype=I32)
    run_expert = jnp.sum(jnp.where((run_of_expert[None, :] == experts[:, None]) & nonempty[None, :],
                                   experts[None, :], 0), axis=1)
    blocks = jnp.arange(n_blocks, dtype=I32)
    first = ((blocks == 0) | (group_e != jnp.roll(group_e, 1))) & (blocks < n_used)
    per_tile = lambda a: (a * CHUNK).T.astype(I32)
    return dict(
        run_begin_rows=per_tile(local_chunk)[:, None, :], run_end_rows=per_tile(local_end)[:, None, :],
        run_begin_cols=per_tile(local_chunk)[:, :, None], run_end_cols=per_tile(local_end)[:, :, None],
        block_run=(jnp.cumsum(first.astype(I32)) - 1).astype(I32), block_first=first.astype(I32),
        run_expert=run_expert.astype(I32), n_runs=jnp.sum(nonempty.astype(I32)).reshape(1),
        slot_chunk=slot_chunk.reshape(-1).astype(I32),
        tile_chunks=jnp.sum(nch, axis=0).astype(I32),
        tail_chunk=((region_start + rows) // CHUNK).astype(I32),
        tail_chunks=((region - rows) // CHUNK).astype(I32),
        group_e=group_e, n_used=(region_end[-1:] // EXPERT_ROWS).astype(I32))


def _chunk_rows(ref, chunk_index):
    start = chunk_index * CHUNK
    if not isinstance(start, int):
        start = pl.multiple_of(start, CHUNK)
    return ref.at[pl.ds(start, CHUNK), :]


def _pos_digits(pos):
    hi = jnp.floor(pos * (1.0 / POS_RADIX))
    return hi.astype(BF16), (pos - POS_RADIX * hi).astype(BF16)


def _wait_chunks(count, src_ref, dst_ref, sem):
    group_rows = pl.ds(0, WAIT_GROUP * CHUNK)

    def wait_group(c, carry):
        pltpu.make_async_copy(src_ref.at[group_rows, :], dst_ref.at[group_rows, :], sem).wait()
        return carry

    def wait_one(c, carry):
        pltpu.make_async_copy(_chunk_rows(src_ref, 0), _chunk_rows(dst_ref, 0), sem).wait()
        return carry

    lax.fori_loop(0, lax.shift_right_logical(count, WAIT_GROUP.bit_length() - 1), wait_group, 0)
    lax.fori_loop(0, count & (WAIT_GROUP - 1), wait_one, 0)


def _dispatch_kernel(slot_ref, tile_ref, tail_ref, tailn_ref, used_ref,
                     x_ref, sh_ref, sc_ref, pos_ref, begin_ref, end_ref,
                     xg_hbm, stage_ref, zero_ref, sems, sem_blk):
    tr = x_ref.shape[0]
    i = pl.program_id(0)
    n_blocks = xg_hbm.shape[0] // EXPERT_ROWS
    chunks_per_tile = LOCAL_ROWS // CHUNK

    def drain(count, sem):
        _wait_chunks(count, stage_ref.at[0], xg_hbm, sem)

    @pl.when(i == 0)
    def _():
        zero_ref[...] = jnp.zeros_like(zero_ref)

        def block_copy(j):
            rows = pl.ds(pl.multiple_of(j * EXPERT_ROWS, EXPERT_ROWS), EXPERT_ROWS)
            return pltpu.make_async_copy(zero_ref, xg_hbm.at[rows, :], sem_blk)

        def start_block(j, carry):
            block_copy(j).start()
            return carry

        def wait_block(j, carry):
            block_copy(j).wait()
            return carry

        lax.fori_loop(used_ref[0], n_blocks, start_block, 0)

        def per_expert(e, total):
            def per_chunk(c, carry):
                pltpu.make_async_copy(_chunk_rows(zero_ref, 0),
                                      _chunk_rows(xg_hbm, tail_ref[e] + c), sems.at[0]).start()
                return carry
            lax.fori_loop(0, tailn_ref[e], per_chunk, 0)
            return total + tailn_ref[e]

        drain(lax.fori_loop(0, N_EXPERTS, per_expert, 0), sems.at[0])
        lax.fori_loop(used_ref[0], n_blocks, wait_block, 0)

    hf = (x_ref[...] * (1.0 + sc_ref[...]) + sh_ref[...]).astype(BF16)
    row_e = lax.broadcasted_iota(I32, (LOCAL_ROWS, N_EXPERTS), 0)
    owner = jnp.where((row_e >= begin_ref[...]) & (row_e < end_ref[...]), 1.0, 0.0).astype(BF16)
    hi_digit, lo_digit = _pos_digits(pos_ref[...])
    slot_pos = (POS_RADIX * jnp.dot(owner, hi_digit, preferred_element_type=F32)
                + jnp.dot(owner, lo_digit, preferred_element_type=F32))
    row_t = lax.broadcasted_iota(I32, (LOCAL_ROWS, tr), 0).astype(F32)
    perm = jnp.where(slot_pos == row_t, 1.0, 0.0).astype(BF16)
    rows = jnp.dot(perm, hf, preferred_element_type=F32)
    half = rows.shape[1] // 2
    buf = stage_ref.at[i % 2]
    buf[...] = (lax.bitcast_convert_type(rows[:, :half], U32)
                | (lax.bitcast_convert_type(rows[:, half:], U32) >> 16))

    def send(c, carry):
        pltpu.make_async_copy(_chunk_rows(buf, c),
                              _chunk_rows(xg_hbm, slot_ref[i * chunks_per_tile + c]),
                              sems.at[i % 2]).start()
        return carry

    lax.fori_loop(0, tile_ref[i], send, 0)

    @pl.when(i > 0)
    def _():
        drain(tile_ref[jnp.maximum(i - 1, 0)], sems.at[(i + 1) % 2])

    @pl.when(i == pl.num_programs(0) - 1)
    def _():
        drain(tile_ref[i], sems.at[i % 2])


def _dispatch(plan, x2, mod3, pos, n_slots, seq):
    t, d = x2.shape
    tr = TOK_TILE
    tpb = seq // tr
    return pl.pallas_call(
        _dispatch_kernel,
        grid_spec=pltpu.PrefetchScalarGridSpec(
            num_scalar_prefetch=5,
            grid=(t // tr,),
            in_specs=[pl.BlockSpec((tr, d), lambda i, *_: (i, 0)),
                      _mod_spec(3, d, tpb), _mod_spec(4, d, tpb),
                      pl.BlockSpec((N_EXPERTS, tr), lambda i, *_: (0, i)),
                      pl.BlockSpec((None, 1, N_EXPERTS), lambda i, *_: (i, 0, 0)),
                      pl.BlockSpec((None, 1, N_EXPERTS), lambda i, *_: (i, 0, 0))],
            out_specs=pl.BlockSpec(memory_space=pl.ANY),
            scratch_shapes=[pltpu.VMEM((2, LOCAL_ROWS, d // 2), U32),
                            pltpu.VMEM((EXPERT_ROWS, d // 2), U32),
                            pltpu.SemaphoreType.DMA((2,)), pltpu.SemaphoreType.DMA],
        ),
        out_shape=jax.ShapeDtypeStruct((n_slots, d // 2), U32),
        compiler_params=_cparams(("arbitrary",)),
    )(plan["slot_chunk"], plan["tile_chunks"], plan["tail_chunk"], plan["tail_chunks"],
      plan["n_used"], x2, mod3, mod3, pos, plan["run_begin_rows"], plan["run_end_rows"])


def _expert_kernel(run_ref, first_ref, rexp_ref, nrun_ref, nu_ref,
                   xg_hbm, wgu_hbm, wdn_hbm, yg_hbm,
                   xbuf, ybuf, wgu_f32, wdn_f32, wgu_bf, wdn_bf, x_sem, y_sem, w_sem, z_sem,
                   *, layer):
    n_used = nu_ref[0]
    n_blocks = yg_hbm.shape[0] // EXPERT_ROWS
    depth, _, half = xbuf.shape
    f = wdn_bf.shape[0]

    def rows(j):
        return pl.ds(pl.multiple_of(j * EXPERT_ROWS, EXPERT_ROWS), EXPERT_ROWS)

    def x_copy(j):
        return pltpu.make_async_copy(xg_hbm.at[rows(j), :], xbuf.at[j % depth],
                                     x_sem.at[j % depth])

    def y_copy(j):
        return pltpu.make_async_copy(ybuf.at[j % depth], yg_hbm.at[rows(j), :],
                                     y_sem.at[j % depth])

    def w_copies(r):
        e = rexp_ref[r]
        return (pltpu.make_async_copy(wgu_hbm.at[layer, e], wgu_f32.at[r % 2], w_sem.at[r % 2]),
                pltpu.make_async_copy(wdn_hbm.at[layer, e], wdn_f32.at[r % 2], w_sem.at[r % 2]))

    def zero_copy(j):
        return pltpu.make_async_copy(ybuf.at[0], yg_hbm.at[rows(j), :], z_sem)

    ybuf[0] = jnp.zeros(ybuf.shape[1:], U32)

    def zero_start(j, carry):
        zero_copy(j).start()
        return carry

    def zero_wait(j, carry):
        zero_copy(j).wait()
        return carry

    lax.fori_loop(n_used, n_blocks, zero_start, 0)
    lax.fori_loop(n_used, n_blocks, zero_wait, 0)

    for cp in w_copies(0):
        cp.start()
    for a in range(depth - 1):
        @pl.when(a < n_used)
        def _():
            x_copy(a).start()

    def block(j, carry):
        r = run_ref[j]

        @pl.when(first_ref[j] == 1)
        def _():
            for cp in w_copies(r):
                cp.wait()

            @pl.when(r + 1 < nrun_ref[0])
            def _():
                for cp in w_copies(r + 1):
                    cp.start()

            wgu_bf[...] = wgu_f32[r % 2].astype(BF16)
            wdn_bf[...] = wdn_f32[r % 2].astype(BF16)

        x_copy(j).wait()

        @pl.when(j + (depth - 1) < n_used)
        def _():
            x_copy(j + (depth - 1)).start()

        @pl.when(j >= depth)
        def _():
            y_copy(j - depth).wait()

        hi, lo = _unpack_bf16_pairs(xbuf[j % depth])
        gu = (jnp.dot(hi.astype(BF16), wgu_bf[0:half, :], preferred_element_type=F32)
              + jnp.dot(lo.astype(BF16), wgu_bf[half:, :], preferred_element_type=F32))
        act = (_silu(gu[:, :f]) * gu[:, f:]).astype(BF16)
        ybuf[j % depth] = _pack_bf16_pairs(jnp.dot(act, wdn_bf[...], preferred_element_type=F32))
        y_copy(j).start()
        return carry

    lax.fori_loop(0, n_used, block, 0)

    for a in range(depth, 0, -1):
        @pl.when(n_used >= a)
        def _():
            y_copy(n_used - a).wait()


def _expert_gemm(plan, xg, w_gu, w_down, layer):
    n_slots, half = xg.shape
    _, ne, d, f2 = w_gu.shape
    f = w_down.shape[2]
    hbm = pl.BlockSpec(memory_space=pl.ANY)
    return pl.pallas_call(
        functools.partial(_expert_kernel, layer=layer),
        grid_spec=pltpu.PrefetchScalarGridSpec(
            num_scalar_prefetch=5,
            grid=(1,),
            in_specs=[hbm, hbm, hbm],
            out_specs=hbm,
            scratch_shapes=[pltpu.VMEM((GEMM_DEPTH, EXPERT_ROWS, half), U32),
                            pltpu.VMEM((GEMM_DEPTH, EXPERT_ROWS, half), U32),
                            pltpu.VMEM((2, d, f2), F32), pltpu.VMEM((2, f, d), F32),
                            pltpu.VMEM((d, f2), BF16), pltpu.VMEM((f, d), BF16),
                            pltpu.SemaphoreType.DMA((GEMM_DEPTH,)),
                            pltpu.SemaphoreType.DMA((GEMM_DEPTH,)),
                            pltpu.SemaphoreType.DMA((2,)), pltpu.SemaphoreType.DMA],
        ),
        out_shape=jax.ShapeDtypeStruct((n_slots, half), U32),
        compiler_params=_cparams(("arbitrary",)),
    )(plan["block_run"], plan["block_first"], plan["run_expert"], plan["n_runs"], plan["n_used"],
      xg, w_gu, w_down)


def _combine_kernel(slot_ref, tile_ref,
                    x_ref, sh_ref, sc_ref, g_ref, pos_ref, wt_ref, begin_ref, end_ref,
                    wsgu_ref, wsdn_ref, lng_ref, lnb_ref, yg_hbm, o_ref, stage_ref, sems):
    tr, d = x_ref.shape
    i = pl.program_id(0)
    n_tiles = pl.num_programs(0)
    chunks_per_tile = LOCAL_ROWS // CHUNK

    def fetch(tile, buffer):
        def body(c, carry):
            pltpu.make_async_copy(_chunk_rows(yg_hbm, slot_ref[tile * chunks_per_tile + c]),
                                  _chunk_rows(stage_ref.at[buffer], c), sems.at[buffer]).start()
            return carry
        lax.fori_loop(0, tile_ref[tile], body, 0)

    @pl.when(i == 0)
    def _():
        stage_ref[...] = jnp.zeros_like(stage_ref)
        fetch(0, 0)

    nxt = jnp.minimum(i + 1, n_tiles - 1)
    for buffer in range(2):
        @pl.when((i + 1 < n_tiles) & (nxt % 2 == buffer))
        def _():
            fetch(nxt, buffer)

    x = x_ref[...]
    hf = (x * (1.0 + sc_ref[...]) + sh_ref[...]).astype(BF16)
    f = wsdn_ref.shape[0]
    su = jnp.dot(hf, wsgu_ref[...], preferred_element_type=F32)
    act = (_silu(su[:, :f]) * su[:, f:]).astype(BF16)
    shared = jnp.dot(act, wsdn_ref[...], preferred_element_type=F32)

    col_e = lax.broadcasted_iota(I32, (N_EXPERTS, LOCAL_ROWS), 1)
    owner = jnp.where((col_e >= begin_ref[...]) & (col_e < end_ref[...]), 1.0, 0.0).astype(BF16)
    hi_digit, lo_digit = _pos_digits(pos_ref[...])
    slot_pos = (POS_RADIX * jnp.dot(hi_digit, owner, preferred_element_type=F32)
                + jnp.dot(lo_digit, owner, preferred_element_type=F32))
    holds = slot_pos == lax.broadcasted_iota(I32, (tr, LOCAL_ROWS), 1).astype(F32)
    w_tok = wt_ref[...]
    w_tok_hi = w_tok.astype(BF16)
    w_tok_lo = (w_tok - w_tok_hi.astype(F32)).astype(BF16)
    part = lambda w: jnp.where(holds, jnp.dot(w, owner, preferred_element_type=F32),
                               0.0).astype(BF16)
    w_both = jnp.concatenate([part(w_tok_hi), part(w_tok_lo)], axis=0)

    buf = stage_ref.at[i % 2]
    _wait_chunks(tile_ref[i], yg_hbm, buf, sems.at[i % 2])

    y_hi, y_lo = _unpack_bf16_pairs(buf[...])
    y = jnp.concatenate([y_hi.astype(BF16), y_lo.astype(BF16)], axis=1)
    both = jnp.dot(w_both, y, preferred_element_type=F32)
    routed = both[:tr, :] + both[tr:, :]
    v = ALPHA * x + (1.0 + g_ref[...]) * (routed + shared)
    o_ref[...] = _layer_norm(v, lng_ref[...], lnb_ref[...])


def _combine(plan, x2, mod3, pos_tok, w_tok, w_sh_gu, w_sh_down, ln_g, ln_b, yg, seq):
    t, d = x2.shape
    tr = TOK_TILE
    tpb = seq // tr
    full = lambda shape: pl.BlockSpec(shape, lambda i, *_: (0,) * len(shape))
    return pl.pallas_call(
        _combine_kernel,
        grid_spec=pltpu.PrefetchScalarGridSpec(
            num_scalar_prefetch=2,
            grid=(t // tr,),
            in_specs=[pl.BlockSpec((tr, d), lambda i, *_: (i, 0)),
                      _mod_spec(3, d, tpb), _mod_spec(4, d, tpb), _mod_spec(5, d, tpb),
                      pl.BlockSpec((tr, N_EXPERTS), lambda i, *_: (i, 0)),
                      pl.BlockSpec((tr, N_EXPERTS), lambda i, *_: (i, 0)),
                      pl.BlockSpec((None, N_EXPERTS, 1), lambda i, *_: (i, 0, 0)),
                      pl.BlockSpec((None, N_EXPERTS, 1), lambda i, *_: (i, 0, 0)),
                      full(w_sh_gu.shape), full(w_sh_down.shape), full((1, d)), full((1, d)),
                      pl.BlockSpec(memory_space=pl.ANY)],
            out_specs=pl.BlockSpec((tr, d), lambda i, *_: (i, 0)),
            scratch_shapes=[pltpu.VMEM((2, LOCAL_ROWS, d // 2), U32),
                            pltpu.SemaphoreType.DMA((2,))],
        ),
        out_shape=jax.ShapeDtypeStruct((t, d), F32),
        compiler_params=_cparams(("arbitrary",)),
    )(plan["slot_chunk"], plan["tile_chunks"],
      x2, mod3, mod3, mod3, pos_tok, w_tok, plan["run_begin_cols"], plan["run_end_cols"],
      w_sh_gu, w_sh_down, ln_g.reshape(1, d), ln_b.reshape(1, d), yg)


def _moe_layer(x2, mod3, w_router, router_b, w_gu, w_down, layer, w_sh_gu, w_sh_down,
               ln_g, ln_b, seq):
    t, d = x2.shape
    n_tiles = t // TOK_TILE
    logits = _router(x2, mod3, w_router, seq)
    w_top, pos, counts = _topk_route(logits.T, router_b)
    bound = t * TOP_K + n_tiles * N_EXPERTS * (CHUNK - 1) + N_EXPERTS * (EXPERT_ROWS - 1)
    n_blocks = -(-bound // EXPERT_ROWS)
    plan = _slot_plan(counts, n_tiles, n_blocks)
    xg = _dispatch(plan, x2, mod3, pos, n_blocks * EXPERT_ROWS, seq)
    yg = _expert_gemm(plan, xg, w_gu, w_down, layer)
    return _combine(plan, x2, mod3, pos.T, w_top.T, w_sh_gu, w_sh_down, ln_g, ln_b, yg, seq)


def kernel(x, c, ada_w, ada_b, pool_w_in, pool_w_grp, pool_scale, pool_w_out, attn_w_in, attn_w_out, ln1_g, ln1_b, router_w, router_b, exp_w_gu, exp_w_down, sh_w_gu, sh_w_down, ln2_g, ln2_b):
    batch, seq, d = x.shape
    depth = ada_w.shape[0]
    t = batch * seq
    mod = _modulation(c, ada_w, ada_b).reshape(depth, batch, 1, 6 * d)
    perms = jnp.stack([_perm_matrix(dil) for _, dil in ATTN_PATTERNS[1:]])
    perms_t = jnp.swapaxes(perms, 1, 2)
    expand = (jnp.arange(LSE_LANES)[:, None] == (jnp.arange(d)[None, :] // HEAD_DIM)).astype(BF16)
    x2 = x.reshape(t, d)
    for i in range(depth):
        mod3 = mod[i]
        j = i // 2
        if i % 2 == 0:
            x2 = _pool_layer(x2, mod3, pool_w_in[j].astype(BF16), pool_w_grp[j].astype(BF16),
                             pool_scale[j], pool_w_out[j].astype(BF16), ln1_g[i], ln1_b[i], seq)
        else:
            qkv = _qkv_proj(x2, mod3, perms, attn_w_in[j].astype(BF16), seq)
            res = [_attention_group(qkv, g, batch, seq) for g in range(len(ATTN_PATTERNS))]
            x2 = _attn_out(x2, mod3, [r[0] for r in res], [r[1] for r in res], perms_t, expand,
                           attn_w_out[j].astype(BF16), ln1_g[i], ln1_b[i], seq)
        x2 = _moe_layer(x2, mod3, router_w[i], router_b[i], exp_w_gu, exp_w_down, i,
                        sh_w_gu[i].astype(BF16), sh_w_down[i].astype(BF16),
                        ln2_g[i], ln2_b[i], seq)
    return x2.reshape(batch, seq, d)
```

```python
import functools
import math

import jax
import jax.numpy as jnp
from jax import lax
from jax.experimental import pallas as pl
from jax.experimental.pallas import tpu as pltpu

F32 = jnp.float32
BF16 = jnp.bfloat16
U32 = jnp.uint32
I32 = jnp.int32

POOL_WINDOWS = (2, 4, 8, 16)
ATTN_PATTERNS = ((128, 1), (512, 4), (2048, 16))
HEAD_DIM = 64
N_HEADS = 16
Q_BLOCK = 128
N_EXPERTS = 64
TOP_K = 8
N_EXPERT_GROUPS = 8
TOPK_GROUPS = 4
ROUTED_SCALE = 2.5
EXPERT_ROWS = 256
DEPTH = 4
ALPHA = (2 * DEPTH) ** 0.25
LN_EPS = 1e-5

PERM_TILE = 256
POOL_HALO = 16
TOK_TILE = 256
CHUNK = 8
LOCAL_ROWS = -(-(TOK_TILE * TOP_K + N_EXPERTS * (CHUNK - 1)) // 256) * 256
NO_SLOT = -64.0
POS_RADIX = 64.0
WAIT_GROUP = 32
GEMM_DEPTH = 4
LSE_LANES = 128
VMEM_LIMIT = 56 * 1024 * 1024
NEG_BIG = -1e30


def _cparams(sem):
    return pltpu.CompilerParams(dimension_semantics=sem, vmem_limit_bytes=VMEM_LIMIT)


def _layer_norm(v, g, b):
    mu = jnp.mean(v, axis=-1, keepdims=True)
    c = v - mu
    var = jnp.mean(c * c, axis=-1, keepdims=True)
    return c * lax.rsqrt(var + LN_EPS) * g + b


def _silu(v):
    return v * (1.0 / (1.0 + jnp.exp(-v)))


def _pack_bf16_pairs(v):
    n = v.shape[1] // 2
    hi = lax.bitcast_convert_type(v[:, :n].astype(BF16).astype(F32), U32)
    lo = lax.bitcast_convert_type(v[:, n:].astype(BF16).astype(F32), U32)
    return hi | (lo >> 16)


def _unpack_bf16_pairs(p):
    hi = lax.bitcast_convert_type(p & jnp.uint32(0xFFFF0000), F32)
    lo = lax.bitcast_convert_type(p << 16, F32)
    return hi, lo


def _mod_kernel(c_ref, w_ref, b_ref, o_ref):
    cs = _silu(c_ref[...])
    o_ref[...] = jnp.dot(cs, w_ref[...], preferred_element_type=F32) + b_ref[...]


def _modulation(c, ada_w, ada_b):
    depth, d, n6 = ada_w.shape
    b = c.shape[0]
    tn = 1536
    return pl.pallas_call(
        _mod_kernel,
        grid=(depth, n6 // tn),
        in_specs=[
            pl.BlockSpec((b, d), lambda i, n: (0, 0)),
            pl.BlockSpec((None, d, tn), lambda i, n: (i, 0, n)),
            pl.BlockSpec((None, 1, tn), lambda i, n: (i, 0, n)),
        ],
        out_specs=pl.BlockSpec((None, b, tn), lambda i, n: (i, 0, n)),
        out_shape=jax.ShapeDtypeStruct((depth, b, n6), F32),
        compiler_params=_cparams(("arbitrary", "arbitrary")),
    )(c, ada_w, ada_b.reshape(depth, 1, n6))


def _mod_spec(chunk, d, tiles_per_batch):
    return pl.BlockSpec((None, 1, d), lambda *idx: (idx[0] // tiles_per_batch, 0, chunk))


def _pool_kernel(x_ref, sh_ref, sc_ref, g_ref, win_ref, wgrp_ref, cs_ref, wout_ref,
                 lng_ref, lnb_ref, o_ref, ext_ref, *, tiles_per_batch):
    tm, d = x_ref.shape
    s_idx = pl.program_id(0) % tiles_per_batch
    x = x_ref[...]
    h = (x * (1.0 + sc_ref[...]) + sh_ref[...]).astype(BF16)
    u = jnp.dot(h, win_ref[...], preferred_element_type=F32)

    @pl.when(s_idx == 0)
    def _():
        ext_ref[0:POOL_HALO, :] = jnp.zeros((POOL_HALO, d), F32)

    @pl.when(s_idx != 0)
    def _():
        ext_ref[0:POOL_HALO, :] = ext_ref[tm:tm + POOL_HALO, :]

    ext_ref[POOL_HALO:POOL_HALO + tm, :] = u

    pos = s_idx * tm + lax.broadcasted_iota(I32, (tm, 1), 0) + 1
    gc = d // len(POOL_WINDOWS)
    ys = []
    for g, w in enumerate(POOL_WINDOWS):
        cols = slice(g * gc, (g + 1) * gc)
        acc = u[:, cols]
        for j in range(1, w):
            acc = acc + ext_ref[POOL_HALO - j:POOL_HALO - j + tm, cols]
        cnt = jnp.minimum(pos, w).astype(F32)
        z = (acc / cnt - u[:, cols]).astype(BF16)
        ys.append(jnp.dot(z, wgrp_ref[g], preferred_element_type=F32))
    y = (jnp.concatenate(ys, axis=1) * cs_ref[...]).astype(BF16)
    out = jnp.dot(y, wout_ref[...], preferred_element_type=F32)
    v = ALPHA * x + (1.0 + g_ref[...]) * out
    o_ref[...] = _layer_norm(v, lng_ref[...], lnb_ref[...])


def _pool_layer(x2, mod3, w_in, w_grp, ch_scale, w_out, ln_g, ln_b, seq):
    t, d = x2.shape
    tm = 512
    tpb = seq // tm
    full = lambda shape: pl.BlockSpec(shape, lambda i: (0,) * len(shape))
    return pl.pallas_call(
        functools.partial(_pool_kernel, tiles_per_batch=tpb),
        grid=(t // tm,),
        in_specs=[
            pl.BlockSpec((tm, d), lambda i: (i, 0)),
            _mod_spec(0, d, tpb), _mod_spec(1, d, tpb), _mod_spec(2, d, tpb),
            full((d, d)), full(w_grp.shape), full((1, d)), full((d, d)),
            full((1, d)), full((1, d)),
        ],
        out_specs=pl.BlockSpec((tm, d), lambda i: (i, 0)),
        out_shape=jax.ShapeDtypeStruct((t, d), F32),
        scratch_shapes=[pltpu.VMEM((tm + POOL_HALO, d), F32)],
        compiler_params=_cparams(("arbitrary",)),
    )(x2, mod3, mod3, mod3, w_in, w_grp, ch_scale.reshape(1, d), w_out,
      ln_g.reshape(1, d), ln_b.reshape(1, d))


def _perm_matrix(dil):
    p = jnp.arange(PERM_TILE)
    chunk = PERM_TILE // dil
    src = (p % chunk) * dil + p // chunk
    return (src[:, None] == jnp.arange(PERM_TILE)[None, :]).astype(BF16)


def _qkv_kernel(x_ref, sh_ref, sc_ref, p_ref, w_ref, o_ref, h_ref):
    tm = x_ref.shape[0]
    g = pl.program_id(1)
    part = pl.program_id(2)

    @pl.when((g == 0) & (part == 0))
    def _():
        h = (x_ref[...] * (1.0 + sc_ref[...]) + sh_ref[...]).astype(BF16)
        h_ref[0] = h
        for gi in range(1, len(ATTN_PATTERNS)):
            for s in range(tm // PERM_TILE):
                rows = slice(s * PERM_TILE, (s + 1) * PERM_TILE)
                h_ref[gi, rows, :] = jnp.dot(
                    p_ref[gi - 1], h[rows, :], preferred_element_type=F32).astype(BF16)

    o_ref[...] = jnp.dot(h_ref[g], w_ref[...], preferred_element_type=F32).astype(BF16)


def _qkv_proj(x2, mod3, perms, w_in, seq):
    t, d = x2.shape
    ng = len(ATTN_PATTERNS)
    tm = 1024
    tpb = seq // tm
    return pl.pallas_call(
        _qkv_kernel,
        grid=(t // tm, ng, 3),
        in_specs=[
            pl.BlockSpec((tm, d), lambda m, g, p: (m, 0)),
            _mod_spec(0, d, tpb), _mod_spec(1, d, tpb),
            pl.BlockSpec(perms.shape, lambda m, g, p: (0, 0, 0)),
            pl.BlockSpec((d, d), lambda m, g, p: (0, g * 3 + p)),
        ],
        out_specs=pl.BlockSpec((tm, d), lambda m, g, p: (m, g * 3 + p)),
        out_shape=jax.ShapeDtypeStruct((t, ng * 3 * d), BF16),
        scratch_shapes=[pltpu.VMEM((ng, tm, d), BF16)],
        compiler_params=_cparams(("arbitrary", "arbitrary", "arbitrary")),
    )(x2, mod3, mod3, perms, w_in)


def _attn_kernel(q_ref, kp_ref, kc_ref, vp_ref, vc_ref, o_ref, lse_ref, *, group, dil):
    bq = Q_BLOCK
    d = N_HEADS * HEAD_DIM
    j = pl.program_id(2)
    q = q_ref[...].reshape(bq, d) * (HEAD_DIM ** -0.5)
    kp = kp_ref[...].reshape(bq, d)
    kc = kc_ref[...].reshape(bq, d)
    vp = vp_ref[...].reshape(bq, d)
    vc = vc_ref[...].reshape(bq, d)

    qi = lax.broadcasted_iota(I32, (bq, 2 * bq), 0)
    kj = lax.broadcasted_iota(I32, (bq, 2 * bq), 1)
    dist = qi + bq - kj
    steps = ATTN_PATTERNS[group][0] // dil
    valid = (dist >= 0) & (dist <= steps) & ((kj >= bq) | (j > 0))
    neg_dist = jnp.where(valid, (dist * -dil).astype(F32), NEG_BIG)
    lane = lax.broadcasted_iota(I32, (bq, LSE_LANES), 1)
    lse_tile = jnp.zeros((bq, LSE_LANES), F32)
    n_tot = len(ATTN_PATTERNS) * N_HEADS
    outs = []
    for h in range(N_HEADS):
        cols = slice(h * HEAD_DIM, (h + 1) * HEAD_DIM)
        slope = 2.0 ** (-8.0 * (group * N_HEADS + h + 1) / n_tot)
        kh = jnp.concatenate([kp[:, cols], kc[:, cols]], axis=0)
        vh = jnp.concatenate([vp[:, cols], vc[:, cols]], axis=0)
        s = lax.dot_general(q[:, cols], kh, (((1,), (1,)), ((), ())),
                            preferred_element_type=F32)
        s = s + slope * neg_dist
        m = jnp.max(s, axis=-1, keepdims=True)
        p = jnp.exp(s - m)
        den = jnp.sum(p, axis=-1, keepdims=True)
        o = jnp.dot(p.astype(BF16), vh, preferred_element_type=F32) / den
        outs.append(o)
        lse_tile = jnp.where(lane == h, m + jnp.log(den), lse_tile)
    o_ref[...] = jnp.concatenate(outs, axis=1).astype(BF16).reshape(o_ref.shape)
    lse_ref[...] = lse_tile.reshape(lse_ref.shape)


def _attention_group(qkv, group, batch, seq):
    dil = ATTN_PATTERNS[group][1]
    d = N_HEADS * HEAD_DIM
    t = qkv.shape[0]
    sub = seq // dil
    nb = sub // Q_BLOCK
    rows = Q_BLOCK if dil == 1 else PERM_TILE // dil
    chunks = Q_BLOCK // rows
    u = seq // (rows * dil)
    view = lambda a, c: a.reshape(batch, u, dil, rows, c)
    blk = lambda c: (None, chunks, None, rows, c)
    col0 = group * 3
    q_spec = pl.BlockSpec(blk(d), lambda b, r, j: (b, j, r, 0, col0))
    kc_spec = pl.BlockSpec(blk(d), lambda b, r, j: (b, j, r, 0, col0 + 1))
    kp_spec = pl.BlockSpec(blk(d), lambda b, r, j: (b, jnp.maximum(j - 1, 0), r, 0, col0 + 1))
    vc_spec = pl.BlockSpec(blk(d), lambda b, r, j: (b, j, r, 0, col0 + 2))
    vp_spec = pl.BlockSpec(blk(d), lambda b, r, j: (b, jnp.maximum(j - 1, 0), r, 0, col0 + 2))
    qkv5 = view(qkv, qkv.shape[1])
    o, lse = pl.pallas_call(
        functools.partial(_attn_kernel, group=group, dil=dil),
        grid=(batch, dil, nb),
        in_specs=[q_spec, kp_spec, kc_spec, vp_spec, vc_spec],
        out_specs=[
            pl.BlockSpec(blk(d), lambda b, r, j: (b, j, r, 0, 0)),
            pl.BlockSpec(blk(LSE_LANES), lambda b, r, j: (b, j, r, 0, 0)),
        ],
        out_shape=[
            jax.ShapeDtypeStruct((batch, u, dil, rows, d), BF16),
            jax.ShapeDtypeStruct((batch, u, dil, rows, LSE_LANES), F32),
        ],
        compiler_params=_cparams(("arbitrary", "arbitrary", "arbitrary")),
    )(qkv5, qkv5, qkv5, qkv5, qkv5)
    return o.reshape(t, d), lse.reshape(t, LSE_LANES)


def _split3(v):
    a = v.astype(BF16)
    r = v - a.astype(F32)
    b = r.astype(BF16)
    c = (r - b.astype(F32)).astype(BF16)
    return a, b, c


def _attn_out_kernel(x_ref, g_ref, o0_ref, o1_ref, o2_ref, l0_ref, l1_ref, l2_ref,
                     pt_ref, e_ref, wout_ref, lng_ref, lnb_ref, out_ref):
    tm, d = x_ref.shape
    o_refs = (o0_ref, o1_ref, o2_ref)
    l_refs = (l0_ref, l1_ref, l2_ref)
    n_sub = tm // PERM_TILE

    def unperm(gi, val_bf16):
        if gi == 0:
            return val_bf16.astype(F32)
        parts = [jnp.dot(pt_ref[gi - 1], val_bf16[s * PERM_TILE:(s + 1) * PERM_TILE, :],
                         preferred_element_type=F32) for s in range(n_sub)]
        return jnp.concatenate(parts, axis=0)

    lses = []
    for gi in range(3):
        l = l_refs[gi][...]
        if gi == 0:
            lses.append(l)
        else:
            a, b, c = _split3(l)
            lses.append(unperm(gi, a) + unperm(gi, b) + unperm(gi, c))
    mx = jnp.maximum(jnp.maximum(lses[0], lses[1]), lses[2])
    es = [jnp.exp(l - mx) for l in lses]
    tot = es[0] + es[1] + es[2]
    mixed = jnp.zeros((tm, d), F32)
    for gi in range(3):
        w = es[gi] / tot
        a, b, c = _split3(w)
        wide = (jnp.dot(a, e_ref[...], preferred_element_type=F32)
                + jnp.dot(b, e_ref[...], preferred_element_type=F32)
                + jnp.dot(c, e_ref[...], preferred_element_type=F32))
        mixed = mixed + wide * unperm(gi, o_refs[gi][...])
    y = jnp.dot(mixed.astype(BF16), wout_ref[...], preferred_element_type=F32)
    v = ALPHA * x_ref[...] + (1.0 + g_ref[...]) * y
    out_ref[...] = _layer_norm(v, lng_ref[...], lnb_ref[...])


def _attn_out(x2, mod3, os_, lses, perms_t, expand, w_out, ln_g, ln_b, seq):
    t, d = x2.shape
    tm = 512
    tpb = seq // tm
    row = lambda c: pl.BlockSpec((tm, c), lambda i: (i, 0))
    full = lambda shape: pl.BlockSpec(shape, lambda i: (0,) * len(shape))
    return pl.pallas_call(
        _attn_out_kernel,
        grid=(t // tm,),
        in_specs=[row(d), _mod_spec(2, d, tpb), row(d), row(d), row(d),
                  row(LSE_LANES), row(LSE_LANES), row(LSE_LANES),
                  full(perms_t.shape), full(expand.shape), full((d, d)),
                  full((1, d)), full((1, d))],
        out_specs=row(d),
        out_shape=jax.ShapeDtypeStruct((t, d), F32),
        compiler_params=_cparams(("arbitrary",)),
    )(x2, mod3, *os_, *lses, perms_t, expand, w_out, ln_g.reshape(1, d), ln_b.reshape(1, d))


def _router_kernel(x_ref, sh_ref, sc_ref, wr_ref, lg_ref):
    hf = x_ref[...] * (1.0 + sc_ref[...]) + sh_ref[...]
    lg_ref[...] = jnp.dot(hf, wr_ref[...], preferred_element_type=F32,
                          precision=lax.Precision.HIGHEST)


def _router(x2, mod3, w_router, seq):
    t, d = x2.shape
    tm = 512
    tpb = seq // tm
    ne = w_router.shape[1]
    return pl.pallas_call(
        _router_kernel,
        grid=(t // tm,),
        in_specs=[pl.BlockSpec((tm, d), lambda i: (i, 0)),
                  _mod_spec(3, d, tpb), _mod_spec(4, d, tpb),
                  pl.BlockSpec((d, ne), lambda i: (0, 0))],
        out_specs=pl.BlockSpec((tm, ne), lambda i: (i, 0)),
        out_shape=jax.ShapeDtypeStruct((t, ne), F32),
        compiler_params=_cparams(("arbitrary",)),
    )(x2, mod3, mod3, w_router)


def _first_index_of_max(v, iota, size):
    m = jnp.max(v, axis=0, keepdims=True)
    idx = jnp.min(jnp.where(v == m, iota, float(size)), axis=0, keepdims=True)
    return m, idx


def _topk_kernel(lg_ref, b_ref, tri_ref, ltri_ref, w_ref, pos_ref, cnt_ref):
    ne, tr = lg_ref.shape
    gsz = ne // N_EXPERT_GROUPS
    tile = pl.program_id(0)

    scores = 1.0 / (1.0 + jnp.exp(-lg_ref[...]))
    sel = scores + b_ref[...]
    iota_g = lax.broadcasted_iota(I32, (gsz, tr), 0).astype(F32)
    iota_n = lax.broadcasted_iota(I32, (N_EXPERT_GROUPS, tr), 0).astype(F32)
    gs = jnp.zeros((N_EXPERT_GROUPS, tr), F32)
    for g in range(N_EXPERT_GROUPS):
        blk = sel[g * gsz:(g + 1) * gsz, :]
        m1, i1 = _first_index_of_max(blk, iota_g, gsz)
        m2 = jnp.max(jnp.where(iota_g == i1, -jnp.inf, blk), axis=0, keepdims=True)
        gs = jnp.where(iota_n == float(g), m1 + m2, gs)
    gmask = jnp.zeros((N_EXPERT_GROUPS, tr), F32)
    for _ in range(TOPK_GROUPS):
        _, gi = _first_index_of_max(gs, iota_n, N_EXPERT_GROUPS)
        hit = iota_n == gi
        gmask = jnp.where(hit, 1.0, gmask)
        gs = jnp.where(hit, -jnp.inf, gs)
    masked_rows = []
    for g in range(N_EXPERT_GROUPS):
        keep = jnp.broadcast_to(gmask[g:g + 1, :], (gsz, tr)) > 0.5
        masked_rows.append(jnp.where(keep, sel[g * gsz:(g + 1) * gsz, :], -jnp.inf))
    cur = jnp.concatenate(masked_rows, axis=0)
    iota_e = lax.broadcasted_iota(I32, (ne, tr), 0).astype(F32)
    chosen = jnp.zeros((ne, tr), F32)
    for _ in range(TOP_K):
        _, ei = _first_index_of_max(cur, iota_e, ne)
        hit = iota_e == ei
        cur = jnp.where(hit, -jnp.inf, cur)
        chosen = jnp.where(hit, 1.0, chosen)
    picked = chosen > 0.5
    top_scores = jnp.where(picked, scores, 0.0)
    wsum = jnp.sum(top_scores, axis=0, keepdims=True)
    w_ref[...] = top_scores / wsum * ROUTED_SCALE
    before = jnp.dot(chosen.astype(BF16), tri_ref[...], preferred_element_type=F32)
    n = jnp.sum(chosen, axis=1, keepdims=True)
    n_chunks = jnp.floor((n + (CHUNK - 1)) * (1.0 / CHUNK))
    run_off = jnp.dot(ltri_ref[...], jnp.broadcast_to(n_chunks, (ne, 128)).astype(BF16),
                      preferred_element_type=F32)[:, 0:1] * CHUNK
    pos_ref[...] = jnp.where(picked, before + run_off, NO_SLOT)

    @pl.when(tile == 0)
    def _():
        cnt_ref[...] = jnp.zeros_like(cnt_ref)

    lane = lax.broadcasted_iota(I32, cnt_ref.shape, 1)
    cnt_ref[...] = jnp.where(lane == tile, jnp.broadcast_to(n, cnt_ref.shape).astype(I32),
                             cnt_ref[...])


def _topk_route(logits_t, router_b):
    ne, t = logits_t.shape
    tr = TOK_TILE
    assert t // tr <= 128
    tri = (jnp.arange(tr)[:, None] < jnp.arange(tr)[None, :]).astype(BF16)
    ltri = (jnp.arange(ne)[None, :] < jnp.arange(ne)[:, None]).astype(BF16)
    out = lambda dt: jax.ShapeDtypeStruct((ne, t), dt)
    row = pl.BlockSpec((ne, tr), lambda i: (0, i))
    return pl.pallas_call(
        _topk_kernel,
        grid=(t // tr,),
        in_specs=[pl.BlockSpec((ne, tr), lambda i: (0, i)),
                  pl.BlockSpec((ne, 1), lambda i: (0, 0)),
                  pl.BlockSpec((tr, tr), lambda i: (0, 0)),
                  pl.BlockSpec((ne, ne), lambda i: (0, 0))],
        out_specs=[row, row, pl.BlockSpec((ne, 128), lambda i: (0, 0))],
        out_shape=[out(F32), out(F32), jax.ShapeDtypeStruct((ne, 128), I32)],
        compiler_params=_cparams(("arbitrary",)),
    )(logits_t, router_b.reshape(ne, 1), tri, ltri)


def _slot_plan(counts, n_tiles, n_blocks):
    n = counts[:, :n_tiles]
    nch = (n + (CHUNK - 1)) // CHUNK
    rows = jnp.sum(nch, axis=1) * CHUNK
    region = ((rows + EXPERT_ROWS - 1) // EXPERT_ROWS) * EXPERT_ROWS
    region_end = jnp.cumsum(region)
    region_start = region_end - region
    run_chunk = region_start[:, None] // CHUNK + jnp.cumsum(nch, axis=1) - nch
    local_end = jnp.cumsum(nch, axis=0)
    local_chunk = local_end - nch
    c = jnp.arange(LOCAL_ROWS // CHUNK, dtype=I32)
    expert_of_c = jnp.sum((local_end[:, :, None] <= c[None, None, :]).astype(I32), axis=0)
    owner = expert_of_c[None] == jnp.arange(N_EXPERTS, dtype=I32)[:, None, None]
    slot_chunk = jnp.sum(jnp.where(owner, (run_chunk - local_chunk)[:, :, None], 0), axis=0) + c
    block_row = jnp.arange(n_blocks, dtype=I32) * EXPERT_ROWS
    group_e = jnp.minimum(jnp.sum((region_end[None, :] <= block_row[:, None]).astype(I32), axis=1),
                          N_EXPERTS - 1)
    n_used = region_end[-1] // EXPERT_ROWS
    nonempty = region > 0
    run_of_expert = jnp.cumsum(nonempty.astype(I32)) - 1
    experts = jnp.arange(N_EXPERTS, dtype=I32)
    run_expert = jnp.sum(jnp.where((run_of_expert[None, :] == experts[:, None]) & nonempty[None, :],
                                   experts[None, :], 0), axis=1)
    blocks = jnp.arange(n_blocks, dtype=I32)
    first = ((blocks == 0) | (group_e != jnp.roll(group_e, 1))) & (blocks < n_used)
    per_tile = lambda a: (a * CHUNK).T.astype(I32)
    return dict(
        run_begin_rows=per_tile(local_chunk)[:, None, :], run_end_rows=per_tile(local_end)[:, None, :],
        run_begin_cols=per_tile(local_chunk)[:, :, None], run_end_cols=per_tile(local_end)[:, :, None],
        block_run=(jnp.cumsum(first.astype(I32)) - 1).astype(I32), block_first=first.astype(I32),
        run_expert=run_expert.astype(I32), n_runs=jnp.sum(nonempty.astype(I32)).reshape(1),
        slot_chunk=slot_chunk.reshape(-1).astype(I32),
        tile_chunks=jnp.sum(nch, axis=0).astype(I32),
        tail_chunk=((region_start + rows) // CHUNK).astype(I32),
        tail_chunks=((region - rows) // CHUNK).astype(I32),
        group_e=group_e, n_used=(region_end[-1:] // EXPERT_ROWS).astype(I32))


def _chunk_rows(ref, chunk_index):
    start = chunk_index * CHUNK
    if not isinstance(start, int):
        start = pl.multiple_of(start, CHUNK)
    return ref.at[pl.ds(start, CHUNK), :]


def _pos_digits(pos):
    hi = jnp.floor(pos * (1.0 / POS_RADIX))
    return hi.astype(BF16), (pos - POS_RADIX * hi).astype(BF16)


def _for_chunks_two_queues(count, start):
    def pair(p, carry):
        start(2 * p, 0)
        start(2 * p + 1, 1)
        return carry

    lax.fori_loop(0, lax.shift_right_logical(count, 1), pair, 0)

    @pl.when((count & 1) == 1)
    def _():
        start(count - 1, 0)


def _wait_chunks(count, src_ref, dst_ref, sem):
    group_rows = pl.ds(0, WAIT_GROUP * CHUNK)

    def wait_group(c, carry):
        pltpu.make_async_copy(src_ref.at[group_rows, :], dst_ref.at[group_rows, :], sem).wait()
        return carry

    def wait_one(c, carry):
        pltpu.make_async_copy(_chunk_rows(src_ref, 0), _chunk_rows(dst_ref, 0), sem).wait()
        return carry

    lax.fori_loop(0, lax.shift_right_logical(count, WAIT_GROUP.bit_length() - 1), wait_group, 0)
    lax.fori_loop(0, count & (WAIT_GROUP - 1), wait_one, 0)


def _dispatch_kernel(slot_ref, tile_ref, tail_ref, tailn_ref, used_ref,
                     x_ref, sh_ref, sc_ref, pos_ref, begin_ref, end_ref,
                     xg_hbm, stage_ref, zero_ref, sems, sem_blk):
    tr = x_ref.shape[0]
    i = pl.program_id(0)
    n_blocks = xg_hbm.shape[0] // EXPERT_ROWS
    chunks_per_tile = LOCAL_ROWS // CHUNK

    def drain(count, sem):
        _wait_chunks(count, stage_ref.at[0], xg_hbm, sem)

    @pl.when(i == 0)
    def _():
        zero_ref[...] = jnp.zeros_like(zero_ref)

        def block_copy(j):
            rows = pl.ds(pl.multiple_of(j * EXPERT_ROWS, EXPERT_ROWS), EXPERT_ROWS)
            return pltpu.make_async_copy(zero_ref, xg_hbm.at[rows, :], sem_blk)

        def start_block(j, carry):
            block_copy(j).start()
            return carry

        def wait_block(j, carry):
            block_copy(j).wait()
            return carry

        lax.fori_loop(used_ref[0], n_blocks, start_block, 0)

        def per_expert(e, total):
            def per_chunk(c, carry):
                pltpu.make_async_copy(_chunk_rows(zero_ref, 0),
                                      _chunk_rows(xg_hbm, tail_ref[e] + c), sems.at[0]).start()
                return carry
            lax.fori_loop(0, tailn_ref[e], per_chunk, 0)
            return total + tailn_ref[e]

        drain(lax.fori_loop(0, N_EXPERTS, per_expert, 0), sems.at[0])
        lax.fori_loop(used_ref[0], n_blocks, wait_block, 0)

    hf = (x_ref[...] * (1.0 + sc_ref[...]) + sh_ref[...]).astype(BF16)
    row_e = lax.broadcasted_iota(I32, (LOCAL_ROWS, N_EXPERTS), 0)
    owner = jnp.where((row_e >= begin_ref[...]) & (row_e < end_ref[...]), 1.0, 0.0).astype(BF16)
    hi_digit, lo_digit = _pos_digits(pos_ref[...])
    slot_pos = (POS_RADIX * jnp.dot(owner, hi_digit, preferred_element_type=F32)
                + jnp.dot(owner, lo_digit, preferred_element_type=F32))
    row_t = lax.broadcasted_iota(I32, (LOCAL_ROWS, tr), 0).astype(F32)
    perm = jnp.where(slot_pos == row_t, 1.0, 0.0).astype(BF16)
    rows = jnp.dot(perm, hf, preferred_element_type=F32)
    half = rows.shape[1] // 2
    buf = stage_ref.at[i % 2]
    buf[...] = (lax.bitcast_convert_type(rows[:, :half], U32)
                | (lax.bitcast_convert_type(rows[:, half:], U32) >> 16))

    def send(c, priority):
        pltpu.make_async_copy(_chunk_rows(buf, c),
                              _chunk_rows(xg_hbm, slot_ref[i * chunks_per_tile + c]),
                              sems.at[i % 2]).start(priority=priority)

    _for_chunks_two_queues(tile_ref[i], send)

    @pl.when(i > 0)
    def _():
        drain(tile_ref[jnp.maximum(i - 1, 0)], sems.at[(i + 1) % 2])

    @pl.when(i == pl.num_programs(0) - 1)
    def _():
        drain(tile_ref[i], sems.at[i % 2])


def _dispatch(plan, x2, mod3, pos, n_slots, seq):
    t, d = x2.shape
    tr = TOK_TILE
    tpb = seq // tr
    return pl.pallas_call(
        _dispatch_kernel,
        grid_spec=pltpu.PrefetchScalarGridSpec(
            num_scalar_prefetch=5,
            grid=(t // tr,),
            in_specs=[pl.BlockSpec((tr, d), lambda i, *_: (i, 0)),
                      _mod_spec(3, d, tpb), _mod_spec(4, d, tpb),
                      pl.BlockSpec((N_EXPERTS, tr), lambda i, *_: (0, i)),
                      pl.BlockSpec((None, 1, N_EXPERTS), lambda i, *_: (i, 0, 0)),
                      pl.BlockSpec((None, 1, N_EXPERTS), lambda i, *_: (i, 0, 0))],
            out_specs=pl.BlockSpec(memory_space=pl.ANY),
            scratch_shapes=[pltpu.VMEM((2, LOCAL_ROWS, d // 2), U32),
                            pltpu.VMEM((EXPERT_ROWS, d // 2), U32),
                            pltpu.SemaphoreType.DMA((2,)), pltpu.SemaphoreType.DMA],
        ),
        out_shape=jax.ShapeDtypeStruct((n_slots, d // 2), U32),
        compiler_params=_cparams(("arbitrary",)),
    )(plan["slot_chunk"], plan["tile_chunks"], plan["tail_chunk"], plan["tail_chunks"],
      plan["n_used"], x2, mod3, mod3, pos, plan["run_begin_rows"], plan["run_end_rows"])


def _expert_kernel(run_ref, first_ref, rexp_ref, nrun_ref, nu_ref,
                   xg_hbm, wgu_hbm, wdn_hbm, yg_hbm,
                   xbuf, ybuf, wgu_f32, wdn_f32, wgu_bf, wdn_bf, x_sem, y_sem, w_sem, z_sem,
                   *, layer):
    n_used = nu_ref[0]
    n_blocks = yg_hbm.shape[0] // EXPERT_ROWS
    depth, _, half = xbuf.shape
    f = wdn_bf.shape[0]

    def rows(j):
        return pl.ds(pl.multiple_of(j * EXPERT_ROWS, EXPERT_ROWS), EXPERT_ROWS)

    def x_copy(j):
        return pltpu.make_async_copy(xg_hbm.at[rows(j), :], xbuf.at[j % depth],
                                     x_sem.at[j % depth])

    def y_copy(j):
        return pltpu.make_async_copy(ybuf.at[j % depth], yg_hbm.at[rows(j), :],
                                     y_sem.at[j % depth])

    def w_copies(r):
        e = rexp_ref[r]
        return (pltpu.make_async_copy(wgu_hbm.at[layer, e], wgu_f32.at[r % 2], w_sem.at[r % 2]),
                pltpu.make_async_copy(wdn_hbm.at[layer, e], wdn_f32.at[r % 2], w_sem.at[r % 2]))

    def zero_copy(j):
        return pltpu.make_async_copy(ybuf.at[0], yg_hbm.at[rows(j), :], z_sem)

    ybuf[0] = jnp.zeros(ybuf.shape[1:], U32)

    def zero_start(j, carry):
        zero_copy(j).start()
        return carry

    def zero_wait(j, carry):
        zero_copy(j).wait()
        return carry

    lax.fori_loop(n_used, n_blocks, zero_start, 0)
    lax.fori_loop(n_used, n_blocks, zero_wait, 0)

    for cp in w_copies(0):
        cp.start()
    for a in range(depth - 1):
        @pl.when(a < n_used)
        def _():
            x_copy(a).start()

    def block(j, carry):
        r = run_ref[j]

        @pl.when(first_ref[j] == 1)
        def _():
            for cp in w_copies(r):
                cp.wait()

            @pl.when(r + 1 < nrun_ref[0])
            def _():
                for cp in w_copies(r + 1):
                    cp.start()

            wgu_bf[...] = wgu_f32[r % 2].astype(BF16)
            wdn_bf[...] = wdn_f32[r % 2].astype(BF16)

        x_copy(j).wait()

        @pl.when(j + (depth - 1) < n_used)
        def _():
            x_copy(j + (depth - 1)).start()

        @pl.when(j >= depth)
        def _():
            y_copy(j - depth).wait()

        hi, lo = _unpack_bf16_pairs(xbuf[j % depth])
        gu = (jnp.dot(hi.astype(BF16), wgu_bf[0:half, :], preferred_element_type=F32)
              + jnp.dot(lo.astype(BF16), wgu_bf[half:, :], preferred_element_type=F32))
        act = (_silu(gu[:, :f]) * gu[:, f:]).astype(BF16)
        ybuf[j % depth] = _pack_bf16_pairs(jnp.dot(act, wdn_bf[...], preferred_element_type=F32))
        y_copy(j).start()
        return carry

    lax.fori_loop(0, n_used, block, 0)

    for a in range(depth, 0, -1):
        @pl.when(n_used >= a)
        def _():
            y_copy(n_used - a).wait()


def _expert_gemm(plan, xg, w_gu, w_down, layer):
    n_slots, half = xg.shape
    _, ne, d, f2 = w_gu.shape
    f = w_down.shape[2]
    hbm = pl.BlockSpec(memory_space=pl.ANY)
    return pl.pallas_call(
        functools.partial(_expert_kernel, layer=layer),
        grid_spec=pltpu.PrefetchScalarGridSpec(
            num_scalar_prefetch=5,
            grid=(1,),
            in_specs=[hbm, hbm, hbm],
            out_specs=hbm,
            scratch_shapes=[pltpu.VMEM((GEMM_DEPTH, EXPERT_ROWS, half), U32),
                            pltpu.VMEM((GEMM_DEPTH, EXPERT_ROWS, half), U32),
                            pltpu.VMEM((2, d, f2), F32), pltpu.VMEM((2, f, d), F32),
                            pltpu.VMEM((d, f2), BF16), pltpu.VMEM((f, d), BF16),
                            pltpu.SemaphoreType.DMA((GEMM_DEPTH,)),
                            pltpu.SemaphoreType.DMA((GEMM_DEPTH,)),
                            pltpu.SemaphoreType.DMA((2,)), pltpu.SemaphoreType.DMA],
        ),
        out_shape=jax.ShapeDtypeStruct((n_slots, half), U32),
        compiler_params=_cparams(("arbitrary",)),
    )(plan["block_run"], plan["block_first"], plan["run_expert"], plan["n_runs"], plan["n_used"],
      xg, w_gu, w_down)


def _combine_kernel(slot_ref, tile_ref,
                    x_ref, sh_ref, sc_ref, g_ref, pos_ref, wt_ref, begin_ref, end_ref,
                    wsgu_ref, wsdn_ref, lng_ref, lnb_ref, yg_hbm, o_ref, stage_ref, sems):
    tr, d = x_ref.shape
    i = pl.program_id(0)
    n_tiles = pl.num_programs(0)
    chunks_per_tile = LOCAL_ROWS // CHUNK

    def fetch(tile, buffer):
        def start(c, priority):
            pltpu.make_async_copy(_chunk_rows(yg_hbm, slot_ref[tile * chunks_per_tile + c]),
                                  _chunk_rows(stage_ref.at[buffer], c),
                                  sems.at[buffer]).start(priority=priority)
        _for_chunks_two_queues(tile_ref[tile], start)

    @pl.when(i == 0)
    def _():
        stage_ref[...] = jnp.zeros_like(stage_ref)
        fetch(0, 0)

    nxt = jnp.minimum(i + 1, n_tiles - 1)
    for buffer in range(2):
        @pl.when((i + 1 < n_tiles) & (nxt % 2 == buffer))
        def _():
            fetch(nxt, buffer)

    x = x_ref[...]
    hf = (x * (1.0 + sc_ref[...]) + sh_ref[...]).astype(BF16)
    f = wsdn_ref.shape[0]
    su = jnp.dot(hf, wsgu_ref[...], preferred_element_type=F32)
    act = (_silu(su[:, :f]) * su[:, f:]).astype(BF16)
    shared = jnp.dot(act, wsdn_ref[...], preferred_element_type=F32)

    col_e = lax.broadcasted_iota(I32, (N_EXPERTS, LOCAL_ROWS), 1)
    owner = jnp.where((col_e >= begin_ref[...]) & (col_e < end_ref[...]), 1.0, 0.0).astype(BF16)
    hi_digit, lo_digit = _pos_digits(pos_ref[...])
    slot_pos = (POS_RADIX * jnp.dot(hi_digit, owner, preferred_element_type=F32)
                + jnp.dot(lo_digit, owner, preferred_element_type=F32))
    holds = slot_pos == lax.broadcasted_iota(I32, (tr, LOCAL_ROWS), 1).astype(F32)
    w_tok = wt_ref[...]
    w_tok_hi = w_tok.astype(BF16)
    w_tok_lo = (w_tok - w_tok_hi.astype(F32)).astype(BF16)
    part = lambda w: jnp.where(holds, jnp.dot(w, owner, preferred_element_type=F32),
                               0.0).astype(BF16)
    w_both = jnp.concatenate([part(w_tok_hi), part(w_tok_lo)], axis=0)

    buf = stage_ref.at[i % 2]
    _wait_chunks(tile_ref[i], yg_hbm, buf, sems.at[i % 2])

    y_hi, y_lo = _unpack_bf16_pairs(buf[...])
    y = jnp.concatenate([y_hi.astype(BF16), y_lo.astype(BF16)], axis=1)
    both = jnp.dot(w_both, y, preferred_element_type=F32)
    routed = both[:tr, :] + both[tr:, :]
    v = ALPHA * x + (1.0 + g_ref[...]) * (routed + shared)
    o_ref[...] = _layer_norm(v, lng_ref[...], lnb_ref[...])


def _combine(plan, x2, mod3, pos_tok, w_tok, w_sh_gu, w_sh_down, ln_g, ln_b, yg, seq):
    t, d = x2.shape
    tr = TOK_TILE
    tpb = seq // tr
    full = lambda shape: pl.BlockSpec(shape, lambda i, *_: (0,) * len(shape))
    return pl.pallas_call(
        _combine_kernel,
        grid_spec=pltpu.PrefetchScalarGridSpec(
            num_scalar_prefetch=2,
            grid=(t // tr,),
            in_specs=[pl.BlockSpec((tr, d), lambda i, *_: (i, 0)),
                      _mod_spec(3, d, tpb), _mod_spec(4, d, tpb), _mod_spec(5, d, tpb),
                      pl.BlockSpec((tr, N_EXPERTS), lambda i, *_: (i, 0)),
                      pl.BlockSpec((tr, N_EXPERTS), lambda i, *_: (i, 0)),
                      pl.BlockSpec((None, N_EXPERTS, 1), lambda i, *_: (i, 0, 0)),
                      pl.BlockSpec((None, N_EXPERTS, 1), lambda i, *_: (i, 0, 0)),
                      full(w_sh_gu.shape), full(w_sh_down.shape), full((1, d)), full((1, d)),
                      pl.BlockSpec(memory_space=pl.ANY)],
            out_specs=pl.BlockSpec((tr, d), lambda i, *_: (i, 0)),
            scratch_shapes=[pltpu.VMEM((2, LOCAL_ROWS, d // 2), U32),
                            pltpu.SemaphoreType.DMA((2,))],
        ),
        out_shape=jax.ShapeDtypeStruct((t, d), F32),
        compiler_params=_cparams(("arbitrary",)),
    )(plan["slot_chunk"], plan["tile_chunks"],
      x2, mod3, mod3, mod3, pos_tok, w_tok, plan["run_begin_cols"], plan["run_end_cols"],
      w_sh_gu, w_sh_down, ln_g.reshape(1, d), ln_b.reshape(1, d), yg)


def _moe_layer(x2, mod3, w_router, router_b, w_gu, w_down, layer, w_sh_gu, w_sh_down,
               ln_g, ln_b, seq):
    t, d = x2.shape
    n_tiles = t // TOK_TILE
    logits = _router(x2, mod3, w_router, seq)
    w_top, pos, counts = _topk_route(logits.T, router_b)
    bound = t * TOP_K + n_tiles * N_EXPERTS * (CHUNK - 1) + N_EXPERTS * (EXPERT_ROWS - 1)
    n_blocks = -(-bound // EXPERT_ROWS)
    plan = _slot_plan(counts, n_tiles, n_blocks)
    xg = _dispatch(plan, x2, mod3, pos, n_blocks * EXPERT_ROWS, seq)
    yg = _expert_gemm(plan, xg, w_gu, w_down, layer)
    return _combine(plan, x2, mod3, pos.T, w_top.T, w_sh_gu, w_sh_down, ln_g, ln_b, yg, seq)


def kernel(x, c, ada_w, ada_b, pool_w_in, pool_w_grp, pool_scale, pool_w_out, attn_w_in, attn_w_out, ln1_g, ln1_b, router_w, router_b, exp_w_gu, exp_w_down, sh_w_gu, sh_w_down, ln2_g, ln2_b):
    batch, seq, d = x.shape
    depth = ada_w.shape[0]
    t = batch * seq
    mod = _modulation(c, ada_w, ada_b).reshape(depth, batch, 1, 6 * d)
    perms = jnp.stack([_perm_matrix(dil) for _, dil in ATTN_PATTERNS[1:]])
    perms_t = jnp.swapaxes(perms, 1, 2)
    expand = (jnp.arange(LSE_LANES)[:, None] == (jnp.arange(d)[None, :] // HEAD_DIM)).astype(BF16)
    x2 = x.reshape(t, d)
    for i in range(depth):
        mod3 = mod[i]
        j = i // 2
        if i % 2 == 0:
            x2 = _pool_layer(x2, mod3, pool_w_in[j].astype(BF16), pool_w_grp[j].astype(BF16),
                             pool_scale[j], pool_w_out[j].astype(BF16), ln1_g[i], ln1_b[i], seq)
        else:
            qkv = _qkv_proj(x2, mod3, perms, attn_w_in[j].astype(BF16), seq)
            res = [_attention_group(qkv, g, batch, seq) for g in range(len(ATTN_PATTERNS))]
            x2 = _attn_out(x2, mod3, [r[0] for r in res], [r[1] for r in res], perms_t, expand,
                           attn_w_out[j].astype(BF16), ln1_g[i], ln1_b[i], seq)
        x2 = _moe_layer(x2, mod3, router_w[i], router_b[i], exp_w_gu, exp_w_down, i,
                        sh_w_gu[i].astype(BF16), sh_w_down[i].astype(BF16),
                        ln2_g[i], ln2_b[i], seq)
    return x2.reshape(batch, seq, d)
```

```python
import functools
import math

import jax
import jax.numpy as jnp
from jax import lax
from jax.experimental import pallas as pl
from jax.experimental.pallas import tpu as pltpu

F32 = jnp.float32
BF16 = jnp.bfloat16
U32 = jnp.uint32
I32 = jnp.int32

POOL_WINDOWS = (2, 4, 8, 16)
ATTN_PATTERNS = ((128, 1), (512, 4), (2048, 16))
HEAD_DIM = 64
N_HEADS = 16
Q_BLOCK = 128
N_EXPERTS = 64
TOP_K = 8
N_EXPERT_GROUPS = 8
TOPK_GROUPS = 4
ROUTED_SCALE = 2.5
EXPERT_ROWS = 256
DEPTH = 4
ALPHA = (2 * DEPTH) ** 0.25
LN_EPS = 1e-5

PERM_TILE = 256
POOL_HALO = 16
TOK_TILE = 256
CHUNK = 8
LOCAL_ROWS = -(-(TOK_TILE * TOP_K + N_EXPERTS * (CHUNK - 1)) // 256) * 256
NO_SLOT = -64.0
POS_RADIX = 64.0
WAIT_GROUP = 32
GEMM_DEPTH = 4
LSE_LANES = 128
VMEM_LIMIT = 56 * 1024 * 1024
NEG_BIG = -1e30


def _cparams(sem):
    return pltpu.CompilerParams(dimension_semantics=sem, vmem_limit_bytes=VMEM_LIMIT)


def _layer_norm(v, g, b):
    mu = jnp.mean(v, axis=-1, keepdims=True)
    c = v - mu
    var = jnp.mean(c * c, axis=-1, keepdims=True)
    return c * lax.rsqrt(var + LN_EPS) * g + b


def _silu(v):
    return v * (1.0 / (1.0 + jnp.exp(-v)))


def _pack_bf16_pairs(v):
    n = v.shape[1] // 2
    hi = lax.bitcast_convert_type(v[:, :n].astype(BF16).astype(F32), U32)
    lo = lax.bitcast_convert_type(v[:, n:].astype(BF16).astype(F32), U32)
    return hi | (lo >> 16)


def _unpack_bf16_pairs(p):
    hi = lax.bitcast_convert_type(p & jnp.uint32(0xFFFF0000), F32)
    lo = lax.bitcast_convert_type(p << 16, F32)
    return hi, lo


def _mod_kernel(c_ref, w_ref, b_ref, o_ref):
    cs = _silu(c_ref[...])
    o_ref[...] = jnp.dot(cs, w_ref[...], preferred_element_type=F32) + b_ref[...]


def _modulation(c, ada_w, ada_b):
    depth, d, n6 = ada_w.shape
    b = c.shape[0]
    tn = 1536
    return pl.pallas_call(
        _mod_kernel,
        grid=(depth, n6 // tn),
        in_specs=[
            pl.BlockSpec((b, d), lambda i, n: (0, 0)),
            pl.BlockSpec((None, d, tn), lambda i, n: (i, 0, n)),
            pl.BlockSpec((None, 1, tn), lambda i, n: (i, 0, n)),
        ],
        out_specs=pl.BlockSpec((None, b, tn), lambda i, n: (i, 0, n)),
        out_shape=jax.ShapeDtypeStruct((depth, b, n6), F32),
        compiler_params=_cparams(("arbitrary", "arbitrary")),
    )(c, ada_w, ada_b.reshape(depth, 1, n6))


def _mod_spec(chunk, d, tiles_per_batch):
    return pl.BlockSpec((None, 1, d), lambda *idx: (idx[0] // tiles_per_batch, 0, chunk))


def _pool_kernel(x_ref, sh_ref, sc_ref, g_ref, win_ref, wgrp_ref, cs_ref, wout_ref,
                 lng_ref, lnb_ref, o_ref, ext_ref, *, tiles_per_batch):
    tm, d = x_ref.shape
    s_idx = pl.program_id(0) % tiles_per_batch
    x = x_ref[...]
    h = (x * (1.0 + sc_ref[...]) + sh_ref[...]).astype(BF16)
    u = jnp.dot(h, win_ref[...], preferred_element_type=F32)

    @pl.when(s_idx == 0)
    def _():
        ext_ref[0:POOL_HALO, :] = jnp.zeros((POOL_HALO, d), F32)

    @pl.when(s_idx != 0)
    def _():
        ext_ref[0:POOL_HALO, :] = ext_ref[tm:tm + POOL_HALO, :]

    ext_ref[POOL_HALO:POOL_HALO + tm, :] = u

    pos = s_idx * tm + lax.broadcasted_iota(I32, (tm, 1), 0) + 1
    gc = d // len(POOL_WINDOWS)
    ys = []
    for g, w in enumerate(POOL_WINDOWS):
        cols = slice(g * gc, (g + 1) * gc)
        acc = u[:, cols]
        for j in range(1, w):
            acc = acc + ext_ref[POOL_HALO - j:POOL_HALO - j + tm, cols]
        cnt = jnp.minimum(pos, w).astype(F32)
        z = (acc / cnt - u[:, cols]).astype(BF16)
        ys.append(jnp.dot(z, wgrp_ref[g], preferred_element_type=F32))
    y = (jnp.concatenate(ys, axis=1) * cs_ref[...]).astype(BF16)
    out = jnp.dot(y, wout_ref[...], preferred_element_type=F32)
    v = ALPHA * x + (1.0 + g_ref[...]) * out
    o_ref[...] = _layer_norm(v, lng_ref[...], lnb_ref[...])


def _pool_layer(x2, mod3, w_in, w_grp, ch_scale, w_out, ln_g, ln_b, seq):
    t, d = x2.shape
    tm = 512
    tpb = seq // tm
    full = lambda shape: pl.BlockSpec(shape, lambda i: (0,) * len(shape))
    return pl.pallas_call(
        functools.partial(_pool_kernel, tiles_per_batch=tpb),
        grid=(t // tm,),
        in_specs=[
            pl.BlockSpec((tm, d), lambda i: (i, 0)),
            _mod_spec(0, d, tpb), _mod_spec(1, d, tpb), _mod_spec(2, d, tpb),
            full((d, d)), full(w_grp.shape), full((1, d)), full((d, d)),
            full((1, d)), full((1, d)),
        ],
        out_specs=pl.BlockSpec((tm, d), lambda i: (i, 0)),
        out_shape=jax.ShapeDtypeStruct((t, d), F32),
        scratch_shapes=[pltpu.VMEM((tm + POOL_HALO, d), F32)],
        compiler_params=_cparams(("arbitrary",)),
    )(x2, mod3, mod3, mod3, w_in, w_grp, ch_scale.reshape(1, d), w_out,
      ln_g.reshape(1, d), ln_b.reshape(1, d))


def _perm_matrix(dil):
    p = jnp.arange(PERM_TILE)
    chunk = PERM_TILE // dil
    src = (p % chunk) * dil + p // chunk
    return (src[:, None] == jnp.arange(PERM_TILE)[None, :]).astype(BF16)


def _qkv_kernel(x_ref, sh_ref, sc_ref, p_ref, w_ref, o_ref, h_ref):
    tm = x_ref.shape[0]
    g = pl.program_id(1)
    part = pl.program_id(2)

    @pl.when((g == 0) & (part == 0))
    def _():
        h = (x_ref[...] * (1.0 + sc_ref[...]) + sh_ref[...]).astype(BF16)
        h_ref[0] = h
        for gi in range(1, len(ATTN_PATTERNS)):
            for s in range(tm // PERM_TILE):
                rows = slice(s * PERM_TILE, (s + 1) * PERM_TILE)
                h_ref[gi, rows, :] = jnp.dot(
                    p_ref[gi - 1], h[rows, :], preferred_element_type=F32).astype(BF16)

    o_ref[...] = jnp.dot(h_ref[g], w_ref[...], preferred_element_type=F32).astype(BF16)


def _qkv_proj(x2, mod3, perms, w_in, seq):
    t, d = x2.shape
    ng = len(ATTN_PATTERNS)
    tm = 1024
    tpb = seq // tm
    return pl.pallas_call(
        _qkv_kernel,
        grid=(t // tm, ng, 3),
        in_specs=[
            pl.BlockSpec((tm, d), lambda m, g, p: (m, 0)),
            _mod_spec(0, d, tpb), _mod_spec(1, d, tpb),
            pl.BlockSpec(perms.shape, lambda m, g, p: (0, 0, 0)),
            pl.BlockSpec((d, d), lambda m, g, p: (0, g * 3 + p)),
        ],
        out_specs=pl.BlockSpec((tm, d), lambda m, g, p: (m, g * 3 + p)),
        out_shape=jax.ShapeDtypeStruct((t, ng * 3 * d), BF16),
        scratch_shapes=[pltpu.VMEM((ng, tm, d), BF16)],
        compiler_params=_cparams(("arbitrary", "arbitrary", "arbitrary")),
    )(x2, mod3, mod3, perms, w_in)


def _attn_kernel(q_ref, kp_ref, kc_ref, vp_ref, vc_ref, o_ref, lse_ref, *, group, dil):
    bq = Q_BLOCK
    d = N_HEADS * HEAD_DIM
    j = pl.program_id(2)
    q = q_ref[...].reshape(bq, d)
    kp = kp_ref[...].reshape(bq, d)
    kc = kc_ref[...].reshape(bq, d)
    vp = vp_ref[...].reshape(bq, d)
    vc = vc_ref[...].reshape(bq, d)

    qi = lax.broadcasted_iota(I32, (bq, 2 * bq), 0)
    kj = lax.broadcasted_iota(I32, (bq, 2 * bq), 1)
    dist = qi + bq - kj
    steps = ATTN_PATTERNS[group][0] // dil
    valid = (dist >= 0) & (dist <= steps) & ((kj >= bq) | (j > 0))
    neg_dist = jnp.where(valid, (dist * -dil).astype(F32), NEG_BIG)
    lane = lax.broadcasted_iota(I32, (bq, LSE_LANES), 1)
    lse_tile = jnp.zeros((bq, LSE_LANES), F32)
    n_tot = len(ATTN_PATTERNS) * N_HEADS
    pair = 2 * HEAD_DIM
    first_half = lax.broadcasted_iota(I32, (bq, pair), 1) < HEAD_DIM
    outs = []
    for hp in range(N_HEADS // 2):
        cols = slice(hp * pair, (hp + 1) * pair)
        q2 = q[:, cols]
        k2 = jnp.concatenate([kp[:, cols], kc[:, cols]], axis=0)
        v2 = jnp.concatenate([vp[:, cols], vc[:, cols]], axis=0)
        halves = []
        for sub in range(2):
            h = 2 * hp + sub
            slope = 2.0 ** (-8.0 * (group * N_HEADS + h + 1) / n_tot)
            mine = first_half if sub == 0 else jnp.logical_not(first_half)
            qh = jnp.where(mine, q2, jnp.zeros_like(q2))
            s = lax.dot_general(qh, k2, (((1,), (1,)), ((), ())), preferred_element_type=F32)
            s = s + slope * neg_dist
            m = jnp.max(s, axis=-1, keepdims=True)
            p = jnp.exp(s - m)
            den = jnp.sum(p, axis=-1, keepdims=True)
            halves.append(jnp.dot(p.astype(BF16), v2, preferred_element_type=F32) / den)
            lse_tile = jnp.where(lane == h, m + jnp.log(den), lse_tile)
        outs.append(jnp.where(first_half, halves[0], halves[1]))
    o_ref[...] = jnp.concatenate(outs, axis=1).astype(BF16).reshape(o_ref.shape)
    lse_ref[...] = lse_tile.reshape(lse_ref.shape)


def _attention_group(qkv, group, batch, seq):
    dil = ATTN_PATTERNS[group][1]
    d = N_HEADS * HEAD_DIM
    t = qkv.shape[0]
    sub = seq // dil
    nb = sub // Q_BLOCK
    rows = Q_BLOCK if dil == 1 else PERM_TILE // dil
    chunks = Q_BLOCK // rows
    u = seq // (rows * dil)
    view = lambda a, c: a.reshape(batch, u, dil, rows, c)
    blk = lambda c: (None, chunks, None, rows, c)
    col0 = group * 3
    q_spec = pl.BlockSpec(blk(d), lambda b, r, j: (b, j, r, 0, col0))
    kc_spec = pl.BlockSpec(blk(d), lambda b, r, j: (b, j, r, 0, col0 + 1))
    kp_spec = pl.BlockSpec(blk(d), lambda b, r, j: (b, jnp.maximum(j - 1, 0), r, 0, col0 + 1))
    vc_spec = pl.BlockSpec(blk(d), lambda b, r, j: (b, j, r, 0, col0 + 2))
    vp_spec = pl.BlockSpec(blk(d), lambda b, r, j: (b, jnp.maximum(j - 1, 0), r, 0, col0 + 2))
    qkv5 = view(qkv, qkv.shape[1])
    o, lse = pl.pallas_call(
        functools.partial(_attn_kernel, group=group, dil=dil),
        grid=(batch, dil, nb),
        in_specs=[q_spec, kp_spec, kc_spec, vp_spec, vc_spec],
        out_specs=[
            pl.BlockSpec(blk(d), lambda b, r, j: (b, j, r, 0, 0)),
            pl.BlockSpec(blk(LSE_LANES), lambda b, r, j: (b, j, r, 0, 0)),
        ],
        out_shape=[
            jax.ShapeDtypeStruct((batch, u, dil, rows, d), BF16),
            jax.ShapeDtypeStruct((batch, u, dil, rows, LSE_LANES), F32),
        ],
        compiler_params=_cparams(("arbitrary", "arbitrary", "arbitrary")),
    )(qkv5, qkv5, qkv5, qkv5, qkv5)
    return o.reshape(t, d), lse.reshape(t, LSE_LANES)


def _split3(v):
    a = v.astype(BF16)
    r = v - a.astype(F32)
    b = r.astype(BF16)
    c = (r - b.astype(F32)).astype(BF16)
    return a, b, c


def _attn_out_kernel(x_ref, g_ref, o0_ref, o1_ref, o2_ref, l0_ref, l1_ref, l2_ref,
                     pt_ref, e_ref, wout_ref, lng_ref, lnb_ref, out_ref):
    tm, d = x_ref.shape
    o_refs = (o0_ref, o1_ref, o2_ref)
    l_refs = (l0_ref, l1_ref, l2_ref)
    n_sub = tm // PERM_TILE

    def unperm(gi, val_bf16):
        if gi == 0:
            return val_bf16.astype(F32)
        parts = [jnp.dot(pt_ref[gi - 1], val_bf16[s * PERM_TILE:(s + 1) * PERM_TILE, :],
                         preferred_element_type=F32) for s in range(n_sub)]
        return jnp.concatenate(parts, axis=0)

    lses = []
    for gi in range(3):
        l = l_refs[gi][...]
        if gi == 0:
            lses.append(l)
        else:
            a, b, c = _split3(l)
            lses.append(unperm(gi, a) + unperm(gi, b) + unperm(gi, c))
    mx = jnp.maximum(jnp.maximum(lses[0], lses[1]), lses[2])
    es = [jnp.exp(l - mx) for l in lses]
    tot = es[0] + es[1] + es[2]
    mixed = jnp.zeros((tm, d), F32)
    for gi in range(3):
        w = es[gi] / tot
        a, b, c = _split3(w)
        wide = (jnp.dot(a, e_ref[...], preferred_element_type=F32)
                + jnp.dot(b, e_ref[...], preferred_element_type=F32)
                + jnp.dot(c, e_ref[...], preferred_element_type=F32))
        mixed = mixed + wide * unperm(gi, o_refs[gi][...])
    y = jnp.dot(mixed.astype(BF16), wout_ref[...], preferred_element_type=F32)
    v = ALPHA * x_ref[...] + (1.0 + g_ref[...]) * y
    out_ref[...] = _layer_norm(v, lng_ref[...], lnb_ref[...])


def _attn_out(x2, mod3, os_, lses, perms_t, expand, w_out, ln_g, ln_b, seq):
    t, d = x2.shape
    tm = 512
    tpb = seq // tm
    row = lambda c: pl.BlockSpec((tm, c), lambda i: (i, 0))
    full = lambda shape: pl.BlockSpec(shape, lambda i: (0,) * len(shape))
    return pl.pallas_call(
        _attn_out_kernel,
        grid=(t // tm,),
        in_specs=[row(d), _mod_spec(2, d, tpb), row(d), row(d), row(d),
                  row(LSE_LANES), row(LSE_LANES), row(LSE_LANES),
                  full(perms_t.shape), full(expand.shape), full((d, d)),
                  full((1, d)), full((1, d))],
        out_specs=row(d),
        out_shape=jax.ShapeDtypeStruct((t, d), F32),
        compiler_params=_cparams(("arbitrary",)),
    )(x2, mod3, *os_, *lses, perms_t, expand, w_out, ln_g.reshape(1, d), ln_b.reshape(1, d))


def _router_kernel(x_ref, sh_ref, sc_ref, wr_ref, lg_ref):
    hf = x_ref[...] * (1.0 + sc_ref[...]) + sh_ref[...]
    lg_ref[...] = jnp.dot(hf, wr_ref[...], preferred_element_type=F32,
                          precision=lax.Precision.HIGHEST)


def _router(x2, mod3, w_router, seq):
    t, d = x2.shape
    tm = 512
    tpb = seq // tm
    ne = w_router.shape[1]
    return pl.pallas_call(
        _router_kernel,
        grid=(t // tm,),
        in_specs=[pl.BlockSpec((tm, d), lambda i: (i, 0)),
                  _mod_spec(3, d, tpb), _mod_spec(4, d, tpb),
                  pl.BlockSpec((d, ne), lambda i: (0, 0))],
        out_specs=pl.BlockSpec((tm, ne), lambda i: (i, 0)),
        out_shape=jax.ShapeDtypeStruct((t, ne), F32),
        compiler_params=_cparams(("arbitrary",)),
    )(x2, mod3, mod3, w_router)


def _first_index_of_max(v, iota, size):
    m = jnp.max(v, axis=0, keepdims=True)
    idx = jnp.min(jnp.where(v == m, iota, float(size)), axis=0, keepdims=True)
    return m, idx


def _topk_kernel(lg_ref, b_ref, tri_ref, ltri_ref, w_ref, pos_ref, cnt_ref):
    ne, tr = lg_ref.shape
    gsz = ne // N_EXPERT_GROUPS
    tile = pl.program_id(0)

    scores = 1.0 / (1.0 + jnp.exp(-lg_ref[...]))
    sel = scores + b_ref[...]
    iota_g = lax.broadcasted_iota(I32, (gsz, tr), 0).astype(F32)
    iota_n = lax.broadcasted_iota(I32, (N_EXPERT_GROUPS, tr), 0).astype(F32)
    gs = jnp.zeros((N_EXPERT_GROUPS, tr), F32)
    for g in range(N_EXPERT_GROUPS):
        blk = sel[g * gsz:(g + 1) * gsz, :]
        m1, i1 = _first_index_of_max(blk, iota_g, gsz)
        m2 = jnp.max(jnp.where(iota_g == i1, -jnp.inf, blk), axis=0, keepdims=True)
        gs = jnp.where(iota_n == float(g), m1 + m2, gs)
    gmask = jnp.zeros((N_EXPERT_GROUPS, tr), F32)
    for _ in range(TOPK_GROUPS):
        _, gi = _first_index_of_max(gs, iota_n, N_EXPERT_GROUPS)
        hit = iota_n == gi
        gmask = jnp.where(hit, 1.0, gmask)
        gs = jnp.where(hit, -jnp.inf, gs)
    masked_rows = []
    for g in range(N_EXPERT_GROUPS):
        keep = jnp.broadcast_to(gmask[g:g + 1, :], (gsz, tr)) > 0.5
        masked_rows.append(jnp.where(keep, sel[g * gsz:(g + 1) * gsz, :], -jnp.inf))
    cur = jnp.concatenate(masked_rows, axis=0)
    iota_e = lax.broadcasted_iota(I32, (ne, tr), 0).astype(F32)
    chosen = jnp.zeros((ne, tr), F32)
    for _ in range(TOP_K):
        _, ei = _first_index_of_max(cur, iota_e, ne)
        hit = iota_e == ei
        cur = jnp.where(hit, -jnp.inf, cur)
        chosen = jnp.where(hit, 1.0, chosen)
    picked = chosen > 0.5
    top_scores = jnp.where(picked, scores, 0.0)
    wsum = jnp.sum(top_scores, axis=0, keepdims=True)
    w_ref[...] = top_scores / wsum * ROUTED_SCALE
    before = jnp.dot(chosen.astype(BF16), tri_ref[...], preferred_element_type=F32)
    n = jnp.sum(chosen, axis=1, keepdims=True)
    n_chunks = jnp.floor((n + (CHUNK - 1)) * (1.0 / CHUNK))
    run_off = jnp.dot(ltri_ref[...], jnp.broadcast_to(n_chunks, (ne, 128)).astype(BF16),
                      preferred_element_type=F32)[:, 0:1] * CHUNK
    pos_ref[...] = jnp.where(picked, before + run_off, NO_SLOT)

    @pl.when(tile == 0)
    def _():
        cnt_ref[...] = jnp.zeros_like(cnt_ref)

    lane = lax.broadcasted_iota(I32, cnt_ref.shape, 1)
    cnt_ref[...] = jnp.where(lane == tile, jnp.broadcast_to(n, cnt_ref.shape).astype(I32),
                             cnt_ref[...])


def _topk_route(logits_t, router_b):
    ne, t = logits_t.shape
    tr = TOK_TILE
    assert t // tr <= 128
    tri = (jnp.arange(tr)[:, None] < jnp.arange(tr)[None, :]).astype(BF16)
    ltri = (jnp.arange(ne)[None, :] < jnp.arange(ne)[:, None]).astype(BF16)
    out = lambda dt: jax.ShapeDtypeStruct((ne, t), dt)
    row = pl.BlockSpec((ne, tr), lambda i: (0, i))
    return pl.pallas_call(
        _topk_kernel,
        grid=(t // tr,),
        in_specs=[pl.BlockSpec((ne, tr), lambda i: (0, i)),
                  pl.BlockSpec((ne, 1), lambda i: (0, 0)),
                  pl.BlockSpec((tr, tr), lambda i: (0, 0)),
                  pl.BlockSpec((ne, ne), lambda i: (0, 0))],
        out_specs=[row, row, pl.BlockSpec((ne, 128), lambda i: (0, 0))],
        out_shape=[out(F32), out(F32), jax.ShapeDtypeStruct((ne, 128), I32)],
        compiler_params=_cparams(("arbitrary",)),
    )(logits_t, router_b.reshape(ne, 1), tri, ltri)


def _slot_plan(counts, n_tiles, n_blocks):
    n = counts[:, :n_tiles]
    nch = (n + (CHUNK - 1)) // CHUNK
    rows = jnp.sum(nch, axis=1) * CHUNK
    region = ((rows + EXPERT_ROWS - 1) // EXPERT_ROWS) * EXPERT_ROWS
    region_end = jnp.cumsum(region)
    region_start = region_end - region
    run_chunk = region_start[:, None] // CHUNK + jnp.cumsum(nch, axis=1) - nch
    local_end = jnp.cumsum(nch, axis=0)
    local_chunk = local_end - nch
    c = jnp.arange(LOCAL_ROWS // CHUNK, dtype=I32)
    expert_of_c = jnp.sum((local_end[:, :, None] <= c[None, None, :]).astype(I32), axis=0)
    owner = expert_of_c[None] == jnp.arange(N_EXPERTS, dtype=I32)[:, None, None]
    slot_chunk = jnp.sum(jnp.where(owner, (run_chunk - local_chunk)[:, :, None], 0), axis=0) + c
    block_row = jnp.arange(n_blocks, dtype=I32) * EXPERT_ROWS
    group_e = jnp.minimum(jnp.sum((region_end[None, :] <= block_row[:, None]).astype(I32), axis=1),
                          N_EXPERTS - 1)
    n_used = region_end[-1] // EXPERT_ROWS
    nonempty = region > 0
    run_of_expert = jnp.cumsum(nonempty.astype(I32)) - 1
    experts = jnp.arange(N_EXPERTS, dtype=I32)
    run_expert = jnp.sum(jnp.where((run_of_expert[None, :] == experts[:, None]) & nonempty[None, :],
                                   experts[None, :], 0), axis=1)
    blocks = jnp.arange(n_blocks, dtype=I32)
    first = ((blocks == 0) | (group_e != jnp.roll(group_e, 1))) & (blocks < n_used)
    per_tile = lambda a: (a * CHUNK).T.astype(I32)
    return dict(
        run_begin_rows=per_tile(local_chunk)[:, None, :], run_end_rows=per_tile(local_end)[:, None, :],
        run_begin_cols=per_tile(local_chunk)[:, :, None], run_end_cols=per_tile(local_end)[:, :, None],
        block_run=(jnp.cumsum(first.astype(I32)) - 1).astype(I32), block_first=first.astype(I32),
        run_expert=run_expert.astype(I32), n_runs=jnp.sum(nonempty.astype(I32)).reshape(1),
        slot_chunk=slot_chunk.reshape(-1).astype(I32),
        tile_chunks=jnp.sum(nch, axis=0).astype(I32),
        tail_chunk=((region_start + rows) // CHUNK).astype(I32),
        tail_chunks=((region - rows) // CHUNK).astype(I32),
        group_e=group_e, n_used=(region_end[-1:] // EXPERT_ROWS).astype(I32))


def _chunk_rows(ref, chunk_index):
    start = chunk_index * CHUNK
    if not isinstance(start, int):
        start = pl.multiple_of(start, CHUNK)
    return ref.at[pl.ds(start, CHUNK), :]


def _pos_digits(pos):
    hi = jnp.floor(pos * (1.0 / POS_RADIX))
    return hi.astype(BF16), (pos - POS_RADIX * hi).astype(BF16)


def _for_chunks_two_queues(count, start):
    def pair(p, carry):
        start(2 * p, 0)
        start(2 * p + 1, 1)
        return carry

    lax.fori_loop(0, lax.shift_right_logical(count, 1), pair, 0)

    @pl.when((count & 1) == 1)
    def _():
        start(count - 1, 0)


def _wait_chunks(count, src_ref, dst_ref, sem):
    group_rows = pl.ds(0, WAIT_GROUP * CHUNK)

    def wait_group(c, carry):
        pltpu.make_async_copy(src_ref.at[group_rows, :], dst_ref.at[group_rows, :], sem).wait()
        return carry

    def wait_one(c, carry):
        pltpu.make_async_copy(_chunk_rows(src_ref, 0), _chunk_rows(dst_ref, 0), sem).wait()
        return carry

    lax.fori_loop(0, lax.shift_right_logical(count, WAIT_GROUP.bit_length() - 1), wait_group, 0)
    lax.fori_loop(0, count & (WAIT_GROUP - 1), wait_one, 0)


def _dispatch_kernel(slot_ref, tile_ref, tail_ref, tailn_ref, used_ref,
                     x_ref, sh_ref, sc_ref, pos_ref, begin_ref, end_ref,
                     xg_hbm, stage_ref, zero_ref, sems, sem_blk):
    tr = x_ref.shape[0]
    i = pl.program_id(0)
    n_blocks = xg_hbm.shape[0] // EXPERT_ROWS
    chunks_per_tile = LOCAL_ROWS // CHUNK

    def drain(count, sem):
        _wait_chunks(count, stage_ref.at[0], xg_hbm, sem)

    @pl.when(i == 0)
    def _():
        zero_ref[...] = jnp.zeros_like(zero_ref)

        def block_copy(j):
            rows = pl.ds(pl.multiple_of(j * EXPERT_ROWS, EXPERT_ROWS), EXPERT_ROWS)
            return pltpu.make_async_copy(zero_ref, xg_hbm.at[rows, :], sem_blk)

        def start_block(j, carry):
            block_copy(j).start()
            return carry

        def wait_block(j, carry):
            block_copy(j).wait()
            return carry

        lax.fori_loop(used_ref[0], n_blocks, start_block, 0)

        def per_expert(e, total):
            def per_chunk(c, carry):
                pltpu.make_async_copy(_chunk_rows(zero_ref, 0),
                                      _chunk_rows(xg_hbm, tail_ref[e] + c), sems.at[0]).start()
                return carry
            lax.fori_loop(0, tailn_ref[e], per_chunk, 0)
            return total + tailn_ref[e]

        drain(lax.fori_loop(0, N_EXPERTS, per_expert, 0), sems.at[0])
        lax.fori_loop(used_ref[0], n_blocks, wait_block, 0)

    hf = (x_ref[...] * (1.0 + sc_ref[...]) + sh_ref[...]).astype(BF16)
    row_e = lax.broadcasted_iota(I32, (LOCAL_ROWS, N_EXPERTS), 0)
    owner = jnp.where((row_e >= begin_ref[...]) & (row_e < end_ref[...]), 1.0, 0.0).astype(BF16)
    hi_digit, lo_digit = _pos_digits(pos_ref[...])
    slot_pos = (POS_RADIX * jnp.dot(owner, hi_digit, preferred_element_type=F32)
                + jnp.dot(owner, lo_digit, preferred_element_type=F32))
    row_t = lax.broadcasted_iota(I32, (LOCAL_ROWS, tr), 0).astype(F32)
    perm = jnp.where(slot_pos == row_t, 1.0, 0.0).astype(BF16)
    rows = jnp.dot(perm, hf, preferred_element_type=F32)
    half = rows.shape[1] // 2
    buf = stage_ref.at[i % 2]
    buf[...] = (lax.bitcast_convert_type(rows[:, :half], U32)
                | (lax.bitcast_convert_type(rows[:, half:], U32) >> 16))

    def send(c, priority):
        pltpu.make_async_copy(_chunk_rows(buf, c),
                              _chunk_rows(xg_hbm, slot_ref[i * chunks_per_tile + c]),
                              sems.at[i % 2]).start(priority=priority)

    _for_chunks_two_queues(tile_ref[i], send)

    @pl.when(i > 0)
    def _():
        drain(tile_ref[jnp.maximum(i - 1, 0)], sems.at[(i + 1) % 2])

    @pl.when(i == pl.num_programs(0) - 1)
    def _():
        drain(tile_ref[i], sems.at[i % 2])


def _dispatch(plan, x2, mod3, pos, n_slots, seq):
    t, d = x2.shape
    tr = TOK_TILE
    tpb = seq // tr
    return pl.pallas_call(
        _dispatch_kernel,
        grid_spec=pltpu.PrefetchScalarGridSpec(
            num_scalar_prefetch=5,
            grid=(t // tr,),
            in_specs=[pl.BlockSpec((tr, d), lambda i, *_: (i, 0)),
                      _mod_spec(3, d, tpb), _mod_spec(4, d, tpb),
                      pl.BlockSpec((N_EXPERTS, tr), lambda i, *_: (0, i)),
                      pl.BlockSpec((None, 1, N_EXPERTS), lambda i, *_: (i, 0, 0)),
                      pl.BlockSpec((None, 1, N_EXPERTS), lambda i, *_: (i, 0, 0))],
            out_specs=pl.BlockSpec(memory_space=pl.ANY),
            scratch_shapes=[pltpu.VMEM((2, LOCAL_ROWS, d // 2), U32),
                            pltpu.VMEM((EXPERT_ROWS, d // 2), U32),
                            pltpu.SemaphoreType.DMA((2,)), pltpu.SemaphoreType.DMA],
        ),
        out_shape=jax.ShapeDtypeStruct((n_slots, d // 2), U32),
        compiler_params=_cparams(("arbitrary",)),
    )(plan["slot_chunk"], plan["tile_chunks"], plan["tail_chunk"], plan["tail_chunks"],
      plan["n_used"], x2, mod3, mod3, pos, plan["run_begin_rows"], plan["run_end_rows"])


def _expert_kernel(run_ref, first_ref, rexp_ref, nrun_ref, nu_ref,
                   xg_hbm, wgu_hbm, wdn_hbm, yg_hbm,
                   xbuf, ybuf, wgu_f32, wdn_f32, wgu_bf, wdn_bf, x_sem, y_sem, w_sem, z_sem,
                   *, layer):
    n_used = nu_ref[0]
    n_blocks = yg_hbm.shape[0] // EXPERT_ROWS
    depth, _, half = xbuf.shape
    f = wdn_bf.shape[0]

    def rows(j):
        return pl.ds(pl.multiple_of(j * EXPERT_ROWS, EXPERT_ROWS), EXPERT_ROWS)

    def x_copy(j):
        return pltpu.make_async_copy(xg_hbm.at[rows(j), :], xbuf.at[j % depth],
                                     x_sem.at[j % depth])

    def y_copy(j):
        return pltpu.make_async_copy(ybuf.at[j % depth], yg_hbm.at[rows(j), :],
                                     y_sem.at[j % depth])

    def w_copies(r):
        e = rexp_ref[r]
        return (pltpu.make_async_copy(wgu_hbm.at[layer, e], wgu_f32.at[r % 2], w_sem.at[r % 2]),
                pltpu.make_async_copy(wdn_hbm.at[layer, e], wdn_f32.at[r % 2], w_sem.at[r % 2]))

    def zero_copy(j):
        return pltpu.make_async_copy(ybuf.at[0], yg_hbm.at[rows(j), :], z_sem)

    ybuf[0] = jnp.zeros(ybuf.shape[1:], U32)

    def zero_start(j, carry):
        zero_copy(j).start()
        return carry

    def zero_wait(j, carry):
        zero_copy(j).wait()
        return carry

    lax.fori_loop(n_used, n_blocks, zero_start, 0)
    lax.fori_loop(n_used, n_blocks, zero_wait, 0)

    for cp in w_copies(0):
        cp.start()
    for a in range(depth - 1):
        @pl.when(a < n_used)
        def _():
            x_copy(a).start()

    def block(j, carry):
        r = run_ref[j]

        @pl.when(first_ref[j] == 1)
        def _():
            for cp in w_copies(r):
                cp.wait()

            @pl.when(r + 1 < nrun_ref[0])
            def _():
                for cp in w_copies(r + 1):
                    cp.start()

            wgu_bf[...] = wgu_f32[r % 2].astype(BF16)
            wdn_bf[...] = wdn_f32[r % 2].astype(BF16)

        x_copy(j).wait()

        @pl.when(j + (depth - 1) < n_used)
        def _():
            x_copy(j + (depth - 1)).start()

        @pl.when(j >= depth)
        def _():
            y_copy(j - depth).wait()

        hi, lo = _unpack_bf16_pairs(xbuf[j % depth])
        gu = (jnp.dot(hi.astype(BF16), wgu_bf[0:half, :], preferred_element_type=F32)
              + jnp.dot(lo.astype(BF16), wgu_bf[half:, :], preferred_element_type=F32))
        act = (_silu(gu[:, :f]) * gu[:, f:]).astype(BF16)
        ybuf[j % depth] = _pack_bf16_pairs(jnp.dot(act, wdn_bf[...], preferred_element_type=F32))
        y_copy(j).start()
        return carry

    lax.fori_loop(0, n_used, block, 0)

    for a in range(depth, 0, -1):
        @pl.when(n_used >= a)
        def _():
            y_copy(n_used - a).wait()


def _expert_gemm(plan, xg, w_gu, w_down, layer):
    n_slots, half = xg.shape
    _, ne, d, f2 = w_gu.shape
    f = w_down.shape[2]
    hbm = pl.BlockSpec(memory_space=pl.ANY)
    return pl.pallas_call(
        functools.partial(_expert_kernel, layer=layer),
        grid_spec=pltpu.PrefetchScalarGridSpec(
            num_scalar_prefetch=5,
            grid=(1,),
            in_specs=[hbm, hbm, hbm],
            out_specs=hbm,
            scratch_shapes=[pltpu.VMEM((GEMM_DEPTH, EXPERT_ROWS, half), U32),
                            pltpu.VMEM((GEMM_DEPTH, EXPERT_ROWS, half), U32),
                            pltpu.VMEM((2, d, f2), F32), pltpu.VMEM((2, f, d), F32),
                            pltpu.VMEM((d, f2), BF16), pltpu.VMEM((f, d), BF16),
                            pltpu.SemaphoreType.DMA((GEMM_DEPTH,)),
                            pltpu.SemaphoreType.DMA((GEMM_DEPTH,)),
                            pltpu.SemaphoreType.DMA((2,)), pltpu.SemaphoreType.DMA],
        ),
        out_shape=jax.ShapeDtypeStruct((n_slots, half), U32),
        compiler_params=_cparams(("arbitrary",)),
    )(plan["block_run"], plan["block_first"], plan["run_expert"], plan["n_runs"], plan["n_used"],
      xg, w_gu, w_down)


def _combine_kernel(slot_ref, tile_ref,
                    x_ref, sh_ref, sc_ref, g_ref, pos_ref, wt_ref, begin_ref, end_ref,
                    wsgu_ref, wsdn_ref, lng_ref, lnb_ref, yg_hbm, o_ref, stage_ref, sems):
    tr, d = x_ref.shape
    i = pl.program_id(0)
    n_tiles = pl.num_programs(0)
    chunks_per_tile = LOCAL_ROWS // CHUNK

    def fetch(tile, buffer):
        def start(c, priority):
            pltpu.make_async_copy(_chunk_rows(yg_hbm, slot_ref[tile * chunks_per_tile + c]),
                                  _chunk_rows(stage_ref.at[buffer], c),
                                  sems.at[buffer]).start(priority=priority)
        _for_chunks_two_queues(tile_ref[tile], start)

    @pl.when(i == 0)
    def _():
        stage_ref[...] = jnp.zeros_like(stage_ref)
        fetch(0, 0)

    nxt = jnp.minimum(i + 1, n_tiles - 1)
    for buffer in range(2):
        @pl.when((i + 1 < n_tiles) & (nxt % 2 == buffer))
        def _():
            fetch(nxt, buffer)

    x = x_ref[...]
    hf = (x * (1.0 + sc_ref[...]) + sh_ref[...]).astype(BF16)
    f = wsdn_ref.shape[0]
    su = jnp.dot(hf, wsgu_ref[...], preferred_element_type=F32)
    act = (_silu(su[:, :f]) * su[:, f:]).astype(BF16)
    shared = jnp.dot(act, wsdn_ref[...], preferred_element_type=F32)

    col_e = lax.broadcasted_iota(I32, (N_EXPERTS, LOCAL_ROWS), 1)
    owner = jnp.where((col_e >= begin_ref[...]) & (col_e < end_ref[...]), 1.0, 0.0).astype(BF16)
    hi_digit, lo_digit = _pos_digits(pos_ref[...])
    slot_pos = (POS_RADIX * jnp.dot(hi_digit, owner, preferred_element_type=F32)
                + jnp.dot(lo_digit, owner, preferred_element_type=F32))
    holds = slot_pos == lax.broadcasted_iota(I32, (tr, LOCAL_ROWS), 1).astype(F32)
    w_tok = wt_ref[...]
    w_tok_hi = w_tok.astype(BF16)
    w_tok_lo = (w_tok - w_tok_hi.astype(F32)).astype(BF16)
    part = lambda w: jnp.where(holds, jnp.dot(w, owner, preferred_element_type=F32),
                               0.0).astype(BF16)
    w_both = jnp.concatenate([part(w_tok_hi), part(w_tok_lo)], axis=0)

    buf = stage_ref.at[i % 2]
    _wait_chunks(tile_ref[i], yg_hbm, buf, sems.at[i % 2])

    y_hi, y_lo = _unpack_bf16_pairs(buf[...])
    y = jnp.concatenate([y_hi.astype(BF16), y_lo.astype(BF16)], axis=1)
    both = jnp.dot(w_both, y, preferred_element_type=F32)
    routed = both[:tr, :] + both[tr:, :]
    v = ALPHA * x + (1.0 + g_ref[...]) * (routed + shared)
    o_ref[...] = _layer_norm(v, lng_ref[...], lnb_ref[...])


def _combine(plan, x2, mod3, pos_tok, w_tok, w_sh_gu, w_sh_down, ln_g, ln_b, yg, seq):
    t, d = x2.shape
    tr = TOK_TILE
    tpb = seq // tr
    full = lambda shape: pl.BlockSpec(shape, lambda i, *_: (0,) * len(shape))
    return pl.pallas_call(
        _combine_kernel,
        grid_spec=pltpu.PrefetchScalarGridSpec(
            num_scalar_prefetch=2,
            grid=(t // tr,),
            in_specs=[pl.BlockSpec((tr, d), lambda i, *_: (i, 0)),
                      _mod_spec(3, d, tpb), _mod_spec(4, d, tpb), _mod_spec(5, d, tpb),
                      pl.BlockSpec((tr, N_EXPERTS), lambda i, *_: (i, 0)),
                      pl.BlockSpec((tr, N_EXPERTS), lambda i, *_: (i, 0)),
                      pl.BlockSpec((None, N_EXPERTS, 1), lambda i, *_: (i, 0, 0)),
                      pl.BlockSpec((None, N_EXPERTS, 1), lambda i, *_: (i, 0, 0)),
                      full(w_sh_gu.shape), full(w_sh_down.shape), full((1, d)), full((1, d)),
                      pl.BlockSpec(memory_space=pl.ANY)],
            out_specs=pl.BlockSpec((tr, d), lambda i, *_: (i, 0)),
            scratch_shapes=[pltpu.VMEM((2, LOCAL_ROWS, d // 2), U32),
                            pltpu.SemaphoreType.DMA((2,))],
        ),
        out_shape=jax.ShapeDtypeStruct((t, d), F32),
        compiler_params=_cparams(("arbitrary",)),
    )(plan["slot_chunk"], plan["tile_chunks"],
      x2, mod3, mod3, mod3, pos_tok, w_tok, plan["run_begin_cols"], plan["run_end_cols"],
      w_sh_gu, w_sh_down, ln_g.reshape(1, d), ln_b.reshape(1, d), yg)


def _moe_layer(x2, mod3, w_router, router_b, w_gu, w_down, layer, w_sh_gu, w_sh_down,
               ln_g, ln_b, seq):
    t, d = x2.shape
    n_tiles = t // TOK_TILE
    logits = _router(x2, mod3, w_router, seq)
    w_top, pos, counts = _topk_route(logits.T, router_b)
    bound = t * TOP_K + n_tiles * N_EXPERTS * (CHUNK - 1) + N_EXPERTS * (EXPERT_ROWS - 1)
    n_blocks = -(-bound // EXPERT_ROWS)
    plan = _slot_plan(counts, n_tiles, n_blocks)
    xg = _dispatch(plan, x2, mod3, pos, n_blocks * EXPERT_ROWS, seq)
    yg = _expert_gemm(plan, xg, w_gu, w_down, layer)
    return _combine(plan, x2, mod3, pos.T, w_top.T, w_sh_gu, w_sh_down, ln_g, ln_b, yg, seq)


def kernel(x, c, ada_w, ada_b, pool_w_in, pool_w_grp, pool_scale, pool_w_out, attn_w_in, attn_w_out, ln1_g, ln1_b, router_w, router_b, exp_w_gu, exp_w_down, sh_w_gu, sh_w_down, ln2_g, ln2_b):
    batch, seq, d = x.shape
    depth = ada_w.shape[0]
    t = batch * seq
    mod = _modulation(c, ada_w, ada_b).reshape(depth, batch, 1, 6 * d)
    perms = jnp.stack([_perm_matrix(dil) for _, dil in ATTN_PATTERNS[1:]])
    perms_t = jnp.swapaxes(perms, 1, 2)
    expand = (jnp.arange(LSE_LANES)[:, None] == (jnp.arange(d)[None, :] // HEAD_DIM)).astype(BF16)
    part = (jnp.arange(attn_w_in.shape[2]) // d) % 3
    q_scale = jnp.where(part == 0, HEAD_DIM ** -0.5, 1.0).astype(F32)
    x2 = x.reshape(t, d)
    for i in range(depth):
        mod3 = mod[i]
        j = i // 2
        if i % 2 == 0:
            x2 = _pool_layer(x2, mod3, pool_w_in[j].astype(BF16), pool_w_grp[j].astype(BF16),
                             pool_scale[j], pool_w_out[j].astype(BF16), ln1_g[i], ln1_b[i], seq)
        else:
            qkv = _qkv_proj(x2, mod3, perms, (attn_w_in[j] * q_scale).astype(BF16), seq)
            res = [_attention_group(qkv, g, batch, seq) for g in range(len(ATTN_PATTERNS))]
            x2 = _attn_out(x2, mod3, [r[0] for r in res], [r[1] for r in res], perms_t, expand,
                           attn_w_out[j].astype(BF16), ln1_g[i], ln1_b[i], seq)
        x2 = _moe_layer(x2, mod3, router_w[i], router_b[i], exp_w_gu, exp_w_down, i,
                        sh_w_gu[i].astype(BF16), sh_w_down[i].astype(BF16),
                        ln2_g[i], ln2_b[i], seq)
    return x2.reshape(batch, seq, d)
```

```python
import functools
import math

import jax
import jax.numpy as jnp
from jax import lax
from jax.experimental import pallas as pl
from jax.experimental.pallas import tpu as pltpu

F32 = jnp.float32
BF16 = jnp.bfloat16
U32 = jnp.uint32
I32 = jnp.int32

POOL_WINDOWS = (2, 4, 8, 16)
ATTN_PATTERNS = ((128, 1), (512, 4), (2048, 16))
HEAD_DIM = 64
N_HEADS = 16
Q_BLOCK = 128
N_EXPERTS = 64
TOP_K = 8
N_EXPERT_GROUPS = 8
TOPK_GROUPS = 4
ROUTED_SCALE = 2.5
EXPERT_ROWS = 512
DEPTH = 4
ALPHA = (2 * DEPTH) ** 0.25
LN_EPS = 1e-5

PERM_TILE = 256
POOL_HALO = 16
TOK_TILE = 256
CHUNK = 8
LOCAL_ROWS = -(-(TOK_TILE * TOP_K + N_EXPERTS * (CHUNK - 1)) // 256) * 256
NO_SLOT = -64.0
POS_RADIX = 64.0
WAIT_GROUP = 32
GEMM_SUB_ROWS = 256
GEMM_DEPTH = 4
LSE_LANES = 128
VMEM_LIMIT = 56 * 1024 * 1024
NEG_BIG = -1e30


def _cparams(sem):
    return pltpu.CompilerParams(dimension_semantics=sem, vmem_limit_bytes=VMEM_LIMIT)


def _layer_norm(v, g, b):
    mu = jnp.mean(v, axis=-1, keepdims=True)
    c = v - mu
    var = jnp.mean(c * c, axis=-1, keepdims=True)
    return c * lax.rsqrt(var + LN_EPS) * g + b


def _silu(v):
    return v * (1.0 / (1.0 + jnp.exp(-v)))


def _pack_bf16_pairs(v):
    n = v.shape[1] // 2
    hi = lax.bitcast_convert_type(v[:, :n].astype(BF16).astype(F32), U32)
    lo = lax.bitcast_convert_type(v[:, n:].astype(BF16).astype(F32), U32)
    return hi | (lo >> 16)


def _unpack_bf16_pairs(p):
    hi = lax.bitcast_convert_type(p & jnp.uint32(0xFFFF0000), F32)
    lo = lax.bitcast_convert_type(p << 16, F32)
    return hi, lo


def _mod_kernel(c_ref, w_ref, b_ref, o_ref):
    cs = _silu(c_ref[...])
    o_ref[...] = jnp.dot(cs, w_ref[...], preferred_element_type=F32) + b_ref[...]


def _modulation(c, ada_w, ada_b):
    depth, d, n6 = ada_w.shape
    b = c.shape[0]
    tn = 1536
    return pl.pallas_call(
        _mod_kernel,
        grid=(depth, n6 // tn),
        in_specs=[
            pl.BlockSpec((b, d), lambda i, n: (0, 0)),
            pl.BlockSpec((None, d, tn), lambda i, n: (i, 0, n)),
            pl.BlockSpec((None, 1, tn), lambda i, n: (i, 0, n)),
        ],
        out_specs=pl.BlockSpec((None, b, tn), lambda i, n: (i, 0, n)),
        out_shape=jax.ShapeDtypeStruct((depth, b, n6), F32),
        compiler_params=_cparams(("arbitrary", "arbitrary")),
    )(c, ada_w, ada_b.reshape(depth, 1, n6))


def _mod_spec(chunk, d, tiles_per_batch):
    return pl.BlockSpec((None, 1, d), lambda *idx: (idx[0] // tiles_per_batch, 0, chunk))


def _pool_kernel(x_ref, sh_ref, sc_ref, g_ref, win_ref, wgrp_ref, cs_ref, wout_ref,
                 lng_ref, lnb_ref, o_ref, ext_ref, *, tiles_per_batch):
    tm, d = x_ref.shape
    s_idx = pl.program_id(0) % tiles_per_batch
    x = x_ref[...]
    h = (x * (1.0 + sc_ref[...]) + sh_ref[...]).astype(BF16)
    u = jnp.dot(h, win_ref[...], preferred_element_type=F32)

    @pl.when(s_idx == 0)
    def _():
        ext_ref[0:POOL_HALO, :] = jnp.zeros((POOL_HALO, d), F32)

    @pl.when(s_idx != 0)
    def _():
        ext_ref[0:POOL_HALO, :] = ext_ref[tm:tm + POOL_HALO, :]

    ext_ref[POOL_HALO:POOL_HALO + tm, :] = u

    pos = s_idx * tm + lax.broadcasted_iota(I32, (tm, 1), 0) + 1
    gc = d // len(POOL_WINDOWS)
    ys = []
    for g, w in enumerate(POOL_WINDOWS):
        cols = slice(g * gc, (g + 1) * gc)
        acc = u[:, cols]
        for j in range(1, w):
            acc = acc + ext_ref[POOL_HALO - j:POOL_HALO - j + tm, cols]
        cnt = jnp.minimum(pos, w).astype(F32)
        z = (acc / cnt - u[:, cols]).astype(BF16)
        ys.append(jnp.dot(z, wgrp_ref[g], preferred_element_type=F32))
    y = (jnp.concatenate(ys, axis=1) * cs_ref[...]).astype(BF16)
    out = jnp.dot(y, wout_ref[...], preferred_element_type=F32)
    v = ALPHA * x + (1.0 + g_ref[...]) * out
    o_ref[...] = _layer_norm(v, lng_ref[...], lnb_ref[...])


def _pool_layer(x2, mod3, w_in, w_grp, ch_scale, w_out, ln_g, ln_b, seq):
    t, d = x2.shape
    tm = 512
    tpb = seq // tm
    full = lambda shape: pl.BlockSpec(shape, lambda i: (0,) * len(shape))
    return pl.pallas_call(
        functools.partial(_pool_kernel, tiles_per_batch=tpb),
        grid=(t // tm,),
        in_specs=[
            pl.BlockSpec((tm, d), lambda i: (i, 0)),
            _mod_spec(0, d, tpb), _mod_spec(1, d, tpb), _mod_spec(2, d, tpb),
            full((d, d)), full(w_grp.shape), full((1, d)), full((d, d)),
            full((1, d)), full((1, d)),
        ],
        out_specs=pl.BlockSpec((tm, d), lambda i: (i, 0)),
        out_shape=jax.ShapeDtypeStruct((t, d), F32),
        scratch_shapes=[pltpu.VMEM((tm + POOL_HALO, d), F32)],
        compiler_params=_cparams(("arbitrary",)),
    )(x2, mod3, mod3, mod3, w_in, w_grp, ch_scale.reshape(1, d), w_out,
      ln_g.reshape(1, d), ln_b.reshape(1, d))


def _perm_matrix(dil):
    p = jnp.arange(PERM_TILE)
    chunk = PERM_TILE // dil
    src = (p % chunk) * dil + p // chunk
    return (src[:, None] == jnp.arange(PERM_TILE)[None, :]).astype(BF16)


def _qkv_kernel(x_ref, sh_ref, sc_ref, p_ref, w_ref, o_ref, h_ref):
    tm = x_ref.shape[0]
    g = pl.program_id(1)

    @pl.when(g == 0)
    def _():
        h = (x_ref[...] * (1.0 + sc_ref[...]) + sh_ref[...]).astype(BF16)
        h_ref[0] = h
        for gi in range(1, len(ATTN_PATTERNS)):
            for s in range(tm // PERM_TILE):
                rows = slice(s * PERM_TILE, (s + 1) * PERM_TILE)
                h_ref[gi, rows, :] = jnp.dot(
                    p_ref[gi - 1], h[rows, :], preferred_element_type=F32).astype(BF16)

    o_ref[...] = jnp.dot(h_ref[g], w_ref[...], preferred_element_type=F32).astype(BF16)


def _qkv_proj(x2, mod3, perms, w_in, seq):
    t, d = x2.shape
    ng = len(ATTN_PATTERNS)
    tm = 1024
    tpb = seq // tm
    return pl.pallas_call(
        _qkv_kernel,
        grid=(t // tm, ng),
        in_specs=[
            pl.BlockSpec((tm, d), lambda m, g: (m, 0)),
            _mod_spec(0, d, tpb), _mod_spec(1, d, tpb),
            pl.BlockSpec(perms.shape, lambda m, g: (0, 0, 0)),
            pl.BlockSpec((d, 3 * d), lambda m, g: (0, g)),
        ],
        out_specs=pl.BlockSpec((tm, 3 * d), lambda m, g: (m, g)),
        out_shape=jax.ShapeDtypeStruct((t, ng * 3 * d), BF16),
        scratch_shapes=[pltpu.VMEM((ng, tm, d), BF16)],
        compiler_params=_cparams(("arbitrary", "arbitrary")),
    )(x2, mod3, mod3, perms, w_in)


def _attn_kernel(q_ref, kp_ref, kc_ref, vp_ref, vc_ref, o_ref, lse_ref, *, group, dil):
    bq = Q_BLOCK
    d = N_HEADS * HEAD_DIM
    j = pl.program_id(2)
    q = q_ref[...].reshape(bq, d)
    kp = kp_ref[...].reshape(bq, d)
    kc = kc_ref[...].reshape(bq, d)
    vp = vp_ref[...].reshape(bq, d)
    vc = vc_ref[...].reshape(bq, d)

    qi = lax.broadcasted_iota(I32, (bq, 2 * bq), 0)
    kj = lax.broadcasted_iota(I32, (bq, 2 * bq), 1)
    dist = qi + bq - kj
    steps = ATTN_PATTERNS[group][0] // dil
    valid = (dist >= 0) & (dist <= steps) & ((kj >= bq) | (j > 0))
    neg_dist = jnp.where(valid, (dist * -dil).astype(F32), NEG_BIG)
    lane = lax.broadcasted_iota(I32, (bq, LSE_LANES), 1)
    lse_tile = jnp.zeros((bq, LSE_LANES), F32)
    n_tot = len(ATTN_PATTERNS) * N_HEADS
    pair = 2 * HEAD_DIM
    first_half = lax.broadcasted_iota(I32, (bq, pair), 1) < HEAD_DIM
    outs = []
    for hp in range(N_HEADS // 2):
        cols = slice(hp * pair, (hp + 1) * pair)
        q2 = q[:, cols]
        k2 = jnp.concatenate([kp[:, cols], kc[:, cols]], axis=0)
        v2 = jnp.concatenate([vp[:, cols], vc[:, cols]], axis=0)
        halves = []
        for sub in range(2):
            h = 2 * hp + sub
            slope = 2.0 ** (-8.0 * (group * N_HEADS + h + 1) / n_tot)
            mine = first_half if sub == 0 else jnp.logical_not(first_half)
            qh = jnp.where(mine, q2, jnp.zeros_like(q2))
            s = lax.dot_general(qh, k2, (((1,), (1,)), ((), ())), preferred_element_type=F32)
            s = s + slope * neg_dist
            m = jnp.max(s, axis=-1, keepdims=True)
            p = jnp.exp(s - m)
            den = jnp.sum(p, axis=-1, keepdims=True)
            halves.append(jnp.dot(p.astype(BF16), v2, preferred_element_type=F32) / den)
            lse_tile = jnp.where(lane == h, m + jnp.log(den), lse_tile)
        outs.append(jnp.where(first_half, halves[0], halves[1]))
    o_ref[...] = jnp.concatenate(outs, axis=1).astype(BF16).reshape(o_ref.shape)
    lse_ref[...] = lse_tile.reshape(lse_ref.shape)


def _attention_group(qkv, group, batch, seq):
    dil = ATTN_PATTERNS[group][1]
    d = N_HEADS * HEAD_DIM
    t = qkv.shape[0]
    sub = seq // dil
    nb = sub // Q_BLOCK
    rows = Q_BLOCK if dil == 1 else PERM_TILE // dil
    chunks = Q_BLOCK // rows
    u = seq // (rows * dil)
    view = lambda a, c: a.reshape(batch, u, dil, rows, c)
    blk = lambda c: (None, chunks, None, rows, c)
    col0 = group * 3
    q_spec = pl.BlockSpec(blk(d), lambda b, r, j: (b, j, r, 0, col0))
    kc_spec = pl.BlockSpec(blk(d), lambda b, r, j: (b, j, r, 0, col0 + 1))
    kp_spec = pl.BlockSpec(blk(d), lambda b, r, j: (b, jnp.maximum(j - 1, 0), r, 0, col0 + 1))
    vc_spec = pl.BlockSpec(blk(d), lambda b, r, j: (b, j, r, 0, col0 + 2))
    vp_spec = pl.BlockSpec(blk(d), lambda b, r, j: (b, jnp.maximum(j - 1, 0), r, 0, col0 + 2))
    qkv5 = view(qkv, qkv.shape[1])
    o, lse = pl.pallas_call(
        functools.partial(_attn_kernel, group=group, dil=dil),
        grid=(batch, dil, nb),
        in_specs=[q_spec, kp_spec, kc_spec, vp_spec, vc_spec],
        out_specs=[
            pl.BlockSpec(blk(d), lambda b, r, j: (b, j, r, 0, 0)),
            pl.BlockSpec(blk(LSE_LANES), lambda b, r, j: (b, j, r, 0, 0)),
        ],
        out_shape=[
            jax.ShapeDtypeStruct((batch, u, dil, rows, d), BF16),
            jax.ShapeDtypeStruct((batch, u, dil, rows, LSE_LANES), F32),
        ],
        compiler_params=_cparams(("arbitrary", "arbitrary", "arbitrary")),
    )(qkv5, qkv5, qkv5, qkv5, qkv5)
    return o.reshape(t, d), lse.reshape(t, LSE_LANES)


def _split3(v):
    a = v.astype(BF16)
    r = v - a.astype(F32)
    b = r.astype(BF16)
    c = (r - b.astype(F32)).astype(BF16)
    return a, b, c


def _attn_out_kernel(x_ref, g_ref, o0_ref, o1_ref, o2_ref, l0_ref, l1_ref, l2_ref,
                     pt_ref, e_ref, wout_ref, lng_ref, lnb_ref, out_ref):
    tm, d = x_ref.shape
    o_refs = (o0_ref, o1_ref, o2_ref)
    l_refs = (l0_ref, l1_ref, l2_ref)
    n_sub = tm // PERM_TILE

    def unperm(gi, val_bf16):
        if gi == 0:
            return val_bf16.astype(F32)
        parts = [jnp.dot(pt_ref[gi - 1], val_bf16[s * PERM_TILE:(s + 1) * PERM_TILE, :],
                         preferred_element_type=F32) for s in range(n_sub)]
        return jnp.concatenate(parts, axis=0)

    lses = []
    for gi in range(3):
        l = l_refs[gi][...]
        if gi == 0:
            lses.append(l)
        else:
            a, b, c = _split3(l)
            lses.append(unperm(gi, a) + unperm(gi, b) + unperm(gi, c))
    mx = jnp.maximum(jnp.maximum(lses[0], lses[1]), lses[2])
    es = [jnp.exp(l - mx) for l in lses]
    tot = es[0] + es[1] + es[2]
    mixed = jnp.zeros((tm, d), F32)
    for gi in range(3):
        w = es[gi] / tot
        a, b, c = _split3(w)
        wide = (jnp.dot(a, e_ref[...], preferred_element_type=F32)
                + jnp.dot(b, e_ref[...], preferred_element_type=F32)
                + jnp.dot(c, e_ref[...], preferred_element_type=F32))
        mixed = mixed + wide * unperm(gi, o_refs[gi][...])
    y = jnp.dot(mixed.astype(BF16), wout_ref[...], preferred_element_type=F32)
    v = ALPHA * x_ref[...] + (1.0 + g_ref[...]) * y
    out_ref[...] = _layer_norm(v, lng_ref[...], lnb_ref[...])


def _attn_out(x2, mod3, os_, lses, perms_t, expand, w_out, ln_g, ln_b, seq):
    t, d = x2.shape
    tm = 512
    tpb = seq // tm
    row = lambda c: pl.BlockSpec((tm, c), lambda i: (i, 0))
    full = lambda shape: pl.BlockSpec(shape, lambda i: (0,) * len(shape))
    return pl.pallas_call(
        _attn_out_kernel,
        grid=(t // tm,),
        in_specs=[row(d), _mod_spec(2, d, tpb), row(d), row(d), row(d),
                  row(LSE_LANES), row(LSE_LANES), row(LSE_LANES),
                  full(perms_t.shape), full(expand.shape), full((d, d)),
                  full((1, d)), full((1, d))],
        out_specs=row(d),
        out_shape=jax.ShapeDtypeStruct((t, d), F32),
        compiler_params=_cparams(("arbitrary",)),
    )(x2, mod3, *os_, *lses, perms_t, expand, w_out, ln_g.reshape(1, d), ln_b.reshape(1, d))


def _router_kernel(x_ref, sh_ref, sc_ref, wr_ref, lg_ref):
    hf = x_ref[...] * (1.0 + sc_ref[...]) + sh_ref[...]
    lg_ref[...] = jnp.dot(hf, wr_ref[...], preferred_element_type=F32,
                          precision=lax.Precision.HIGHEST)


def _router(x2, mod3, w_router, seq):
    t, d = x2.shape
    tm = 512
    tpb = seq // tm
    ne = w_router.shape[1]
    return pl.pallas_call(
        _router_kernel,
        grid=(t // tm,),
        in_specs=[pl.BlockSpec((tm, d), lambda i: (i, 0)),
                  _mod_spec(3, d, tpb), _mod_spec(4, d, tpb),
                  pl.BlockSpec((d, ne), lambda i: (0, 0))],
        out_specs=pl.BlockSpec((tm, ne), lambda i: (i, 0)),
        out_shape=jax.ShapeDtypeStruct((t, ne), F32),
        compiler_params=_cparams(("arbitrary",)),
    )(x2, mod3, mod3, w_router)


def _first_index_of_max(v, iota, size):
    m = jnp.max(v, axis=0, keepdims=True)
    idx = jnp.min(jnp.where(v == m, iota, float(size)), axis=0, keepdims=True)
    return m, idx


def _topk_kernel(lg_ref, b_ref, tri_ref, ltri_ref, w_ref, pos_ref, cnt_ref):
    ne, tr = lg_ref.shape
    gsz = ne // N_EXPERT_GROUPS
    tile = pl.program_id(0)

    scores = 1.0 / (1.0 + jnp.exp(-lg_ref[...]))
    sel = scores + b_ref[...]
    iota_g = lax.broadcasted_iota(I32, (gsz, tr), 0).astype(F32)
    iota_n = lax.broadcasted_iota(I32, (N_EXPERT_GROUPS, tr), 0).astype(F32)
    gs = jnp.zeros((N_EXPERT_GROUPS, tr), F32)
    for g in range(N_EXPERT_GROUPS):
        blk = sel[g * gsz:(g + 1) * gsz, :]
        m1, i1 = _first_index_of_max(blk, iota_g, gsz)
        m2 = jnp.max(jnp.where(iota_g == i1, -jnp.inf, blk), axis=0, keepdims=True)
        gs = jnp.where(iota_n == float(g), m1 + m2, gs)
    gmask = jnp.zeros((N_EXPERT_GROUPS, tr), F32)
    for _ in range(TOPK_GROUPS):
        _, gi = _first_index_of_max(gs, iota_n, N_EXPERT_GROUPS)
        hit = iota_n == gi
        gmask = jnp.where(hit, 1.0, gmask)
        gs = jnp.where(hit, -jnp.inf, gs)
    masked_rows = []
    for g in range(N_EXPERT_GROUPS):
        keep = jnp.broadcast_to(gmask[g:g + 1, :], (gsz, tr)) > 0.5
        masked_rows.append(jnp.where(keep, sel[g * gsz:(g + 1) * gsz, :], -jnp.inf))
    cur = jnp.concatenate(masked_rows, axis=0)
    iota_e = lax.broadcasted_iota(I32, (ne, tr), 0).astype(F32)
    chosen = jnp.zeros((ne, tr), F32)
    for _ in range(TOP_K):
        _, ei = _first_index_of_max(cur, iota_e, ne)
        hit = iota_e == ei
        cur = jnp.where(hit, -jnp.inf, cur)
        chosen = jnp.where(hit, 1.0, chosen)
    picked = chosen > 0.5
    top_scores = jnp.where(picked, scores, 0.0)
    wsum = jnp.sum(top_scores, axis=0, keepdims=True)
    w_ref[...] = top_scores / wsum * ROUTED_SCALE
    before = jnp.dot(chosen.astype(BF16), tri_ref[...], preferred_element_type=F32)
    n = jnp.sum(chosen, axis=1, keepdims=True)
    n_chunks = jnp.floor((n + (CHUNK - 1)) * (1.0 / CHUNK))
    run_off = jnp.dot(ltri_ref[...], jnp.broadcast_to(n_chunks, (ne, 128)).astype(BF16),
                      preferred_element_type=F32)[:, 0:1] * CHUNK
    pos_ref[...] = jnp.where(picked, before + run_off, NO_SLOT)

    @pl.when(tile == 0)
    def _():
        cnt_ref[...] = jnp.zeros_like(cnt_ref)

    lane = lax.broadcasted_iota(I32, cnt_ref.shape, 1)
    cnt_ref[...] = jnp.where(lane == tile, jnp.broadcast_to(n, cnt_ref.shape).astype(I32),
                             cnt_ref[...])


def _topk_route(logits_t, router_b):
    ne, t = logits_t.shape
    tr = TOK_TILE
    assert t // tr <= 128
    tri = (jnp.arange(tr)[:, None] < jnp.arange(tr)[None, :]).astype(BF16)
    ltri = (jnp.arange(ne)[None, :] < jnp.arange(ne)[:, None]).astype(BF16)
    out = lambda dt: jax.ShapeDtypeStruct((ne, t), dt)
    row = pl.BlockSpec((ne, tr), lambda i: (0, i))
    return pl.pallas_call(
        _topk_kernel,
        grid=(t // tr,),
        in_specs=[pl.BlockSpec((ne, tr), lambda i: (0, i)),
                  pl.BlockSpec((ne, 1), lambda i: (0, 0)),
                  pl.BlockSpec((tr, tr), lambda i: (0, 0)),
                  pl.BlockSpec((ne, ne), lambda i: (0, 0))],
        out_specs=[row, row, pl.BlockSpec((ne, 128), lambda i: (0, 0))],
        out_shape=[out(F32), out(F32), jax.ShapeDtypeStruct((ne, 128), I32)],
        compiler_params=_cparams(("arbitrary",)),
    )(logits_t, router_b.reshape(ne, 1), tri, ltri)


def _slot_plan(counts, n_tiles, n_blocks):
    n = counts[:, :n_tiles]
    nch = (n + (CHUNK - 1)) // CHUNK
    rows = jnp.sum(nch, axis=1) * CHUNK
    region = ((rows + EXPERT_ROWS - 1) // EXPERT_ROWS) * EXPERT_ROWS
    region_end = jnp.cumsum(region)
    region_start = region_end - region
    run_chunk = region_start[:, None] // CHUNK + jnp.cumsum(nch, axis=1) - nch
    local_end = jnp.cumsum(nch, axis=0)
    local_chunk = local_end - nch
    c = jnp.arange(LOCAL_ROWS // CHUNK, dtype=I32)
    expert_of_c = jnp.sum((local_end[:, :, None] <= c[None, None, :]).astype(I32), axis=0)
    owner = expert_of_c[None] == jnp.arange(N_EXPERTS, dtype=I32)[:, None, None]
    slot_chunk = jnp.sum(jnp.where(owner, (run_chunk - local_chunk)[:, :, None], 0), axis=0) + c
    block_row = jnp.arange(n_blocks, dtype=I32) * EXPERT_ROWS
    group_e = jnp.minimum(jnp.sum((region_end[None, :] <= block_row[:, None]).astype(I32), axis=1),
                          N_EXPERTS - 1)
    n_used = region_end[-1] // EXPERT_ROWS
    nonempty = region > 0
    run_of_expert = jnp.cumsum(nonempty.astype(I32)) - 1
    experts = jnp.arange(N_EXPERTS, dtype=I32)
    run_expert = jnp.sum(jnp.where((run_of_expert[None, :] == experts[:, None]) & nonempty[None, :],
                                   experts[None, :], 0), axis=1)
    blocks = jnp.arange(n_blocks, dtype=I32)
    first = ((blocks == 0) | (group_e != jnp.roll(group_e, 1))) & (blocks < n_used)
    per_tile = lambda a: (a * CHUNK).T.astype(I32)
    return dict(
        run_begin_rows=per_tile(local_chunk)[:, None, :], run_end_rows=per_tile(local_end)[:, None, :],
        run_begin_cols=per_tile(local_chunk)[:, :, None], run_end_cols=per_tile(local_end)[:, :, None],
        block_run=(jnp.cumsum(first.astype(I32)) - 1).astype(I32), block_first=first.astype(I32),
        run_expert=run_expert.astype(I32), n_runs=jnp.sum(nonempty.astype(I32)).reshape(1),
        slot_chunk=slot_chunk.reshape(-1).astype(I32),
        tile_chunks=jnp.sum(nch, axis=0).astype(I32),
        tail_chunk=((region_start + rows) // CHUNK).astype(I32),
        tail_chunks=((region - rows) // CHUNK).astype(I32),
        group_e=group_e, n_used=(region_end[-1:] // EXPERT_ROWS).astype(I32))


def _chunk_rows(ref, chunk_index):
    start = chunk_index * CHUNK
    if not isinstance(start, int):
        start = pl.multiple_of(start, CHUNK)
    return ref.at[pl.ds(start, CHUNK), :]


def _pos_digits(pos):
    hi = jnp.floor(pos * (1.0 / POS_RADIX))
    return hi.astype(BF16), (pos - POS_RADIX * hi).astype(BF16)


def _for_chunks_two_queues(count, start):
    def pair(p, carry):
        start(2 * p, 0)
        start(2 * p + 1, 1)
        return carry

    lax.fori_loop(0, lax.shift_right_logical(count, 1), pair, 0)

    @pl.when((count & 1) == 1)
    def _():
        start(count - 1, 0)


def _wait_chunks(count, src_ref, dst_ref, sem):
    group_rows = pl.ds(0, WAIT_GROUP * CHUNK)

    def wait_group(c, carry):
        pltpu.make_async_copy(src_ref.at[group_rows, :], dst_ref.at[group_rows, :], sem).wait()
        return carry

    def wait_one(c, carry):
        pltpu.make_async_copy(_chunk_rows(src_ref, 0), _chunk_rows(dst_ref, 0), sem).wait()
        return carry

    lax.fori_loop(0, lax.shift_right_logical(count, WAIT_GROUP.bit_length() - 1), wait_group, 0)
    lax.fori_loop(0, count & (WAIT_GROUP - 1), wait_one, 0)


def _dispatch_kernel(slot_ref, tile_ref, tail_ref, tailn_ref, used_ref,
                     x_ref, sh_ref, sc_ref, pos_ref, begin_ref, end_ref,
                     xg_hbm, stage_ref, zero_ref, sems, sem_blk):
    tr = x_ref.shape[0]
    i = pl.program_id(0)
    n_blocks = xg_hbm.shape[0] // EXPERT_ROWS
    chunks_per_tile = LOCAL_ROWS // CHUNK

    def drain(count, sem):
        _wait_chunks(count, stage_ref.at[0], xg_hbm, sem)

    @pl.when(i == 0)
    def _():
        zero_ref[...] = jnp.zeros_like(zero_ref)

        def block_copy(j):
            rows = pl.ds(pl.multiple_of(j * EXPERT_ROWS, EXPERT_ROWS), EXPERT_ROWS)
            return pltpu.make_async_copy(zero_ref, xg_hbm.at[rows, :], sem_blk)

        def start_block(j, carry):
            block_copy(j).start()
            return carry

        def wait_block(j, carry):
            block_copy(j).wait()
            return carry

        lax.fori_loop(used_ref[0], n_blocks, start_block, 0)

        def per_expert(e, total):
            def per_chunk(c, carry):
                pltpu.make_async_copy(_chunk_rows(zero_ref, 0),
                                      _chunk_rows(xg_hbm, tail_ref[e] + c), sems.at[0]).start()
                return carry
            lax.fori_loop(0, tailn_ref[e], per_chunk, 0)
            return total + tailn_ref[e]

        drain(lax.fori_loop(0, N_EXPERTS, per_expert, 0), sems.at[0])
        lax.fori_loop(used_ref[0], n_blocks, wait_block, 0)

    hf = (x_ref[...] * (1.0 + sc_ref[...]) + sh_ref[...]).astype(BF16)
    row_e = lax.broadcasted_iota(I32, (LOCAL_ROWS, N_EXPERTS), 0)
    owner = jnp.where((row_e >= begin_ref[...]) & (row_e < end_ref[...]), 1.0, 0.0).astype(BF16)
    hi_digit, lo_digit = _pos_digits(pos_ref[...])
    slot_pos = (POS_RADIX * jnp.dot(owner, hi_digit, preferred_element_type=F32)
                + jnp.dot(owner, lo_digit, preferred_element_type=F32))
    row_t = lax.broadcasted_iota(I32, (LOCAL_ROWS, tr), 0).astype(F32)
    perm = jnp.where(slot_pos == row_t, 1.0, 0.0).astype(BF16)
    rows = jnp.dot(perm, hf, preferred_element_type=F32)
    half = rows.shape[1] // 2
    buf = stage_ref.at[i % 2]
    buf[...] = (lax.bitcast_convert_type(rows[:, :half], U32)
                | (lax.bitcast_convert_type(rows[:, half:], U32) >> 16))

    def send(c, priority):
        pltpu.make_async_copy(_chunk_rows(buf, c),
                              _chunk_rows(xg_hbm, slot_ref[i * chunks_per_tile + c]),
                              sems.at[i % 2]).start(priority=priority)

    _for_chunks_two_queues(tile_ref[i], send)

    @pl.when(i > 0)
    def _():
        drain(tile_ref[jnp.maximum(i - 1, 0)], sems.at[(i + 1) % 2])

    @pl.when(i == pl.num_programs(0) - 1)
    def _():
        drain(tile_ref[i], sems.at[i % 2])


def _dispatch(plan, x2, mod3, pos, n_slots, seq):
    t, d = x2.shape
    tr = TOK_TILE
    tpb = seq // tr
    return pl.pallas_call(
        _dispatch_kernel,
        grid_spec=pltpu.PrefetchScalarGridSpec(
            num_scalar_prefetch=5,
            grid=(t // tr,),
            in_specs=[pl.BlockSpec((tr, d), lambda i, *_: (i, 0)),
                      _mod_spec(3, d, tpb), _mod_spec(4, d, tpb),
                      pl.BlockSpec((N_EXPERTS, tr), lambda i, *_: (0, i)),
                      pl.BlockSpec((None, 1, N_EXPERTS), lambda i, *_: (i, 0, 0)),
                      pl.BlockSpec((None, 1, N_EXPERTS), lambda i, *_: (i, 0, 0))],
            out_specs=pl.BlockSpec(memory_space=pl.ANY),
            scratch_shapes=[pltpu.VMEM((2, LOCAL_ROWS, d // 2), U32),
                            pltpu.VMEM((EXPERT_ROWS, d // 2), U32),
                            pltpu.SemaphoreType.DMA((2,)), pltpu.SemaphoreType.DMA],
        ),
        out_shape=jax.ShapeDtypeStruct((n_slots, d // 2), U32),
        compiler_params=_cparams(("arbitrary",)),
    )(plan["slot_chunk"], plan["tile_chunks"], plan["tail_chunk"], plan["tail_chunks"],
      plan["n_used"], x2, mod3, mod3, pos, plan["run_begin_rows"], plan["run_end_rows"])


def _expert_kernel(run_ref, first_ref, rexp_ref, nrun_ref, nu_ref,
                   xg_hbm, wgu_hbm, wdn_hbm, yg_hbm,
                   xbuf, ybuf, wgu_f32, wdn_f32, wgu_bf, wdn_bf, x_sem, y_sem, w_sem, z_sem,
                   *, layer):
    n_used = nu_ref[0]
    n_blocks = yg_hbm.shape[0] // EXPERT_ROWS
    depth, _, half = xbuf.shape
    f = wdn_bf.shape[0]

    def rows(j):
        return pl.ds(pl.multiple_of(j * EXPERT_ROWS, EXPERT_ROWS), EXPERT_ROWS)

    def x_copy(j):
        return pltpu.make_async_copy(xg_hbm.at[rows(j), :], xbuf.at[j % depth],
                                     x_sem.at[j % depth])

    def y_copy(j):
        return pltpu.make_async_copy(ybuf.at[j % depth], yg_hbm.at[rows(j), :],
                                     y_sem.at[j % depth])

    def w_copies(r):
        e = rexp_ref[r]
        return (pltpu.make_async_copy(wgu_hbm.at[layer, e], wgu_f32.at[r % 2], w_sem.at[r % 2]),
                pltpu.make_async_copy(wdn_hbm.at[layer, e], wdn_f32.at[r % 2], w_sem.at[r % 2]))

    def zero_copy(j):
        return pltpu.make_async_copy(ybuf.at[0], yg_hbm.at[rows(j), :], z_sem)

    ybuf[0] = jnp.zeros(ybuf.shape[1:], U32)

    def zero_start(j, carry):
        zero_copy(j).start()
        return carry

    def zero_wait(j, carry):
        zero_copy(j).wait()
        return carry

    lax.fori_loop(n_used, n_blocks, zero_start, 0)
    lax.fori_loop(n_used, n_blocks, zero_wait, 0)

    for cp in w_copies(0):
        cp.start()
    for a in range(depth - 1):
        @pl.when(a < n_used)
        def _():
            x_copy(a).start()

    def block(j, carry):
        r = run_ref[j]

        @pl.when(first_ref[j] == 1)
        def _():
            for cp in w_copies(r):
                cp.wait()

            @pl.when(r + 1 < nrun_ref[0])
            def _():
                for cp in w_copies(r + 1):
                    cp.start()

            wgu_bf[...] = wgu_f32[r % 2].astype(BF16)
            wdn_bf[...] = wdn_f32[r % 2].astype(BF16)

        x_copy(j).wait()

        @pl.when(j + (depth - 1) < n_used)
        def _():
            x_copy(j + (depth - 1)).start()

        @pl.when(j >= depth)
        def _():
            y_copy(j - depth).wait()

        for g in range(EXPERT_ROWS // GEMM_SUB_ROWS):
            rws = slice(g * GEMM_SUB_ROWS, (g + 1) * GEMM_SUB_ROWS)
            hi, lo = _unpack_bf16_pairs(xbuf[j % depth, rws, :])
            gu = (jnp.dot(hi.astype(BF16), wgu_bf[0:half, :], preferred_element_type=F32)
                  + jnp.dot(lo.astype(BF16), wgu_bf[half:, :], preferred_element_type=F32))
            act = (_silu(gu[:, :f]) * gu[:, f:]).astype(BF16)
            ybuf[j % depth, rws, :] = _pack_bf16_pairs(
                jnp.dot(act, wdn_bf[...], preferred_element_type=F32))
        y_copy(j).start()
        return carry

    lax.fori_loop(0, n_used, block, 0)

    for a in range(depth, 0, -1):
        @pl.when(n_used >= a)
        def _():
            y_copy(n_used - a).wait()


def _expert_gemm(plan, xg, w_gu, w_down, layer):
    n_slots, half = xg.shape
    _, ne, d, f2 = w_gu.shape
    f = w_down.shape[2]
    hbm = pl.BlockSpec(memory_space=pl.ANY)
    return pl.pallas_call(
        functools.partial(_expert_kernel, layer=layer),
        grid_spec=pltpu.PrefetchScalarGridSpec(
            num_scalar_prefetch=5,
            grid=(1,),
            in_specs=[hbm, hbm, hbm],
            out_specs=hbm,
            scratch_shapes=[pltpu.VMEM((GEMM_DEPTH, EXPERT_ROWS, half), U32),
                            pltpu.VMEM((GEMM_DEPTH, EXPERT_ROWS, half), U32),
                            pltpu.VMEM((2, d, f2), F32), pltpu.VMEM((2, f, d), F32),
                            pltpu.VMEM((d, f2), BF16), pltpu.VMEM((f, d), BF16),
                            pltpu.SemaphoreType.DMA((GEMM_DEPTH,)),
                            pltpu.SemaphoreType.DMA((GEMM_DEPTH,)),
                            pltpu.SemaphoreType.DMA((2,)), pltpu.SemaphoreType.DMA],
        ),
        out_shape=jax.ShapeDtypeStruct((n_slots, half), U32),
        compiler_params=_cparams(("arbitrary",)),
    )(plan["block_run"], plan["block_first"], plan["run_expert"], plan["n_runs"], plan["n_used"],
      xg, w_gu, w_down)


def _combine_kernel(slot_ref, tile_ref,
                    x_ref, sh_ref, sc_ref, g_ref, pos_ref, wt_ref, begin_ref, end_ref,
                    wsgu_ref, wsdn_ref, lng_ref, lnb_ref, yg_hbm, o_ref, stage_ref, sems):
    tr, d = x_ref.shape
    i = pl.program_id(0)
    n_tiles = pl.num_programs(0)
    chunks_per_tile = LOCAL_ROWS // CHUNK

    def fetch(tile, buffer):
        def start(c, priority):
            pltpu.make_async_copy(_chunk_rows(yg_hbm, slot_ref[tile * chunks_per_tile + c]),
                                  _chunk_rows(stage_ref.at[buffer], c),
                                  sems.at[buffer]).start(priority=priority)
        _for_chunks_two_queues(tile_ref[tile], start)

    @pl.when(i == 0)
    def _():
        stage_ref[...] = jnp.zeros_like(stage_ref)
        fetch(0, 0)

    nxt = jnp.minimum(i + 1, n_tiles - 1)
    for buffer in range(2):
        @pl.when((i + 1 < n_tiles) & (nxt % 2 == buffer))
        def _():
            fetch(nxt, buffer)

    x = x_ref[...]
    hf = (x * (1.0 + sc_ref[...]) + sh_ref[...]).astype(BF16)
    f = wsdn_ref.shape[0]
    su = jnp.dot(hf, wsgu_ref[...], preferred_element_type=F32)
    act = (_silu(su[:, :f]) * su[:, f:]).astype(BF16)
    shared = jnp.dot(act, wsdn_ref[...], preferred_element_type=F32)

    col_e = lax.broadcasted_iota(I32, (N_EXPERTS, LOCAL_ROWS), 1)
    owner = jnp.where((col_e >= begin_ref[...]) & (col_e < end_ref[...]), 1.0, 0.0).astype(BF16)
    hi_digit, lo_digit = _pos_digits(pos_ref[...])
    slot_pos = (POS_RADIX * jnp.dot(hi_digit, owner, preferred_element_type=F32)
                + jnp.dot(lo_digit, owner, preferred_element_type=F32))
    holds = slot_pos == lax.broadcasted_iota(I32, (tr, LOCAL_ROWS), 1).astype(F32)
    w_tok = wt_ref[...]
    w_tok_hi = w_tok.astype(BF16)
    w_tok_lo = (w_tok - w_tok_hi.astype(F32)).astype(BF16)
    part = lambda w: jnp.where(holds, jnp.dot(w, owner, preferred_element_type=F32),
                               0.0).astype(BF16)
    w_both = jnp.concatenate([part(w_tok_hi), part(w_tok_lo)], axis=0)

    buf = stage_ref.at[i % 2]
    _wait_chunks(tile_ref[i], yg_hbm, buf, sems.at[i % 2])

    y_hi, y_lo = _unpack_bf16_pairs(buf[...])
    y = jnp.concatenate([y_hi.astype(BF16), y_lo.astype(BF16)], axis=1)
    both = jnp.dot(w_both, y, preferred_element_type=F32)
    routed = both[:tr, :] + both[tr:, :]
    v = ALPHA * x + (1.0 + g_ref[...]) * (routed + shared)
    o_ref[...] = _layer_norm(v, lng_ref[...], lnb_ref[...])


def _combine(plan, x2, mod3, pos_tok, w_tok, w_sh_gu, w_sh_down, ln_g, ln_b, yg, seq):
    t, d = x2.shape
    tr = TOK_TILE
    tpb = seq // tr
    full = lambda shape: pl.BlockSpec(shape, lambda i, *_: (0,) * len(shape))
    return pl.pallas_call(
        _combine_kernel,
        grid_spec=pltpu.PrefetchScalarGridSpec(
            num_scalar_prefetch=2,
            grid=(t // tr,),
            in_specs=[pl.BlockSpec((tr, d), lambda i, *_: (i, 0)),
                      _mod_spec(3, d, tpb), _mod_spec(4, d, tpb), _mod_spec(5, d, tpb),
                      pl.BlockSpec((tr, N_EXPERTS), lambda i, *_: (i, 0)),
                      pl.BlockSpec((tr, N_EXPERTS), lambda i, *_: (i, 0)),
                      pl.BlockSpec((None, N_EXPERTS, 1), lambda i, *_: (i, 0, 0)),
                      pl.BlockSpec((None, N_EXPERTS, 1), lambda i, *_: (i, 0, 0)),
                      full(w_sh_gu.shape), full(w_sh_down.shape), full((1, d)), full((1, d)),
                      pl.BlockSpec(memory_space=pl.ANY)],
            out_specs=pl.BlockSpec((tr, d), lambda i, *_: (i, 0)),
            scratch_shapes=[pltpu.VMEM((2, LOCAL_ROWS, d // 2), U32),
                            pltpu.SemaphoreType.DMA((2,))],
        ),
        out_shape=jax.ShapeDtypeStruct((t, d), F32),
        compiler_params=_cparams(("arbitrary",)),
    )(plan["slot_chunk"], plan["tile_chunks"],
      x2, mod3, mod3, mod3, pos_tok, w_tok, plan["run_begin_cols"], plan["run_end_cols"],
      w_sh_gu, w_sh_down, ln_g.reshape(1, d), ln_b.reshape(1, d), yg)


def _moe_layer(x2, mod3, w_router, router_b, w_gu, w_down, layer, w_sh_gu, w_sh_down,
               ln_g, ln_b, seq):
    t, d = x2.shape
    n_tiles = t // TOK_TILE
    logits = _router(x2, mod3, w_router, seq)
    w_top, pos, counts = _topk_route(logits.T, router_b)
    bound = t * TOP_K + n_tiles * N_EXPERTS * (CHUNK - 1) + N_EXPERTS * (EXPERT_ROWS - 1)
    n_blocks = -(-bound // EXPERT_ROWS)
    plan = _slot_plan(counts, n_tiles, n_blocks)
    xg = _dispatch(plan, x2, mod3, pos, n_blocks * EXPERT_ROWS, seq)
    yg = _expert_gemm(plan, xg, w_gu, w_down, layer)
    return _combine(plan, x2, mod3, pos.T, w_top.T, w_sh_gu, w_sh_down, ln_g, ln_b, yg, seq)


def kernel(x, c, ada_w, ada_b, pool_w_in, pool_w_grp, pool_scale, pool_w_out, attn_w_in, attn_w_out, ln1_g, ln1_b, router_w, router_b, exp_w_gu, exp_w_down, sh_w_gu, sh_w_down, ln2_g, ln2_b):
    batch, seq, d = x.shape
    depth = ada_w.shape[0]
    t = batch * seq
    mod = _modulation(c, ada_w, ada_b).reshape(depth, batch, 1, 6 * d)
    perms = jnp.stack([_perm_matrix(dil) for _, dil in ATTN_PATTERNS[1:]])
    perms_t = jnp.swapaxes(perms, 1, 2)
    expand = (jnp.arange(LSE_LANES)[:, None] == (jnp.arange(d)[None, :] // HEAD_DIM)).astype(BF16)
    part = (jnp.arange(attn_w_in.shape[2]) // d) % 3
    q_scale = jnp.where(part == 0, HEAD_DIM ** -0.5, 1.0).astype(F32)
    x2 = x.reshape(t, d)
    for i in range(depth):
        mod3 = mod[i]
        j = i // 2
        if i % 2 == 0:
            x2 = _pool_layer(x2, mod3, pool_w_in[j].astype(BF16), pool_w_grp[j].astype(BF16),
                             pool_scale[j], pool_w_out[j].astype(BF16), ln1_g[i], ln1_b[i], seq)
        else:
            qkv = _qkv_proj(x2, mod3, perms, (attn_w_in[j] * q_scale).astype(BF16), seq)
            res = [_attention_group(qkv, g, batch, seq) for g in range(len(ATTN_PATTERNS))]
            x2 = _attn_out(x2, mod3, [r[0] for r in res], [r[1] for r in res], perms_t, expand,
                           attn_w_out[j].astype(BF16), ln1_g[i], ln1_b[i], seq)
        x2 = _moe_layer(x2, mod3, router_w[i], router_b[i], exp_w_gu, exp_w_down, i,
                        sh_w_gu[i].astype(BF16), sh_w_down[i].astype(BF16),
                        ln2_g[i], ln2_b[i], seq)
    return x2.reshape(batch, seq, d)
```

```python
import functools
import math

import jax
import jax.numpy as jnp
from jax import lax
from jax.experimental import pallas as pl
from jax.experimental.pallas import tpu as pltpu

F32 = jnp.float32
BF16 = jnp.bfloat16
U32 = jnp.uint32
I32 = jnp.int32

POOL_WINDOWS = (2, 4, 8, 16)
ATTN_PATTERNS = ((128, 1), (512, 4), (2048, 16))
HEAD_DIM = 64
N_HEADS = 16
Q_BLOCK = 128
N_EXPERTS = 64
TOP_K = 8
N_EXPERT_GROUPS = 8
TOPK_GROUPS = 4
ROUTED_SCALE = 2.5
EXPERT_ROWS = 512
DEPTH = 4
ALPHA = (2 * DEPTH) ** 0.25
LN_EPS = 1e-5

PERM_TILE = 256
POOL_HALO = 16
TOK_TILE = 256
CHUNK = 8
LOCAL_ROWS = -(-(TOK_TILE * TOP_K + N_EXPERTS * (CHUNK - 1)) // 256) * 256
NO_SLOT = -64.0
POS_RADIX = 64.0
WAIT_GROUP = 32
GEMM_SUB_ROWS = 256
GEMM_DEPTH = 4
LSE_LANES = 128
VMEM_LIMIT = 56 * 1024 * 1024
NEG_BIG = -1e30


def _cparams(sem):
    return pltpu.CompilerParams(dimension_semantics=sem, vmem_limit_bytes=VMEM_LIMIT)


def _layer_norm(v, g, b):
    mu = jnp.mean(v, axis=-1, keepdims=True)
    c = v - mu
    var = jnp.mean(c * c, axis=-1, keepdims=True)
    return c * lax.rsqrt(var + LN_EPS) * g + b


def _silu(v):
    return v * (1.0 / (1.0 + jnp.exp(-v)))


def _pack_bf16_pairs(v):
    n = v.shape[1] // 2
    hi = lax.bitcast_convert_type(v[:, :n].astype(BF16).astype(F32), U32)
    lo = lax.bitcast_convert_type(v[:, n:].astype(BF16).astype(F32), U32)
    return hi | (lo >> 16)


def _unpack_bf16_pairs(p):
    hi = lax.bitcast_convert_type(p & jnp.uint32(0xFFFF0000), F32)
    lo = lax.bitcast_convert_type(p << 16, F32)
    return hi, lo


def _mod_kernel(c_ref, w_ref, b_ref, o_ref):
    cs = _silu(c_ref[...])
    o_ref[...] = jnp.dot(cs, w_ref[...], preferred_element_type=F32) + b_ref[...]


def _modulation(c, ada_w, ada_b):
    depth, d, n6 = ada_w.shape
    b = c.shape[0]
    tn = 1536
    return pl.pallas_call(
        _mod_kernel,
        grid=(depth, n6 // tn),
        in_specs=[
            pl.BlockSpec((b, d), lambda i, n: (0, 0)),
            pl.BlockSpec((None, d, tn), lambda i, n: (i, 0, n)),
            pl.BlockSpec((None, 1, tn), lambda i, n: (i, 0, n)),
        ],
        out_specs=pl.BlockSpec((None, b, tn), lambda i, n: (i, 0, n)),
        out_shape=jax.ShapeDtypeStruct((depth, b, n6), F32),
        compiler_params=_cparams(("arbitrary", "arbitrary")),
    )(c, ada_w, ada_b.reshape(depth, 1, n6))


def _mod_spec(chunk, d, tiles_per_batch):
    return pl.BlockSpec((None, 1, d), lambda *idx: (idx[0] // tiles_per_batch, 0, chunk))


def _pool_kernel(x_ref, sh_ref, sc_ref, g_ref, win_ref, wgrp_ref, cs_ref, wout_ref,
                 lng_ref, lnb_ref, o_ref, ext_ref, *, tiles_per_batch):
    tm, d = x_ref.shape
    s_idx = pl.program_id(0) % tiles_per_batch
    x = x_ref[...]
    h = (x * (1.0 + sc_ref[...]) + sh_ref[...]).astype(BF16)
    u = jnp.dot(h, win_ref[...], preferred_element_type=F32)

    @pl.when(s_idx == 0)
    def _():
        ext_ref[0:POOL_HALO, :] = jnp.zeros((POOL_HALO, d), F32)

    @pl.when(s_idx != 0)
    def _():
        ext_ref[0:POOL_HALO, :] = ext_ref[tm:tm + POOL_HALO, :]

    ext_ref[POOL_HALO:POOL_HALO + tm, :] = u

    pos = s_idx * tm + lax.broadcasted_iota(I32, (tm, 1), 0) + 1
    gc = d // len(POOL_WINDOWS)
    ys = []
    for g, w in enumerate(POOL_WINDOWS):
        cols = slice(g * gc, (g + 1) * gc)
        acc = u[:, cols]
        for j in range(1, w):
            acc = acc + ext_ref[POOL_HALO - j:POOL_HALO - j + tm, cols]
        cnt = jnp.minimum(pos, w).astype(F32)
        z = (acc / cnt - u[:, cols]).astype(BF16)
        ys.append(jnp.dot(z, wgrp_ref[g], preferred_element_type=F32))
    y = (jnp.concatenate(ys, axis=1) * cs_ref[...]).astype(BF16)
    out = jnp.dot(y, wout_ref[...], preferred_element_type=F32)
    v = ALPHA * x + (1.0 + g_ref[...]) * out
    o_ref[...] = _layer_norm(v, lng_ref[...], lnb_ref[...])


def _pool_layer(x2, mod3, w_in, w_grp, ch_scale, w_out, ln_g, ln_b, seq):
    t, d = x2.shape
    tm = 512
    tpb = seq // tm
    full = lambda shape: pl.BlockSpec(shape, lambda i: (0,) * len(shape))
    return pl.pallas_call(
        functools.partial(_pool_kernel, tiles_per_batch=tpb),
        grid=(t // tm,),
        in_specs=[
            pl.BlockSpec((tm, d), lambda i: (i, 0)),
            _mod_spec(0, d, tpb), _mod_spec(1, d, tpb), _mod_spec(2, d, tpb),
            full((d, d)), full(w_grp.shape), full((1, d)), full((d, d)),
            full((1, d)), full((1, d)),
        ],
        out_specs=pl.BlockSpec((tm, d), lambda i: (i, 0)),
        out_shape=jax.ShapeDtypeStruct((t, d), F32),
        scratch_shapes=[pltpu.VMEM((tm + POOL_HALO, d), F32)],
        compiler_params=_cparams(("arbitrary",)),
    )(x2, mod3, mod3, mod3, w_in, w_grp, ch_scale.reshape(1, d), w_out,
      ln_g.reshape(1, d), ln_b.reshape(1, d))


def _perm_matrix(dil):
    p = jnp.arange(PERM_TILE)
    chunk = PERM_TILE // dil
    src = (p % chunk) * dil + p // chunk
    return (src[:, None] == jnp.arange(PERM_TILE)[None, :]).astype(BF16)


def _qkv_kernel(x_ref, sh_ref, sc_ref, p_ref, w_ref, o_ref, h_ref):
    tm = x_ref.shape[0]
    g = pl.program_id(1)

    @pl.when(g == 0)
    def _():
        h = (x_ref[...] * (1.0 + sc_ref[...]) + sh_ref[...]).astype(BF16)
        h_ref[0] = h
        for gi in range(1, len(ATTN_PATTERNS)):
            for s in range(tm // PERM_TILE):
                rows = slice(s * PERM_TILE, (s + 1) * PERM_TILE)
                h_ref[gi, rows, :] = jnp.dot(
                    p_ref[gi - 1], h[rows, :], preferred_element_type=F32).astype(BF16)

    o_ref[...] = jnp.dot(h_ref[g], w_ref[...], preferred_element_type=F32).astype(BF16)


def _qkv_proj(x2, mod3, perms, w_in, seq):
    t, d = x2.shape
    ng = len(ATTN_PATTERNS)
    tm = 1024
    tpb = seq // tm
    return pl.pallas_call(
        _qkv_kernel,
        grid=(t // tm, ng),
        in_specs=[
            pl.BlockSpec((tm, d), lambda m, g: (m, 0)),
            _mod_spec(0, d, tpb), _mod_spec(1, d, tpb),
            pl.BlockSpec(perms.shape, lambda m, g: (0, 0, 0)),
            pl.BlockSpec((d, 3 * d), lambda m, g: (0, g)),
        ],
        out_specs=pl.BlockSpec((tm, 3 * d), lambda m, g: (m, g)),
        out_shape=jax.ShapeDtypeStruct((t, ng * 3 * d), BF16),
        scratch_shapes=[pltpu.VMEM((ng, tm, d), BF16)],
        compiler_params=_cparams(("arbitrary", "arbitrary")),
    )(x2, mod3, mod3, perms, w_in)


def _attn_kernel(q_ref, kp_ref, kc_ref, vp_ref, vc_ref, o_ref, lse_ref, *, group, dil):
    bq = Q_BLOCK
    d = N_HEADS * HEAD_DIM
    j = pl.program_id(2)
    q = q_ref[...].reshape(bq, d)
    kp = kp_ref[...].reshape(bq, d)
    kc = kc_ref[...].reshape(bq, d)
    vp = vp_ref[...].reshape(bq, d)
    vc = vc_ref[...].reshape(bq, d)

    qi = lax.broadcasted_iota(I32, (bq, 2 * bq), 0)
    kj = lax.broadcasted_iota(I32, (bq, 2 * bq), 1)
    dist = qi + bq - kj
    steps = ATTN_PATTERNS[group][0] // dil
    valid = (dist >= 0) & (dist <= steps) & ((kj >= bq) | (j > 0))
    neg_dist = jnp.where(valid, (dist * -dil).astype(F32), NEG_BIG)
    lane = lax.broadcasted_iota(I32, (bq, LSE_LANES), 1)
    lse_tile = jnp.zeros((bq, LSE_LANES), F32)
    n_tot = len(ATTN_PATTERNS) * N_HEADS
    pair = 2 * HEAD_DIM
    first_half = lax.broadcasted_iota(I32, (bq, pair), 1) < HEAD_DIM
    outs = []
    for hp in range(N_HEADS // 2):
        cols = slice(hp * pair, (hp + 1) * pair)
        q2 = q[:, cols]
        k2 = jnp.concatenate([kp[:, cols], kc[:, cols]], axis=0)
        v2 = jnp.concatenate([vp[:, cols], vc[:, cols]], axis=0)
        halves = []
        for sub in range(2):
            h = 2 * hp + sub
            slope = 2.0 ** (-8.0 * (group * N_HEADS + h + 1) / n_tot)
            mine = first_half if sub == 0 else jnp.logical_not(first_half)
            qh = jnp.where(mine, q2, jnp.zeros_like(q2))
            s = lax.dot_general(qh, k2, (((1,), (1,)), ((), ())), preferred_element_type=F32)
            s = s + slope * neg_dist
            m = jnp.max(s, axis=-1, keepdims=True)
            p = jnp.exp(s - m)
            den = jnp.sum(p, axis=-1, keepdims=True)
            halves.append(jnp.dot(p.astype(BF16), v2, preferred_element_type=F32) / den)
            lse_tile = jnp.where(lane == h, m + jnp.log(den), lse_tile)
        outs.append(jnp.where(first_half, halves[0], halves[1]))
    o_ref[...] = jnp.concatenate(outs, axis=1).astype(BF16).reshape(o_ref.shape)
    lse_ref[...] = lse_tile.reshape(lse_ref.shape)


def _attention_group(qkv, group, batch, seq):
    dil = ATTN_PATTERNS[group][1]
    d = N_HEADS * HEAD_DIM
    t = qkv.shape[0]
    sub = seq // dil
    nb = sub // Q_BLOCK
    rows = Q_BLOCK if dil == 1 else PERM_TILE // dil
    chunks = Q_BLOCK // rows
    u = seq // (rows * dil)
    view = lambda a, c: a.reshape(batch, u, dil, rows, c)
    blk = lambda c: (None, chunks, None, rows, c)
    col0 = group * 3
    q_spec = pl.BlockSpec(blk(d), lambda b, r, j: (b, j, r, 0, col0))
    kc_spec = pl.BlockSpec(blk(d), lambda b, r, j: (b, j, r, 0, col0 + 1))
    kp_spec = pl.BlockSpec(blk(d), lambda b, r, j: (b, jnp.maximum(j - 1, 0), r, 0, col0 + 1))
    vc_spec = pl.BlockSpec(blk(d), lambda b, r, j: (b, j, r, 0, col0 + 2))
    vp_spec = pl.BlockSpec(blk(d), lambda b, r, j: (b, jnp.maximum(j - 1, 0), r, 0, col0 + 2))
    qkv5 = view(qkv, qkv.shape[1])
    o, lse = pl.pallas_call(
        functools.partial(_attn_kernel, group=group, dil=dil),
        grid=(batch, dil, nb),
        in_specs=[q_spec, kp_spec, kc_spec, vp_spec, vc_spec],
        out_specs=[
            pl.BlockSpec(blk(d), lambda b, r, j: (b, j, r, 0, 0)),
            pl.BlockSpec(blk(LSE_LANES), lambda b, r, j: (b, j, r, 0, 0)),
        ],
        out_shape=[
            jax.ShapeDtypeStruct((batch, u, dil, rows, d), BF16),
            jax.ShapeDtypeStruct((batch, u, dil, rows, LSE_LANES), F32),
        ],
        compiler_params=_cparams(("arbitrary", "arbitrary", "arbitrary")),
    )(qkv5, qkv5, qkv5, qkv5, qkv5)
    return o.reshape(t, d), lse.reshape(t, LSE_LANES)


def _split3(v):
    a = v.astype(BF16)
    r = v - a.astype(F32)
    b = r.astype(BF16)
    c = (r - b.astype(F32)).astype(BF16)
    return a, b, c


def _attn_out_kernel(x_ref, g_ref, o0_ref, o1_ref, o2_ref, l0_ref, l1_ref, l2_ref,
                     pt_ref, e_ref, wout_ref, lng_ref, lnb_ref, out_ref):
    tm, d = x_ref.shape
    o_refs = (o0_ref, o1_ref, o2_ref)
    l_refs = (l0_ref, l1_ref, l2_ref)
    n_sub = tm // PERM_TILE

    def unperm(gi, val_bf16):
        if gi == 0:
            return val_bf16.astype(F32)
        parts = [jnp.dot(pt_ref[gi - 1], val_bf16[s * PERM_TILE:(s + 1) * PERM_TILE, :],
                         preferred_element_type=F32) for s in range(n_sub)]
        return jnp.concatenate(parts, axis=0)

    lses = []
    for gi in range(3):
        l = l_refs[gi][...]
        if gi == 0:
            lses.append(l)
        else:
            a, b, c = _split3(l)
            lses.append(unperm(gi, a) + unperm(gi, b) + unperm(gi, c))
    mx = jnp.maximum(jnp.maximum(lses[0], lses[1]), lses[2])
    es = [jnp.exp(l - mx) for l in lses]
    tot = es[0] + es[1] + es[2]
    mixed = jnp.zeros((tm, d), F32)
    for gi in range(3):
        w = es[gi] / tot
        a, b, c = _split3(w)
        wide = (jnp.dot(a, e_ref[...], preferred_element_type=F32)
                + jnp.dot(b, e_ref[...], preferred_element_type=F32)
                + jnp.dot(c, e_ref[...], preferred_element_type=F32))
        mixed = mixed + wide * unperm(gi, o_refs[gi][...])
    y = jnp.dot(mixed.astype(BF16), wout_ref[...], preferred_element_type=F32)
    v = ALPHA * x_ref[...] + (1.0 + g_ref[...]) * y
    out_ref[...] = _layer_norm(v, lng_ref[...], lnb_ref[...])


def _attn_out(x2, mod3, os_, lses, perms_t, expand, w_out, ln_g, ln_b, seq):
    t, d = x2.shape
    tm = 512
    tpb = seq // tm
    row = lambda c: pl.BlockSpec((tm, c), lambda i: (i, 0))
    full = lambda shape: pl.BlockSpec(shape, lambda i: (0,) * len(shape))
    return pl.pallas_call(
        _attn_out_kernel,
        grid=(t // tm,),
        in_specs=[row(d), _mod_spec(2, d, tpb), row(d), row(d), row(d),
                  row(LSE_LANES), row(LSE_LANES), row(LSE_LANES),
                  full(perms_t.shape), full(expand.shape), full((d, d)),
                  full((1, d)), full((1, d))],
        out_specs=row(d),
        out_shape=jax.ShapeDtypeStruct((t, d), F32),
        compiler_params=_cparams(("arbitrary",)),
    )(x2, mod3, *os_, *lses, perms_t, expand, w_out, ln_g.reshape(1, d), ln_b.reshape(1, d))


def _router_kernel(x_ref, sh_ref, sc_ref, wr_ref, lg_ref):
    hf = x_ref[...] * (1.0 + sc_ref[...]) + sh_ref[...]
    lg_ref[...] = jnp.dot(hf, wr_ref[...], preferred_element_type=F32,
                          precision=lax.Precision.HIGHEST)


def _router(x2, mod3, w_router, seq):
    t, d = x2.shape
    tm = 512
    tpb = seq // tm
    ne = w_router.shape[1]
    return pl.pallas_call(
        _router_kernel,
        grid=(t // tm,),
        in_specs=[pl.BlockSpec((tm, d), lambda i: (i, 0)),
                  _mod_spec(3, d, tpb), _mod_spec(4, d, tpb),
                  pl.BlockSpec((d, ne), lambda i: (0, 0))],
        out_specs=pl.BlockSpec((tm, ne), lambda i: (i, 0)),
        out_shape=jax.ShapeDtypeStruct((t, ne), F32),
        compiler_params=_cparams(("arbitrary",)),
    )(x2, mod3, mod3, w_router)


def _first_index_of_max(v, iota, size):
    m = jnp.max(v, axis=0, keepdims=True)
    idx = jnp.min(jnp.where(v == m, iota, float(size)), axis=0, keepdims=True)
    return m, idx


def _topk_kernel(lg_ref, b_ref, tri_ref, ltri_ref, w_ref, pos_ref, cnt_ref):
    ne, tr = lg_ref.shape
    gsz = ne // N_EXPERT_GROUPS
    tile = pl.program_id(0)

    scores = 1.0 / (1.0 + jnp.exp(-lg_ref[...]))
    sel = scores + b_ref[...]
    iota_g = lax.broadcasted_iota(I32, (gsz, tr), 0).astype(F32)
    iota_n = lax.broadcasted_iota(I32, (N_EXPERT_GROUPS, tr), 0).astype(F32)
    gs = jnp.zeros((N_EXPERT_GROUPS, tr), F32)
    for g in range(N_EXPERT_GROUPS):
        blk = sel[g * gsz:(g + 1) * gsz, :]
        m1, i1 = _first_index_of_max(blk, iota_g, gsz)
        m2 = jnp.max(jnp.where(iota_g == i1, -jnp.inf, blk), axis=0, keepdims=True)
        gs = jnp.where(iota_n == float(g), m1 + m2, gs)
    gmask = jnp.zeros((N_EXPERT_GROUPS, tr), F32)
    for _ in range(TOPK_GROUPS):
        _, gi = _first_index_of_max(gs, iota_n, N_EXPERT_GROUPS)
        hit = iota_n == gi
        gmask = jnp.where(hit, 1.0, gmask)
        gs = jnp.where(hit, -jnp.inf, gs)
    masked_rows = []
    for g in range(N_EXPERT_GROUPS):
        keep = jnp.broadcast_to(gmask[g:g + 1, :], (gsz, tr)) > 0.5
        masked_rows.append(jnp.where(keep, sel[g * gsz:(g + 1) * gsz, :], -jnp.inf))
    cur = jnp.concatenate(masked_rows, axis=0)
    iota_e = lax.broadcasted_iota(I32, (ne, tr), 0).astype(F32)
    chosen = jnp.zeros((ne, tr), F32)
    for _ in range(TOP_K):
        _, ei = _first_index_of_max(cur, iota_e, ne)
        hit = iota_e == ei
        cur = jnp.where(hit, -jnp.inf, cur)
        chosen = jnp.where(hit, 1.0, chosen)
    picked = chosen > 0.5
    top_scores = jnp.where(picked, scores, 0.0)
    wsum = jnp.sum(top_scores, axis=0, keepdims=True)
    w_ref[...] = top_scores / wsum * ROUTED_SCALE
    before = jnp.dot(chosen.astype(BF16), tri_ref[...], preferred_element_type=F32)
    n = jnp.sum(chosen, axis=1, keepdims=True)
    n_chunks = jnp.floor((n + (CHUNK - 1)) * (1.0 / CHUNK))
    run_off = jnp.dot(ltri_ref[...], jnp.broadcast_to(n_chunks, (ne, 128)).astype(BF16),
                      preferred_element_type=F32)[:, 0:1] * CHUNK
    pos_ref[...] = jnp.where(picked, before + run_off, NO_SLOT)

    @pl.when(tile == 0)
    def _():
        cnt_ref[...] = jnp.zeros_like(cnt_ref)

    lane = lax.broadcasted_iota(I32, cnt_ref.shape, 1)
    cnt_ref[...] = jnp.where(lane == tile, jnp.broadcast_to(n, cnt_ref.shape).astype(I32),
                             cnt_ref[...])


def _topk_route(logits_t, router_b):
    ne, t = logits_t.shape
    tr = TOK_TILE
    assert t // tr <= 128
    tri = (jnp.arange(tr)[:, None] < jnp.arange(tr)[None, :]).astype(BF16)
    ltri = (jnp.arange(ne)[None, :] < jnp.arange(ne)[:, None]).astype(BF16)
    out = lambda dt: jax.ShapeDtypeStruct((ne, t), dt)
    row = pl.BlockSpec((ne, tr), lambda i: (0, i))
    return pl.pallas_call(
        _topk_kernel,
        grid=(t // tr,),
        in_specs=[pl.BlockSpec((ne, tr), lambda i: (0, i)),
                  pl.BlockSpec((ne, 1), lambda i: (0, 0)),
                  pl.BlockSpec((tr, tr), lambda i: (0, 0)),
                  pl.BlockSpec((ne, ne), lambda i: (0, 0))],
        out_specs=[row, row, pl.BlockSpec((ne, 128), lambda i: (0, 0))],
        out_shape=[out(F32), out(F32), jax.ShapeDtypeStruct((ne, 128), I32)],
        compiler_params=_cparams(("arbitrary",)),
    )(logits_t, router_b.reshape(ne, 1), tri, ltri)


def _slot_plan(counts, n_tiles, n_blocks):
    n = counts[:, :n_tiles]
    nch = (n + (CHUNK - 1)) // CHUNK
    rows = jnp.sum(nch, axis=1) * CHUNK
    region = ((rows + EXPERT_ROWS - 1) // EXPERT_ROWS) * EXPERT_ROWS
    region_end = jnp.cumsum(region)
    region_start = region_end - region
    run_chunk = region_start[:, None] // CHUNK + jnp.cumsum(nch, axis=1) - nch
    local_end = jnp.cumsum(nch, axis=0)
    local_chunk = local_end - nch
    c = jnp.arange(LOCAL_ROWS // CHUNK, dtype=I32)
    expert_of_c = jnp.sum((local_end[:, :, None] <= c[None, None, :]).astype(I32), axis=0)
    owner = expert_of_c[None] == jnp.arange(N_EXPERTS, dtype=I32)[:, None, None]
    slot_chunk = jnp.sum(jnp.where(owner, (run_chunk - local_chunk)[:, :, None], 0), axis=0) + c
    block_row = jnp.arange(n_blocks, dtype=I32) * EXPERT_ROWS
    group_e = jnp.minimum(jnp.sum((region_end[None, :] <= block_row[:, None]).astype(I32), axis=1),
                          N_EXPERTS - 1)
    n_used = region_end[-1] // EXPERT_ROWS
    nonempty = region > 0
    run_of_expert = jnp.cumsum(nonempty.astype(I32)) - 1
    experts = jnp.arange(N_EXPERTS, dtype=I32)
    run_expert = jnp.sum(jnp.where((run_of_expert[None, :] == experts[:, None]) & nonempty[None, :],
                                   experts[None, :], 0), axis=1)
    blocks = jnp.arange(n_blocks, dtype=I32)
    first = ((blocks == 0) | (group_e != jnp.roll(group_e, 1))) & (blocks < n_used)
    per_tile = lambda a: (a * CHUNK).T.astype(I32)
    return dict(
        run_begin_rows=per_tile(local_chunk)[:, None, :], run_end_rows=per_tile(local_end)[:, None, :],
        run_begin_cols=per_tile(local_chunk)[:, :, None], run_end_cols=per_tile(local_end)[:, :, None],
        block_run=(jnp.cumsum(first.astype(I32)) - 1).astype(I32), block_first=first.astype(I32),
        run_expert=run_expert.astype(I32), n_runs=jnp.sum(nonempty.astype(I32)).reshape(1),
        slot_chunk=slot_chunk.reshape(-1).astype(I32),
        tile_chunks=jnp.sum(nch, axis=0).astype(I32),
        tail_chunk=((region_start + rows) // CHUNK).astype(I32),
        tail_chunks=((region - rows) // CHUNK).astype(I32),
        group_e=group_e, n_used=(region_end[-1:] // EXPERT_ROWS).astype(I32))


def _chunk_rows(ref, chunk_index):
    start = chunk_index * CHUNK
    if not isinstance(start, int):
        start = pl.multiple_of(start, CHUNK)
    return ref.at[pl.ds(start, CHUNK), :]


def _pos_digits(pos):
    hi = jnp.floor(pos * (1.0 / POS_RADIX))
    return hi.astype(BF16), (pos - POS_RADIX * hi).astype(BF16)


def _for_chunks_two_queues(count, start):
    def pair(p, carry):
        start(2 * p, 0)
        start(2 * p + 1, 1)
        return carry

    lax.fori_loop(0, lax.shift_right_logical(count, 1), pair, 0)

    @pl.when((count & 1) == 1)
    def _():
        start(count - 1, 0)


def _wait_chunks(count, src_ref, dst_ref, sem):
    group_rows = pl.ds(0, WAIT_GROUP * CHUNK)

    def wait_group(c, carry):
        pltpu.make_async_copy(src_ref.at[group_rows, :], dst_ref.at[group_rows, :], sem).wait()
        return carry

    def wait_one(c, carry):
        pltpu.make_async_copy(_chunk_rows(src_ref, 0), _chunk_rows(dst_ref, 0), sem).wait()
        return carry

    lax.fori_loop(0, lax.shift_right_logical(count, WAIT_GROUP.bit_length() - 1), wait_group, 0)
    lax.fori_loop(0, count & (WAIT_GROUP - 1), wait_one, 0)


def _dispatch_kernel(slot_ref, tile_ref, tail_ref, tailn_ref, used_ref,
                     x_ref, sh_ref, sc_ref, pos_ref, begin_ref, end_ref,
                     xg_hbm, stage_ref, zero_ref, sems, sem_blk):
    tr = x_ref.shape[0]
    i = pl.program_id(0)
    n_blocks = xg_hbm.shape[0] // EXPERT_ROWS
    chunks_per_tile = LOCAL_ROWS // CHUNK

    def drain(count, sem):
        _wait_chunks(count, stage_ref.at[0], xg_hbm, sem)

    @pl.when(i == 0)
    def _():
        zero_ref[...] = jnp.zeros_like(zero_ref)

        def block_copy(j):
            rows = pl.ds(pl.multiple_of(j * EXPERT_ROWS, EXPERT_ROWS), EXPERT_ROWS)
            return pltpu.make_async_copy(zero_ref, xg_hbm.at[rows, :], sem_blk)

        def start_block(j, carry):
            block_copy(j).start()
            return carry

        def wait_block(j, carry):
            block_copy(j).wait()
            return carry

        lax.fori_loop(used_ref[0], n_blocks, start_block, 0)

        def per_expert(e, total):
            def per_chunk(c, carry):
                pltpu.make_async_copy(_chunk_rows(zero_ref, 0),
                                      _chunk_rows(xg_hbm, tail_ref[e] + c), sems.at[0]).start()
                return carry
            lax.fori_loop(0, tailn_ref[e], per_chunk, 0)
            return total + tailn_ref[e]

        drain(lax.fori_loop(0, N_EXPERTS, per_expert, 0), sems.at[0])
        lax.fori_loop(used_ref[0], n_blocks, wait_block, 0)

    hf = (x_ref[...] * (1.0 + sc_ref[...]) + sh_ref[...]).astype(BF16)
    row_e = lax.broadcasted_iota(I32, (LOCAL_ROWS, N_EXPERTS), 0)
    owner = jnp.where((row_e >= begin_ref[...]) & (row_e < end_ref[...]), 1.0, 0.0).astype(BF16)
    hi_digit, lo_digit = _pos_digits(pos_ref[...])
    slot_pos = (POS_RADIX * jnp.dot(owner, hi_digit, preferred_element_type=F32)
                + jnp.dot(owner, lo_digit, preferred_element_type=F32))
    row_t = lax.broadcasted_iota(I32, (LOCAL_ROWS, tr), 0).astype(F32)
    perm = jnp.where(slot_pos == row_t, 1.0, 0.0).astype(BF16)
    rows = jnp.dot(perm, hf, preferred_element_type=F32)
    half = rows.shape[1] // 2
    buf = stage_ref.at[i % 2]
    buf[...] = (lax.bitcast_convert_type(rows[:, :half], U32)
                | (lax.bitcast_convert_type(rows[:, half:], U32) >> 16))

    def send(c, priority):
        pltpu.make_async_copy(_chunk_rows(buf, c),
                              _chunk_rows(xg_hbm, slot_ref[i * chunks_per_tile + c]),
                              sems.at[i % 2]).start(priority=priority)

    _for_chunks_two_queues(tile_ref[i], send)

    @pl.when(i > 0)
    def _():
        drain(tile_ref[jnp.maximum(i - 1, 0)], sems.at[(i + 1) % 2])

    @pl.when(i == pl.num_programs(0) - 1)
    def _():
        drain(tile_ref[i], sems.at[i % 2])


def _dispatch(plan, x2, mod3, pos, n_slots, seq):
    t, d = x2.shape
    tr = TOK_TILE
    tpb = seq // tr
    return pl.pallas_call(
        _dispatch_kernel,
        grid_spec=pltpu.PrefetchScalarGridSpec(
            num_scalar_prefetch=5,
            grid=(t // tr,),
            in_specs=[pl.BlockSpec((tr, d), lambda i, *_: (i, 0)),
                      _mod_spec(3, d, tpb), _mod_spec(4, d, tpb),
                      pl.BlockSpec((N_EXPERTS, tr), lambda i, *_: (0, i)),
                      pl.BlockSpec((None, 1, N_EXPERTS), lambda i, *_: (i, 0, 0)),
                      pl.BlockSpec((None, 1, N_EXPERTS), lambda i, *_: (i, 0, 0))],
            out_specs=pl.BlockSpec(memory_space=pl.ANY),
            scratch_shapes=[pltpu.VMEM((2, LOCAL_ROWS, d // 2), U32),
                            pltpu.VMEM((EXPERT_ROWS, d // 2), U32),
                            pltpu.SemaphoreType.DMA((2,)), pltpu.SemaphoreType.DMA],
        ),
        out_shape=jax.ShapeDtypeStruct((n_slots, d // 2), U32),
        compiler_params=_cparams(("arbitrary",)),
    )(plan["slot_chunk"], plan["tile_chunks"], plan["tail_chunk"], plan["tail_chunks"],
      plan["n_used"], x2, mod3, mod3, pos, plan["run_begin_rows"], plan["run_end_rows"])


def _expert_kernel(run_ref, first_ref, rexp_ref, nrun_ref, nu_ref,
                   xg_hbm, wgu_hbm, wdn_hbm, yg_hbm,
                   xbuf, ybuf, wgu_f32, wdn_f32, wgu_bf, wdn_bf, x_sem, y_sem, w_sem, z_sem,
                   *, layer):
    n_used = nu_ref[0]
    n_blocks = yg_hbm.shape[0] // EXPERT_ROWS
    depth, _, half = xbuf.shape
    f = wdn_bf.shape[0]

    def rows(j):
        return pl.ds(pl.multiple_of(j * EXPERT_ROWS, EXPERT_ROWS), EXPERT_ROWS)

    def x_copy(j):
        return pltpu.make_async_copy(xg_hbm.at[rows(j), :], xbuf.at[j % depth],
                                     x_sem.at[j % depth])

    def y_copy(j):
        return pltpu.make_async_copy(ybuf.at[j % depth], yg_hbm.at[rows(j), :],
                                     y_sem.at[j % depth])

    def w_copies(r):
        e = rexp_ref[r]
        return (pltpu.make_async_copy(wgu_hbm.at[layer, e], wgu_f32.at[r % 2], w_sem.at[r % 2]),
                pltpu.make_async_copy(wdn_hbm.at[layer, e], wdn_f32.at[r % 2], w_sem.at[r % 2]))

    def zero_copy(j):
        return pltpu.make_async_copy(ybuf.at[0], yg_hbm.at[rows(j), :], z_sem)

    ybuf[0] = jnp.zeros(ybuf.shape[1:], U32)

    def zero_start(j, carry):
        zero_copy(j).start()
        return carry

    def zero_wait(j, carry):
        zero_copy(j).wait()
        return carry

    lax.fori_loop(n_used, n_blocks, zero_start, 0)
    lax.fori_loop(n_used, n_blocks, zero_wait, 0)

    for cp in w_copies(0):
        cp.start()
    for a in range(depth - 1):
        @pl.when(a < n_used)
        def _():
            x_copy(a).start()

    def block(j, carry):
        r = run_ref[j]

        @pl.when(first_ref[j] == 1)
        def _():
            for cp in w_copies(r):
                cp.wait()

            @pl.when(r + 1 < nrun_ref[0])
            def _():
                for cp in w_copies(r + 1):
                    cp.start()

            wgu_bf[...] = wgu_f32[r % 2].astype(BF16)
            wdn_bf[...] = wdn_f32[r % 2].astype(BF16)

        x_copy(j).wait()

        @pl.when(j + (depth - 1) < n_used)
        def _():
            x_copy(j + (depth - 1)).start()

        @pl.when(j >= depth)
        def _():
            y_copy(j - depth).wait()

        for g in range(EXPERT_ROWS // GEMM_SUB_ROWS):
            rws = slice(g * GEMM_SUB_ROWS, (g + 1) * GEMM_SUB_ROWS)
            hi, lo = _unpack_bf16_pairs(xbuf[j % depth, rws, :])
            gu = (jnp.dot(hi.astype(BF16), wgu_bf[0:half, :], preferred_element_type=F32)
                  + jnp.dot(lo.astype(BF16), wgu_bf[half:, :], preferred_element_type=F32))
            act = (_silu(gu[:, :f]) * gu[:, f:]).astype(BF16)
            ybuf[j % depth, rws, :] = _pack_bf16_pairs(
                jnp.dot(act, wdn_bf[...], preferred_element_type=F32))
        y_copy(j).start()
        return carry

    lax.fori_loop(0, n_used, block, 0)

    for a in range(depth, 0, -1):
        @pl.when(n_used >= a)
        def _():
            y_copy(n_used - a).wait()


def _expert_gemm(plan, xg, w_gu, w_down, layer):
    n_slots, half = xg.shape
    _, ne, d, f2 = w_gu.shape
    f = w_down.shape[2]
    hbm = pl.BlockSpec(memory_space=pl.ANY)
    return pl.pallas_call(
        functools.partial(_expert_kernel, layer=layer),
        grid_spec=pltpu.PrefetchScalarGridSpec(
            num_scalar_prefetch=5,
            grid=(1,),
            in_specs=[hbm, hbm, hbm],
            out_specs=hbm,
            scratch_shapes=[pltpu.VMEM((GEMM_DEPTH, EXPERT_ROWS, half), U32),
                            pltpu.VMEM((GEMM_DEPTH, EXPERT_ROWS, half), U32),
                            pltpu.VMEM((2, d, f2), F32), pltpu.VMEM((2, f, d), F32),
                            pltpu.VMEM((d, f2), BF16), pltpu.VMEM((f, d), BF16),
                            pltpu.SemaphoreType.DMA((GEMM_DEPTH,)),
                            pltpu.SemaphoreType.DMA((GEMM_DEPTH,)),
                            pltpu.SemaphoreType.DMA((2,)), pltpu.SemaphoreType.DMA],
        ),
        out_shape=jax.ShapeDtypeStruct((n_slots, half), U32),
        compiler_params=_cparams(("arbitrary",)),
    )(plan["block_run"], plan["block_first"], plan["run_expert"], plan["n_runs"], plan["n_used"],
      xg, w_gu, w_down)


def _combine_kernel(slot_ref, tile_ref,
                    x_ref, sh_ref, sc_ref, g_ref, pos_ref, wt_ref, begin_ref, end_ref,
                    wsgu_ref, wsdn_ref, lng_ref, lnb_ref, yg_hbm, o_ref, stage_ref, sems):
    tr, d = x_ref.shape
    i = pl.program_id(0)
    n_tiles = pl.num_programs(0)
    chunks_per_tile = LOCAL_ROWS // CHUNK

    def fetch(tile, buffer):
        def start(c, priority):
            pltpu.make_async_copy(_chunk_rows(yg_hbm, slot_ref[tile * chunks_per_tile + c]),
                                  _chunk_rows(stage_ref.at[buffer], c),
                                  sems.at[buffer]).start(priority=priority)
        _for_chunks_two_queues(tile_ref[tile], start)

    @pl.when(i == 0)
    def _():
        stage_ref[...] = jnp.zeros_like(stage_ref)
        fetch(0, 0)

    nxt = jnp.minimum(i + 1, n_tiles - 1)
    for buffer in range(2):
        @pl.when((i + 1 < n_tiles) & (nxt % 2 == buffer))
        def _():
            fetch(nxt, buffer)

    x = x_ref[...]
    hf = (x * (1.0 + sc_ref[...]) + sh_ref[...]).astype(BF16)
    f = wsdn_ref.shape[0]
    su = jnp.dot(hf, wsgu_ref[...], preferred_element_type=F32)
    act = (_silu(su[:, :f]) * su[:, f:]).astype(BF16)
    shared = jnp.dot(act, wsdn_ref[...], preferred_element_type=F32)

    col_e = lax.broadcasted_iota(I32, (N_EXPERTS, LOCAL_ROWS), 1)
    owner = jnp.where((col_e >= begin_ref[...]) & (col_e < end_ref[...]), 1.0, 0.0).astype(BF16)
    hi_digit, lo_digit = _pos_digits(pos_ref[...])
    slot_pos = (POS_RADIX * jnp.dot(hi_digit, owner, preferred_element_type=F32)
                + jnp.dot(lo_digit, owner, preferred_element_type=F32))
    holds = slot_pos == lax.broadcasted_iota(I32, (tr, LOCAL_ROWS), 1).astype(F32)
    weights = jnp.where(holds, jnp.dot(wt_ref[...].astype(BF16), owner,
                                       preferred_element_type=F32), 0.0).astype(BF16)

    buf = stage_ref.at[i % 2]
    _wait_chunks(tile_ref[i], yg_hbm, buf, sems.at[i % 2])

    y_hi, y_lo = _unpack_bf16_pairs(buf[...])
    y = jnp.concatenate([y_hi.astype(BF16), y_lo.astype(BF16)], axis=1)
    routed = jnp.dot(weights, y, preferred_element_type=F32)
    v = ALPHA * x + (1.0 + g_ref[...]) * (routed + shared)
    o_ref[...] = _layer_norm(v, lng_ref[...], lnb_ref[...])


def _combine(plan, x2, mod3, pos_tok, w_tok, w_sh_gu, w_sh_down, ln_g, ln_b, yg, seq):
    t, d = x2.shape
    tr = TOK_TILE
    tpb = seq // tr
    full = lambda shape: pl.BlockSpec(shape, lambda i, *_: (0,) * len(shape))
    return pl.pallas_call(
        _combine_kernel,
        grid_spec=pltpu.PrefetchScalarGridSpec(
            num_scalar_prefetch=2,
            grid=(t // tr,),
            in_specs=[pl.BlockSpec((tr, d), lambda i, *_: (i, 0)),
                      _mod_spec(3, d, tpb), _mod_spec(4, d, tpb), _mod_spec(5, d, tpb),
                      pl.BlockSpec((tr, N_EXPERTS), lambda i, *_: (i, 0)),
                      pl.BlockSpec((tr, N_EXPERTS), lambda i, *_: (i, 0)),
                      pl.BlockSpec((None, N_EXPERTS, 1), lambda i, *_: (i, 0, 0)),
                      pl.BlockSpec((None, N_EXPERTS, 1), lambda i, *_: (i, 0, 0)),
                      full(w_sh_gu.shape), full(w_sh_down.shape), full((1, d)), full((1, d)),
                      pl.BlockSpec(memory_space=pl.ANY)],
            out_specs=pl.BlockSpec((tr, d), lambda i, *_: (i, 0)),
            scratch_shapes=[pltpu.VMEM((2, LOCAL_ROWS, d // 2), U32),
                            pltpu.SemaphoreType.DMA((2,))],
        ),
        out_shape=jax.ShapeDtypeStruct((t, d), F32),
        compiler_params=_cparams(("arbitrary",)),
    )(plan["slot_chunk"], plan["tile_chunks"],
      x2, mod3, mod3, mod3, pos_tok, w_tok, plan["run_begin_cols"], plan["run_end_cols"],
      w_sh_gu, w_sh_down, ln_g.reshape(1, d), ln_b.reshape(1, d), yg)


def _moe_layer(x2, mod3, w_router, router_b, w_gu, w_down, layer, w_sh_gu, w_sh_down,
               ln_g, ln_b, seq):
    t, d = x2.shape
    n_tiles = t // TOK_TILE
    logits = _router(x2, mod3, w_router, seq)
    w_top, pos, counts = _topk_route(logits.T, router_b)
    bound = t * TOP_K + n_tiles * N_EXPERTS * (CHUNK - 1) + N_EXPERTS * (EXPERT_ROWS - 1)
    n_blocks = -(-bound // EXPERT_ROWS)
    plan = _slot_plan(counts, n_tiles, n_blocks)
    xg = _dispatch(plan, x2, mod3, pos, n_blocks * EXPERT_ROWS, seq)
    yg = _expert_gemm(plan, xg, w_gu, w_down, layer)
    return _combine(plan, x2, mod3, pos.T, w_top.T, w_sh_gu, w_sh_down, ln_g, ln_b, yg, seq)


def kernel(x, c, ada_w, ada_b, pool_w_in, pool_w_grp, pool_scale, pool_w_out, attn_w_in, attn_w_out, ln1_g, ln1_b, router_w, router_b, exp_w_gu, exp_w_down, sh_w_gu, sh_w_down, ln2_g, ln2_b):
    batch, seq, d = x.shape
    depth = ada_w.shape[0]
    t = batch * seq
    mod = _modulation(c, ada_w, ada_b).reshape(depth, batch, 1, 6 * d)
    perms = jnp.stack([_perm_matrix(dil) for _, dil in ATTN_PATTERNS[1:]])
    perms_t = jnp.swapaxes(perms, 1, 2)
    expand = (jnp.arange(LSE_LANES)[:, None] == (jnp.arange(d)[None, :] // HEAD_DIM)).astype(BF16)
    part = (jnp.arange(attn_w_in.shape[2]) // d) % 3
    q_scale = jnp.where(part == 0, HEAD_DIM ** -0.5, 1.0).astype(F32)
    x2 = x.reshape(t, d)
    for i in range(depth):
        mod3 = mod[i]
        j = i // 2
        if i % 2 == 0:
            x2 = _pool_layer(x2, mod3, pool_w_in[j].astype(BF16), pool_w_grp[j].astype(BF16),
                             pool_scale[j], pool_w_out[j].astype(BF16), ln1_g[i], ln1_b[i], seq)
        else:
            qkv = _qkv_proj(x2, mod3, perms, (attn_w_in[j] * q_scale).astype(BF16), seq)
            res = [_attention_group(qkv, g, batch, seq) for g in range(len(ATTN_PATTERNS))]
            x2 = _attn_out(x2, mod3, [r[0] for r in res], [r[1] for r in res], perms_t, expand,
                           attn_w_out[j].astype(BF16), ln1_g[i], ln1_b[i], seq)
        x2 = _moe_layer(x2, mod3, router_w[i], router_b[i], exp_w_gu, exp_w_down, i,
                        sh_w_gu[i].astype(BF16), sh_w_down[i].astype(BF16),
                        ln2_g[i], ln2_b[i], seq)
    return x2.reshape(batch, seq, d)
```

```python
import functools
import math

import jax
import jax.numpy as jnp
from jax import lax
from jax.experimental import pallas as pl
from jax.experimental.pallas import tpu as pltpu

F32 = jnp.float32
BF16 = jnp.bfloat16
U32 = jnp.uint32
I32 = jnp.int32

POOL_WINDOWS = (2, 4, 8, 16)
ATTN_PATTERNS = ((128, 1), (512, 4), (2048, 16))
HEAD_DIM = 64
N_HEADS = 16
Q_BLOCK = 128
N_EXPERTS = 64
TOP_K = 8
N_EXPERT_GROUPS = 8
TOPK_GROUPS = 4
ROUTED_SCALE = 2.5
EXPERT_ROWS = 512
DEPTH = 4
ALPHA = (2 * DEPTH) ** 0.25
LN_EPS = 1e-5

PERM_TILE = 256
POOL_HALO = 16
TOK_TILE = 256
CHUNK = 8
LOCAL_ROWS = -(-(TOK_TILE * TOP_K + N_EXPERTS * (CHUNK - 1)) // 256) * 256
NO_SLOT = -64.0
POS_RADIX = 64.0
WAIT_GROUP = 32
GEMM_SUB_ROWS = 256
GEMM_DEPTH = 4
LSE_LANES = 128
VMEM_LIMIT = 56 * 1024 * 1024
NEG_BIG = -1e30


def _cparams(sem):
    return pltpu.CompilerParams(dimension_semantics=sem, vmem_limit_bytes=VMEM_LIMIT)


def _layer_norm(v, g, b):
    mu = jnp.mean(v, axis=-1, keepdims=True)
    c = v - mu
    var = jnp.mean(c * c, axis=-1, keepdims=True)
    return c * lax.rsqrt(var + LN_EPS) * g + b


def _silu(v):
    return v * (1.0 / (1.0 + jnp.exp(-v)))


def _pack_bf16_pairs(v):
    n = v.shape[1] // 2
    hi = lax.bitcast_convert_type(v[:, :n].astype(BF16).astype(F32), U32)
    lo = lax.bitcast_convert_type(v[:, n:].astype(BF16).astype(F32), U32)
    return hi | (lo >> 16)


def _unpack_bf16_pairs(p):
    hi = lax.bitcast_convert_type(p & jnp.uint32(0xFFFF0000), F32)
    lo = lax.bitcast_convert_type(p << 16, F32)
    return hi, lo


def _mod_kernel(c_ref, w_ref, b_ref, o_ref):
    cs = _silu(c_ref[...])
    o_ref[...] = jnp.dot(cs, w_ref[...], preferred_element_type=F32) + b_ref[...]


def _modulation(c, ada_w, ada_b):
    depth, d, n6 = ada_w.shape
    b = c.shape[0]
    tn = 1536
    return pl.pallas_call(
        _mod_kernel,
        grid=(depth, n6 // tn),
        in_specs=[
            pl.BlockSpec((b, d), lambda i, n: (0, 0)),
            pl.BlockSpec((None, d, tn), lambda i, n: (i, 0, n)),
            pl.BlockSpec((None, 1, tn), lambda i, n: (i, 0, n)),
        ],
        out_specs=pl.BlockSpec((None, b, tn), lambda i, n: (i, 0, n)),
        out_shape=jax.ShapeDtypeStruct((depth, b, n6), F32),
        compiler_params=_cparams(("arbitrary", "arbitrary")),
    )(c, ada_w, ada_b.reshape(depth, 1, n6))


def _mod_spec(chunk, d, tiles_per_batch):
    return pl.BlockSpec((None, 1, d), lambda *idx: (idx[0] // tiles_per_batch, 0, chunk))


def _pool_kernel(x_ref, sh_ref, sc_ref, g_ref, win_ref, wgrp_ref, cs_ref, wout_ref,
                 lng_ref, lnb_ref, o_ref, ext_ref, *, tiles_per_batch):
    tm, d = x_ref.shape
    s_idx = pl.program_id(0) % tiles_per_batch
    x = x_ref[...]
    h = (x * (1.0 + sc_ref[...]) + sh_ref[...]).astype(BF16)
    u = jnp.dot(h, win_ref[...], preferred_element_type=F32)

    @pl.when(s_idx == 0)
    def _():
        ext_ref[0:POOL_HALO, :] = jnp.zeros((POOL_HALO, d), F32)

    @pl.when(s_idx != 0)
    def _():
        ext_ref[0:POOL_HALO, :] = ext_ref[tm:tm + POOL_HALO, :]

    ext_ref[POOL_HALO:POOL_HALO + tm, :] = u

    pos = s_idx * tm + lax.broadcasted_iota(I32, (tm, 1), 0) + 1
    gc = d // len(POOL_WINDOWS)
    ys = []
    for g, w in enumerate(POOL_WINDOWS):
        cols = slice(g * gc, (g + 1) * gc)
        acc = u[:, cols]
        for j in range(1, w):
            acc = acc + ext_ref[POOL_HALO - j:POOL_HALO - j + tm, cols]
        cnt = jnp.minimum(pos, w).astype(F32)
        z = (acc / cnt - u[:, cols]).astype(BF16)
        ys.append(jnp.dot(z, wgrp_ref[g], preferred_element_type=F32))
    y = (jnp.concatenate(ys, axis=1) * cs_ref[...]).astype(BF16)
    out = jnp.dot(y, wout_ref[...], preferred_element_type=F32)
    v = ALPHA * x + (1.0 + g_ref[...]) * out
    o_ref[...] = _layer_norm(v, lng_ref[...], lnb_ref[...])


def _pool_layer(x2, mod3, w_in, w_grp, ch_scale, w_out, ln_g, ln_b, seq):
    t, d = x2.shape
    tm = 512
    tpb = seq // tm
    full = lambda shape: pl.BlockSpec(shape, lambda i: (0,) * len(shape))
    return pl.pallas_call(
        functools.partial(_pool_kernel, tiles_per_batch=tpb),
        grid=(t // tm,),
        in_specs=[
            pl.BlockSpec((tm, d), lambda i: (i, 0)),
            _mod_spec(0, d, tpb), _mod_spec(1, d, tpb), _mod_spec(2, d, tpb),
            full((d, d)), full(w_grp.shape), full((1, d)), full((d, d)),
            full((1, d)), full((1, d)),
        ],
        out_specs=pl.BlockSpec((tm, d), lambda i: (i, 0)),
        out_shape=jax.ShapeDtypeStruct((t, d), F32),
        scratch_shapes=[pltpu.VMEM((tm + POOL_HALO, d), F32)],
        compiler_params=_cparams(("arbitrary",)),
    )(x2, mod3, mod3, mod3, w_in, w_grp, ch_scale.reshape(1, d), w_out,
      ln_g.reshape(1, d), ln_b.reshape(1, d))


def _perm_matrix(dil):
    p = jnp.arange(PERM_TILE)
    chunk = PERM_TILE // dil
    src = (p % chunk) * dil + p // chunk
    return (src[:, None] == jnp.arange(PERM_TILE)[None, :]).astype(BF16)


def _qkv_kernel(x_ref, sh_ref, sc_ref, p_ref, w_ref, o_ref, h_ref):
    tm = x_ref.shape[0]
    g = pl.program_id(1)

    @pl.when(g == 0)
    def _():
        h = (x_ref[...] * (1.0 + sc_ref[...]) + sh_ref[...]).astype(BF16)
        h_ref[0] = h
        for gi in range(1, len(ATTN_PATTERNS)):
            for s in range(tm // PERM_TILE):
                rows = slice(s * PERM_TILE, (s + 1) * PERM_TILE)
                h_ref[gi, rows, :] = jnp.dot(
                    p_ref[gi - 1], h[rows, :], preferred_element_type=F32).astype(BF16)

    o_ref[...] = jnp.dot(h_ref[g], w_ref[...], preferred_element_type=F32).astype(BF16)


def _qkv_proj(x2, mod3, perms, w_in, seq):
    t, d = x2.shape
    ng = len(ATTN_PATTERNS)
    tm = 1024
    tpb = seq // tm
    return pl.pallas_call(
        _qkv_kernel,
        grid=(t // tm, ng),
        in_specs=[
            pl.BlockSpec((tm, d), lambda m, g: (m, 0)),
            _mod_spec(0, d, tpb), _mod_spec(1, d, tpb),
            pl.BlockSpec(perms.shape, lambda m, g: (0, 0, 0)),
            pl.BlockSpec((d, 3 * d), lambda m, g: (0, g)),
        ],
        out_specs=pl.BlockSpec((tm, 3 * d), lambda m, g: (m, g)),
        out_shape=jax.ShapeDtypeStruct((t, ng * 3 * d), BF16),
        scratch_shapes=[pltpu.VMEM((ng, tm, d), BF16)],
        compiler_params=_cparams(("arbitrary", "arbitrary")),
    )(x2, mod3, mod3, perms, w_in)


def _attn_kernel(q_ref, kp_ref, kc_ref, vp_ref, vc_ref, o_ref, lse_ref, *, group, dil):
    bq = Q_BLOCK
    d = N_HEADS * HEAD_DIM
    j = pl.program_id(2)
    q = q_ref[...].reshape(bq, d)
    kp = kp_ref[...].reshape(bq, d)
    kc = kc_ref[...].reshape(bq, d)
    vp = vp_ref[...].reshape(bq, d)
    vc = vc_ref[...].reshape(bq, d)

    qi = lax.broadcasted_iota(I32, (bq, 2 * bq), 0)
    kj = lax.broadcasted_iota(I32, (bq, 2 * bq), 1)
    dist = qi + bq - kj
    steps = ATTN_PATTERNS[group][0] // dil
    valid = (dist >= 0) & (dist <= steps) & ((kj >= bq) | (j > 0))
    neg_dist = jnp.where(valid, (dist * -dil).astype(F32), NEG_BIG)
    lane = lax.broadcasted_iota(I32, (bq, LSE_LANES), 1)
    lse_tile = jnp.zeros((bq, LSE_LANES), F32)
    n_tot = len(ATTN_PATTERNS) * N_HEADS
    pair = 2 * HEAD_DIM
    first_half = lax.broadcasted_iota(I32, (bq, pair), 1) < HEAD_DIM
    outs = []
    for hp in range(N_HEADS // 2):
        cols = slice(hp * pair, (hp + 1) * pair)
        q2 = q[:, cols]
        k2 = jnp.concatenate([kp[:, cols], kc[:, cols]], axis=0)
        v2 = jnp.concatenate([vp[:, cols], vc[:, cols]], axis=0)
        halves = []
        for sub in range(2):
            h = 2 * hp + sub
            slope = 2.0 ** (-8.0 * (group * N_HEADS + h + 1) / n_tot)
            mine = first_half if sub == 0 else jnp.logical_not(first_half)
            qh = jnp.where(mine, q2, jnp.zeros_like(q2))
            s = lax.dot_general(qh, k2, (((1,), (1,)), ((), ())), preferred_element_type=F32)
            s = s + slope * neg_dist
            m = jnp.max(s, axis=-1, keepdims=True)
            p = jnp.exp(s - m)
            den = jnp.sum(p, axis=-1, keepdims=True)
            halves.append(jnp.dot(p.astype(BF16), v2, preferred_element_type=F32) / den)
            lse_tile = jnp.where(lane == h, m + jnp.log(den), lse_tile)
        outs.append(jnp.where(first_half, halves[0], halves[1]))
    o_ref[...] = jnp.concatenate(outs, axis=1).astype(BF16).reshape(o_ref.shape)
    lse_ref[...] = lse_tile.reshape(lse_ref.shape)


def _attention_group(qkv, group, batch, seq):
    dil = ATTN_PATTERNS[group][1]
    d = N_HEADS * HEAD_DIM
    t = qkv.shape[0]
    sub = seq // dil
    nb = sub // Q_BLOCK
    rows = Q_BLOCK if dil == 1 else PERM_TILE // dil
    chunks = Q_BLOCK // rows
    u = seq // (rows * dil)
    view = lambda a, c: a.reshape(batch, u, dil, rows, c)
    blk = lambda c: (None, chunks, None, rows, c)
    col0 = group * 3
    q_spec = pl.BlockSpec(blk(d), lambda b, r, j: (b, j, r, 0, col0))
    kc_spec = pl.BlockSpec(blk(d), lambda b, r, j: (b, j, r, 0, col0 + 1))
    kp_spec = pl.BlockSpec(blk(d), lambda b, r, j: (b, jnp.maximum(j - 1, 0), r, 0, col0 + 1))
    vc_spec = pl.BlockSpec(blk(d), lambda b, r, j: (b, j, r, 0, col0 + 2))
    vp_spec = pl.BlockSpec(blk(d), lambda b, r, j: (b, jnp.maximum(j - 1, 0), r, 0, col0 + 2))
    qkv5 = view(qkv, qkv.shape[1])
    o, lse = pl.pallas_call(
        functools.partial(_attn_kernel, group=group, dil=dil),
        grid=(batch, dil, nb),
        in_specs=[q_spec, kp_spec, kc_spec, vp_spec, vc_spec],
        out_specs=[
            pl.BlockSpec(blk(d), lambda b, r, j: (b, j, r, 0, 0)),
            pl.BlockSpec(blk(LSE_LANES), lambda b, r, j: (b, j, r, 0, 0)),
        ],
        out_shape=[
            jax.ShapeDtypeStruct((batch, u, dil, rows, d), BF16),
            jax.ShapeDtypeStruct((batch, u, dil, rows, LSE_LANES), F32),
        ],
        compiler_params=_cparams(("arbitrary", "arbitrary", "arbitrary")),
    )(qkv5, qkv5, qkv5, qkv5, qkv5)
    return o.reshape(t, d), lse.reshape(t, LSE_LANES)


def _split3(v):
    a = v.astype(BF16)
    r = v - a.astype(F32)
    b = r.astype(BF16)
    c = (r - b.astype(F32)).astype(BF16)
    return a, b, c


def _attn_out_kernel(x_ref, g_ref, o0_ref, o1_ref, o2_ref, l0_ref, l1_ref, l2_ref,
                     pt_ref, e_ref, wout_ref, lng_ref, lnb_ref, out_ref):
    tm, d = x_ref.shape
    o_refs = (o0_ref, o1_ref, o2_ref)
    l_refs = (l0_ref, l1_ref, l2_ref)
    n_sub = tm // PERM_TILE

    def unperm(gi, val_bf16):
        if gi == 0:
            return val_bf16.astype(F32)
        parts = [jnp.dot(pt_ref[gi - 1], val_bf16[s * PERM_TILE:(s + 1) * PERM_TILE, :],
                         preferred_element_type=F32) for s in range(n_sub)]
        return jnp.concatenate(parts, axis=0)

    lses = []
    for gi in range(3):
        l = l_refs[gi][...]
        if gi == 0:
            lses.append(l)
        else:
            a, b, c = _split3(l)
            lses.append(unperm(gi, a) + unperm(gi, b) + unperm(gi, c))
    mx = jnp.maximum(jnp.maximum(lses[0], lses[1]), lses[2])
    es = [jnp.exp(l - mx) for l in lses]
    tot = es[0] + es[1] + es[2]
    mixed = jnp.zeros((tm, d), F32)
    for gi in range(3):
        w = es[gi] / tot
        a, b, _ = _split3(w)
        wide = (jnp.dot(a, e_ref[...], preferred_element_type=F32)
                + jnp.dot(b, e_ref[...], preferred_element_type=F32))
        mixed = mixed + wide * unperm(gi, o_refs[gi][...])
    y = jnp.dot(mixed.astype(BF16), wout_ref[...], preferred_element_type=F32)
    v = ALPHA * x_ref[...] + (1.0 + g_ref[...]) * y
    out_ref[...] = _layer_norm(v, lng_ref[...], lnb_ref[...])


def _attn_out(x2, mod3, os_, lses, perms_t, expand, w_out, ln_g, ln_b, seq):
    t, d = x2.shape
    tm = 512
    tpb = seq // tm
    row = lambda c: pl.BlockSpec((tm, c), lambda i: (i, 0))
    full = lambda shape: pl.BlockSpec(shape, lambda i: (0,) * len(shape))
    return pl.pallas_call(
        _attn_out_kernel,
        grid=(t // tm,),
        in_specs=[row(d), _mod_spec(2, d, tpb), row(d), row(d), row(d),
                  row(LSE_LANES), row(LSE_LANES), row(LSE_LANES),
                  full(perms_t.shape), full(expand.shape), full((d, d)),
                  full((1, d)), full((1, d))],
        out_specs=row(d),
        out_shape=jax.ShapeDtypeStruct((t, d), F32),
        compiler_params=_cparams(("arbitrary",)),
    )(x2, mod3, *os_, *lses, perms_t, expand, w_out, ln_g.reshape(1, d), ln_b.reshape(1, d))


def _first_index_of_max(v, iota, size):
    m = jnp.max(v, axis=0, keepdims=True)
    idx = jnp.min(jnp.where(v == m, iota, float(size)), axis=0, keepdims=True)
    return m, idx


def _topk_kernel(x_ref, sh_ref, sc_ref, wr_ref, b_ref, tri_ref, ltri_ref,
                 w_ref, pos_ref, cnt_ref):
    tr = x_ref.shape[0]
    ne = b_ref.shape[0]
    gsz = ne // N_EXPERT_GROUPS
    tile = pl.program_id(0)

    hf = x_ref[...] * (1.0 + sc_ref[...]) + sh_ref[...]
    hf_hi = hf.astype(BF16)
    hf_lo = (hf - hf_hi.astype(F32)).astype(BF16)
    nt = (((1,), (1,)), ((), ()))
    by_hi = lax.dot_general(wr_ref[...], hf_hi, nt, preferred_element_type=F32)
    logits = (by_hi[:ne, :] + by_hi[ne:, :]
              + lax.dot_general(wr_ref[0:ne, :], hf_lo, nt, preferred_element_type=F32))
    scores = 1.0 / (1.0 + jnp.exp(-logits))
    sel = scores + b_ref[...]
    iota_g = lax.broadcasted_iota(I32, (gsz, tr), 0).astype(F32)
    iota_n = lax.broadcasted_iota(I32, (N_EXPERT_GROUPS, tr), 0).astype(F32)
    gs = jnp.zeros((N_EXPERT_GROUPS, tr), F32)
    for g in range(N_EXPERT_GROUPS):
        blk = sel[g * gsz:(g + 1) * gsz, :]
        m1, i1 = _first_index_of_max(blk, iota_g, gsz)
        m2 = jnp.max(jnp.where(iota_g == i1, -jnp.inf, blk), axis=0, keepdims=True)
        gs = jnp.where(iota_n == float(g), m1 + m2, gs)
    gmask = jnp.zeros((N_EXPERT_GROUPS, tr), F32)
    for _ in range(TOPK_GROUPS):
        _, gi = _first_index_of_max(gs, iota_n, N_EXPERT_GROUPS)
        hit = iota_n == gi
        gmask = jnp.where(hit, 1.0, gmask)
        gs = jnp.where(hit, -jnp.inf, gs)
    masked_rows = []
    for g in range(N_EXPERT_GROUPS):
        keep = jnp.broadcast_to(gmask[g:g + 1, :], (gsz, tr)) > 0.5
        masked_rows.append(jnp.where(keep, sel[g * gsz:(g + 1) * gsz, :], -jnp.inf))
    cur = jnp.concatenate(masked_rows, axis=0)
    iota_e = lax.broadcasted_iota(I32, (ne, tr), 0).astype(F32)
    chosen = jnp.zeros((ne, tr), F32)
    for _ in range(TOP_K):
        _, ei = _first_index_of_max(cur, iota_e, ne)
        hit = iota_e == ei
        cur = jnp.where(hit, -jnp.inf, cur)
        chosen = jnp.where(hit, 1.0, chosen)
    picked = chosen > 0.5
    top_scores = jnp.where(picked, scores, 0.0)
    wsum = jnp.sum(top_scores, axis=0, keepdims=True)
    w_ref[...] = top_scores / wsum * ROUTED_SCALE
    before = jnp.dot(chosen.astype(BF16), tri_ref[...], preferred_element_type=F32)
    n = jnp.sum(chosen, axis=1, keepdims=True)
    n_chunks = jnp.floor((n + (CHUNK - 1)) * (1.0 / CHUNK))
    run_off = jnp.dot(ltri_ref[...], jnp.broadcast_to(n_chunks, (ne, 128)).astype(BF16),
                      preferred_element_type=F32)[:, 0:1] * CHUNK
    pos_ref[...] = jnp.where(picked, before + run_off, NO_SLOT)

    @pl.when(tile == 0)
    def _():
        cnt_ref[...] = jnp.zeros_like(cnt_ref)

    lane = lax.broadcasted_iota(I32, cnt_ref.shape, 1)
    cnt_ref[...] = jnp.where(lane == tile, jnp.broadcast_to(n, cnt_ref.shape).astype(I32),
                             cnt_ref[...])


def _topk_route(x2, mod3, w_router, router_b, seq):
    t, d = x2.shape
    ne = w_router.shape[1]
    tr = TOK_TILE
    tpb = seq // tr
    assert t // tr <= 128
    w_t = w_router.T
    w_hi = w_t.astype(BF16)
    w_split = jnp.concatenate([w_hi, (w_t - w_hi.astype(F32)).astype(BF16)], axis=0)
    tri = (jnp.arange(tr)[:, None] < jnp.arange(tr)[None, :]).astype(BF16)
    ltri = (jnp.arange(ne)[None, :] < jnp.arange(ne)[:, None]).astype(BF16)
    out = lambda dt: jax.ShapeDtypeStruct((ne, t), dt)
    row = pl.BlockSpec((ne, tr), lambda i: (0, i))
    return pl.pallas_call(
        _topk_kernel,
        grid=(t // tr,),
        in_specs=[pl.BlockSpec((tr, d), lambda i: (i, 0)),
                  _mod_spec(3, d, tpb), _mod_spec(4, d, tpb),
                  pl.BlockSpec((2 * ne, d), lambda i: (0, 0)),
                  pl.BlockSpec((ne, 1), lambda i: (0, 0)),
                  pl.BlockSpec((tr, tr), lambda i: (0, 0)),
                  pl.BlockSpec((ne, ne), lambda i: (0, 0))],
        out_specs=[row, row, pl.BlockSpec((ne, 128), lambda i: (0, 0))],
        out_shape=[out(F32), out(F32), jax.ShapeDtypeStruct((ne, 128), I32)],
        compiler_params=_cparams(("arbitrary",)),
    )(x2, mod3, mod3, w_split, router_b.reshape(ne, 1), tri, ltri)


def _slot_plan(counts, n_tiles, n_blocks):
    n = counts[:, :n_tiles]
    nch = (n + (CHUNK - 1)) // CHUNK
    rows = jnp.sum(nch, axis=1) * CHUNK
    region = ((rows + EXPERT_ROWS - 1) // EXPERT_ROWS) * EXPERT_ROWS
    region_end = jnp.cumsum(region)
    region_start = region_end - region
    run_chunk = region_start[:, None] // CHUNK + jnp.cumsum(nch, axis=1) - nch
    local_end = jnp.cumsum(nch, axis=0)
    local_chunk = local_end - nch
    c = jnp.arange(LOCAL_ROWS // CHUNK, dtype=I32)
    expert_of_c = jnp.sum((local_end[:, :, None] <= c[None, None, :]).astype(I32), axis=0)
    owner = expert_of_c[None] == jnp.arange(N_EXPERTS, dtype=I32)[:, None, None]
    slot_chunk = jnp.sum(jnp.where(owner, (run_chunk - local_chunk)[:, :, None], 0), axis=0) + c
    block_row = jnp.arange(n_blocks, dtype=I32) * EXPERT_ROWS
    group_e = jnp.minimum(jnp.sum((region_end[None, :] <= block_row[:, None]).astype(I32), axis=1),
                          N_EXPERTS - 1)
    n_used = region_end[-1] // EXPERT_ROWS
    nonempty = region > 0
    run_of_expert = jnp.cumsum(nonempty.astype(I32)) - 1
    experts = jnp.arange(N_EXPERTS, dtype=I32)
    run_expert = jnp.sum(jnp.where((run_of_expert[None, :] == experts[:, None]) & nonempty[None, :],
                                   experts[None, :], 0), axis=1)
    blocks = jnp.arange(n_blocks, dtype=I32)
    first = ((blocks == 0) | (group_e != jnp.roll(group_e, 1))) & (blocks < n_used)
    per_tile = lambda a: (a * CHUNK).T.astype(I32)
    return dict(
        run_begin_rows=per_tile(local_chunk)[:, None, :], run_end_rows=per_tile(local_end)[:, None, :],
        run_begin_cols=per_tile(local_chunk)[:, :, None], run_end_cols=per_tile(local_end)[:, :, None],
        block_run=(jnp.cumsum(first.astype(I32)) - 1).astype(I32), block_first=first.astype(I32),
        run_expert=run_expert.astype(I32), n_runs=jnp.sum(nonempty.astype(I32)).reshape(1),
        slot_chunk=slot_chunk.reshape(-1).astype(I32),
        tile_chunks=jnp.sum(nch, axis=0).astype(I32),
        tail_chunk=((region_start + rows) // CHUNK).astype(I32),
        tail_chunks=((region - rows) // CHUNK).astype(I32),
        group_e=group_e, n_used=(region_end[-1:] // EXPERT_ROWS).astype(I32))


def _chunk_rows(ref, chunk_index):
    start = chunk_index * CHUNK
    if not isinstance(start, int):
        start = pl.multiple_of(start, CHUNK)
    return ref.at[pl.ds(start, CHUNK), :]


def _pos_digits(pos):
    hi = jnp.floor(pos * (1.0 / POS_RADIX))
    return hi.astype(BF16), (pos - POS_RADIX * hi).astype(BF16)


def _for_chunks_two_queues(count, start):
    def pair(p, carry):
        start(2 * p, 0)
        start(2 * p + 1, 1)
        return carry

    lax.fori_loop(0, lax.shift_right_logical(count, 1), pair, 0)

    @pl.when((count & 1) == 1)
    def _():
        start(count - 1, 0)


def _wait_chunks(count, src_ref, dst_ref, sem):
    group_rows = pl.ds(0, WAIT_GROUP * CHUNK)

    def wait_group(c, carry):
        pltpu.make_async_copy(src_ref.at[group_rows, :], dst_ref.at[group_rows, :], sem).wait()
        return carry

    def wait_one(c, carry):
        pltpu.make_async_copy(_chunk_rows(src_ref, 0), _chunk_rows(dst_ref, 0), sem).wait()
        return carry

    lax.fori_loop(0, lax.shift_right_logical(count, WAIT_GROUP.bit_length() - 1), wait_group, 0)
    lax.fori_loop(0, count & (WAIT_GROUP - 1), wait_one, 0)


def _dispatch_kernel(slot_ref, tile_ref, tail_ref, tailn_ref, used_ref,
                     x_ref, sh_ref, sc_ref, pos_ref, begin_ref, end_ref,
                     xg_hbm, stage_ref, zero_ref, sems, sem_blk):
    tr = x_ref.shape[0]
    i = pl.program_id(0)
    n_blocks = xg_hbm.shape[0] // EXPERT_ROWS
    chunks_per_tile = LOCAL_ROWS // CHUNK

    def drain(count, sem):
        _wait_chunks(count, stage_ref.at[0], xg_hbm, sem)

    @pl.when(i == 0)
    def _():
        zero_ref[...] = jnp.zeros_like(zero_ref)

        def block_copy(j):
            rows = pl.ds(pl.multiple_of(j * EXPERT_ROWS, EXPERT_ROWS), EXPERT_ROWS)
            return pltpu.make_async_copy(zero_ref, xg_hbm.at[rows, :], sem_blk)

        def start_block(j, carry):
            block_copy(j).start()
            return carry

        def wait_block(j, carry):
            block_copy(j).wait()
            return carry

        lax.fori_loop(used_ref[0], n_blocks, start_block, 0)

        def per_expert(e, total):
            def per_chunk(c, carry):
                pltpu.make_async_copy(_chunk_rows(zero_ref, 0),
                                      _chunk_rows(xg_hbm, tail_ref[e] + c), sems.at[0]).start()
                return carry
            lax.fori_loop(0, tailn_ref[e], per_chunk, 0)
            return total + tailn_ref[e]

        drain(lax.fori_loop(0, N_EXPERTS, per_expert, 0), sems.at[0])
        lax.fori_loop(used_ref[0], n_blocks, wait_block, 0)

    hf = (x_ref[...] * (1.0 + sc_ref[...]) + sh_ref[...]).astype(BF16)
    row_e = lax.broadcasted_iota(I32, (LOCAL_ROWS, N_EXPERTS), 0)
    owner = jnp.where((row_e >= begin_ref[...]) & (row_e < end_ref[...]), 1.0, 0.0).astype(BF16)
    hi_digit, lo_digit = _pos_digits(pos_ref[...])
    slot_pos = (POS_RADIX * jnp.dot(owner, hi_digit, preferred_element_type=F32)
                + jnp.dot(owner, lo_digit, preferred_element_type=F32))
    row_t = lax.broadcasted_iota(I32, (LOCAL_ROWS, tr), 0).astype(F32)
    perm = jnp.where(slot_pos == row_t, 1.0, 0.0).astype(BF16)
    rows = jnp.dot(perm, hf, preferred_element_type=F32)
    half = rows.shape[1] // 2
    buf = stage_ref.at[i % 2]
    buf[...] = (lax.bitcast_convert_type(rows[:, :half], U32)
                | (lax.bitcast_convert_type(rows[:, half:], U32) >> 16))

    def send(c, priority):
        pltpu.make_async_copy(_chunk_rows(buf, c),
                              _chunk_rows(xg_hbm, slot_ref[i * chunks_per_tile + c]),
                              sems.at[i % 2]).start(priority=priority)

    _for_chunks_two_queues(tile_ref[i], send)

    @pl.when(i > 0)
    def _():
        drain(tile_ref[jnp.maximum(i - 1, 0)], sems.at[(i + 1) % 2])

    @pl.when(i == pl.num_programs(0) - 1)
    def _():
        drain(tile_ref[i], sems.at[i % 2])


def _dispatch(plan, x2, mod3, pos, n_slots, seq):
    t, d = x2.shape
    tr = TOK_TILE
    tpb = seq // tr
    return pl.pallas_call(
        _dispatch_kernel,
        grid_spec=pltpu.PrefetchScalarGridSpec(
            num_scalar_prefetch=5,
            grid=(t // tr,),
            in_specs=[pl.BlockSpec((tr, d), lambda i, *_: (i, 0)),
                      _mod_spec(3, d, tpb), _mod_spec(4, d, tpb),
                      pl.BlockSpec((N_EXPERTS, tr), lambda i, *_: (0, i)),
                      pl.BlockSpec((None, 1, N_EXPERTS), lambda i, *_: (i, 0, 0)),
                      pl.BlockSpec((None, 1, N_EXPERTS), lambda i, *_: (i, 0, 0))],
            out_specs=pl.BlockSpec(memory_space=pl.ANY),
            scratch_shapes=[pltpu.VMEM((2, LOCAL_ROWS, d // 2), U32),
                            pltpu.VMEM((EXPERT_ROWS, d // 2), U32),
                            pltpu.SemaphoreType.DMA((2,)), pltpu.SemaphoreType.DMA],
        ),
        out_shape=jax.ShapeDtypeStruct((n_slots, d // 2), U32),
        compiler_params=_cparams(("arbitrary",)),
    )(plan["slot_chunk"], plan["tile_chunks"], plan["tail_chunk"], plan["tail_chunks"],
      plan["n_used"], x2, mod3, mod3, pos, plan["run_begin_rows"], plan["run_end_rows"])


def _expert_kernel(run_ref, first_ref, rexp_ref, nrun_ref, nu_ref,
                   xg_hbm, wgu_hbm, wdn_hbm, yg_hbm,
                   xbuf, ybuf, wgu_f32, wdn_f32, wgu_bf, wdn_bf, x_sem, y_sem, w_sem, z_sem,
                   *, layer):
    n_used = nu_ref[0]
    n_blocks = yg_hbm.shape[0] // EXPERT_ROWS
    depth, _, half = xbuf.shape
    f = wdn_bf.shape[0]

    def rows(j):
        return pl.ds(pl.multiple_of(j * EXPERT_ROWS, EXPERT_ROWS), EXPERT_ROWS)

    def x_copy(j):
        return pltpu.make_async_copy(xg_hbm.at[rows(j), :], xbuf.at[j % depth],
                                     x_sem.at[j % depth])

    def y_copy(j):
        return pltpu.make_async_copy(ybuf.at[j % depth], yg_hbm.at[rows(j), :],
                                     y_sem.at[j % depth])

    def w_copies(r):
        e = rexp_ref[r]
        return (pltpu.make_async_copy(wgu_hbm.at[layer, e], wgu_f32.at[r % 2], w_sem.at[r % 2]),
                pltpu.make_async_copy(wdn_hbm.at[layer, e], wdn_f32.at[r % 2], w_sem.at[r % 2]))

    def zero_copy(j):
        return pltpu.make_async_copy(ybuf.at[0], yg_hbm.at[rows(j), :], z_sem)

    ybuf[0] = jnp.zeros(ybuf.shape[1:], U32)

    def zero_start(j, carry):
        zero_copy(j).start()
        return carry

    def zero_wait(j, carry):
        zero_copy(j).wait()
        return carry

    lax.fori_loop(n_used, n_blocks, zero_start, 0)
    lax.fori_loop(n_used, n_blocks, zero_wait, 0)

    for cp in w_copies(0):
        cp.start()
    for a in range(depth - 1):
        @pl.when(a < n_used)
        def _():
            x_copy(a).start()

    def block(j, carry):
        r = run_ref[j]

        @pl.when(first_ref[j] == 1)
        def _():
            for cp in w_copies(r):
                cp.wait()

            @pl.when(r + 1 < nrun_ref[0])
            def _():
                for cp in w_copies(r + 1):
                    cp.start()

            wgu_bf[...] = wgu_f32[r % 2].astype(BF16)
            wdn_bf[...] = wdn_f32[r % 2].astype(BF16)

        x_copy(j).wait()

        @pl.when(j + (depth - 1) < n_used)
        def _():
            x_copy(j + (depth - 1)).start()

        @pl.when(j >= depth)
        def _():
            y_copy(j - depth).wait()

        for g in range(EXPERT_ROWS // GEMM_SUB_ROWS):
            rws = slice(g * GEMM_SUB_ROWS, (g + 1) * GEMM_SUB_ROWS)
            hi, lo = _unpack_bf16_pairs(xbuf[j % depth, rws, :])
            gu = (jnp.dot(hi.astype(BF16), wgu_bf[0:half, :], preferred_element_type=F32)
                  + jnp.dot(lo.astype(BF16), wgu_bf[half:, :], preferred_element_type=F32))
            act = (_silu(gu[:, :f]) * gu[:, f:]).astype(BF16)
            ybuf[j % depth, rws, :] = _pack_bf16_pairs(
                jnp.dot(act, wdn_bf[...], preferred_element_type=F32))
        y_copy(j).start()
        return carry

    lax.fori_loop(0, n_used, block, 0)

    for a in range(depth, 0, -1):
        @pl.when(n_used >= a)
        def _():
            y_copy(n_used - a).wait()


def _expert_gemm(plan, xg, w_gu, w_down, layer):
    n_slots, half = xg.shape
    _, ne, d, f2 = w_gu.shape
    f = w_down.shape[2]
    hbm = pl.BlockSpec(memory_space=pl.ANY)
    return pl.pallas_call(
        functools.partial(_expert_kernel, layer=layer),
        grid_spec=pltpu.PrefetchScalarGridSpec(
            num_scalar_prefetch=5,
            grid=(1,),
            in_specs=[hbm, hbm, hbm],
            out_specs=hbm,
            scratch_shapes=[pltpu.VMEM((GEMM_DEPTH, EXPERT_ROWS, half), U32),
                            pltpu.VMEM((GEMM_DEPTH, EXPERT_ROWS, half), U32),
                            pltpu.VMEM((2, d, f2), F32), pltpu.VMEM((2, f, d), F32),
                            pltpu.VMEM((d, f2), BF16), pltpu.VMEM((f, d), BF16),
                            pltpu.SemaphoreType.DMA((GEMM_DEPTH,)),
                            pltpu.SemaphoreType.DMA((GEMM_DEPTH,)),
                            pltpu.SemaphoreType.DMA((2,)), pltpu.SemaphoreType.DMA],
        ),
        out_shape=jax.ShapeDtypeStruct((n_slots, half), U32),
        compiler_params=_cparams(("arbitrary",)),
    )(plan["block_run"], plan["block_first"], plan["run_expert"], plan["n_runs"], plan["n_used"],
      xg, w_gu, w_down)


def _combine_kernel(slot_ref, tile_ref,
                    x_ref, sh_ref, sc_ref, g_ref, pos_ref, wt_ref, begin_ref, end_ref,
                    wsgu_ref, wsdn_ref, lng_ref, lnb_ref, yg_hbm, o_ref, stage_ref, sems):
    tr, d = x_ref.shape
    i = pl.program_id(0)
    n_tiles = pl.num_programs(0)
    chunks_per_tile = LOCAL_ROWS // CHUNK

    def fetch(tile, buffer):
        def start(c, priority):
            pltpu.make_async_copy(_chunk_rows(yg_hbm, slot_ref[tile * chunks_per_tile + c]),
                                  _chunk_rows(stage_ref.at[buffer], c),
                                  sems.at[buffer]).start(priority=priority)
        _for_chunks_two_queues(tile_ref[tile], start)

    @pl.when(i == 0)
    def _():
        stage_ref[...] = jnp.zeros_like(stage_ref)
        fetch(0, 0)

    nxt = jnp.minimum(i + 1, n_tiles - 1)
    for buffer in range(2):
        @pl.when((i + 1 < n_tiles) & (nxt % 2 == buffer))
        def _():
            fetch(nxt, buffer)

    x = x_ref[...]
    hf = (x * (1.0 + sc_ref[...]) + sh_ref[...]).astype(BF16)
    f = wsdn_ref.shape[0]
    su = jnp.dot(hf, wsgu_ref[...], preferred_element_type=F32)
    act = (_silu(su[:, :f]) * su[:, f:]).astype(BF16)
    shared = jnp.dot(act, wsdn_ref[...], preferred_element_type=F32)

    col_e = lax.broadcasted_iota(I32, (N_EXPERTS, LOCAL_ROWS), 1)
    owner = jnp.where((col_e >= begin_ref[...]) & (col_e < end_ref[...]), 1.0, 0.0).astype(BF16)
    hi_digit, lo_digit = _pos_digits(pos_ref[...])
    slot_pos = (POS_RADIX * jnp.dot(hi_digit, owner, preferred_element_type=F32)
                + jnp.dot(lo_digit, owner, preferred_element_type=F32))
    holds = slot_pos == lax.broadcasted_iota(I32, (tr, LOCAL_ROWS), 1).astype(F32)
    weights = jnp.where(holds, jnp.dot(wt_ref[...].astype(BF16), owner,
                                       preferred_element_type=F32), 0.0).astype(BF16)

    buf = stage_ref.at[i % 2]
    _wait_chunks(tile_ref[i], yg_hbm, buf, sems.at[i % 2])

    y_hi, y_lo = _unpack_bf16_pairs(buf[...])
    y = jnp.concatenate([y_hi.astype(BF16), y_lo.astype(BF16)], axis=1)
    routed = jnp.dot(weights, y, preferred_element_type=F32)
    v = ALPHA * x + (1.0 + g_ref[...]) * (routed + shared)
    o_ref[...] = _layer_norm(v, lng_ref[...], lnb_ref[...])


def _combine(plan, x2, mod3, pos_tok, w_tok, w_sh_gu, w_sh_down, ln_g, ln_b, yg, seq):
    t, d = x2.shape
    tr = TOK_TILE
    tpb = seq // tr
    full = lambda shape: pl.BlockSpec(shape, lambda i, *_: (0,) * len(shape))
    return pl.pallas_call(
        _combine_kernel,
        grid_spec=pltpu.PrefetchScalarGridSpec(
            num_scalar_prefetch=2,
            grid=(t // tr,),
            in_specs=[pl.BlockSpec((tr, d), lambda i, *_: (i, 0)),
                      _mod_spec(3, d, tpb), _mod_spec(4, d, tpb), _mod_spec(5, d, tpb),
                      pl.BlockSpec((tr, N_EXPERTS), lambda i, *_: (i, 0)),
                      pl.BlockSpec((tr, N_EXPERTS), lambda i, *_: (i, 0)),
                      pl.BlockSpec((None, N_EXPERTS, 1), lambda i, *_: (i, 0, 0)),
                      pl.BlockSpec((None, N_EXPERTS, 1), lambda i, *_: (i, 0, 0)),
                      full(w_sh_gu.shape), full(w_sh_down.shape), full((1, d)), full((1, d)),
                      pl.BlockSpec(memory_space=pl.ANY)],
            out_specs=pl.BlockSpec((tr, d), lambda i, *_: (i, 0)),
            scratch_shapes=[pltpu.VMEM((2, LOCAL_ROWS, d // 2), U32),
                            pltpu.SemaphoreType.DMA((2,))],
        ),
        out_shape=jax.ShapeDtypeStruct((t, d), F32),
        compiler_params=_cparams(("arbitrary",)),
    )(plan["slot_chunk"], plan["tile_chunks"],
      x2, mod3, mod3, mod3, pos_tok, w_tok, plan["run_begin_cols"], plan["run_end_cols"],
      w_sh_gu, w_sh_down, ln_g.reshape(1, d), ln_b.reshape(1, d), yg)


def _moe_layer(x2, mod3, w_router, router_b, w_gu, w_down, layer, w_sh_gu, w_sh_down,
               ln_g, ln_b, seq):
    t, d = x2.shape
    n_tiles = t // TOK_TILE
    w_top, pos, counts = _topk_route(x2, mod3, w_router, router_b, seq)
    bound = t * TOP_K + n_tiles * N_EXPERTS * (CHUNK - 1) + N_EXPERTS * (EXPERT_ROWS - 1)
    n_blocks = -(-bound // EXPERT_ROWS)
    plan = _slot_plan(counts, n_tiles, n_blocks)
    xg = _dispatch(plan, x2, mod3, pos, n_blocks * EXPERT_ROWS, seq)
    yg = _expert_gemm(plan, xg, w_gu, w_down, layer)
    return _combine(plan, x2, mod3, pos.T, w_top.T, w_sh_gu, w_sh_down, ln_g, ln_b, yg, seq)


def kernel(x, c, ada_w, ada_b, pool_w_in, pool_w_grp, pool_scale, pool_w_out, attn_w_in, attn_w_out, ln1_g, ln1_b, router_w, router_b, exp_w_gu, exp_w_down, sh_w_gu, sh_w_down, ln2_g, ln2_b):
    batch, seq, d = x.shape
    depth = ada_w.shape[0]
    t = batch * seq
    mod = _modulation(c, ada_w, ada_b).reshape(depth, batch, 1, 6 * d)
    perms = jnp.stack([_perm_matrix(dil) for _, dil in ATTN_PATTERNS[1:]])
    perms_t = jnp.swapaxes(perms, 1, 2)
    expand = (jnp.arange(LSE_LANES)[:, None] == (jnp.arange(d)[None, :] // HEAD_DIM)).astype(BF16)
    part = (jnp.arange(attn_w_in.shape[2]) // d) % 3
    q_scale = jnp.where(part == 0, HEAD_DIM ** -0.5, 1.0).astype(F32)
    x2 = x.reshape(t, d)
    for i in range(depth):
        mod3 = mod[i]
        j = i // 2
        if i % 2 == 0:
            x2 = _pool_layer(x2, mod3, pool_w_in[j].astype(BF16), pool_w_grp[j].astype(BF16),
                             pool_scale[j], pool_w_out[j].astype(BF16), ln1_g[i], ln1_b[i], seq)
        else:
            qkv = _qkv_proj(x2, mod3, perms, (attn_w_in[j] * q_scale).astype(BF16), seq)
            res = [_attention_group(qkv, g, batch, seq) for g in range(len(ATTN_PATTERNS))]
            x2 = _attn_out(x2, mod3, [r[0] for r in res], [r[1] for r in res], perms_t, expand,
                           attn_w_out[j].astype(BF16), ln1_g[i], ln1_b[i], seq)
        x2 = _moe_layer(x2, mod3, router_w[i], router_b[i], exp_w_gu, exp_w_down, i,
                        sh_w_gu[i].astype(BF16), sh_w_down[i].astype(BF16),
                        ln2_g[i], ln2_b[i], seq)
    return x2.reshape(batch, seq, d)
```

```python
import functools
import math

import jax
import jax.numpy as jnp
from jax import lax
from jax.experimental import pallas as pl
from jax.experimental.pallas import tpu as pltpu

F32 = jnp.float32
BF16 = jnp.bfloat16
U32 = jnp.uint32
I32 = jnp.int32

POOL_WINDOWS = (2, 4, 8, 16)
ATTN_PATTERNS = ((128, 1), (512, 4), (2048, 16))
HEAD_DIM = 64
N_HEADS = 16
Q_BLOCK = 128
N_EXPERTS = 64
TOP_K = 8
N_EXPERT_GROUPS = 8
TOPK_GROUPS = 4
ROUTED_SCALE = 2.5
EXPERT_ROWS = 512
DEPTH = 4
ALPHA = (2 * DEPTH) ** 0.25
LN_EPS = 1e-5

PERM_TILE = 256
POOL_HALO = 16
TOK_TILE = 256
CHUNK = 8
LOCAL_ROWS = -(-(TOK_TILE * TOP_K + N_EXPERTS * (CHUNK - 1)) // 256) * 256
NO_SLOT = -64.0
POS_RADIX = 64.0
WAIT_GROUP = 32
GEMM_SUB_ROWS = 256
GEMM_DEPTH = 6
LSE_LANES = 128
VMEM_LIMIT = 56 * 1024 * 1024
NEG_BIG = -1e30


def _cparams(sem):
    return pltpu.CompilerParams(dimension_semantics=sem, vmem_limit_bytes=VMEM_LIMIT)


def _layer_norm(v, g, b):
    mu = jnp.mean(v, axis=-1, keepdims=True)
    c = v - mu
    var = jnp.mean(c * c, axis=-1, keepdims=True)
    return c * lax.rsqrt(var + LN_EPS) * g + b


def _silu(v):
    return v * (1.0 / (1.0 + jnp.exp(-v)))


def _pack_bf16_pairs(v):
    n = v.shape[1] // 2
    hi = lax.bitcast_convert_type(v[:, :n].astype(BF16).astype(F32), U32)
    lo = lax.bitcast_convert_type(v[:, n:].astype(BF16).astype(F32), U32)
    return hi | (lo >> 16)


def _unpack_bf16_pairs(p):
    hi = lax.bitcast_convert_type(p & jnp.uint32(0xFFFF0000), F32)
    lo = lax.bitcast_convert_type(p << 16, F32)
    return hi, lo


def _mod_kernel(c_ref, w_ref, b_ref, o_ref):
    cs = _silu(c_ref[...])
    o_ref[...] = jnp.dot(cs, w_ref[...], preferred_element_type=F32) + b_ref[...]


def _modulation(c, ada_w, ada_b):
    depth, d, n6 = ada_w.shape
    b = c.shape[0]
    tn = 1536
    return pl.pallas_call(
        _mod_kernel,
        grid=(depth, n6 // tn),
        in_specs=[
            pl.BlockSpec((b, d), lambda i, n: (0, 0)),
            pl.BlockSpec((None, d, tn), lambda i, n: (i, 0, n)),
            pl.BlockSpec((None, 1, tn), lambda i, n: (i, 0, n)),
        ],
        out_specs=pl.BlockSpec((None, b, tn), lambda i, n: (i, 0, n)),
        out_shape=jax.ShapeDtypeStruct((depth, b, n6), F32),
        compiler_params=_cparams(("arbitrary", "arbitrary")),
    )(c, ada_w, ada_b.reshape(depth, 1, n6))


def _mod_spec(chunk, d, tiles_per_batch):
    return pl.BlockSpec((None, 1, d), lambda *idx: (idx[0] // tiles_per_batch, 0, chunk))


def _pool_kernel(x_ref, sh_ref, sc_ref, g_ref, win_ref, wgrp_ref, cs_ref, wout_ref,
                 lng_ref, lnb_ref, o_ref, ext_ref, *, tiles_per_batch):
    tm, d = x_ref.shape
    s_idx = pl.program_id(0) % tiles_per_batch
    x = x_ref[...]
    h = (x * (1.0 + sc_ref[...]) + sh_ref[...]).astype(BF16)
    u = jnp.dot(h, win_ref[...], preferred_element_type=F32)

    @pl.when(s_idx == 0)
    def _():
        ext_ref[0:POOL_HALO, :] = jnp.zeros((POOL_HALO, d), F32)

    @pl.when(s_idx != 0)
    def _():
        ext_ref[0:POOL_HALO, :] = ext_ref[tm:tm + POOL_HALO, :]

    ext_ref[POOL_HALO:POOL_HALO + tm, :] = u

    pos = s_idx * tm + lax.broadcasted_iota(I32, (tm, 1), 0) + 1
    gc = d // len(POOL_WINDOWS)
    ys = []
    for g, w in enumerate(POOL_WINDOWS):
        cols = slice(g * gc, (g + 1) * gc)
        acc = u[:, cols]
        for j in range(1, w):
            acc = acc + ext_ref[POOL_HALO - j:POOL_HALO - j + tm, cols]
        cnt = jnp.minimum(pos, w).astype(F32)
        z = (acc / cnt - u[:, cols]).astype(BF16)
        ys.append(jnp.dot(z, wgrp_ref[g], preferred_element_type=F32))
    y = (jnp.concatenate(ys, axis=1) * cs_ref[...]).astype(BF16)
    out = jnp.dot(y, wout_ref[...], preferred_element_type=F32)
    v = ALPHA * x + (1.0 + g_ref[...]) * out
    o_ref[...] = _layer_norm(v, lng_ref[...], lnb_ref[...])


def _pool_layer(x2, mod3, w_in, w_grp, ch_scale, w_out, ln_g, ln_b, seq):
    t, d = x2.shape
    tm = 512
    tpb = seq // tm
    full = lambda shape: pl.BlockSpec(shape, lambda i: (0,) * len(shape))
    return pl.pallas_call(
        functools.partial(_pool_kernel, tiles_per_batch=tpb),
        grid=(t // tm,),
        in_specs=[
            pl.BlockSpec((tm, d), lambda i: (i, 0)),
            _mod_spec(0, d, tpb), _mod_spec(1, d, tpb), _mod_spec(2, d, tpb),
            full((d, d)), full(w_grp.shape), full((1, d)), full((d, d)),
            full((1, d)), full((1, d)),
        ],
        out_specs=pl.BlockSpec((tm, d), lambda i: (i, 0)),
        out_shape=jax.ShapeDtypeStruct((t, d), F32),
        scratch_shapes=[pltpu.VMEM((tm + POOL_HALO, d), F32)],
        compiler_params=_cparams(("arbitrary",)),
    )(x2, mod3, mod3, mod3, w_in, w_grp, ch_scale.reshape(1, d), w_out,
      ln_g.reshape(1, d), ln_b.reshape(1, d))


def _perm_matrix(dil):
    p = jnp.arange(PERM_TILE)
    chunk = PERM_TILE // dil
    src = (p % chunk) * dil + p // chunk
    return (src[:, None] == jnp.arange(PERM_TILE)[None, :]).astype(BF16)


def _qkv_kernel(x_ref, sh_ref, sc_ref, p_ref, w_ref, o_ref, h_ref):
    tm = x_ref.shape[0]
    g = pl.program_id(1)

    @pl.when(g == 0)
    def _():
        h = (x_ref[...] * (1.0 + sc_ref[...]) + sh_ref[...]).astype(BF16)
        h_ref[0] = h
        for gi in range(1, len(ATTN_PATTERNS)):
            for s in range(tm // PERM_TILE):
                rows = slice(s * PERM_TILE, (s + 1) * PERM_TILE)
                h_ref[gi, rows, :] = jnp.dot(
                    p_ref[gi - 1], h[rows, :], preferred_element_type=F32).astype(BF16)

    o_ref[...] = jnp.dot(h_ref[g], w_ref[...], preferred_element_type=F32).astype(BF16)


def _qkv_proj(x2, mod3, perms, w_in, seq):
    t, d = x2.shape
    ng = len(ATTN_PATTERNS)
    tm = 1024
    tpb = seq // tm
    return pl.pallas_call(
        _qkv_kernel,
        grid=(t // tm, ng),
        in_specs=[
            pl.BlockSpec((tm, d), lambda m, g: (m, 0)),
            _mod_spec(0, d, tpb), _mod_spec(1, d, tpb),
            pl.BlockSpec(perms.shape, lambda m, g: (0, 0, 0)),
            pl.BlockSpec((d, 3 * d), lambda m, g: (0, g)),
        ],
        out_specs=pl.BlockSpec((tm, 3 * d), lambda m, g: (m, g)),
        out_shape=jax.ShapeDtypeStruct((t, ng * 3 * d), BF16),
        scratch_shapes=[pltpu.VMEM((ng, tm, d), BF16)],
        compiler_params=_cparams(("arbitrary", "arbitrary")),
    )(x2, mod3, mod3, perms, w_in)


def _attn_kernel(q_ref, kp_ref, kc_ref, vp_ref, vc_ref, o_ref, lse_ref, *, group, dil):
    bq = Q_BLOCK
    d = N_HEADS * HEAD_DIM
    j = pl.program_id(2)
    q = q_ref[...].reshape(bq, d)
    kp = kp_ref[...].reshape(bq, d)
    kc = kc_ref[...].reshape(bq, d)
    vp = vp_ref[...].reshape(bq, d)
    vc = vc_ref[...].reshape(bq, d)

    qi = lax.broadcasted_iota(I32, (bq, 2 * bq), 0)
    kj = lax.broadcasted_iota(I32, (bq, 2 * bq), 1)
    dist = qi + bq - kj
    steps = ATTN_PATTERNS[group][0] // dil
    valid = (dist >= 0) & (dist <= steps) & ((kj >= bq) | (j > 0))
    neg_dist = jnp.where(valid, (dist * -dil).astype(F32), NEG_BIG)
    lane = lax.broadcasted_iota(I32, (bq, LSE_LANES), 1)
    lse_tile = jnp.zeros((bq, LSE_LANES), F32)
    n_tot = len(ATTN_PATTERNS) * N_HEADS
    pair = 2 * HEAD_DIM
    first_half = lax.broadcasted_iota(I32, (bq, pair), 1) < HEAD_DIM
    outs = []
    for hp in range(N_HEADS // 2):
        cols = slice(hp * pair, (hp + 1) * pair)
        q2 = q[:, cols]
        k2 = jnp.concatenate([kp[:, cols], kc[:, cols]], axis=0)
        v2 = jnp.concatenate([vp[:, cols], vc[:, cols]], axis=0)
        halves = []
        for sub in range(2):
            h = 2 * hp + sub
            slope = 2.0 ** (-8.0 * (group * N_HEADS + h + 1) / n_tot)
            mine = first_half if sub == 0 else jnp.logical_not(first_half)
            qh = jnp.where(mine, q2, jnp.zeros_like(q2))
            s = lax.dot_general(qh, k2, (((1,), (1,)), ((), ())), preferred_element_type=F32)
            s = s + slope * neg_dist
            m = jnp.max(s, axis=-1, keepdims=True)
            p = jnp.exp(s - m)
            den = jnp.sum(p, axis=-1, keepdims=True)
            halves.append(jnp.dot(p.astype(BF16), v2, preferred_element_type=F32) / den)
            lse_tile = jnp.where(lane == h, m + jnp.log(den), lse_tile)
        outs.append(jnp.where(first_half, halves[0], halves[1]))
    o_ref[...] = jnp.concatenate(outs, axis=1).astype(BF16).reshape(o_ref.shape)
    lse_ref[...] = lse_tile.reshape(lse_ref.shape)


def _attention_group(qkv, group, batch, seq):
    dil = ATTN_PATTERNS[group][1]
    d = N_HEADS * HEAD_DIM
    t = qkv.shape[0]
    sub = seq // dil
    nb = sub // Q_BLOCK
    rows = Q_BLOCK if dil == 1 else PERM_TILE // dil
    chunks = Q_BLOCK // rows
    u = seq // (rows * dil)
    view = lambda a, c: a.reshape(batch, u, dil, rows, c)
    blk = lambda c: (None, chunks, None, rows, c)
    col0 = group * 3
    q_spec = pl.BlockSpec(blk(d), lambda b, r, j: (b, j, r, 0, col0))
    kc_spec = pl.BlockSpec(blk(d), lambda b, r, j: (b, j, r, 0, col0 + 1))
    kp_spec = pl.BlockSpec(blk(d), lambda b, r, j: (b, jnp.maximum(j - 1, 0), r, 0, col0 + 1))
    vc_spec = pl.BlockSpec(blk(d), lambda b, r, j: (b, j, r, 0, col0 + 2))
    vp_spec = pl.BlockSpec(blk(d), lambda b, r, j: (b, jnp.maximum(j - 1, 0), r, 0, col0 + 2))
    qkv5 = view(qkv, qkv.shape[1])
    o, lse = pl.pallas_call(
        functools.partial(_attn_kernel, group=group, dil=dil),
        grid=(batch, dil, nb),
        in_specs=[q_spec, kp_spec, kc_spec, vp_spec, vc_spec],
        out_specs=[
            pl.BlockSpec(blk(d), lambda b, r, j: (b, j, r, 0, 0)),
            pl.BlockSpec(blk(LSE_LANES), lambda b, r, j: (b, j, r, 0, 0)),
        ],
        out_shape=[
            jax.ShapeDtypeStruct((batch, u, dil, rows, d), BF16),
            jax.ShapeDtypeStruct((batch, u, dil, rows, LSE_LANES), F32),
        ],
        compiler_params=_cparams(("arbitrary", "arbitrary", "arbitrary")),
    )(qkv5, qkv5, qkv5, qkv5, qkv5)
    return o.reshape(t, d), lse.reshape(t, LSE_LANES)


def _split3(v):
    a = v.astype(BF16)
    r = v - a.astype(F32)
    b = r.astype(BF16)
    c = (r - b.astype(F32)).astype(BF16)
    return a, b, c


def _attn_out_kernel(x_ref, g_ref, o0_ref, o1_ref, o2_ref, l0_ref, l1_ref, l2_ref,
                     pt_ref, e_ref, wout_ref, lng_ref, lnb_ref, out_ref):
    tm, d = x_ref.shape
    o_refs = (o0_ref, o1_ref, o2_ref)
    l_refs = (l0_ref, l1_ref, l2_ref)
    n_sub = tm // PERM_TILE

    def unperm(gi, val_bf16):
        if gi == 0:
            return val_bf16.astype(F32)
        parts = [jnp.dot(pt_ref[gi - 1], val_bf16[s * PERM_TILE:(s + 1) * PERM_TILE, :],
                         preferred_element_type=F32) for s in range(n_sub)]
        return jnp.concatenate(parts, axis=0)

    lses = []
    for gi in range(3):
        l = l_refs[gi][...]
        if gi == 0:
            lses.append(l)
        else:
            a, b, c = _split3(l)
            lses.append(unperm(gi, a) + unperm(gi, b) + unperm(gi, c))
    mx = jnp.maximum(jnp.maximum(lses[0], lses[1]), lses[2])
    es = [jnp.exp(l - mx) for l in lses]
    tot = es[0] + es[1] + es[2]
    mixed = jnp.zeros((tm, d), F32)
    for gi in range(3):
        w = es[gi] / tot
        a, b, _ = _split3(w)
        wide = (jnp.dot(a, e_ref[...], preferred_element_type=F32)
                + jnp.dot(b, e_ref[...], preferred_element_type=F32))
        mixed = mixed + wide * unperm(gi, o_refs[gi][...])
    y = jnp.dot(mixed.astype(BF16), wout_ref[...], preferred_element_type=F32)
    v = ALPHA * x_ref[...] + (1.0 + g_ref[...]) * y
    out_ref[...] = _layer_norm(v, lng_ref[...], lnb_ref[...])


def _attn_out(x2, mod3, os_, lses, perms_t, expand, w_out, ln_g, ln_b, seq):
    t, d = x2.shape
    tm = 512
    tpb = seq // tm
    row = lambda c: pl.BlockSpec((tm, c), lambda i: (i, 0))
    full = lambda shape: pl.BlockSpec(shape, lambda i: (0,) * len(shape))
    return pl.pallas_call(
        _attn_out_kernel,
        grid=(t // tm,),
        in_specs=[row(d), _mod_spec(2, d, tpb), row(d), row(d), row(d),
                  row(LSE_LANES), row(LSE_LANES), row(LSE_LANES),
                  full(perms_t.shape), full(expand.shape), full((d, d)),
                  full((1, d)), full((1, d))],
        out_specs=row(d),
        out_shape=jax.ShapeDtypeStruct((t, d), F32),
        compiler_params=_cparams(("arbitrary",)),
    )(x2, mod3, *os_, *lses, perms_t, expand, w_out, ln_g.reshape(1, d), ln_b.reshape(1, d))


def _first_index_of_max(v, iota, size):
    m = jnp.max(v, axis=0, keepdims=True)
    idx = jnp.min(jnp.where(v == m, iota, float(size)), axis=0, keepdims=True)
    return m, idx


def _topk_kernel(x_ref, sh_ref, sc_ref, wr_ref, b_ref, tri_ref, ltri_ref,
                 w_ref, pos_ref, cnt_ref):
    tr = x_ref.shape[0]
    ne = b_ref.shape[0]
    gsz = ne // N_EXPERT_GROUPS
    tile = pl.program_id(0)

    hf = x_ref[...] * (1.0 + sc_ref[...]) + sh_ref[...]
    hf_hi = hf.astype(BF16)
    hf_lo = (hf - hf_hi.astype(F32)).astype(BF16)
    nt = (((1,), (1,)), ((), ()))
    by_hi = lax.dot_general(wr_ref[...], hf_hi, nt, preferred_element_type=F32)
    logits = (by_hi[:ne, :] + by_hi[ne:, :]
              + lax.dot_general(wr_ref[0:ne, :], hf_lo, nt, preferred_element_type=F32))
    scores = 1.0 / (1.0 + jnp.exp(-logits))
    sel = scores + b_ref[...]
    iota_g = lax.broadcasted_iota(I32, (gsz, tr), 0).astype(F32)
    iota_n = lax.broadcasted_iota(I32, (N_EXPERT_GROUPS, tr), 0).astype(F32)
    gs = jnp.zeros((N_EXPERT_GROUPS, tr), F32)
    for g in range(N_EXPERT_GROUPS):
        blk = sel[g * gsz:(g + 1) * gsz, :]
        m1, i1 = _first_index_of_max(blk, iota_g, gsz)
        m2 = jnp.max(jnp.where(iota_g == i1, -jnp.inf, blk), axis=0, keepdims=True)
        gs = jnp.where(iota_n == float(g), m1 + m2, gs)
    gmask = jnp.zeros((N_EXPERT_GROUPS, tr), F32)
    for _ in range(TOPK_GROUPS):
        _, gi = _first_index_of_max(gs, iota_n, N_EXPERT_GROUPS)
        hit = iota_n == gi
        gmask = jnp.where(hit, 1.0, gmask)
        gs = jnp.where(hit, -jnp.inf, gs)
    masked_rows = []
    for g in range(N_EXPERT_GROUPS):
        keep = jnp.broadcast_to(gmask[g:g + 1, :], (gsz, tr)) > 0.5
        masked_rows.append(jnp.where(keep, sel[g * gsz:(g + 1) * gsz, :], -jnp.inf))
    cur = jnp.concatenate(masked_rows, axis=0)
    iota_e = lax.broadcasted_iota(I32, (ne, tr), 0).astype(F32)
    chosen = jnp.zeros((ne, tr), F32)
    for _ in range(TOP_K):
        _, ei = _first_index_of_max(cur, iota_e, ne)
        hit = iota_e == ei
        cur = jnp.where(hit, -jnp.inf, cur)
        chosen = jnp.where(hit, 1.0, chosen)
    picked = chosen > 0.5
    top_scores = jnp.where(picked, scores, 0.0)
    wsum = jnp.sum(top_scores, axis=0, keepdims=True)
    w_ref[...] = top_scores / wsum * ROUTED_SCALE
    before = jnp.dot(chosen.astype(BF16), tri_ref[...], preferred_element_type=F32)
    n = jnp.sum(chosen, axis=1, keepdims=True)
    n_chunks = jnp.floor((n + (CHUNK - 1)) * (1.0 / CHUNK))
    run_off = jnp.dot(ltri_ref[...], jnp.broadcast_to(n_chunks, (ne, 128)).astype(BF16),
                      preferred_element_type=F32)[:, 0:1] * CHUNK
    pos_ref[...] = jnp.where(picked, before + run_off, NO_SLOT)

    @pl.when(tile == 0)
    def _():
        cnt_ref[...] = jnp.zeros_like(cnt_ref)

    lane = lax.broadcasted_iota(I32, cnt_ref.shape, 1)
    cnt_ref[...] = jnp.where(lane == tile, jnp.broadcast_to(n, cnt_ref.shape).astype(I32),
                             cnt_ref[...])


def _topk_route(x2, mod3, w_router, router_b, seq):
    t, d = x2.shape
    ne = w_router.shape[1]
    tr = TOK_TILE
    tpb = seq // tr
    assert t // tr <= 128
    w_t = w_router.T
    w_hi = w_t.astype(BF16)
    w_split = jnp.concatenate([w_hi, (w_t - w_hi.astype(F32)).astype(BF16)], axis=0)
    tri = (jnp.arange(tr)[:, None] < jnp.arange(tr)[None, :]).astype(BF16)
    ltri = (jnp.arange(ne)[None, :] < jnp.arange(ne)[:, None]).astype(BF16)
    out = lambda dt: jax.ShapeDtypeStruct((ne, t), dt)
    row = pl.BlockSpec((ne, tr), lambda i: (0, i))
    return pl.pallas_call(
        _topk_kernel,
        grid=(t // tr,),
        in_specs=[pl.BlockSpec((tr, d), lambda i: (i, 0)),
                  _mod_spec(3, d, tpb), _mod_spec(4, d, tpb),
                  pl.BlockSpec((2 * ne, d), lambda i: (0, 0)),
                  pl.BlockSpec((ne, 1), lambda i: (0, 0)),
                  pl.BlockSpec((tr, tr), lambda i: (0, 0)),
                  pl.BlockSpec((ne, ne), lambda i: (0, 0))],
        out_specs=[row, row, pl.BlockSpec((ne, 128), lambda i: (0, 0))],
        out_shape=[out(F32), out(F32), jax.ShapeDtypeStruct((ne, 128), I32)],
        compiler_params=_cparams(("arbitrary",)),
    )(x2, mod3, mod3, w_split, router_b.reshape(ne, 1), tri, ltri)


def _slot_plan(counts, n_tiles, n_blocks):
    n = counts[:, :n_tiles]
    nch = (n + (CHUNK - 1)) // CHUNK
    rows = jnp.sum(nch, axis=1) * CHUNK
    region = ((rows + EXPERT_ROWS - 1) // EXPERT_ROWS) * EXPERT_ROWS
    region_end = jnp.cumsum(region)
    region_start = region_end - region
    run_chunk = region_start[:, None] // CHUNK + jnp.cumsum(nch, axis=1) - nch
    local_end = jnp.cumsum(nch, axis=0)
    local_chunk = local_end - nch
    c = jnp.arange(LOCAL_ROWS // CHUNK, dtype=I32)
    expert_of_c = jnp.sum((local_end[:, :, None] <= c[None, None, :]).astype(I32), axis=0)
    owner = expert_of_c[None] == jnp.arange(N_EXPERTS, dtype=I32)[:, None, None]
    slot_chunk = jnp.sum(jnp.where(owner, (run_chunk - local_chunk)[:, :, None], 0), axis=0) + c
    block_row = jnp.arange(n_blocks, dtype=I32) * EXPERT_ROWS
    group_e = jnp.minimum(jnp.sum((region_end[None, :] <= block_row[:, None]).astype(I32), axis=1),
                          N_EXPERTS - 1)
    n_used = region_end[-1] // EXPERT_ROWS
    nonempty = region > 0
    run_of_expert = jnp.cumsum(nonempty.astype(I32)) - 1
    experts = jnp.arange(N_EXPERTS, dtype=I32)
    run_expert = jnp.sum(jnp.where((run_of_expert[None, :] == experts[:, None]) & nonempty[None, :],
                                   experts[None, :], 0), axis=1)
    blocks = jnp.arange(n_blocks, dtype=I32)
    first = ((blocks == 0) | (group_e != jnp.roll(group_e, 1))) & (blocks < n_used)
    per_tile = lambda a: (a * CHUNK).T.astype(I32)
    return dict(
        run_begin_rows=per_tile(local_chunk)[:, None, :], run_end_rows=per_tile(local_end)[:, None, :],
        run_begin_cols=per_tile(local_chunk)[:, :, None], run_end_cols=per_tile(local_end)[:, :, None],
        block_run=(jnp.cumsum(first.astype(I32)) - 1).astype(I32), block_first=first.astype(I32),
        run_expert=run_expert.astype(I32), n_runs=jnp.sum(nonempty.astype(I32)).reshape(1),
        slot_chunk=slot_chunk.reshape(-1).astype(I32),
        tile_chunks=jnp.sum(nch, axis=0).astype(I32),
        tail_chunk=((region_start + rows) // CHUNK).astype(I32),
        tail_chunks=((region - rows) // CHUNK).astype(I32),
        group_e=group_e, n_used=(region_end[-1:] // EXPERT_ROWS).astype(I32))


def _chunk_rows(ref, chunk_index):
    start = chunk_index * CHUNK
    if not isinstance(start, int):
        start = pl.multiple_of(start, CHUNK)
    return ref.at[pl.ds(start, CHUNK), :]


def _pos_digits(pos):
    hi = jnp.floor(pos * (1.0 / POS_RADIX))
    return hi.astype(BF16), (pos - POS_RADIX * hi).astype(BF16)


def _for_chunks_two_queues(count, start):
    def pair(p, carry):
        start(2 * p, 0)
        start(2 * p + 1, 1)
        return carry

    lax.fori_loop(0, lax.shift_right_logical(count, 1), pair, 0)

    @pl.when((count & 1) == 1)
    def _():
        start(count - 1, 0)


def _wait_chunks(count, src_ref, dst_ref, sem):
    group_rows = pl.ds(0, WAIT_GROUP * CHUNK)

    def wait_group(c, carry):
        pltpu.make_async_copy(src_ref.at[group_rows, :], dst_ref.at[group_rows, :], sem).wait()
        return carry

    def wait_one(c, carry):
        pltpu.make_async_copy(_chunk_rows(src_ref, 0), _chunk_rows(dst_ref, 0), sem).wait()
        return carry

    lax.fori_loop(0, lax.shift_right_logical(count, WAIT_GROUP.bit_length() - 1), wait_group, 0)
    lax.fori_loop(0, count & (WAIT_GROUP - 1), wait_one, 0)


def _dispatch_kernel(slot_ref, tile_ref, tail_ref, tailn_ref, used_ref,
                     x_ref, sh_ref, sc_ref, pos_ref, begin_ref, end_ref,
                     xg_hbm, stage_ref, zero_ref, sems, sem_blk):
    tr = x_ref.shape[0]
    i = pl.program_id(0)
    n_blocks = xg_hbm.shape[0] // EXPERT_ROWS
    chunks_per_tile = LOCAL_ROWS // CHUNK

    def drain(count, sem):
        _wait_chunks(count, stage_ref.at[0], xg_hbm, sem)

    @pl.when(i == 0)
    def _():
        zero_ref[...] = jnp.zeros_like(zero_ref)

        def block_copy(j):
            rows = pl.ds(pl.multiple_of(j * EXPERT_ROWS, EXPERT_ROWS), EXPERT_ROWS)
            return pltpu.make_async_copy(zero_ref, xg_hbm.at[rows, :], sem_blk)

        def start_block(j, carry):
            block_copy(j).start()
            return carry

        def wait_block(j, carry):
            block_copy(j).wait()
            return carry

        lax.fori_loop(used_ref[0], n_blocks, start_block, 0)

        def per_expert(e, total):
            def per_chunk(c, carry):
                pltpu.make_async_copy(_chunk_rows(zero_ref, 0),
                                      _chunk_rows(xg_hbm, tail_ref[e] + c), sems.at[0]).start()
                return carry
            lax.fori_loop(0, tailn_ref[e], per_chunk, 0)
            return total + tailn_ref[e]

        drain(lax.fori_loop(0, N_EXPERTS, per_expert, 0), sems.at[0])
        lax.fori_loop(used_ref[0], n_blocks, wait_block, 0)

    hf = (x_ref[...] * (1.0 + sc_ref[...]) + sh_ref[...]).astype(BF16)
    row_e = lax.broadcasted_iota(I32, (LOCAL_ROWS, N_EXPERTS), 0)
    owner = jnp.where((row_e >= begin_ref[...]) & (row_e < end_ref[...]), 1.0, 0.0).astype(BF16)
    hi_digit, lo_digit = _pos_digits(pos_ref[...])
    slot_pos = (POS_RADIX * jnp.dot(owner, hi_digit, preferred_element_type=F32)
                + jnp.dot(owner, lo_digit, preferred_element_type=F32))
    row_t = lax.broadcasted_iota(I32, (LOCAL_ROWS, tr), 0).astype(F32)
    perm = jnp.where(slot_pos == row_t, 1.0, 0.0).astype(BF16)
    rows = jnp.dot(perm, hf, preferred_element_type=F32)
    half = rows.shape[1] // 2
    buf = stage_ref.at[i % 2]
    buf[...] = (lax.bitcast_convert_type(rows[:, :half], U32)
                | (lax.bitcast_convert_type(rows[:, half:], U32) >> 16))

    def send(c, priority):
        pltpu.make_async_copy(_chunk_rows(buf, c),
                              _chunk_rows(xg_hbm, slot_ref[i * chunks_per_tile + c]),
                              sems.at[i % 2]).start(priority=priority)

    _for_chunks_two_queues(tile_ref[i], send)

    @pl.when(i > 0)
    def _():
        drain(tile_ref[jnp.maximum(i - 1, 0)], sems.at[(i + 1) % 2])

    @pl.when(i == pl.num_programs(0) - 1)
    def _():
        drain(tile_ref[i], sems.at[i % 2])


def _dispatch(plan, x2, mod3, pos, n_slots, seq):
    t, d = x2.shape
    tr = TOK_TILE
    tpb = seq // tr
    return pl.pallas_call(
        _dispatch_kernel,
        grid_spec=pltpu.PrefetchScalarGridSpec(
            num_scalar_prefetch=5,
            grid=(t // tr,),
            in_specs=[pl.BlockSpec((tr, d), lambda i, *_: (i, 0)),
                      _mod_spec(3, d, tpb), _mod_spec(4, d, tpb),
                      pl.BlockSpec((N_EXPERTS, tr), lambda i, *_: (0, i)),
                      pl.BlockSpec((None, 1, N_EXPERTS), lambda i, *_: (i, 0, 0)),
                      pl.BlockSpec((None, 1, N_EXPERTS), lambda i, *_: (i, 0, 0))],
            out_specs=pl.BlockSpec(memory_space=pl.ANY),
            scratch_shapes=[pltpu.VMEM((2, LOCAL_ROWS, d // 2), U32),
                            pltpu.VMEM((EXPERT_ROWS, d // 2), U32),
                            pltpu.SemaphoreType.DMA((2,)), pltpu.SemaphoreType.DMA],
        ),
        out_shape=jax.ShapeDtypeStruct((n_slots, d // 2), U32),
        compiler_params=_cparams(("arbitrary",)),
    )(plan["slot_chunk"], plan["tile_chunks"], plan["tail_chunk"], plan["tail_chunks"],
      plan["n_used"], x2, mod3, mod3, pos, plan["run_begin_rows"], plan["run_end_rows"])


def _expert_kernel(run_ref, first_ref, rexp_ref, nrun_ref, nu_ref,
                   xg_hbm, wgu_hbm, wdn_hbm, yg_hbm,
                   xbuf, ybuf, wgu_f32, wdn_f32, wgu_bf, wdn_bf, x_sem, y_sem, w_sem, z_sem,
                   *, layer):
    n_used = nu_ref[0]
    n_blocks = yg_hbm.shape[0] // EXPERT_ROWS
    depth, _, half = xbuf.shape
    f = wdn_bf.shape[0]

    def rows(j):
        return pl.ds(pl.multiple_of(j * EXPERT_ROWS, EXPERT_ROWS), EXPERT_ROWS)

    def x_copy(j):
        return pltpu.make_async_copy(xg_hbm.at[rows(j), :], xbuf.at[j % depth],
                                     x_sem.at[j % depth])

    def y_copy(j):
        return pltpu.make_async_copy(ybuf.at[j % depth], yg_hbm.at[rows(j), :],
                                     y_sem.at[j % depth])

    def w_copies(r):
        e = rexp_ref[r]
        return (pltpu.make_async_copy(wgu_hbm.at[layer, e], wgu_f32.at[r % 2], w_sem.at[r % 2]),
                pltpu.make_async_copy(wdn_hbm.at[layer, e], wdn_f32.at[r % 2], w_sem.at[r % 2]))

    def zero_copy(j):
        return pltpu.make_async_copy(ybuf.at[0], yg_hbm.at[rows(j), :], z_sem)

    ybuf[0] = jnp.zeros(ybuf.shape[1:], U32)

    def zero_start(j, carry):
        zero_copy(j).start()
        return carry

    def zero_wait(j, carry):
        zero_copy(j).wait()
        return carry

    lax.fori_loop(n_used, n_blocks, zero_start, 0)
    lax.fori_loop(n_used, n_blocks, zero_wait, 0)

    for cp in w_copies(0):
        cp.start()
    for a in range(depth - 1):
        @pl.when(a < n_used)
        def _():
            x_copy(a).start()

    def block(j, carry):
        r = run_ref[j]

        @pl.when(first_ref[j] == 1)
        def _():
            for cp in w_copies(r):
                cp.wait()

            @pl.when(r + 1 < nrun_ref[0])
            def _():
                for cp in w_copies(r + 1):
                    cp.start()

            wgu_bf[...] = wgu_f32[r % 2].astype(BF16)
            wdn_bf[...] = wdn_f32[r % 2].astype(BF16)

        x_copy(j).wait()

        @pl.when(j + (depth - 1) < n_used)
        def _():
            x_copy(j + (depth - 1)).start()

        @pl.when(j >= depth)
        def _():
            y_copy(j - depth).wait()

        for g in range(EXPERT_ROWS // GEMM_SUB_ROWS):
            rws = slice(g * GEMM_SUB_ROWS, (g + 1) * GEMM_SUB_ROWS)
            hi, lo = _unpack_bf16_pairs(xbuf[j % depth, rws, :])
            gu = (jnp.dot(hi.astype(BF16), wgu_bf[0:half, :], preferred_element_type=F32)
                  + jnp.dot(lo.astype(BF16), wgu_bf[half:, :], preferred_element_type=F32))
            act = (_silu(gu[:, :f]) * gu[:, f:]).astype(BF16)
            ybuf[j % depth, rws, :] = _pack_bf16_pairs(
                jnp.dot(act, wdn_bf[...], preferred_element_type=F32))
        y_copy(j).start()
        return carry

    lax.fori_loop(0, n_used, block, 0)

    for a in range(depth, 0, -1):
        @pl.when(n_used >= a)
        def _():
            y_copy(n_used - a).wait()


def _expert_gemm(plan, xg, w_gu, w_down, layer):
    n_slots, half = xg.shape
    _, ne, d, f2 = w_gu.shape
    f = w_down.shape[2]
    hbm = pl.BlockSpec(memory_space=pl.ANY)
    return pl.pallas_call(
        functools.partial(_expert_kernel, layer=layer),
        grid_spec=pltpu.PrefetchScalarGridSpec(
            num_scalar_prefetch=5,
            grid=(1,),
            in_specs=[hbm, hbm, hbm],
            out_specs=hbm,
            scratch_shapes=[pltpu.VMEM((GEMM_DEPTH, EXPERT_ROWS, half), U32),
                            pltpu.VMEM((GEMM_DEPTH, EXPERT_ROWS, half), U32),
                            pltpu.VMEM((2, d, f2), F32), pltpu.VMEM((2, f, d), F32),
                            pltpu.VMEM((d, f2), BF16), pltpu.VMEM((f, d), BF16),
                            pltpu.SemaphoreType.DMA((GEMM_DEPTH,)),
                            pltpu.SemaphoreType.DMA((GEMM_DEPTH,)),
                            pltpu.SemaphoreType.DMA((2,)), pltpu.SemaphoreType.DMA],
        ),
        out_shape=jax.ShapeDtypeStruct((n_slots, half), U32),
        compiler_params=_cparams(("arbitrary",)),
    )(plan["block_run"], plan["block_first"], plan["run_expert"], plan["n_runs"], plan["n_used"],
      xg, w_gu, w_down)


def _combine_kernel(slot_ref, tile_ref,
                    x_ref, sh_ref, sc_ref, g_ref, pos_ref, wt_ref, begin_ref, end_ref, brow_ref,
                    wsgu_ref, wsdn_ref, lng_ref, lnb_ref, yg_hbm, o_ref, stage_ref, sems):
    tr, d = x_ref.shape
    i = pl.program_id(0)
    n_tiles = pl.num_programs(0)
    chunks_per_tile = LOCAL_ROWS // CHUNK

    def fetch(tile, buffer):
        def start(c, priority):
            pltpu.make_async_copy(_chunk_rows(yg_hbm, slot_ref[tile * chunks_per_tile + c]),
                                  _chunk_rows(stage_ref.at[buffer], c),
                                  sems.at[buffer]).start(priority=priority)
        _for_chunks_two_queues(tile_ref[tile], start)

    @pl.when(i == 0)
    def _():
        stage_ref[...] = jnp.zeros_like(stage_ref)
        fetch(0, 0)

    nxt = jnp.minimum(i + 1, n_tiles - 1)
    for buffer in range(2):
        @pl.when((i + 1 < n_tiles) & (nxt % 2 == buffer))
        def _():
            fetch(nxt, buffer)

    x = x_ref[...]
    hf = (x * (1.0 + sc_ref[...]) + sh_ref[...]).astype(BF16)
    f = wsdn_ref.shape[0]
    su = jnp.dot(hf, wsgu_ref[...], preferred_element_type=F32)
    act = (_silu(su[:, :f]) * su[:, f:]).astype(BF16)
    shared = jnp.dot(act, wsdn_ref[...], preferred_element_type=F32)

    col_e = lax.broadcasted_iota(I32, (N_EXPERTS, LOCAL_ROWS), 1)
    owner = jnp.where((col_e >= begin_ref[...]) & (col_e < end_ref[...]), 1.0, 0.0).astype(BF16)
    begin_row = brow_ref[...].astype(F32)
    pos = pos_ref[...]
    rank1 = jnp.where(pos >= 0.0, pos - begin_row + 1.0, 0.0).astype(BF16)
    slot_rank = jnp.dot(rank1, owner, preferred_element_type=F32)
    b_hi, b_lo = _pos_digits(jnp.broadcast_to(begin_row, (8, N_EXPERTS)))
    run_begin = (POS_RADIX * jnp.dot(b_hi, owner, preferred_element_type=F32)
                 + jnp.dot(b_lo, owner, preferred_element_type=F32))[0:1, :]
    offset1 = lax.broadcasted_iota(I32, (1, LOCAL_ROWS), 1).astype(F32) - run_begin + 1.0
    holds = slot_rank == offset1
    weights = jnp.where(holds, jnp.dot(wt_ref[...].astype(BF16), owner,
                                       preferred_element_type=F32), 0.0).astype(BF16)

    buf = stage_ref.at[i % 2]
    _wait_chunks(tile_ref[i], yg_hbm, buf, sems.at[i % 2])

    y_hi, y_lo = _unpack_bf16_pairs(buf[...])
    y = jnp.concatenate([y_hi.astype(BF16), y_lo.astype(BF16)], axis=1)
    routed = jnp.dot(weights, y, preferred_element_type=F32)
    v = ALPHA * x + (1.0 + g_ref[...]) * (routed + shared)
    o_ref[...] = _layer_norm(v, lng_ref[...], lnb_ref[...])


def _combine(plan, x2, mod3, pos_tok, w_tok, w_sh_gu, w_sh_down, ln_g, ln_b, yg, seq):
    t, d = x2.shape
    tr = TOK_TILE
    tpb = seq // tr
    full = lambda shape: pl.BlockSpec(shape, lambda i, *_: (0,) * len(shape))
    return pl.pallas_call(
        _combine_kernel,
        grid_spec=pltpu.PrefetchScalarGridSpec(
            num_scalar_prefetch=2,
            grid=(t // tr,),
            in_specs=[pl.BlockSpec((tr, d), lambda i, *_: (i, 0)),
                      _mod_spec(3, d, tpb), _mod_spec(4, d, tpb), _mod_spec(5, d, tpb),
                      pl.BlockSpec((tr, N_EXPERTS), lambda i, *_: (i, 0)),
                      pl.BlockSpec((tr, N_EXPERTS), lambda i, *_: (i, 0)),
                      pl.BlockSpec((None, N_EXPERTS, 1), lambda i, *_: (i, 0, 0)),
                      pl.BlockSpec((None, N_EXPERTS, 1), lambda i, *_: (i, 0, 0)),
                      pl.BlockSpec((None, 1, N_EXPERTS), lambda i, *_: (i, 0, 0)),
                      full(w_sh_gu.shape), full(w_sh_down.shape), full((1, d)), full((1, d)),
                      pl.BlockSpec(memory_space=pl.ANY)],
            out_specs=pl.BlockSpec((tr, d), lambda i, *_: (i, 0)),
            scratch_shapes=[pltpu.VMEM((2, LOCAL_ROWS, d // 2), U32),
                            pltpu.SemaphoreType.DMA((2,))],
        ),
        out_shape=jax.ShapeDtypeStruct((t, d), F32),
        compiler_params=_cparams(("arbitrary",)),
    )(plan["slot_chunk"], plan["tile_chunks"],
      x2, mod3, mod3, mod3, pos_tok, w_tok, plan["run_begin_cols"], plan["run_end_cols"],
      plan["run_begin_rows"],
      w_sh_gu, w_sh_down, ln_g.reshape(1, d), ln_b.reshape(1, d), yg)


def _moe_layer(x2, mod3, w_router, router_b, w_gu, w_down, layer, w_sh_gu, w_sh_down,
               ln_g, ln_b, seq):
    t, d = x2.shape
    n_tiles = t // TOK_TILE
    w_top, pos, counts = _topk_route(x2, mod3, w_router, router_b, seq)
    bound = t * TOP_K + n_tiles * N_EXPERTS * (CHUNK - 1) + N_EXPERTS * (EXPERT_ROWS - 1)
    n_blocks = -(-bound // EXPERT_ROWS)
    plan = _slot_plan(counts, n_tiles, n_blocks)
    xg = _dispatch(plan, x2, mod3, pos, n_blocks * EXPERT_ROWS, seq)
    yg = _expert_gemm(plan, xg, w_gu, w_down, layer)
    return _combine(plan, x2, mod3, pos.T, w_top.T, w_sh_gu, w_sh_down, ln_g, ln_b, yg, seq)


def kernel(x, c, ada_w, ada_b, pool_w_in, pool_w_grp, pool_scale, pool_w_out, attn_w_in, attn_w_out, ln1_g, ln1_b, router_w, router_b, exp_w_gu, exp_w_down, sh_w_gu, sh_w_down, ln2_g, ln2_b):
    batch, seq, d = x.shape
    depth = ada_w.shape[0]
    t = batch * seq
    mod = _modulation(c, ada_w, ada_b).reshape(depth, batch, 1, 6 * d)
    perms = jnp.stack([_perm_matrix(dil) for _, dil in ATTN_PATTERNS[1:]])
    perms_t = jnp.swapaxes(perms, 1, 2)
    expand = (jnp.arange(LSE_LANES)[:, None] == (jnp.arange(d)[None, :] // HEAD_DIM)).astype(BF16)
    part = (jnp.arange(attn_w_in.shape[2]) // d) % 3
    q_scale = jnp.where(part == 0, HEAD_DIM ** -0.5, 1.0).astype(F32)
    x2 = x.reshape(t, d)
    for i in range(depth):
        mod3 = mod[i]
        j = i // 2
        if i % 2 == 0:
            x2 = _pool_layer(x2, mod3, pool_w_in[j].astype(BF16), pool_w_grp[j].astype(BF16),
                             pool_scale[j], pool_w_out[j].astype(BF16), ln1_g[i], ln1_b[i], seq)
        else:
            qkv = _qkv_proj(x2, mod3, perms, (attn_w_in[j] * q_scale).astype(BF16), seq)
            res = [_attention_group(qkv, g, batch, seq) for g in range(len(ATTN_PATTERNS))]
            x2 = _attn_out(x2, mod3, [r[0] for r in res], [r[1] for r in res], perms_t, expand,
                           attn_w_out[j].astype(BF16), ln1_g[i], ln1_b[i], seq)
        x2 = _moe_layer(x2, mod3, router_w[i], router_b[i], exp_w_gu, exp_w_down, i,
                        sh_w_gu[i].astype(BF16), sh_w_down[i].astype(BF16),
                        ln2_g[i], ln2_b[i], seq)
    return x2.reshape(batch, seq, d)
```

```python
import functools
import math

import jax
import jax.numpy as jnp
from jax import lax
from jax.experimental import pallas as pl
from jax.experimental.pallas import tpu as pltpu

F32 = jnp.float32
BF16 = jnp.bfloat16
U32 = jnp.uint32
I32 = jnp.int32

POOL_WINDOWS = (2, 4, 8, 16)
ATTN_PATTERNS = ((128, 1), (512, 4), (2048, 16))
HEAD_DIM = 64
N_HEADS = 16
Q_BLOCK = 128
N_EXPERTS = 64
TOP_K = 8
N_EXPERT_GROUPS = 8
TOPK_GROUPS = 4
ROUTED_SCALE = 2.5
EXPERT_ROWS = 512
DEPTH = 4
ALPHA = (2 * DEPTH) ** 0.25
LN_EPS = 1e-5

PERM_TILE = 256
POOL_HALO = 16
TOK_TILE = 256
CHUNK = 8
LOCAL_ROWS = -(-(TOK_TILE * TOP_K + N_EXPERTS * (CHUNK - 1)) // 256) * 256
NO_SLOT = -64.0
POS_RADIX = 64.0
COPY_SIZES = (4, 2, 1)
WAIT_GROUP = 32
GEMM_SUB_ROWS = 256
GEMM_DEPTH = 4
LSE_LANES = 128
VMEM_LIMIT = 56 * 1024 * 1024
NEG_BIG = -1e30


def _cparams(sem):
    return pltpu.CompilerParams(dimension_semantics=sem, vmem_limit_bytes=VMEM_LIMIT)


def _layer_norm(v, g, b):
    mu = jnp.mean(v, axis=-1, keepdims=True)
    c = v - mu
    var = jnp.mean(c * c, axis=-1, keepdims=True)
    return c * lax.rsqrt(var + LN_EPS) * g + b


def _silu(v):
    return v * (1.0 / (1.0 + jnp.exp(-v)))


def _pack_bf16_pairs(v):
    n = v.shape[1] // 2
    hi = lax.bitcast_convert_type(v[:, :n].astype(BF16).astype(F32), U32)
    lo = lax.bitcast_convert_type(v[:, n:].astype(BF16).astype(F32), U32)
    return hi | (lo >> 16)


def _unpack_bf16_pairs(p):
    hi = lax.bitcast_convert_type(p & jnp.uint32(0xFFFF0000), F32)
    lo = lax.bitcast_convert_type(p << 16, F32)
    return hi, lo


def _mod_kernel(c_ref, w_ref, b_ref, o_ref):
    cs = _silu(c_ref[...])
    o_ref[...] = jnp.dot(cs, w_ref[...], preferred_element_type=F32) + b_ref[...]


def _modulation(c, ada_w, ada_b):
    depth, d, n6 = ada_w.shape
    b = c.shape[0]
    tn = 1536
    return pl.pallas_call(
        _mod_kernel,
        grid=(depth, n6 // tn),
        in_specs=[
            pl.BlockSpec((b, d), lambda i, n: (0, 0)),
            pl.BlockSpec((None, d, tn), lambda i, n: (i, 0, n)),
            pl.BlockSpec((None, 1, tn), lambda i, n: (i, 0, n)),
        ],
        out_specs=pl.BlockSpec((None, b, tn), lambda i, n: (i, 0, n)),
        out_shape=jax.ShapeDtypeStruct((depth, b, n6), F32),
        compiler_params=_cparams(("arbitrary", "arbitrary")),
    )(c, ada_w, ada_b.reshape(depth, 1, n6))


def _mod_spec(chunk, d, tiles_per_batch):
    return pl.BlockSpec((None, 1, d), lambda *idx: (idx[0] // tiles_per_batch, 0, chunk))


def _pool_kernel(x_ref, sh_ref, sc_ref, g_ref, win_ref, wgrp_ref, cs_ref, wout_ref,
                 lng_ref, lnb_ref, o_ref, ext_ref, *, tiles_per_batch):
    tm, d = x_ref.shape
    s_idx = pl.program_id(0) % tiles_per_batch
    x = x_ref[...]
    h = (x * (1.0 + sc_ref[...]) + sh_ref[...]).astype(BF16)
    u = jnp.dot(h, win_ref[...], preferred_element_type=F32)

    @pl.when(s_idx == 0)
    def _():
        ext_ref[0:POOL_HALO, :] = jnp.zeros((POOL_HALO, d), F32)

    @pl.when(s_idx != 0)
    def _():
        ext_ref[0:POOL_HALO, :] = ext_ref[tm:tm + POOL_HALO, :]

    ext_ref[POOL_HALO:POOL_HALO + tm, :] = u

    pos = s_idx * tm + lax.broadcasted_iota(I32, (tm, 1), 0) + 1
    gc = d // len(POOL_WINDOWS)
    ys = []
    for g, w in enumerate(POOL_WINDOWS):
        cols = slice(g * gc, (g + 1) * gc)
        acc = u[:, cols]
        for j in range(1, w):
            acc = acc + ext_ref[POOL_HALO - j:POOL_HALO - j + tm, cols]
        cnt = jnp.minimum(pos, w).astype(F32)
        z = (acc / cnt - u[:, cols]).astype(BF16)
        ys.append(jnp.dot(z, wgrp_ref[g], preferred_element_type=F32))
    y = (jnp.concatenate(ys, axis=1) * cs_ref[...]).astype(BF16)
    out = jnp.dot(y, wout_ref[...], preferred_element_type=F32)
    v = ALPHA * x + (1.0 + g_ref[...]) * out
    o_ref[...] = _layer_norm(v, lng_ref[...], lnb_ref[...])


def _pool_layer(x2, mod3, w_in, w_grp, ch_scale, w_out, ln_g, ln_b, seq):
    t, d = x2.shape
    tm = 512
    tpb = seq // tm
    full = lambda shape: pl.BlockSpec(shape, lambda i: (0,) * len(shape))
    return pl.pallas_call(
        functools.partial(_pool_kernel, tiles_per_batch=tpb),
        grid=(t // tm,),
        in_specs=[
            pl.BlockSpec((tm, d), lambda i: (i, 0)),
            _mod_spec(0, d, tpb), _mod_spec(1, d, tpb), _mod_spec(2, d, tpb),
            full((d, d)), full(w_grp.shape), full((1, d)), full((d, d)),
            full((1, d)), full((1, d)),
        ],
        out_specs=pl.BlockSpec((tm, d), lambda i: (i, 0)),
        out_shape=jax.ShapeDtypeStruct((t, d), F32),
        scratch_shapes=[pltpu.VMEM((tm + POOL_HALO, d), F32)],
        compiler_params=_cparams(("arbitrary",)),
    )(x2, mod3, mod3, mod3, w_in, w_grp, ch_scale.reshape(1, d), w_out,
      ln_g.reshape(1, d), ln_b.reshape(1, d))


def _perm_matrix(dil):
    p = jnp.arange(PERM_TILE)
    chunk = PERM_TILE // dil
    src = (p % chunk) * dil + p // chunk
    return (src[:, None] == jnp.arange(PERM_TILE)[None, :]).astype(BF16)


def _qkv_kernel(x_ref, sh_ref, sc_ref, p_ref, w_ref, o_ref, h_ref):
    tm = x_ref.shape[0]
    g = pl.program_id(1)

    @pl.when(g == 0)
    def _():
        h = (x_ref[...] * (1.0 + sc_ref[...]) + sh_ref[...]).astype(BF16)
        h_ref[0] = h
        for gi in range(1, len(ATTN_PATTERNS)):
            for s in range(tm // PERM_TILE):
                rows = slice(s * PERM_TILE, (s + 1) * PERM_TILE)
                h_ref[gi, rows, :] = jnp.dot(
                    p_ref[gi - 1], h[rows, :], preferred_element_type=F32).astype(BF16)

    o_ref[...] = jnp.dot(h_ref[g], w_ref[...], preferred_element_type=F32).astype(BF16)


def _qkv_proj(x2, mod3, perms, w_in, seq):
    t, d = x2.shape
    ng = len(ATTN_PATTERNS)
    tm = 1024
    tpb = seq // tm
    return pl.pallas_call(
        _qkv_kernel,
        grid=(t // tm, ng),
        in_specs=[
            pl.BlockSpec((tm, d), lambda m, g: (m, 0)),
            _mod_spec(0, d, tpb), _mod_spec(1, d, tpb),
            pl.BlockSpec(perms.shape, lambda m, g: (0, 0, 0)),
            pl.BlockSpec((d, 3 * d), lambda m, g: (0, g)),
        ],
        out_specs=pl.BlockSpec((tm, 3 * d), lambda m, g: (m, g)),
        out_shape=jax.ShapeDtypeStruct((t, ng * 3 * d), BF16),
        scratch_shapes=[pltpu.VMEM((ng, tm, d), BF16)],
        compiler_params=_cparams(("arbitrary", "arbitrary")),
    )(x2, mod3, mod3, perms, w_in)


def _attn_kernel(q_ref, kp_ref, kc_ref, vp_ref, vc_ref, o_ref, lse_ref, *, group, dil):
    bq = Q_BLOCK
    d = N_HEADS * HEAD_DIM
    j = pl.program_id(2)
    q = q_ref[...].reshape(bq, d)
    kp = kp_ref[...].reshape(bq, d)
    kc = kc_ref[...].reshape(bq, d)
    vp = vp_ref[...].reshape(bq, d)
    vc = vc_ref[...].reshape(bq, d)

    qi = lax.broadcasted_iota(I32, (bq, 2 * bq), 0)
    kj = lax.broadcasted_iota(I32, (bq, 2 * bq), 1)
    dist = qi + bq - kj
    steps = ATTN_PATTERNS[group][0] // dil
    valid = (dist >= 0) & (dist <= steps) & ((kj >= bq) | (j > 0))
    neg_dist = jnp.where(valid, (dist * -dil).astype(F32), NEG_BIG)
    lane = lax.broadcasted_iota(I32, (bq, LSE_LANES), 1)
    lse_tile = jnp.zeros((bq, LSE_LANES), F32)
    n_tot = len(ATTN_PATTERNS) * N_HEADS
    pair = 2 * HEAD_DIM
    first_half = lax.broadcasted_iota(I32, (bq, pair), 1) < HEAD_DIM
    outs = []
    for hp in range(N_HEADS // 2):
        cols = slice(hp * pair, (hp + 1) * pair)
        q2 = q[:, cols]
        k2 = jnp.concatenate([kp[:, cols], kc[:, cols]], axis=0)
        v2 = jnp.concatenate([vp[:, cols], vc[:, cols]], axis=0)
        halves = []
        for sub in range(2):
            h = 2 * hp + sub
            slope = 2.0 ** (-8.0 * (group * N_HEADS + h + 1) / n_tot)
            mine = first_half if sub == 0 else jnp.logical_not(first_half)
            qh = jnp.where(mine, q2, jnp.zeros_like(q2))
            s = lax.dot_general(qh, k2, (((1,), (1,)), ((), ())), preferred_element_type=F32)
            s = s + slope * neg_dist
            m = jnp.max(s, axis=-1, keepdims=True)
            p = jnp.exp(s - m)
            den = jnp.sum(p, axis=-1, keepdims=True)
            halves.append(jnp.dot(p.astype(BF16), v2, preferred_element_type=F32) / den)
            lse_tile = jnp.where(lane == h, m + jnp.log(den), lse_tile)
        outs.append(jnp.where(first_half, halves[0], halves[1]))
    o_ref[...] = jnp.concatenate(outs, axis=1).astype(BF16).reshape(o_ref.shape)
    lse_ref[...] = lse_tile.reshape(lse_ref.shape)


def _attention_group(qkv, group, batch, seq):
    dil = ATTN_PATTERNS[group][1]
    d = N_HEADS * HEAD_DIM
    t = qkv.shape[0]
    sub = seq // dil
    nb = sub // Q_BLOCK
    rows = Q_BLOCK if dil == 1 else PERM_TILE // dil
    chunks = Q_BLOCK // rows
    u = seq // (rows * dil)
    view = lambda a, c: a.reshape(batch, u, dil, rows, c)
    blk = lambda c: (None, chunks, None, rows, c)
    col0 = group * 3
    q_spec = pl.BlockSpec(blk(d), lambda b, r, j: (b, j, r, 0, col0))
    kc_spec = pl.BlockSpec(blk(d), lambda b, r, j: (b, j, r, 0, col0 + 1))
    kp_spec = pl.BlockSpec(blk(d), lambda b, r, j: (b, jnp.maximum(j - 1, 0), r, 0, col0 + 1))
    vc_spec = pl.BlockSpec(blk(d), lambda b, r, j: (b, j, r, 0, col0 + 2))
    vp_spec = pl.BlockSpec(blk(d), lambda b, r, j: (b, jnp.maximum(j - 1, 0), r, 0, col0 + 2))
    qkv5 = view(qkv, qkv.shape[1])
    o, lse = pl.pallas_call(
        functools.partial(_attn_kernel, group=group, dil=dil),
        grid=(batch, dil, nb),
        in_specs=[q_spec, kp_spec, kc_spec, vp_spec, vc_spec],
        out_specs=[
            pl.BlockSpec(blk(d), lambda b, r, j: (b, j, r, 0, 0)),
            pl.BlockSpec(blk(LSE_LANES), lambda b, r, j: (b, j, r, 0, 0)),
        ],
        out_shape=[
            jax.ShapeDtypeStruct((batch, u, dil, rows, d), BF16),
            jax.ShapeDtypeStruct((batch, u, dil, rows, LSE_LANES), F32),
        ],
        compiler_params=_cparams(("arbitrary", "arbitrary", "arbitrary")),
    )(qkv5, qkv5, qkv5, qkv5, qkv5)
    return o.reshape(t, d), lse.reshape(t, LSE_LANES)


def _split3(v):
    a = v.astype(BF16)
    r = v - a.astype(F32)
    b = r.astype(BF16)
    c = (r - b.astype(F32)).astype(BF16)
    return a, b, c


def _attn_out_kernel(x_ref, g_ref, o0_ref, o1_ref, o2_ref, l0_ref, l1_ref, l2_ref,
                     pt_ref, e_ref, wout_ref, lng_ref, lnb_ref, out_ref):
    tm, d = x_ref.shape
    o_refs = (o0_ref, o1_ref, o2_ref)
    l_refs = (l0_ref, l1_ref, l2_ref)
    n_sub = tm // PERM_TILE

    def unperm(gi, val_bf16):
        if gi == 0:
            return val_bf16.astype(F32)
        parts = [jnp.dot(pt_ref[gi - 1], val_bf16[s * PERM_TILE:(s + 1) * PERM_TILE, :],
                         preferred_element_type=F32) for s in range(n_sub)]
        return jnp.concatenate(parts, axis=0)

    lses = []
    for gi in range(3):
        l = l_refs[gi][...]
        if gi == 0:
            lses.append(l)
        else:
            a, b, c = _split3(l)
            lses.append(unperm(gi, a) + unperm(gi, b) + unperm(gi, c))
    mx = jnp.maximum(jnp.maximum(lses[0], lses[1]), lses[2])
    es = [jnp.exp(l - mx) for l in lses]
    tot = es[0] + es[1] + es[2]
    mixed = jnp.zeros((tm, d), F32)
    for gi in range(3):
        w = es[gi] / tot
        a, b, _ = _split3(w)
        wide = (jnp.dot(a, e_ref[...], preferred_element_type=F32)
                + jnp.dot(b, e_ref[...], preferred_element_type=F32))
        mixed = mixed + wide * unperm(gi, o_refs[gi][...])
    y = jnp.dot(mixed.astype(BF16), wout_ref[...], preferred_element_type=F32)
    v = ALPHA * x_ref[...] + (1.0 + g_ref[...]) * y
    out_ref[...] = _layer_norm(v, lng_ref[...], lnb_ref[...])


def _attn_out(x2, mod3, os_, lses, perms_t, expand, w_out, ln_g, ln_b, seq):
    t, d = x2.shape
    tm = 512
    tpb = seq // tm
    row = lambda c: pl.BlockSpec((tm, c), lambda i: (i, 0))
    full = lambda shape: pl.BlockSpec(shape, lambda i: (0,) * len(shape))
    return pl.pallas_call(
        _attn_out_kernel,
        grid=(t // tm,),
        in_specs=[row(d), _mod_spec(2, d, tpb), row(d), row(d), row(d),
                  row(LSE_LANES), row(LSE_LANES), row(LSE_LANES),
                  full(perms_t.shape), full(expand.shape), full((d, d)),
                  full((1, d)), full((1, d))],
        out_specs=row(d),
        out_shape=jax.ShapeDtypeStruct((t, d), F32),
        compiler_params=_cparams(("arbitrary",)),
    )(x2, mod3, *os_, *lses, perms_t, expand, w_out, ln_g.reshape(1, d), ln_b.reshape(1, d))


def _first_index_of_max(v, iota, size):
    m = jnp.max(v, axis=0, keepdims=True)
    idx = jnp.min(jnp.where(v == m, iota, float(size)), axis=0, keepdims=True)
    return m, idx


def _topk_kernel(x_ref, sh_ref, sc_ref, wr_ref, b_ref, tri_ref, ltri_ref,
                 w_ref, pos_ref, cnt_ref):
    tr = x_ref.shape[0]
    ne = b_ref.shape[0]
    gsz = ne // N_EXPERT_GROUPS
    tile = pl.program_id(0)

    hf = x_ref[...] * (1.0 + sc_ref[...]) + sh_ref[...]
    hf_hi = hf.astype(BF16)
    hf_lo = (hf - hf_hi.astype(F32)).astype(BF16)
    nt = (((1,), (1,)), ((), ()))
    by_hi = lax.dot_general(wr_ref[...], hf_hi, nt, preferred_element_type=F32)
    logits = (by_hi[:ne, :] + by_hi[ne:, :]
              + lax.dot_general(wr_ref[0:ne, :], hf_lo, nt, preferred_element_type=F32))
    scores = 1.0 / (1.0 + jnp.exp(-logits))
    sel = scores + b_ref[...]
    iota_g = lax.broadcasted_iota(I32, (gsz, tr), 0).astype(F32)
    iota_n = lax.broadcasted_iota(I32, (N_EXPERT_GROUPS, tr), 0).astype(F32)
    gs = jnp.zeros((N_EXPERT_GROUPS, tr), F32)
    for g in range(N_EXPERT_GROUPS):
        blk = sel[g * gsz:(g + 1) * gsz, :]
        m1, i1 = _first_index_of_max(blk, iota_g, gsz)
        m2 = jnp.max(jnp.where(iota_g == i1, -jnp.inf, blk), axis=0, keepdims=True)
        gs = jnp.where(iota_n == float(g), m1 + m2, gs)
    gmask = jnp.zeros((N_EXPERT_GROUPS, tr), F32)
    for _ in range(TOPK_GROUPS):
        _, gi = _first_index_of_max(gs, iota_n, N_EXPERT_GROUPS)
        hit = iota_n == gi
        gmask = jnp.where(hit, 1.0, gmask)
        gs = jnp.where(hit, -jnp.inf, gs)
    masked_rows = []
    for g in range(N_EXPERT_GROUPS):
        keep = jnp.broadcast_to(gmask[g:g + 1, :], (gsz, tr)) > 0.5
        masked_rows.append(jnp.where(keep, sel[g * gsz:(g + 1) * gsz, :], -jnp.inf))
    cur = jnp.concatenate(masked_rows, axis=0)
    iota_e = lax.broadcasted_iota(I32, (ne, tr), 0).astype(F32)
    chosen = jnp.zeros((ne, tr), F32)
    for _ in range(TOP_K):
        _, ei = _first_index_of_max(cur, iota_e, ne)
        hit = iota_e == ei
        cur = jnp.where(hit, -jnp.inf, cur)
        chosen = jnp.where(hit, 1.0, chosen)
    picked = chosen > 0.5
    top_scores = jnp.where(picked, scores, 0.0)
    wsum = jnp.sum(top_scores, axis=0, keepdims=True)
    w_ref[...] = top_scores / wsum * ROUTED_SCALE
    before = jnp.dot(chosen.astype(BF16), tri_ref[...], preferred_element_type=F32)
    n = jnp.sum(chosen, axis=1, keepdims=True)
    n_chunks = jnp.floor((n + (CHUNK - 1)) * (1.0 / CHUNK))
    run_off = jnp.dot(ltri_ref[...], jnp.broadcast_to(n_chunks, (ne, 128)).astype(BF16),
                      preferred_element_type=F32)[:, 0:1] * CHUNK
    pos_ref[...] = jnp.where(picked, before + run_off, NO_SLOT)

    @pl.when(tile == 0)
    def _():
        cnt_ref[...] = jnp.zeros_like(cnt_ref)

    lane = lax.broadcasted_iota(I32, cnt_ref.shape, 1)
    cnt_ref[...] = jnp.where(lane == tile, jnp.broadcast_to(n, cnt_ref.shape).astype(I32),
                             cnt_ref[...])


def _topk_route(x2, mod3, w_router, router_b, seq):
    t, d = x2.shape
    ne = w_router.shape[1]
    tr = TOK_TILE
    tpb = seq // tr
    assert t // tr <= 128
    w_t = w_router.T
    w_hi = w_t.astype(BF16)
    w_split = jnp.concatenate([w_hi, (w_t - w_hi.astype(F32)).astype(BF16)], axis=0)
    tri = (jnp.arange(tr)[:, None] < jnp.arange(tr)[None, :]).astype(BF16)
    ltri = (jnp.arange(ne)[None, :] < jnp.arange(ne)[:, None]).astype(BF16)
    out = lambda dt: jax.ShapeDtypeStruct((ne, t), dt)
    row = pl.BlockSpec((ne, tr), lambda i: (0, i))
    return pl.pallas_call(
        _topk_kernel,
        grid=(t // tr,),
        in_specs=[pl.BlockSpec((tr, d), lambda i: (i, 0)),
                  _mod_spec(3, d, tpb), _mod_spec(4, d, tpb),
                  pl.BlockSpec((2 * ne, d), lambda i: (0, 0)),
                  pl.BlockSpec((ne, 1), lambda i: (0, 0)),
                  pl.BlockSpec((tr, tr), lambda i: (0, 0)),
                  pl.BlockSpec((ne, ne), lambda i: (0, 0))],
        out_specs=[row, row, pl.BlockSpec((ne, 128), lambda i: (0, 0))],
        out_shape=[out(F32), out(F32), jax.ShapeDtypeStruct((ne, 128), I32)],
        compiler_params=_cparams(("arbitrary",)),
    )(x2, mod3, mod3, w_split, router_b.reshape(ne, 1), tri, ltri)


def _slot_plan(counts, n_tiles, n_blocks):
    n = counts[:, :n_tiles]
    nch = (n + (CHUNK - 1)) // CHUNK
    rows = jnp.sum(nch, axis=1) * CHUNK
    region = ((rows + EXPERT_ROWS - 1) // EXPERT_ROWS) * EXPERT_ROWS
    region_end = jnp.cumsum(region)
    region_start = region_end - region
    run_chunk = region_start[:, None] // CHUNK + jnp.cumsum(nch, axis=1) - nch
    local_end = jnp.cumsum(nch, axis=0)
    local_chunk = local_end - nch
    experts = jnp.arange(N_EXPERTS, dtype=I32)
    copies = []
    done = jnp.zeros_like(nch)
    for size in COPY_SIZES:
        k = (nch - done) // size
        end = jnp.cumsum(k, axis=0)
        cap = _copy_cap(size)
        m = jnp.arange(cap, dtype=I32)
        run_of_m = jnp.sum((end[:, :, None] <= m[None, None, :]).astype(I32), axis=0)
        owner = run_of_m[None] == experts[:, None, None]
        pick = lambda a: jnp.sum(jnp.where(owner, a[:, :, None], 0), axis=0)
        within = (m[None, :] - pick(end - k)) * size
        copies += [(pick(local_chunk + done) + within).reshape(-1).astype(I32),
                   (pick(run_chunk + done) + within).reshape(-1).astype(I32),
                   end[-1].astype(I32)]
        done = done + k * size
    block_row = jnp.arange(n_blocks, dtype=I32) * EXPERT_ROWS
    group_e = jnp.minimum(jnp.sum((region_end[None, :] <= block_row[:, None]).astype(I32), axis=1),
                          N_EXPERTS - 1)
    n_used = region_end[-1] // EXPERT_ROWS
    nonempty = region > 0
    run_of_expert = jnp.cumsum(nonempty.astype(I32)) - 1
    run_expert = jnp.sum(jnp.where((run_of_expert[None, :] == experts[:, None]) & nonempty[None, :],
                                   experts[None, :], 0), axis=1)
    blocks = jnp.arange(n_blocks, dtype=I32)
    first = ((blocks == 0) | (group_e != jnp.roll(group_e, 1))) & (blocks < n_used)
    per_tile = lambda a: (a * CHUNK).T.astype(I32)
    return dict(
        run_begin_rows=per_tile(local_chunk)[:, None, :], run_end_rows=per_tile(local_end)[:, None, :],
        run_begin_cols=per_tile(local_chunk)[:, :, None], run_end_cols=per_tile(local_end)[:, :, None],
        block_run=(jnp.cumsum(first.astype(I32)) - 1).astype(I32), block_first=first.astype(I32),
        run_expert=run_expert.astype(I32), n_runs=jnp.sum(nonempty.astype(I32)).reshape(1),
        copies=copies,
        tile_chunks=jnp.sum(nch, axis=0).astype(I32),
        tail_chunk=((region_start + rows) // CHUNK).astype(I32),
        tail_chunks=((region - rows) // CHUNK).astype(I32),
        group_e=group_e, n_used=(region_end[-1:] // EXPERT_ROWS).astype(I32))


def _chunk_rows(ref, chunk_index, n_chunks=1):
    start = chunk_index * CHUNK
    if not isinstance(start, int):
        start = pl.multiple_of(start, CHUNK)
    return ref.at[pl.ds(start, n_chunks * CHUNK), :]


def _copy_cap(size):
    return LOCAL_ROWS // CHUNK // size if size == COPY_SIZES[0] else N_EXPERTS


def _start_copies(copy_refs, tile, make_copy):
    for n, size in enumerate(COPY_SIZES):
        src_ref, dst_ref, cnt_ref = copy_refs[3 * n:3 * n + 3]
        base = tile * _copy_cap(size)

        def start(m, priority, src_ref=src_ref, dst_ref=dst_ref, base=base, size=size):
            make_copy(src_ref[base + m], dst_ref[base + m], size).start(priority=priority)

        _for_chunks_two_queues(cnt_ref[tile], start)


def _pos_digits(pos):
    hi = jnp.floor(pos * (1.0 / POS_RADIX))
    return hi.astype(BF16), (pos - POS_RADIX * hi).astype(BF16)


def _for_chunks_two_queues(count, start):
    def pair(p, carry):
        start(2 * p, 0)
        start(2 * p + 1, 1)
        return carry

    lax.fori_loop(0, lax.shift_right_logical(count, 1), pair, 0)

    @pl.when((count & 1) == 1)
    def _():
        start(count - 1, 0)


def _wait_chunks(count, src_ref, dst_ref, sem):
    group_rows = pl.ds(0, WAIT_GROUP * CHUNK)

    def wait_group(c, carry):
        pltpu.make_async_copy(src_ref.at[group_rows, :], dst_ref.at[group_rows, :], sem).wait()
        return carry

    def wait_one(c, carry):
        pltpu.make_async_copy(_chunk_rows(src_ref, 0), _chunk_rows(dst_ref, 0), sem).wait()
        return carry

    lax.fori_loop(0, lax.shift_right_logical(count, WAIT_GROUP.bit_length() - 1), wait_group, 0)
    lax.fori_loop(0, count & (WAIT_GROUP - 1), wait_one, 0)


def _dispatch_kernel(*refs):
    copy_refs, refs = refs[:3 * len(COPY_SIZES)], refs[3 * len(COPY_SIZES):]
    _dispatch_body(copy_refs, *refs)


def _dispatch_body(copy_refs, tile_ref, tail_ref, tailn_ref, used_ref,
                     x_ref, sh_ref, sc_ref, pos_ref, begin_ref, end_ref,
                     xg_hbm, stage_ref, zero_ref, sems, sem_blk):
    tr = x_ref.shape[0]
    i = pl.program_id(0)
    n_blocks = xg_hbm.shape[0] // EXPERT_ROWS

    def drain(count, sem):
        _wait_chunks(count, stage_ref.at[0], xg_hbm, sem)

    @pl.when(i == 0)
    def _():
        zero_ref[...] = jnp.zeros_like(zero_ref)

        def block_copy(j):
            rows = pl.ds(pl.multiple_of(j * EXPERT_ROWS, EXPERT_ROWS), EXPERT_ROWS)
            return pltpu.make_async_copy(zero_ref, xg_hbm.at[rows, :], sem_blk)

        def start_block(j, carry):
            block_copy(j).start()
            return carry

        def wait_block(j, carry):
            block_copy(j).wait()
            return carry

        lax.fori_loop(used_ref[0], n_blocks, start_block, 0)

        def per_expert(e, total):
            def per_chunk(c, carry):
                pltpu.make_async_copy(_chunk_rows(zero_ref, 0),
                                      _chunk_rows(xg_hbm, tail_ref[e] + c), sems.at[0]).start()
                return carry
            lax.fori_loop(0, tailn_ref[e], per_chunk, 0)
            return total + tailn_ref[e]

        drain(lax.fori_loop(0, N_EXPERTS, per_expert, 0), sems.at[0])
        lax.fori_loop(used_ref[0], n_blocks, wait_block, 0)

    hf = (x_ref[...] * (1.0 + sc_ref[...]) + sh_ref[...]).astype(BF16)
    row_e = lax.broadcasted_iota(I32, (LOCAL_ROWS, N_EXPERTS), 0)
    owner = jnp.where((row_e >= begin_ref[...]) & (row_e < end_ref[...]), 1.0, 0.0).astype(BF16)
    hi_digit, lo_digit = _pos_digits(pos_ref[...])
    slot_pos = (POS_RADIX * jnp.dot(owner, hi_digit, preferred_element_type=F32)
                + jnp.dot(owner, lo_digit, preferred_element_type=F32))
    row_t = lax.broadcasted_iota(I32, (LOCAL_ROWS, tr), 0).astype(F32)
    perm = jnp.where(slot_pos == row_t, 1.0, 0.0).astype(BF16)
    rows = jnp.dot(perm, hf, preferred_element_type=F32)
    half = rows.shape[1] // 2
    buf = stage_ref.at[i % 2]
    buf[...] = (lax.bitcast_convert_type(rows[:, :half], U32)
                | (lax.bitcast_convert_type(rows[:, half:], U32) >> 16))

    _start_copies(copy_refs, i, lambda src, dst, n: pltpu.make_async_copy(
        _chunk_rows(buf, src, n), _chunk_rows(xg_hbm, dst, n), sems.at[i % 2]))

    @pl.when(i > 0)
    def _():
        drain(tile_ref[jnp.maximum(i - 1, 0)], sems.at[(i + 1) % 2])

    @pl.when(i == pl.num_programs(0) - 1)
    def _():
        drain(tile_ref[i], sems.at[i % 2])


def _dispatch(plan, x2, mod3, pos, n_slots, seq):
    t, d = x2.shape
    tr = TOK_TILE
    tpb = seq // tr
    return pl.pallas_call(
        _dispatch_kernel,
        grid_spec=pltpu.PrefetchScalarGridSpec(
            num_scalar_prefetch=4 + len(plan["copies"]),
            grid=(t // tr,),
            in_specs=[pl.BlockSpec((tr, d), lambda i, *_: (i, 0)),
                      _mod_spec(3, d, tpb), _mod_spec(4, d, tpb),
                      pl.BlockSpec((N_EXPERTS, tr), lambda i, *_: (0, i)),
                      pl.BlockSpec((None, 1, N_EXPERTS), lambda i, *_: (i, 0, 0)),
                      pl.BlockSpec((None, 1, N_EXPERTS), lambda i, *_: (i, 0, 0))],
            out_specs=pl.BlockSpec(memory_space=pl.ANY),
            scratch_shapes=[pltpu.VMEM((2, LOCAL_ROWS, d // 2), U32),
                            pltpu.VMEM((EXPERT_ROWS, d // 2), U32),
                            pltpu.SemaphoreType.DMA((2,)), pltpu.SemaphoreType.DMA],
        ),
        out_shape=jax.ShapeDtypeStruct((n_slots, d // 2), U32),
        compiler_params=_cparams(("arbitrary",)),
    )(*plan["copies"], plan["tile_chunks"], plan["tail_chunk"], plan["tail_chunks"],
      plan["n_used"], x2, mod3, mod3, pos, plan["run_begin_rows"], plan["run_end_rows"])


def _expert_kernel(run_ref, first_ref, rexp_ref, nrun_ref, nu_ref,
                   xg_hbm, wgu_hbm, wdn_hbm, yg_hbm,
                   xbuf, ybuf, wgu_f32, wdn_f32, wgu_bf, wdn_bf, x_sem, y_sem, w_sem, z_sem,
                   *, layer):
    n_used = nu_ref[0]
    n_blocks = yg_hbm.shape[0] // EXPERT_ROWS
    depth, _, half = xbuf.shape
    f = wdn_bf.shape[0]

    def rows(j):
        return pl.ds(pl.multiple_of(j * EXPERT_ROWS, EXPERT_ROWS), EXPERT_ROWS)

    def x_copy(j):
        return pltpu.make_async_copy(xg_hbm.at[rows(j), :], xbuf.at[j % depth],
                                     x_sem.at[j % depth])

    def y_copy(j):
        return pltpu.make_async_copy(ybuf.at[j % depth], yg_hbm.at[rows(j), :],
                                     y_sem.at[j % depth])

    def w_copies(r):
        e = rexp_ref[r]
        return (pltpu.make_async_copy(wgu_hbm.at[layer, e], wgu_f32.at[r % 2], w_sem.at[r % 2]),
                pltpu.make_async_copy(wdn_hbm.at[layer, e], wdn_f32.at[r % 2], w_sem.at[r % 2]))

    def zero_copy(j):
        return pltpu.make_async_copy(ybuf.at[0], yg_hbm.at[rows(j), :], z_sem)

    ybuf[0] = jnp.zeros(ybuf.shape[1:], U32)

    def zero_start(j, carry):
        zero_copy(j).start()
        return carry

    def zero_wait(j, carry):
        zero_copy(j).wait()
        return carry

    lax.fori_loop(n_used, n_blocks, zero_start, 0)
    lax.fori_loop(n_used, n_blocks, zero_wait, 0)

    for cp in w_copies(0):
        cp.start()
    for a in range(depth - 1):
        @pl.when(a < n_used)
        def _():
            x_copy(a).start()

    def block(j, carry):
        r = run_ref[j]

        @pl.when(first_ref[j] == 1)
        def _():
            for cp in w_copies(r):
                cp.wait()

            @pl.when(r + 1 < nrun_ref[0])
            def _():
                for cp in w_copies(r + 1):
                    cp.start(priority=1)

            wgu_bf[...] = wgu_f32[r % 2].astype(BF16)
            wdn_bf[...] = wdn_f32[r % 2].astype(BF16)

        x_copy(j).wait()

        @pl.when(j + (depth - 1) < n_used)
        def _():
            x_copy(j + (depth - 1)).start()

        @pl.when(j >= depth)
        def _():
            y_copy(j - depth).wait()

        for g in range(EXPERT_ROWS // GEMM_SUB_ROWS):
            rws = slice(g * GEMM_SUB_ROWS, (g + 1) * GEMM_SUB_ROWS)
            hi, lo = _unpack_bf16_pairs(xbuf[j % depth, rws, :])
            gu = (jnp.dot(hi.astype(BF16), wgu_bf[0:half, :], preferred_element_type=F32)
                  + jnp.dot(lo.astype(BF16), wgu_bf[half:, :], preferred_element_type=F32))
            act = (_silu(gu[:, :f]) * gu[:, f:]).astype(BF16)
            ybuf[j % depth, rws, :] = _pack_bf16_pairs(
                jnp.dot(act, wdn_bf[...], preferred_element_type=F32))
        y_copy(j).start(priority=1)
        return carry

    lax.fori_loop(0, n_used, block, 0)

    for a in range(depth, 0, -1):
        @pl.when(n_used >= a)
        def _():
            y_copy(n_used - a).wait()


def _expert_gemm(plan, xg, w_gu, w_down, layer):
    n_slots, half = xg.shape
    _, ne, d, f2 = w_gu.shape
    f = w_down.shape[2]
    hbm = pl.BlockSpec(memory_space=pl.ANY)
    return pl.pallas_call(
        functools.partial(_expert_kernel, layer=layer),
        grid_spec=pltpu.PrefetchScalarGridSpec(
            num_scalar_prefetch=5,
            grid=(1,),
            in_specs=[hbm, hbm, hbm],
            out_specs=hbm,
            scratch_shapes=[pltpu.VMEM((GEMM_DEPTH, EXPERT_ROWS, half), U32),
                            pltpu.VMEM((GEMM_DEPTH, EXPERT_ROWS, half), U32),
                            pltpu.VMEM((2, d, f2), F32), pltpu.VMEM((2, f, d), F32),
                            pltpu.VMEM((d, f2), BF16), pltpu.VMEM((f, d), BF16),
                            pltpu.SemaphoreType.DMA((GEMM_DEPTH,)),
                            pltpu.SemaphoreType.DMA((GEMM_DEPTH,)),
                            pltpu.SemaphoreType.DMA((2,)), pltpu.SemaphoreType.DMA],
        ),
        out_shape=jax.ShapeDtypeStruct((n_slots, half), U32),
        compiler_params=_cparams(("arbitrary",)),
    )(plan["block_run"], plan["block_first"], plan["run_expert"], plan["n_runs"], plan["n_used"],
      xg, w_gu, w_down)


def _combine_kernel(*refs):
    copy_refs, refs = refs[:3 * len(COPY_SIZES)], refs[3 * len(COPY_SIZES):]
    _combine_body(copy_refs, *refs)


def _combine_body(copy_refs, tile_ref,
                    x_ref, sh_ref, sc_ref, g_ref, pos_ref, wt_ref, begin_ref, end_ref, brow_ref,
                    wsgu_ref, wsdn_ref, lng_ref, lnb_ref, yg_hbm, o_ref, stage_ref, sems):
    tr, d = x_ref.shape
    i = pl.program_id(0)
    n_tiles = pl.num_programs(0)

    def fetch(tile, buffer):
        _start_copies(copy_refs, tile, lambda src, dst, n: pltpu.make_async_copy(
            _chunk_rows(yg_hbm, dst, n), _chunk_rows(stage_ref.at[buffer], src, n),
            sems.at[buffer]))

    @pl.when(i == 0)
    def _():
        stage_ref[...] = jnp.zeros_like(stage_ref)
        fetch(0, 0)

    nxt = jnp.minimum(i + 1, n_tiles - 1)
    for buffer in range(2):
        @pl.when((i + 1 < n_tiles) & (nxt % 2 == buffer))
        def _():
            fetch(nxt, buffer)

    x = x_ref[...]
    hf = (x * (1.0 + sc_ref[...]) + sh_ref[...]).astype(BF16)
    f = wsdn_ref.shape[0]
    su = jnp.dot(hf, wsgu_ref[...], preferred_element_type=F32)
    act = (_silu(su[:, :f]) * su[:, f:]).astype(BF16)
    shared = jnp.dot(act, wsdn_ref[...], preferred_element_type=F32)

    col_e = lax.broadcasted_iota(I32, (N_EXPERTS, LOCAL_ROWS), 1)
    owner = jnp.where((col_e >= begin_ref[...]) & (col_e < end_ref[...]), 1.0, 0.0).astype(BF16)
    begin_row = brow_ref[...].astype(F32)
    pos = pos_ref[...]
    rank1 = jnp.where(pos >= 0.0, pos - begin_row + 1.0, 0.0).astype(BF16)
    slot_rank = jnp.dot(rank1, owner, preferred_element_type=F32)
    b_hi, b_lo = _pos_digits(jnp.broadcast_to(begin_row, (8, N_EXPERTS)))
    run_begin = (POS_RADIX * jnp.dot(b_hi, owner, preferred_element_type=F32)
                 + jnp.dot(b_lo, owner, preferred_element_type=F32))[0:1, :]
    offset1 = lax.broadcasted_iota(I32, (1, LOCAL_ROWS), 1).astype(F32) - run_begin + 1.0
    holds = slot_rank == offset1
    weights = jnp.where(holds, jnp.dot(wt_ref[...].astype(BF16), owner,
                                       preferred_element_type=F32), 0.0).astype(BF16)

    buf = stage_ref.at[i % 2]
    _wait_chunks(tile_ref[i], yg_hbm, buf, sems.at[i % 2])

    y_hi, y_lo = _unpack_bf16_pairs(buf[...])
    y = jnp.concatenate([y_hi.astype(BF16), y_lo.astype(BF16)], axis=1)
    routed = jnp.dot(weights, y, preferred_element_type=F32)
    v = ALPHA * x + (1.0 + g_ref[...]) * (routed + shared)
    o_ref[...] = _layer_norm(v, lng_ref[...], lnb_ref[...])


def _combine(plan, x2, mod3, pos_tok, w_tok, w_sh_gu, w_sh_down, ln_g, ln_b, yg, seq):
    t, d = x2.shape
    tr = TOK_TILE
    tpb = seq // tr
    full = lambda shape: pl.BlockSpec(shape, lambda i, *_: (0,) * len(shape))
    return pl.pallas_call(
        _combine_kernel,
        grid_spec=pltpu.PrefetchScalarGridSpec(
            num_scalar_prefetch=1 + len(plan["copies"]),
            grid=(t // tr,),
            in_specs=[pl.BlockSpec((tr, d), lambda i, *_: (i, 0)),
                      _mod_spec(3, d, tpb), _mod_spec(4, d, tpb), _mod_spec(5, d, tpb),
                      pl.BlockSpec((tr, N_EXPERTS), lambda i, *_: (i, 0)),
                      pl.BlockSpec((tr, N_EXPERTS), lambda i, *_: (i, 0)),
                      pl.BlockSpec((None, N_EXPERTS, 1), lambda i, *_: (i, 0, 0)),
                      pl.BlockSpec((None, N_EXPERTS, 1), lambda i, *_: (i, 0, 0)),
                      pl.BlockSpec((None, 1, N_EXPERTS), lambda i, *_: (i, 0, 0)),
                      full(w_sh_gu.shape), full(w_sh_down.shape), full((1, d)), full((1, d)),
                      pl.BlockSpec(memory_space=pl.ANY)],
            out_specs=pl.BlockSpec((tr, d), lambda i, *_: (i, 0)),
            scratch_shapes=[pltpu.VMEM((2, LOCAL_ROWS, d // 2), U32),
                            pltpu.SemaphoreType.DMA((2,))],
        ),
        out_shape=jax.ShapeDtypeStruct((t, d), F32),
        compiler_params=_cparams(("arbitrary",)),
    )(*plan["copies"], plan["tile_chunks"],
      x2, mod3, mod3, mod3, pos_tok, w_tok, plan["run_begin_cols"], plan["run_end_cols"],
      plan["run_begin_rows"],
      w_sh_gu, w_sh_down, ln_g.reshape(1, d), ln_b.reshape(1, d), yg)


def _moe_layer(x2, mod3, w_router, router_b, w_gu, w_down, layer, w_sh_gu, w_sh_down,
               ln_g, ln_b, seq):
    t, d = x2.shape
    n_tiles = t // TOK_TILE
    w_top, pos, counts = _topk_route(x2, mod3, w_router, router_b, seq)
    bound = t * TOP_K + n_tiles * N_EXPERTS * (CHUNK - 1) + N_EXPERTS * (EXPERT_ROWS - 1)
    n_blocks = -(-bound // EXPERT_ROWS)
    plan = _slot_plan(counts, n_tiles, n_blocks)
    xg = _dispatch(plan, x2, mod3, pos, n_blocks * EXPERT_ROWS, seq)
    yg = _expert_gemm(plan, xg, w_gu, w_down, layer)
    return _combine(plan, x2, mod3, pos.T, w_top.T, w_sh_gu, w_sh_down, ln_g, ln_b, yg, seq)


def kernel(x, c, ada_w, ada_b, pool_w_in, pool_w_grp, pool_scale, pool_w_out, attn_w_in, attn_w_out, ln1_g, ln1_b, router_w, router_b, exp_w_gu, exp_w_down, sh_w_gu, sh_w_down, ln2_g, ln2_b):
    batch, seq, d = x.shape
    depth = ada_w.shape[0]
    t = batch * seq
    mod = _modulation(c, ada_w, ada_b).reshape(depth, batch, 1, 6 * d)
    perms = jnp.stack([_perm_matrix(dil) for _, dil in ATTN_PATTERNS[1:]])
    perms_t = jnp.swapaxes(perms, 1, 2)
    expand = (jnp.arange(LSE_LANES)[:, None] == (jnp.arange(d)[None, :] // HEAD_DIM)).astype(BF16)
    part = (jnp.arange(attn_w_in.shape[2]) // d) % 3
    q_scale = jnp.where(part == 0, HEAD_DIM ** -0.5, 1.0).astype(F32)
    x2 = x.reshape(t, d)
    for i in range(depth):
        mod3 = mod[i]
        j = i // 2
        if i % 2 == 0:
            x2 = _pool_layer(x2, mod3, pool_w_in[j].astype(BF16), pool_w_grp[j].astype(BF16),
                             pool_scale[j], pool_w_out[j].astype(BF16), ln1_g[i], ln1_b[i], seq)
        else:
            qkv = _qkv_proj(x2, mod3, perms, (attn_w_in[j] * q_scale).astype(BF16), seq)
            res = [_attention_group(qkv, g, batch, seq) for g in range(len(ATTN_PATTERNS))]
            x2 = _attn_out(x2, mod3, [r[0] for r in res], [r[1] for r in res], perms_t, expand,
                           attn_w_out[j].astype(BF16), ln1_g[i], ln1_b[i], seq)
        x2 = _moe_layer(x2, mod3, router_w[i], router_b[i], exp_w_gu, exp_w_down, i,
                        sh_w_gu[i].astype(BF16), sh_w_down[i].astype(BF16),
                        ln2_g[i], ln2_b[i], seq)
    return x2.reshape(batch, seq, d)
```

```python
import functools
import math

import jax
import jax.numpy as jnp
from jax import lax
from jax.experimental import pallas as pl
from jax.experimental.pallas import tpu as pltpu

F32 = jnp.float32
BF16 = jnp.bfloat16
U32 = jnp.uint32
I32 = jnp.int32

POOL_WINDOWS = (2, 4, 8, 16)
ATTN_PATTERNS = ((128, 1), (512, 4), (2048, 16))
HEAD_DIM = 64
N_HEADS = 16
Q_BLOCK = 128
N_EXPERTS = 64
TOP_K = 8
N_EXPERT_GROUPS = 8
TOPK_GROUPS = 4
ROUTED_SCALE = 2.5
EXPERT_ROWS = 512
DEPTH = 4
ALPHA = (2 * DEPTH) ** 0.25
LN_EPS = 1e-5

PERM_TILE = 256
POOL_HALO = 16
TOK_TILE = 256
CHUNK = 8
LOCAL_ROWS = -(-(TOK_TILE * TOP_K + N_EXPERTS * (CHUNK - 1)) // 256) * 256
NO_SLOT = -64.0
POS_RADIX = 64.0
COPY_SIZES = (4, 2, 1)
WAIT_GROUP = 32
GEMM_SUB_ROWS = 256
GEMM_DEPTH = 4
LSE_LANES = 128
VMEM_LIMIT = 56 * 1024 * 1024
NEG_BIG = -1e30


def _cparams(sem):
    return pltpu.CompilerParams(dimension_semantics=sem, vmem_limit_bytes=VMEM_LIMIT)


def _layer_norm(v, g, b):
    mu = jnp.mean(v, axis=-1, keepdims=True)
    c = v - mu
    var = jnp.mean(c * c, axis=-1, keepdims=True)
    return c * lax.rsqrt(var + LN_EPS) * g + b


def _silu(v):
    return v * (1.0 / (1.0 + jnp.exp(-v)))


def _pack_bf16_pairs(v):
    n = v.shape[1] // 2
    hi = lax.bitcast_convert_type(v[:, :n].astype(BF16).astype(F32), U32)
    lo = lax.bitcast_convert_type(v[:, n:].astype(BF16).astype(F32), U32)
    return hi | (lo >> 16)


def _unpack_bf16_pairs(p):
    hi = lax.bitcast_convert_type(p & jnp.uint32(0xFFFF0000), F32)
    lo = lax.bitcast_convert_type(p << 16, F32)
    return hi, lo


def _mod_kernel(c_ref, w_ref, b_ref, o_ref):
    cs = _silu(c_ref[...])
    o_ref[...] = jnp.dot(cs, w_ref[...], preferred_element_type=F32) + b_ref[...]


def _modulation(c, ada_w, ada_b):
    depth, d, n6 = ada_w.shape
    b = c.shape[0]
    tn = 1536
    return pl.pallas_call(
        _mod_kernel,
        grid=(depth, n6 // tn),
        in_specs=[
            pl.BlockSpec((b, d), lambda i, n: (0, 0)),
            pl.BlockSpec((None, d, tn), lambda i, n: (i, 0, n)),
            pl.BlockSpec((None, 1, tn), lambda i, n: (i, 0, n)),
        ],
        out_specs=pl.BlockSpec((None, b, tn), lambda i, n: (i, 0, n)),
        out_shape=jax.ShapeDtypeStruct((depth, b, n6), F32),
        compiler_params=_cparams(("arbitrary", "arbitrary")),
    )(c, ada_w, ada_b.reshape(depth, 1, n6))


def _mod_spec(chunk, d, tiles_per_batch):
    return pl.BlockSpec((None, 1, d), lambda *idx: (idx[0] // tiles_per_batch, 0, chunk))


def _pool_kernel(x_ref, sh_ref, sc_ref, g_ref, win_ref, wgrp_ref, cs_ref, wout_ref,
                 lng_ref, lnb_ref, o_ref, ext_ref, *, tiles_per_batch):
    tm, d = x_ref.shape
    s_idx = pl.program_id(0) % tiles_per_batch
    x = x_ref[...]
    h = (x * (1.0 + sc_ref[...]) + sh_ref[...]).astype(BF16)
    u = jnp.dot(h, win_ref[...], preferred_element_type=F32)

    @pl.when(s_idx == 0)
    def _():
        ext_ref[0:POOL_HALO, :] = jnp.zeros((POOL_HALO, d), F32)

    @pl.when(s_idx != 0)
    def _():
        ext_ref[0:POOL_HALO, :] = ext_ref[tm:tm + POOL_HALO, :]

    ext_ref[POOL_HALO:POOL_HALO + tm, :] = u

    pos = s_idx * tm + lax.broadcasted_iota(I32, (tm, 1), 0) + 1
    gc = d // len(POOL_WINDOWS)
    ys = []
    for g, w in enumerate(POOL_WINDOWS):
        cols = slice(g * gc, (g + 1) * gc)
        acc = u[:, cols]
        for j in range(1, w):
            acc = acc + ext_ref[POOL_HALO - j:POOL_HALO - j + tm, cols]
        cnt = jnp.minimum(pos, w).astype(F32)
        z = (acc / cnt - u[:, cols]).astype(BF16)
        ys.append(jnp.dot(z, wgrp_ref[g], preferred_element_type=F32))
    y = (jnp.concatenate(ys, axis=1) * cs_ref[...]).astype(BF16)
    out = jnp.dot(y, wout_ref[...], preferred_element_type=F32)
    v = ALPHA * x + (1.0 + g_ref[...]) * out
    o_ref[...] = _layer_norm(v, lng_ref[...], lnb_ref[...])


def _pool_layer(x2, mod3, w_in, w_grp, ch_scale, w_out, ln_g, ln_b, seq):
    t, d = x2.shape
    tm = 512
    tpb = seq // tm
    full = lambda shape: pl.BlockSpec(shape, lambda i: (0,) * len(shape))
    return pl.pallas_call(
        functools.partial(_pool_kernel, tiles_per_batch=tpb),
        grid=(t // tm,),
        in_specs=[
            pl.BlockSpec((tm, d), lambda i: (i, 0)),
            _mod_spec(0, d, tpb), _mod_spec(1, d, tpb), _mod_spec(2, d, tpb),
            full((d, d)), full(w_grp.shape), full((1, d)), full((d, d)),
            full((1, d)), full((1, d)),
        ],
        out_specs=pl.BlockSpec((tm, d), lambda i: (i, 0)),
        out_shape=jax.ShapeDtypeStruct((t, d), F32),
        scratch_shapes=[pltpu.VMEM((tm + POOL_HALO, d), F32)],
        compiler_params=_cparams(("arbitrary",)),
    )(x2, mod3, mod3, mod3, w_in, w_grp, ch_scale.reshape(1, d), w_out,
      ln_g.reshape(1, d), ln_b.reshape(1, d))


def _perm_matrix(dil):
    p = jnp.arange(PERM_TILE)
    chunk = PERM_TILE // dil
    src = (p % chunk) * dil + p // chunk
    return (src[:, None] == jnp.arange(PERM_TILE)[None, :]).astype(BF16)


def _qkv_kernel(x_ref, sh_ref, sc_ref, p_ref, w_ref, o_ref, h_ref):
    tm = x_ref.shape[0]
    g = pl.program_id(1)

    @pl.when(g == 0)
    def _():
        h = (x_ref[...] * (1.0 + sc_ref[...]) + sh_ref[...]).astype(BF16)
        h_ref[0] = h
        for gi in range(1, len(ATTN_PATTERNS)):
            for s in range(tm // PERM_TILE):
                rows = slice(s * PERM_TILE, (s + 1) * PERM_TILE)
                h_ref[gi, rows, :] = jnp.dot(
                    p_ref[gi - 1], h[rows, :], preferred_element_type=F32).astype(BF16)

    o_ref[...] = jnp.dot(h_ref[g], w_ref[...], preferred_element_type=F32).astype(BF16)


def _qkv_proj(x2, mod3, perms, w_in, seq):
    t, d = x2.shape
    ng = len(ATTN_PATTERNS)
    tm = 1024
    tpb = seq // tm
    return pl.pallas_call(
        _qkv_kernel,
        grid=(t // tm, ng),
        in_specs=[
            pl.BlockSpec((tm, d), lambda m, g: (m, 0)),
            _mod_spec(0, d, tpb), _mod_spec(1, d, tpb),
            pl.BlockSpec(perms.shape, lambda m, g: (0, 0, 0)),
            pl.BlockSpec((d, 3 * d), lambda m, g: (0, g)),
        ],
        out_specs=pl.BlockSpec((tm, 3 * d), lambda m, g: (m, g)),
        out_shape=jax.ShapeDtypeStruct((t, ng * 3 * d), BF16),
        scratch_shapes=[pltpu.VMEM((ng, tm, d), BF16)],
        compiler_params=_cparams(("arbitrary", "arbitrary")),
    )(x2, mod3, mod3, perms, w_in)


def _attn_kernel(q_ref, kp_ref, kc_ref, vp_ref, vc_ref, o_ref, lse_ref, *, group, dil):
    bq = Q_BLOCK
    d = N_HEADS * HEAD_DIM
    j = pl.program_id(2)
    q = q_ref[...].reshape(bq, d)
    kp = kp_ref[...].reshape(bq, d)
    kc = kc_ref[...].reshape(bq, d)
    vp = vp_ref[...].reshape(bq, d)
    vc = vc_ref[...].reshape(bq, d)

    qi = lax.broadcasted_iota(I32, (bq, 2 * bq), 0)
    kj = lax.broadcasted_iota(I32, (bq, 2 * bq), 1)
    dist = qi + bq - kj
    steps = ATTN_PATTERNS[group][0] // dil
    valid = (dist >= 0) & (dist <= steps) & ((kj >= bq) | (j > 0))
    neg_dist = jnp.where(valid, (dist * -dil).astype(F32), NEG_BIG)
    lane = lax.broadcasted_iota(I32, (bq, LSE_LANES), 1)
    lse_tile = jnp.zeros((bq, LSE_LANES), F32)
    n_tot = len(ATTN_PATTERNS) * N_HEADS
    pair = 2 * HEAD_DIM
    first_half = lax.broadcasted_iota(I32, (bq, pair), 1) < HEAD_DIM
    head_a_rows = lax.broadcasted_iota(I32, (2 * bq, 1), 0) < bq
    neg_dist2 = jnp.concatenate([neg_dist, neg_dist], axis=0)
    slope_of = lambda h: 2.0 ** (-8.0 * (group * N_HEADS + h + 1) / n_tot)
    outs = []
    for hp in range(N_HEADS // 2):
        cols = slice(hp * pair, (hp + 1) * pair)
        q2 = q[:, cols]
        k2 = jnp.concatenate([kp[:, cols], kc[:, cols]], axis=0)
        v2 = jnp.concatenate([vp[:, cols], vc[:, cols]], axis=0)
        zero = jnp.zeros_like(q2)
        q4 = jnp.concatenate([jnp.where(first_half, q2, zero), jnp.where(first_half, zero, q2)],
                             axis=0)
        slopes = jnp.where(head_a_rows, slope_of(2 * hp), slope_of(2 * hp + 1))
        s = lax.dot_general(q4, k2, (((1,), (1,)), ((), ())), preferred_element_type=F32)
        s = s + slopes * neg_dist2
        m = jnp.max(s, axis=-1, keepdims=True)
        p = jnp.exp(s - m)
        den = jnp.sum(p, axis=-1, keepdims=True)
        o4 = jnp.dot(p.astype(BF16), v2, preferred_element_type=F32) / den
        lse4 = m + jnp.log(den)
        outs.append(jnp.where(first_half, o4[:bq, :], o4[bq:, :]))
        lse_tile = jnp.where(lane == 2 * hp, lse4[:bq, :],
                             jnp.where(lane == 2 * hp + 1, lse4[bq:, :], lse_tile))
    o_ref[...] = jnp.concatenate(outs, axis=1).astype(BF16).reshape(o_ref.shape)
    lse_ref[...] = lse_tile.reshape(lse_ref.shape)


def _attention_group(qkv, group, batch, seq):
    dil = ATTN_PATTERNS[group][1]
    d = N_HEADS * HEAD_DIM
    t = qkv.shape[0]
    sub = seq // dil
    nb = sub // Q_BLOCK
    rows = Q_BLOCK if dil == 1 else PERM_TILE // dil
    chunks = Q_BLOCK // rows
    u = seq // (rows * dil)
    view = lambda a, c: a.reshape(batch, u, dil, rows, c)
    blk = lambda c: (None, chunks, None, rows, c)
    col0 = group * 3
    q_spec = pl.BlockSpec(blk(d), lambda b, r, j: (b, j, r, 0, col0))
    kc_spec = pl.BlockSpec(blk(d), lambda b, r, j: (b, j, r, 0, col0 + 1))
    kp_spec = pl.BlockSpec(blk(d), lambda b, r, j: (b, jnp.maximum(j - 1, 0), r, 0, col0 + 1))
    vc_spec = pl.BlockSpec(blk(d), lambda b, r, j: (b, j, r, 0, col0 + 2))
    vp_spec = pl.BlockSpec(blk(d), lambda b, r, j: (b, jnp.maximum(j - 1, 0), r, 0, col0 + 2))
    qkv5 = view(qkv, qkv.shape[1])
    o, lse = pl.pallas_call(
        functools.partial(_attn_kernel, group=group, dil=dil),
        grid=(batch, dil, nb),
        in_specs=[q_spec, kp_spec, kc_spec, vp_spec, vc_spec],
        out_specs=[
            pl.BlockSpec(blk(d), lambda b, r, j: (b, j, r, 0, 0)),
            pl.BlockSpec(blk(LSE_LANES), lambda b, r, j: (b, j, r, 0, 0)),
        ],
        out_shape=[
            jax.ShapeDtypeStruct((batch, u, dil, rows, d), BF16),
            jax.ShapeDtypeStruct((batch, u, dil, rows, LSE_LANES), F32),
        ],
        compiler_params=_cparams(("arbitrary", "arbitrary", "arbitrary")),
    )(qkv5, qkv5, qkv5, qkv5, qkv5)
    return o.reshape(t, d), lse.reshape(t, LSE_LANES)


def _split3(v):
    a = v.astype(BF16)
    r = v - a.astype(F32)
    b = r.astype(BF16)
    c = (r - b.astype(F32)).astype(BF16)
    return a, b, c


def _attn_out_kernel(x_ref, g_ref, o0_ref, o1_ref, o2_ref, l0_ref, l1_ref, l2_ref,
                     pt_ref, e_ref, wout_ref, lng_ref, lnb_ref, out_ref):
    tm, d = x_ref.shape
    o_refs = (o0_ref, o1_ref, o2_ref)
    l_refs = (l0_ref, l1_ref, l2_ref)
    n_sub = tm // PERM_TILE

    def unperm(gi, val_bf16):
        if gi == 0:
            return val_bf16.astype(F32)
        parts = [jnp.dot(pt_ref[gi - 1], val_bf16[s * PERM_TILE:(s + 1) * PERM_TILE, :],
                         preferred_element_type=F32) for s in range(n_sub)]
        return jnp.concatenate(parts, axis=0)

    lses = []
    for gi in range(3):
        l = l_refs[gi][...]
        if gi == 0:
            lses.append(l)
        else:
            a, b, c = _split3(l)
            lses.append(unperm(gi, a) + unperm(gi, b) + unperm(gi, c))
    mx = jnp.maximum(jnp.maximum(lses[0], lses[1]), lses[2])
    es = [jnp.exp(l - mx) for l in lses]
    tot = es[0] + es[1] + es[2]
    mixed = jnp.zeros((tm, d), F32)
    for gi in range(3):
        w = es[gi] / tot
        a, b, _ = _split3(w)
        wide = (jnp.dot(a, e_ref[...], preferred_element_type=F32)
                + jnp.dot(b, e_ref[...], preferred_element_type=F32))
        mixed = mixed + wide * unperm(gi, o_refs[gi][...])
    y = jnp.dot(mixed.astype(BF16), wout_ref[...], preferred_element_type=F32)
    v = ALPHA * x_ref[...] + (1.0 + g_ref[...]) * y
    out_ref[...] = _layer_norm(v, lng_ref[...], lnb_ref[...])


def _attn_out(x2, mod3, os_, lses, perms_t, expand, w_out, ln_g, ln_b, seq):
    t, d = x2.shape
    tm = 512
    tpb = seq // tm
    row = lambda c: pl.BlockSpec((tm, c), lambda i: (i, 0))
    full = lambda shape: pl.BlockSpec(shape, lambda i: (0,) * len(shape))
    return pl.pallas_call(
        _attn_out_kernel,
        grid=(t // tm,),
        in_specs=[row(d), _mod_spec(2, d, tpb), row(d), row(d), row(d),
                  row(LSE_LANES), row(LSE_LANES), row(LSE_LANES),
                  full(perms_t.shape), full(expand.shape), full((d, d)),
                  full((1, d)), full((1, d))],
        out_specs=row(d),
        out_shape=jax.ShapeDtypeStruct((t, d), F32),
        compiler_params=_cparams(("arbitrary",)),
    )(x2, mod3, *os_, *lses, perms_t, expand, w_out, ln_g.reshape(1, d), ln_b.reshape(1, d))


def _first_index_of_max(v, iota, size):
    m = jnp.max(v, axis=0, keepdims=True)
    idx = jnp.min(jnp.where(v == m, iota, float(size)), axis=0, keepdims=True)
    return m, idx


def _topk_kernel(x_ref, sh_ref, sc_ref, wr_ref, b_ref, tri_ref, ltri_ref,
                 w_ref, pos_ref, cnt_ref):
    tr = x_ref.shape[0]
    ne = b_ref.shape[0]
    gsz = ne // N_EXPERT_GROUPS
    tile = pl.program_id(0)

    hf = x_ref[...] * (1.0 + sc_ref[...]) + sh_ref[...]
    hf_hi = hf.astype(BF16)
    hf_lo = (hf - hf_hi.astype(F32)).astype(BF16)
    nt = (((1,), (1,)), ((), ()))
    by_hi = lax.dot_general(wr_ref[...], hf_hi, nt, preferred_element_type=F32)
    logits = (by_hi[:ne, :] + by_hi[ne:, :]
              + lax.dot_general(wr_ref[0:ne, :], hf_lo, nt, preferred_element_type=F32))
    scores = 1.0 / (1.0 + jnp.exp(-logits))
    sel = scores + b_ref[...]
    iota_g = lax.broadcasted_iota(I32, (gsz, tr), 0).astype(F32)
    iota_n = lax.broadcasted_iota(I32, (N_EXPERT_GROUPS, tr), 0).astype(F32)
    gs = jnp.zeros((N_EXPERT_GROUPS, tr), F32)
    for g in range(N_EXPERT_GROUPS):
        blk = sel[g * gsz:(g + 1) * gsz, :]
        m1, i1 = _first_index_of_max(blk, iota_g, gsz)
        m2 = jnp.max(jnp.where(iota_g == i1, -jnp.inf, blk), axis=0, keepdims=True)
        gs = jnp.where(iota_n == float(g), m1 + m2, gs)
    gmask = jnp.zeros((N_EXPERT_GROUPS, tr), F32)
    for _ in range(TOPK_GROUPS):
        _, gi = _first_index_of_max(gs, iota_n, N_EXPERT_GROUPS)
        hit = iota_n == gi
        gmask = jnp.where(hit, 1.0, gmask)
        gs = jnp.where(hit, -jnp.inf, gs)
    masked_rows = []
    for g in range(N_EXPERT_GROUPS):
        keep = jnp.broadcast_to(gmask[g:g + 1, :], (gsz, tr)) > 0.5
        masked_rows.append(jnp.where(keep, sel[g * gsz:(g + 1) * gsz, :], -jnp.inf))
    cur = jnp.concatenate(masked_rows, axis=0)
    iota_e = lax.broadcasted_iota(I32, (ne, tr), 0).astype(F32)
    chosen = jnp.zeros((ne, tr), F32)
    for _ in range(TOP_K):
        _, ei = _first_index_of_max(cur, iota_e, ne)
        hit = iota_e == ei
        cur = jnp.where(hit, -jnp.inf, cur)
        chosen = jnp.where(hit, 1.0, chosen)
    picked = chosen > 0.5
    top_scores = jnp.where(picked, scores, 0.0)
    wsum = jnp.sum(top_scores, axis=0, keepdims=True)
    w_ref[...] = top_scores / wsum * ROUTED_SCALE
    before = jnp.dot(chosen.astype(BF16), tri_ref[...], preferred_element_type=F32)
    n = jnp.sum(chosen, axis=1, keepdims=True)
    n_chunks = jnp.floor((n + (CHUNK - 1)) * (1.0 / CHUNK))
    run_off = jnp.dot(ltri_ref[...], jnp.broadcast_to(n_chunks, (ne, 128)).astype(BF16),
                      preferred_element_type=F32)[:, 0:1] * CHUNK
    pos_ref[...] = jnp.where(picked, before + run_off, NO_SLOT)

    @pl.when(tile == 0)
    def _():
        cnt_ref[...] = jnp.zeros_like(cnt_ref)

    lane = lax.broadcasted_iota(I32, cnt_ref.shape, 1)
    cnt_ref[...] = jnp.where(lane == tile, jnp.broadcast_to(n, cnt_ref.shape).astype(I32),
                             cnt_ref[...])


def _topk_route(x2, mod3, w_router, router_b, seq):
    t, d = x2.shape
    ne = w_router.shape[1]
    tr = TOK_TILE
    tpb = seq // tr
    assert t // tr <= 128
    w_t = w_router.T
    w_hi = w_t.astype(BF16)
    w_split = jnp.concatenate([w_hi, (w_t - w_hi.astype(F32)).astype(BF16)], axis=0)
    tri = (jnp.arange(tr)[:, None] < jnp.arange(tr)[None, :]).astype(BF16)
    ltri = (jnp.arange(ne)[None, :] < jnp.arange(ne)[:, None]).astype(BF16)
    out = lambda dt: jax.ShapeDtypeStruct((ne, t), dt)
    row = pl.BlockSpec((ne, tr), lambda i: (0, i))
    return pl.pallas_call(
        _topk_kernel,
        grid=(t // tr,),
        in_specs=[pl.BlockSpec((tr, d), lambda i: (i, 0)),
                  _mod_spec(3, d, tpb), _mod_spec(4, d, tpb),
                  pl.BlockSpec((2 * ne, d), lambda i: (0, 0)),
                  pl.BlockSpec((ne, 1), lambda i: (0, 0)),
                  pl.BlockSpec((tr, tr), lambda i: (0, 0)),
                  pl.BlockSpec((ne, ne), lambda i: (0, 0))],
        out_specs=[row, row, pl.BlockSpec((ne, 128), lambda i: (0, 0))],
        out_shape=[out(F32), out(F32), jax.ShapeDtypeStruct((ne, 128), I32)],
        compiler_params=_cparams(("arbitrary",)),
    )(x2, mod3, mod3, w_split, router_b.reshape(ne, 1), tri, ltri)


def _slot_plan(counts, n_tiles, n_blocks):
    n = counts[:, :n_tiles]
    nch = (n + (CHUNK - 1)) // CHUNK
    rows = jnp.sum(nch, axis=1) * CHUNK
    region = ((rows + EXPERT_ROWS - 1) // EXPERT_ROWS) * EXPERT_ROWS
    region_end = jnp.cumsum(region)
    region_start = region_end - region
    run_chunk = region_start[:, None] // CHUNK + jnp.cumsum(nch, axis=1) - nch
    local_end = jnp.cumsum(nch, axis=0)
    local_chunk = local_end - nch
    experts = jnp.arange(N_EXPERTS, dtype=I32)
    copies = []
    done = jnp.zeros_like(nch)
    for size in COPY_SIZES:
        k = (nch - done) // size
        end = jnp.cumsum(k, axis=0)
        cap = _copy_cap(size)
        m = jnp.arange(cap, dtype=I32)
        run_of_m = jnp.sum((end[:, :, None] <= m[None, None, :]).astype(I32), axis=0)
        owner = run_of_m[None] == experts[:, None, None]
        pick = lambda a: jnp.sum(jnp.where(owner, a[:, :, None], 0), axis=0)
        within = (m[None, :] - pick(end - k)) * size
        copies += [(pick(local_chunk + done) + within).reshape(-1).astype(I32),
                   (pick(run_chunk + done) + within).reshape(-1).astype(I32),
                   end[-1].astype(I32)]
        done = done + k * size
    block_row = jnp.arange(n_blocks, dtype=I32) * EXPERT_ROWS
    group_e = jnp.minimum(jnp.sum((region_end[None, :] <= block_row[:, None]).astype(I32), axis=1),
                          N_EXPERTS - 1)
    n_used = region_end[-1] // EXPERT_ROWS
    nonempty = region > 0
    run_of_expert = jnp.cumsum(nonempty.astype(I32)) - 1
    run_expert = jnp.sum(jnp.where((run_of_expert[None, :] == experts[:, None]) & nonempty[None, :],
                                   experts[None, :], 0), axis=1)
    blocks = jnp.arange(n_blocks, dtype=I32)
    first = ((blocks == 0) | (group_e != jnp.roll(group_e, 1))) & (blocks < n_used)
    per_tile = lambda a: (a * CHUNK).T.astype(I32)
    return dict(
        run_begin_rows=per_tile(local_chunk)[:, None, :], run_end_rows=per_tile(local_end)[:, None, :],
        run_begin_cols=per_tile(local_chunk)[:, :, None], run_end_cols=per_tile(local_end)[:, :, None],
        block_run=(jnp.cumsum(first.astype(I32)) - 1).astype(I32), block_first=first.astype(I32),
        run_expert=run_expert.astype(I32), n_runs=jnp.sum(nonempty.astype(I32)).reshape(1),
        copies=copies,
        tile_chunks=jnp.sum(nch, axis=0).astype(I32),
        tail_chunk=((region_start + rows) // CHUNK).astype(I32),
        tail_chunks=((region - rows) // CHUNK).astype(I32),
        group_e=group_e, n_used=(region_end[-1:] // EXPERT_ROWS).astype(I32))


def _chunk_rows(ref, chunk_index, n_chunks=1):
    start = chunk_index * CHUNK
    if not isinstance(start, int):
        start = pl.multiple_of(start, CHUNK)
    return ref.at[pl.ds(start, n_chunks * CHUNK), :]


def _copy_cap(size):
    return LOCAL_ROWS // CHUNK // size if size == COPY_SIZES[0] else N_EXPERTS


def _start_copies(copy_refs, tile, make_copy):
    for n, size in enumerate(COPY_SIZES):
        src_ref, dst_ref, cnt_ref = copy_refs[3 * n:3 * n + 3]
        base = tile * _copy_cap(size)

        def start(m, priority, src_ref=src_ref, dst_ref=dst_ref, base=base, size=size):
            make_copy(src_ref[base + m], dst_ref[base + m], size).start(priority=priority)

        _for_chunks_two_queues(cnt_ref[tile], start)


def _pos_digits(pos):
    hi = jnp.floor(pos * (1.0 / POS_RADIX))
    return hi.astype(BF16), (pos - POS_RADIX * hi).astype(BF16)


def _for_chunks_two_queues(count, start):
    def pair(p, carry):
        start(2 * p, 0)
        start(2 * p + 1, 1)
        return carry

    lax.fori_loop(0, lax.shift_right_logical(count, 1), pair, 0)

    @pl.when((count & 1) == 1)
    def _():
        start(count - 1, 0)


def _wait_chunks(count, src_ref, dst_ref, sem):
    group_rows = pl.ds(0, WAIT_GROUP * CHUNK)

    def wait_group(c, carry):
        pltpu.make_async_copy(src_ref.at[group_rows, :], dst_ref.at[group_rows, :], sem).wait()
        return carry

    def wait_one(c, carry):
        pltpu.make_async_copy(_chunk_rows(src_ref, 0), _chunk_rows(dst_ref, 0), sem).wait()
        return carry

    lax.fori_loop(0, lax.shift_right_logical(count, WAIT_GROUP.bit_length() - 1), wait_group, 0)
    lax.fori_loop(0, count & (WAIT_GROUP - 1), wait_one, 0)


def _dispatch_kernel(*refs):
    copy_refs, refs = refs[:3 * len(COPY_SIZES)], refs[3 * len(COPY_SIZES):]
    _dispatch_body(copy_refs, *refs)


def _dispatch_body(copy_refs, tile_ref, tail_ref, tailn_ref, used_ref,
                     x_ref, sh_ref, sc_ref, pos_ref, begin_ref, end_ref,
                     xg_hbm, stage_ref, zero_ref, sems, sem_blk):
    tr = x_ref.shape[0]
    i = pl.program_id(0)
    n_blocks = xg_hbm.shape[0] // EXPERT_ROWS

    def drain(count, sem):
        _wait_chunks(count, stage_ref.at[0], xg_hbm, sem)

    @pl.when(i == 0)
    def _():
        zero_ref[...] = jnp.zeros_like(zero_ref)

        def block_copy(j):
            rows = pl.ds(pl.multiple_of(j * EXPERT_ROWS, EXPERT_ROWS), EXPERT_ROWS)
            return pltpu.make_async_copy(zero_ref, xg_hbm.at[rows, :], sem_blk)

        def start_block(j, carry):
            block_copy(j).start()
            return carry

        def wait_block(j, carry):
            block_copy(j).wait()
            return carry

        lax.fori_loop(used_ref[0], n_blocks, start_block, 0)

        def per_expert(e, total):
            def per_chunk(c, carry):
                pltpu.make_async_copy(_chunk_rows(zero_ref, 0),
                                      _chunk_rows(xg_hbm, tail_ref[e] + c), sems.at[0]).start()
                return carry
            lax.fori_loop(0, tailn_ref[e], per_chunk, 0)
            return total + tailn_ref[e]

        drain(lax.fori_loop(0, N_EXPERTS, per_expert, 0), sems.at[0])
        lax.fori_loop(used_ref[0], n_blocks, wait_block, 0)

    hf = (x_ref[...] * (1.0 + sc_ref[...]) + sh_ref[...]).astype(BF16)
    row_e = lax.broadcasted_iota(I32, (LOCAL_ROWS, N_EXPERTS), 0)
    owner = jnp.where((row_e >= begin_ref[...]) & (row_e < end_ref[...]), 1.0, 0.0).astype(BF16)
    hi_digit, lo_digit = _pos_digits(pos_ref[...])
    slot_pos = (POS_RADIX * jnp.dot(owner, hi_digit, preferred_element_type=F32)
                + jnp.dot(owner, lo_digit, preferred_element_type=F32))
    row_t = lax.broadcasted_iota(I32, (LOCAL_ROWS, tr), 0).astype(F32)
    perm = jnp.where(slot_pos == row_t, 1.0, 0.0).astype(BF16)
    rows = jnp.dot(perm, hf, preferred_element_type=F32)
    half = rows.shape[1] // 2
    buf = stage_ref.at[i % 2]
    buf[...] = (lax.bitcast_convert_type(rows[:, :half], U32)
                | (lax.bitcast_convert_type(rows[:, half:], U32) >> 16))

    _start_copies(copy_refs, i, lambda src, dst, n: pltpu.make_async_copy(
        _chunk_rows(buf, src, n), _chunk_rows(xg_hbm, dst, n), sems.at[i % 2]))

    @pl.when(i > 0)
    def _():
        drain(tile_ref[jnp.maximum(i - 1, 0)], sems.at[(i + 1) % 2])

    @pl.when(i == pl.num_programs(0) - 1)
    def _():
        drain(tile_ref[i], sems.at[i % 2])


def _dispatch(plan, x2, mod3, pos, n_slots, seq):
    t, d = x2.shape
    tr = TOK_TILE
    tpb = seq // tr
    return pl.pallas_call(
        _dispatch_kernel,
        grid_spec=pltpu.PrefetchScalarGridSpec(
            num_scalar_prefetch=4 + len(plan["copies"]),
            grid=(t // tr,),
            in_specs=[pl.BlockSpec((tr, d), lambda i, *_: (i, 0)),
                      _mod_spec(3, d, tpb), _mod_spec(4, d, tpb),
                      pl.BlockSpec((N_EXPERTS, tr), lambda i, *_: (0, i)),
                      pl.BlockSpec((None, 1, N_EXPERTS), lambda i, *_: (i, 0, 0)),
                      pl.BlockSpec((None, 1, N_EXPERTS), lambda i, *_: (i, 0, 0))],
            out_specs=pl.BlockSpec(memory_space=pl.ANY),
            scratch_shapes=[pltpu.VMEM((2, LOCAL_ROWS, d // 2), U32),
                            pltpu.VMEM((EXPERT_ROWS, d // 2), U32),
                            pltpu.SemaphoreType.DMA((2,)), pltpu.SemaphoreType.DMA],
        ),
        out_shape=jax.ShapeDtypeStruct((n_slots, d // 2), U32),
        compiler_params=_cparams(("arbitrary",)),
    )(*plan["copies"], plan["tile_chunks"], plan["tail_chunk"], plan["tail_chunks"],
      plan["n_used"], x2, mod3, mod3, pos, plan["run_begin_rows"], plan["run_end_rows"])


def _expert_kernel(run_ref, first_ref, rexp_ref, nrun_ref, nu_ref,
                   xg_hbm, wgu_hbm, wdn_hbm, yg_hbm,
                   xbuf, ybuf, wgu_f32, wdn_f32, wgu_bf, wdn_bf, x_sem, y_sem, w_sem, z_sem,
                   *, layer):
    n_used = nu_ref[0]
    n_blocks = yg_hbm.shape[0] // EXPERT_ROWS
    depth, _, half = xbuf.shape
    f = wdn_bf.shape[0]

    def rows(j):
        return pl.ds(pl.multiple_of(j * EXPERT_ROWS, EXPERT_ROWS), EXPERT_ROWS)

    def x_copy(j):
        return pltpu.make_async_copy(xg_hbm.at[rows(j), :], xbuf.at[j % depth],
                                     x_sem.at[j % depth])

    def y_copy(j):
        return pltpu.make_async_copy(ybuf.at[j % depth], yg_hbm.at[rows(j), :],
                                     y_sem.at[j % depth])

    def w_copies(r):
        e = rexp_ref[r]
        return (pltpu.make_async_copy(wgu_hbm.at[layer, e], wgu_f32.at[r % 2], w_sem.at[r % 2]),
                pltpu.make_async_copy(wdn_hbm.at[layer, e], wdn_f32.at[r % 2], w_sem.at[r % 2]))

    def zero_copy(j):
        return pltpu.make_async_copy(ybuf.at[0], yg_hbm.at[rows(j), :], z_sem)

    ybuf[0] = jnp.zeros(ybuf.shape[1:], U32)

    def zero_start(j, carry):
        zero_copy(j).start()
        return carry

    def zero_wait(j, carry):
        zero_copy(j).wait()
        return carry

    lax.fori_loop(n_used, n_blocks, zero_start, 0)
    lax.fori_loop(n_used, n_blocks, zero_wait, 0)

    for cp in w_copies(0):
        cp.start()
    for a in range(depth - 1):
        @pl.when(a < n_used)
        def _():
            x_copy(a).start()

    def block(j, carry):
        r = run_ref[j]

        @pl.when(first_ref[j] == 1)
        def _():
            for cp in w_copies(r):
                cp.wait()

            @pl.when(r + 1 < nrun_ref[0])
            def _():
                for cp in w_copies(r + 1):
                    cp.start()

            wgu_bf[...] = wgu_f32[r % 2].astype(BF16)
            wdn_bf[...] = wdn_f32[r % 2].astype(BF16)

        x_copy(j).wait()

        @pl.when(j + (depth - 1) < n_used)
        def _():
            x_copy(j + (depth - 1)).start()

        @pl.when(j >= depth)
        def _():
            y_copy(j - depth).wait()

        for g in range(EXPERT_ROWS // GEMM_SUB_ROWS):
            rws = slice(g * GEMM_SUB_ROWS, (g + 1) * GEMM_SUB_ROWS)
            hi, lo = _unpack_bf16_pairs(xbuf[j % depth, rws, :])
            gu = (jnp.dot(hi.astype(BF16), wgu_bf[0:half, :], preferred_element_type=F32)
                  + jnp.dot(lo.astype(BF16), wgu_bf[half:, :], preferred_element_type=F32))
            act = (_silu(gu[:, :f]) * gu[:, f:]).astype(BF16)
            ybuf[j % depth, rws, :] = _pack_bf16_pairs(
                jnp.dot(act, wdn_bf[...], preferred_element_type=F32))
        y_copy(j).start()
        return carry

    lax.fori_loop(0, n_used, block, 0)

    for a in range(depth, 0, -1):
        @pl.when(n_used >= a)
        def _():
            y_copy(n_used - a).wait()


def _expert_gemm(plan, xg, w_gu, w_down, layer):
    n_slots, half = xg.shape
    _, ne, d, f2 = w_gu.shape
    f = w_down.shape[2]
    hbm = pl.BlockSpec(memory_space=pl.ANY)
    return pl.pallas_call(
        functools.partial(_expert_kernel, layer=layer),
        grid_spec=pltpu.PrefetchScalarGridSpec(
            num_scalar_prefetch=5,
            grid=(1,),
            in_specs=[hbm, hbm, hbm],
            out_specs=hbm,
            scratch_shapes=[pltpu.VMEM((GEMM_DEPTH, EXPERT_ROWS, half), U32),
                            pltpu.VMEM((GEMM_DEPTH, EXPERT_ROWS, half), U32),
                            pltpu.VMEM((2, d, f2), F32), pltpu.VMEM((2, f, d), F32),
                            pltpu.VMEM((d, f2), BF16), pltpu.VMEM((f, d), BF16),
                            pltpu.SemaphoreType.DMA((GEMM_DEPTH,)),
                            pltpu.SemaphoreType.DMA((GEMM_DEPTH,)),
                            pltpu.SemaphoreType.DMA((2,)), pltpu.SemaphoreType.DMA],
        ),
        out_shape=jax.ShapeDtypeStruct((n_slots, half), U32),
        compiler_params=_cparams(("arbitrary",)),
    )(plan["block_run"], plan["block_first"], plan["run_expert"], plan["n_runs"], plan["n_used"],
      xg, w_gu, w_down)


def _combine_kernel(*refs):
    copy_refs, refs = refs[:3 * len(COPY_SIZES)], refs[3 * len(COPY_SIZES):]
    _combine_body(copy_refs, *refs)


def _combine_body(copy_refs, tile_ref,
                    x_ref, sh_ref, sc_ref, g_ref, pos_ref, wt_ref, begin_ref, end_ref, brow_ref,
                    wsgu_ref, wsdn_ref, lng_ref, lnb_ref, yg_hbm, o_ref, stage_ref, sems):
    tr, d = x_ref.shape
    i = pl.program_id(0)
    n_tiles = pl.num_programs(0)

    def fetch(tile, buffer):
        _start_copies(copy_refs, tile, lambda src, dst, n: pltpu.make_async_copy(
            _chunk_rows(yg_hbm, dst, n), _chunk_rows(stage_ref.at[buffer], src, n),
            sems.at[buffer]))

    @pl.when(i == 0)
    def _():
        stage_ref[...] = jnp.zeros_like(stage_ref)
        fetch(0, 0)

    nxt = jnp.minimum(i + 1, n_tiles - 1)
    for buffer in range(2):
        @pl.when((i + 1 < n_tiles) & (nxt % 2 == buffer))
        def _():
            fetch(nxt, buffer)

    x = x_ref[...]
    hf = (x * (1.0 + sc_ref[...]) + sh_ref[...]).astype(BF16)
    f = wsdn_ref.shape[0]
    su = jnp.dot(hf, wsgu_ref[...], preferred_element_type=F32)
    act = (_silu(su[:, :f]) * su[:, f:]).astype(BF16)
    shared = jnp.dot(act, wsdn_ref[...], preferred_element_type=F32)

    col_e = lax.broadcasted_iota(I32, (N_EXPERTS, LOCAL_ROWS), 1)
    owner = jnp.where((col_e >= begin_ref[...]) & (col_e < end_ref[...]), 1.0, 0.0).astype(BF16)
    begin_row = brow_ref[...].astype(F32)
    pos = pos_ref[...]
    rank1 = jnp.where(pos >= 0.0, pos - begin_row + 1.0, 0.0).astype(BF16)
    slot_rank = jnp.dot(rank1, owner, preferred_element_type=F32)
    b_hi, b_lo = _pos_digits(jnp.broadcast_to(begin_row, (8, N_EXPERTS)))
    run_begin = (POS_RADIX * jnp.dot(b_hi, owner, preferred_element_type=F32)
                 + jnp.dot(b_lo, owner, preferred_element_type=F32))[0:1, :]
    offset1 = lax.broadcasted_iota(I32, (1, LOCAL_ROWS), 1).astype(F32) - run_begin + 1.0
    holds = slot_rank == offset1
    weights = jnp.where(holds, jnp.dot(wt_ref[...].astype(BF16), owner,
                                       preferred_element_type=F32), 0.0).astype(BF16)

    buf = stage_ref.at[i % 2]
    _wait_chunks(tile_ref[i], yg_hbm, buf, sems.at[i % 2])

    y_hi, y_lo = _unpack_bf16_pairs(buf[...])
    y = jnp.concatenate([y_hi.astype(BF16), y_lo.astype(BF16)], axis=1)
    routed = jnp.dot(weights, y, preferred_element_type=F32)
    v = ALPHA * x + (1.0 + g_ref[...]) * (routed + shared)
    o_ref[...] = _layer_norm(v, lng_ref[...], lnb_ref[...])


def _combine(plan, x2, mod3, pos_tok, w_tok, w_sh_gu, w_sh_down, ln_g, ln_b, yg, seq):
    t, d = x2.shape
    tr = TOK_TILE
    tpb = seq // tr
    full = lambda shape: pl.BlockSpec(shape, lambda i, *_: (0,) * len(shape))
    return pl.pallas_call(
        _combine_kernel,
        grid_spec=pltpu.PrefetchScalarGridSpec(
            num_scalar_prefetch=1 + len(plan["copies"]),
            grid=(t // tr,),
            in_specs=[pl.BlockSpec((tr, d), lambda i, *_: (i, 0)),
                      _mod_spec(3, d, tpb), _mod_spec(4, d, tpb), _mod_spec(5, d, tpb),
                      pl.BlockSpec((tr, N_EXPERTS), lambda i, *_: (i, 0)),
                      pl.BlockSpec((tr, N_EXPERTS), lambda i, *_: (i, 0)),
                      pl.BlockSpec((None, N_EXPERTS, 1), lambda i, *_: (i, 0, 0)),
                      pl.BlockSpec((None, N_EXPERTS, 1), lambda i, *_: (i, 0, 0)),
                      pl.BlockSpec((None, 1, N_EXPERTS), lambda i, *_: (i, 0, 0)),
                      full(w_sh_gu.shape), full(w_sh_down.shape), full((1, d)), full((1, d)),
                      pl.BlockSpec(memory_space=pl.ANY)],
            out_specs=pl.BlockSpec((tr, d), lambda i, *_: (i, 0)),
            scratch_shapes=[pltpu.VMEM((2, LOCAL_ROWS, d // 2), U32),
                            pltpu.SemaphoreType.DMA((2,))],
        ),
        out_shape=jax.ShapeDtypeStruct((t, d), F32),
        compiler_params=_cparams(("arbitrary",)),
    )(*plan["copies"], plan["tile_chunks"],
      x2, mod3, mod3, mod3, pos_tok, w_tok, plan["run_begin_cols"], plan["run_end_cols"],
      plan["run_begin_rows"],
      w_sh_gu, w_sh_down, ln_g.reshape(1, d), ln_b.reshape(1, d), yg)


def _moe_layer(x2, mod3, w_router, router_b, w_gu, w_down, layer, w_sh_gu, w_sh_down,
               ln_g, ln_b, seq):
    t, d = x2.shape
    n_tiles = t // TOK_TILE
    w_top, pos, counts = _topk_route(x2, mod3, w_router, router_b, seq)
    bound = t * TOP_K + n_tiles * N_EXPERTS * (CHUNK - 1) + N_EXPERTS * (EXPERT_ROWS - 1)
    n_blocks = -(-bound // EXPERT_ROWS)
    plan = _slot_plan(counts, n_tiles, n_blocks)
    xg = _dispatch(plan, x2, mod3, pos, n_blocks * EXPERT_ROWS, seq)
    yg = _expert_gemm(plan, xg, w_gu, w_down, layer)
    return _combine(plan, x2, mod3, pos.T, w_top.T, w_sh_gu, w_sh_down, ln_g, ln_b, yg, seq)


def kernel(x, c, ada_w, ada_b, pool_w_in, pool_w_grp, pool_scale, pool_w_out, attn_w_in, attn_w_out, ln1_g, ln1_b, router_w, router_b, exp_w_gu, exp_w_down, sh_w_gu, sh_w_down, ln2_g, ln2_b):
    batch, seq, d = x.shape
    depth = ada_w.shape[0]
    t = batch * seq
    mod = _modulation(c, ada_w, ada_b).reshape(depth, batch, 1, 6 * d)
    perms = jnp.stack([_perm_matrix(dil) for _, dil in ATTN_PATTERNS[1:]])
    perms_t = jnp.swapaxes(perms, 1, 2)
    expand = (jnp.arange(LSE_LANES)[:, None] == (jnp.arange(d)[None, :] // HEAD_DIM)).astype(BF16)
    part = (jnp.arange(attn_w_in.shape[2]) // d) % 3
    q_scale = jnp.where(part == 0, HEAD_DIM ** -0.5, 1.0).astype(F32)
    x2 = x.reshape(t, d)
    for i in range(depth):
        mod3 = mod[i]
        j = i // 2
        if i % 2 == 0:
            x2 = _pool_layer(x2, mod3, pool_w_in[j].astype(BF16), pool_w_grp[j].astype(BF16),
                             pool_scale[j], pool_w_out[j].astype(BF16), ln1_g[i], ln1_b[i], seq)
        else:
            qkv = _qkv_proj(x2, mod3, perms, (attn_w_in[j] * q_scale).astype(BF16), seq)
            res = [_attention_group(qkv, g, batch, seq) for g in range(len(ATTN_PATTERNS))]
            x2 = _attn_out(x2, mod3, [r[0] for r in res], [r[1] for r in res], perms_t, expand,
                           attn_w_out[j].astype(BF16), ln1_g[i], ln1_b[i], seq)
        x2 = _moe_layer(x2, mod3, router_w[i], router_b[i], exp_w_gu, exp_w_down, i,
                        sh_w_gu[i].astype(BF16), sh_w_down[i].astype(BF16),
                        ln2_g[i], ln2_b[i], seq)
    return x2.reshape(batch, seq, d)
```

```python
import functools
import math

import jax
import jax.numpy as jnp
from jax import lax
from jax.experimental import pallas as pl
from jax.experimental.pallas import tpu as pltpu

F32 = jnp.float32
BF16 = jnp.bfloat16
U32 = jnp.uint32
I32 = jnp.int32

POOL_WINDOWS = (2, 4, 8, 16)
ATTN_PATTERNS = ((128, 1), (512, 4), (2048, 16))
HEAD_DIM = 64
N_HEADS = 16
Q_BLOCK = 128
N_EXPERTS = 64
TOP_K = 8
N_EXPERT_GROUPS = 8
TOPK_GROUPS = 4
ROUTED_SCALE = 2.5
EXPERT_ROWS = 512
DEPTH = 4
ALPHA = (2 * DEPTH) ** 0.25
LN_EPS = 1e-5

PERM_TILE = 256
POOL_HALO = 16
TOK_TILE = 256
CHUNK = 8
LOCAL_ROWS = -(-(TOK_TILE * TOP_K + N_EXPERTS * (CHUNK - 1)) // 256) * 256
NO_SLOT = -64.0
POS_RADIX = 64.0
COPY_SIZES = (4, 2, 1)
WAIT_GROUP = 32
GEMM_SUB_ROWS = 256
GEMM_DEPTH = 4
LSE_LANES = 128
VMEM_LIMIT = 56 * 1024 * 1024
NEG_BIG = -1e30


def _cparams(sem):
    return pltpu.CompilerParams(dimension_semantics=sem, vmem_limit_bytes=VMEM_LIMIT)


def _layer_norm(v, g, b):
    mu = jnp.mean(v, axis=-1, keepdims=True)
    c = v - mu
    var = jnp.mean(c * c, axis=-1, keepdims=True)
    return c * lax.rsqrt(var + LN_EPS) * g + b


def _silu(v):
    return v * (1.0 / (1.0 + jnp.exp(-v)))


def _pack_bf16_pairs(v):
    n = v.shape[1] // 2
    hi = lax.bitcast_convert_type(v[:, :n].astype(BF16).astype(F32), U32)
    lo = lax.bitcast_convert_type(v[:, n:].astype(BF16).astype(F32), U32)
    return hi | (lo >> 16)


def _unpack_bf16_pairs(p):
    hi = lax.bitcast_convert_type(p & jnp.uint32(0xFFFF0000), F32)
    lo = lax.bitcast_convert_type(p << 16, F32)
    return hi, lo


def _mod_kernel(c_ref, w_ref, b_ref, o_ref):
    cs = _silu(c_ref[...])
    o_ref[...] = jnp.dot(cs, w_ref[...], preferred_element_type=F32) + b_ref[...]


def _modulation(c, ada_w, ada_b):
    depth, d, n6 = ada_w.shape
    b = c.shape[0]
    tn = 1536
    return pl.pallas_call(
        _mod_kernel,
        grid=(depth, n6 // tn),
        in_specs=[
            pl.BlockSpec((b, d), lambda i, n: (0, 0)),
            pl.BlockSpec((None, d, tn), lambda i, n: (i, 0, n)),
            pl.BlockSpec((None, 1, tn), lambda i, n: (i, 0, n)),
        ],
        out_specs=pl.BlockSpec((None, b, tn), lambda i, n: (i, 0, n)),
        out_shape=jax.ShapeDtypeStruct((depth, b, n6), F32),
        compiler_params=_cparams(("arbitrary", "arbitrary")),
    )(c, ada_w, ada_b.reshape(depth, 1, n6))


def _mod_spec(chunk, d, tiles_per_batch):
    return pl.BlockSpec((None, 1, d), lambda *idx: (idx[0] // tiles_per_batch, 0, chunk))


def _pool_kernel(x_ref, sh_ref, sc_ref, g_ref, win_ref, wgrp_ref, cs_ref, wout_ref,
                 lng_ref, lnb_ref, o_ref, ext_ref, *, tiles_per_batch):
    tm, d = x_ref.shape
    s_idx = pl.program_id(0) % tiles_per_batch
    x = x_ref[...]
    h = (x * (1.0 + sc_ref[...]) + sh_ref[...]).astype(BF16)
    u = jnp.dot(h, win_ref[...], preferred_element_type=F32)

    @pl.when(s_idx == 0)
    def _():
        ext_ref[0:POOL_HALO, :] = jnp.zeros((POOL_HALO, d), F32)

    @pl.when(s_idx != 0)
    def _():
        ext_ref[0:POOL_HALO, :] = ext_ref[tm:tm + POOL_HALO, :]

    ext_ref[POOL_HALO:POOL_HALO + tm, :] = u

    pos = s_idx * tm + lax.broadcasted_iota(I32, (tm, 1), 0) + 1
    gc = d // len(POOL_WINDOWS)
    ys = []
    for g, w in enumerate(POOL_WINDOWS):
        cols = slice(g * gc, (g + 1) * gc)
        acc = u[:, cols]
        for j in range(1, w):
            acc = acc + ext_ref[POOL_HALO - j:POOL_HALO - j + tm, cols]
        cnt = jnp.minimum(pos, w).astype(F32)
        z = (acc / cnt - u[:, cols]).astype(BF16)
        ys.append(jnp.dot(z, wgrp_ref[g], preferred_element_type=F32))
    y = (jnp.concatenate(ys, axis=1) * cs_ref[...]).astype(BF16)
    out = jnp.dot(y, wout_ref[...], preferred_element_type=F32)
    v = ALPHA * x + (1.0 + g_ref[...]) * out
    o_ref[...] = _layer_norm(v, lng_ref[...], lnb_ref[...])


def _pool_layer(x2, mod3, w_in, w_grp, ch_scale, w_out, ln_g, ln_b, seq):
    t, d = x2.shape
    tm = 512
    tpb = seq // tm
    full = lambda shape: pl.BlockSpec(shape, lambda i: (0,) * len(shape))
    return pl.pallas_call(
        functools.partial(_pool_kernel, tiles_per_batch=tpb),
        grid=(t // tm,),
        in_specs=[
            pl.BlockSpec((tm, d), lambda i: (i, 0)),
            _mod_spec(0, d, tpb), _mod_spec(1, d, tpb), _mod_spec(2, d, tpb),
            full((d, d)), full(w_grp.shape), full((1, d)), full((d, d)),
            full((1, d)), full((1, d)),
        ],
        out_specs=pl.BlockSpec((tm, d), lambda i: (i, 0)),
        out_shape=jax.ShapeDtypeStruct((t, d), F32),
        scratch_shapes=[pltpu.VMEM((tm + POOL_HALO, d), F32)],
        compiler_params=_cparams(("arbitrary",)),
    )(x2, mod3, mod3, mod3, w_in, w_grp, ch_scale.reshape(1, d), w_out,
      ln_g.reshape(1, d), ln_b.reshape(1, d))


def _perm_matrix(dil):
    p = jnp.arange(PERM_TILE)
    chunk = PERM_TILE // dil
    src = (p % chunk) * dil + p // chunk
    return (src[:, None] == jnp.arange(PERM_TILE)[None, :]).astype(BF16)


def _qkv_kernel(x_ref, sh_ref, sc_ref, p_ref, w_ref, o_ref, h_ref):
    tm = x_ref.shape[0]
    g = pl.program_id(1)

    @pl.when(g == 0)
    def _():
        h = (x_ref[...] * (1.0 + sc_ref[...]) + sh_ref[...]).astype(BF16)
        h_ref[0] = h
        for gi in range(1, len(ATTN_PATTERNS)):
            for s in range(tm // PERM_TILE):
                rows = slice(s * PERM_TILE, (s + 1) * PERM_TILE)
                h_ref[gi, rows, :] = jnp.dot(
                    p_ref[gi - 1], h[rows, :], preferred_element_type=F32).astype(BF16)

    o_ref[...] = jnp.dot(h_ref[g], w_ref[...], preferred_element_type=F32).astype(BF16)


def _qkv_proj(x2, mod3, perms, w_in, seq):
    t, d = x2.shape
    ng = len(ATTN_PATTERNS)
    tm = 1024
    tpb = seq // tm
    return pl.pallas_call(
        _qkv_kernel,
        grid=(t // tm, ng),
        in_specs=[
            pl.BlockSpec((tm, d), lambda m, g: (m, 0)),
            _mod_spec(0, d, tpb), _mod_spec(1, d, tpb),
            pl.BlockSpec(perms.shape, lambda m, g: (0, 0, 0)),
            pl.BlockSpec((d, 3 * d), lambda m, g: (0, g)),
        ],
        out_specs=pl.BlockSpec((tm, 3 * d), lambda m, g: (m, g)),
        out_shape=jax.ShapeDtypeStruct((t, ng * 3 * d), BF16),
        scratch_shapes=[pltpu.VMEM((ng, tm, d), BF16)],
        compiler_params=_cparams(("arbitrary", "arbitrary")),
    )(x2, mod3, mod3, perms, w_in)


def _attn_kernel(q_ref, kp_ref, kc_ref, vp_ref, vc_ref, o_ref, lse_ref, *, group, dil):
    bq = Q_BLOCK
    d = N_HEADS * HEAD_DIM
    j = pl.program_id(2)
    q = q_ref[...].reshape(bq, d)
    kp = kp_ref[...].reshape(bq, d)
    kc = kc_ref[...].reshape(bq, d)
    vp = vp_ref[...].reshape(bq, d)
    vc = vc_ref[...].reshape(bq, d)

    qi = lax.broadcasted_iota(I32, (bq, 2 * bq), 0)
    kj = lax.broadcasted_iota(I32, (bq, 2 * bq), 1)
    dist = qi + bq - kj
    steps = ATTN_PATTERNS[group][0] // dil
    valid = (dist >= 0) & (dist <= steps) & ((kj >= bq) | (j > 0))
    neg_dist = jnp.where(valid, (dist * -dil).astype(F32), NEG_BIG)
    lane = lax.broadcasted_iota(I32, (bq, LSE_LANES), 1)
    lse_tile = jnp.zeros((bq, LSE_LANES), F32)
    n_tot = len(ATTN_PATTERNS) * N_HEADS
    pair = 2 * HEAD_DIM
    first_half = lax.broadcasted_iota(I32, (bq, pair), 1) < HEAD_DIM
    head_a_rows = lax.broadcasted_iota(I32, (2 * bq, 1), 0) < bq
    neg_dist2 = jnp.concatenate([neg_dist, neg_dist], axis=0)
    slope_of = lambda h: 2.0 ** (-8.0 * (group * N_HEADS + h + 1) / n_tot)
    outs = []
    for hp in range(N_HEADS // 2):
        cols = slice(hp * pair, (hp + 1) * pair)
        q2 = q[:, cols]
        k2 = jnp.concatenate([kp[:, cols], kc[:, cols]], axis=0)
        v2 = jnp.concatenate([vp[:, cols], vc[:, cols]], axis=0)
        zero = jnp.zeros_like(q2)
        q4 = jnp.concatenate([jnp.where(first_half, q2, zero), jnp.where(first_half, zero, q2)],
                             axis=0)
        slopes = jnp.where(head_a_rows, slope_of(2 * hp), slope_of(2 * hp + 1))
        s = lax.dot_general(q4, k2, (((1,), (1,)), ((), ())), preferred_element_type=F32)
        s = s + slopes * neg_dist2
        m = jnp.max(s, axis=-1, keepdims=True)
        p = jnp.exp(s - m)
        den = jnp.sum(p, axis=-1, keepdims=True)
        o4 = jnp.dot(p.astype(BF16), v2, preferred_element_type=F32) / den
        lse4 = m + jnp.log(den)
        outs.append(jnp.where(first_half, o4[:bq, :], o4[bq:, :]))
        lse_tile = jnp.where(lane == 2 * hp, lse4[:bq, :],
                             jnp.where(lane == 2 * hp + 1, lse4[bq:, :], lse_tile))
    o_ref[...] = jnp.concatenate(outs, axis=1).astype(BF16).reshape(o_ref.shape)
    lse_ref[...] = lse_tile.reshape(lse_ref.shape)


def _attention_group(qkv, group, batch, seq):
    dil = ATTN_PATTERNS[group][1]
    d = N_HEADS * HEAD_DIM
    t = qkv.shape[0]
    sub = seq // dil
    nb = sub // Q_BLOCK
    rows = Q_BLOCK if dil == 1 else PERM_TILE // dil
    chunks = Q_BLOCK // rows
    u = seq // (rows * dil)
    view = lambda a, c: a.reshape(batch, u, dil, rows, c)
    blk = lambda c: (None, chunks, None, rows, c)
    col0 = group * 3
    q_spec = pl.BlockSpec(blk(d), lambda b, r, j: (b, j, r, 0, col0))
    kc_spec = pl.BlockSpec(blk(d), lambda b, r, j: (b, j, r, 0, col0 + 1))
    kp_spec = pl.BlockSpec(blk(d), lambda b, r, j: (b, jnp.maximum(j - 1, 0), r, 0, col0 + 1))
    vc_spec = pl.BlockSpec(blk(d), lambda b, r, j: (b, j, r, 0, col0 + 2))
    vp_spec = pl.BlockSpec(blk(d), lambda b, r, j: (b, jnp.maximum(j - 1, 0), r, 0, col0 + 2))
    qkv5 = view(qkv, qkv.shape[1])
    o, lse = pl.pallas_call(
        functools.partial(_attn_kernel, group=group, dil=dil),
        grid=(batch, dil, nb),
        in_specs=[q_spec, kp_spec, kc_spec, vp_spec, vc_spec],
        out_specs=[
            pl.BlockSpec(blk(d), lambda b, r, j: (b, j, r, 0, 0)),
            pl.BlockSpec(blk(LSE_LANES), lambda b, r, j: (b, j, r, 0, 0)),
        ],
        out_shape=[
            jax.ShapeDtypeStruct((batch, u, dil, rows, d), BF16),
            jax.ShapeDtypeStruct((batch, u, dil, rows, LSE_LANES), F32),
        ],
        compiler_params=_cparams(("arbitrary", "arbitrary", "arbitrary")),
    )(qkv5, qkv5, qkv5, qkv5, qkv5)
    return o.reshape(t, d), lse.reshape(t, LSE_LANES)


def _split3(v):
    a = v.astype(BF16)
    r = v - a.astype(F32)
    b = r.astype(BF16)
    c = (r - b.astype(F32)).astype(BF16)
    return a, b, c


def _attn_out_kernel(x_ref, g_ref, o0_ref, o1_ref, o2_ref, l0_ref, l1_ref, l2_ref,
                     pt_ref, e_ref, wout_ref, lng_ref, lnb_ref, out_ref):
    tm, d = x_ref.shape
    o_refs = (o0_ref, o1_ref, o2_ref)
    l_refs = (l0_ref, l1_ref, l2_ref)
    n_sub = tm // PERM_TILE

    def unperm(gi, val_bf16):
        if gi == 0:
            return val_bf16.astype(F32)
        parts = [jnp.dot(pt_ref[gi - 1], val_bf16[s * PERM_TILE:(s + 1) * PERM_TILE, :],
                         preferred_element_type=F32) for s in range(n_sub)]
        return jnp.concatenate(parts, axis=0)

    lses = []
    for gi in range(3):
        l = l_refs[gi][...]
        if gi == 0:
            lses.append(l)
        else:
            a, b, c = _split3(l)
            lses.append(unperm(gi, a) + unperm(gi, b) + unperm(gi, c))
    mx = jnp.maximum(jnp.maximum(lses[0], lses[1]), lses[2])
    es = [jnp.exp(l - mx) for l in lses]
    tot = es[0] + es[1] + es[2]
    mixed = jnp.zeros((tm, d), F32)
    for gi in range(3):
        w = es[gi] / tot
        a, b, _ = _split3(w)
        wide = (jnp.dot(a, e_ref[...], preferred_element_type=F32)
                + jnp.dot(b, e_ref[...], preferred_element_type=F32))
        mixed = mixed + wide * unperm(gi, o_refs[gi][...])
    y = jnp.dot(mixed.astype(BF16), wout_ref[...], preferred_element_type=F32)
    v = ALPHA * x_ref[...] + (1.0 + g_ref[...]) * y
    out_ref[...] = _layer_norm(v, lng_ref[...], lnb_ref[...])


def _attn_out(x2, mod3, os_, lses, perms_t, expand, w_out, ln_g, ln_b, seq):
    t, d = x2.shape
    tm = 512
    tpb = seq // tm
    row = lambda c: pl.BlockSpec((tm, c), lambda i: (i, 0))
    full = lambda shape: pl.BlockSpec(shape, lambda i: (0,) * len(shape))
    return pl.pallas_call(
        _attn_out_kernel,
        grid=(t // tm,),
        in_specs=[row(d), _mod_spec(2, d, tpb), row(d), row(d), row(d),
                  row(LSE_LANES), row(LSE_LANES), row(LSE_LANES),
                  full(perms_t.shape), full(expand.shape), full((d, d)),
                  full((1, d)), full((1, d))],
        out_specs=row(d),
        out_shape=jax.ShapeDtypeStruct((t, d), F32),
        compiler_params=_cparams(("arbitrary",)),
    )(x2, mod3, *os_, *lses, perms_t, expand, w_out, ln_g.reshape(1, d), ln_b.reshape(1, d))


def _first_index_of_max(v, iota, size):
    m = jnp.max(v, axis=0, keepdims=True)
    idx = jnp.min(jnp.where(v == m, iota, float(size)), axis=0, keepdims=True)
    return m, idx


def _topk_kernel(x_ref, sh_ref, sc_ref, wr_ref, b_ref, tri_ref, ltri_ref,
                 w_ref, pos_ref, cnt_ref):
    tr = x_ref.shape[0]
    ne = b_ref.shape[0]
    gsz = ne // N_EXPERT_GROUPS
    tile = pl.program_id(0)

    hf = x_ref[...] * (1.0 + sc_ref[...]) + sh_ref[...]
    hf_hi = hf.astype(BF16)
    hf_lo = (hf - hf_hi.astype(F32)).astype(BF16)
    nt = (((1,), (1,)), ((), ()))
    by_hi = lax.dot_general(wr_ref[...], hf_hi, nt, preferred_element_type=F32)
    logits = (by_hi[:ne, :] + by_hi[ne:, :]
              + lax.dot_general(wr_ref[0:ne, :], hf_lo, nt, preferred_element_type=F32))
    scores = 1.0 / (1.0 + jnp.exp(-logits))
    sel = scores + b_ref[...]
    iota_g = lax.broadcasted_iota(I32, (gsz, tr), 0).astype(F32)
    iota_n = lax.broadcasted_iota(I32, (N_EXPERT_GROUPS, tr), 0).astype(F32)
    gs = jnp.zeros((N_EXPERT_GROUPS, tr), F32)
    for g in range(N_EXPERT_GROUPS):
        blk = sel[g * gsz:(g + 1) * gsz, :]
        m1, i1 = _first_index_of_max(blk, iota_g, gsz)
        m2 = jnp.max(jnp.where(iota_g == i1, -jnp.inf, blk), axis=0, keepdims=True)
        gs = jnp.where(iota_n == float(g), m1 + m2, gs)
    gmask = jnp.zeros((N_EXPERT_GROUPS, tr), F32)
    for _ in range(TOPK_GROUPS):
        _, gi = _first_index_of_max(gs, iota_n, N_EXPERT_GROUPS)
        hit = iota_n == gi
        gmask = jnp.where(hit, 1.0, gmask)
        gs = jnp.where(hit, -jnp.inf, gs)
    masked_rows = []
    for g in range(N_EXPERT_GROUPS):
        keep = jnp.broadcast_to(gmask[g:g + 1, :], (gsz, tr)) > 0.5
        masked_rows.append(jnp.where(keep, sel[g * gsz:(g + 1) * gsz, :], -jnp.inf))
    cur = jnp.concatenate(masked_rows, axis=0)
    iota_e = lax.broadcasted_iota(I32, (ne, tr), 0).astype(F32)
    chosen = jnp.zeros((ne, tr), F32)
    for _ in range(TOP_K):
        _, ei = _first_index_of_max(cur, iota_e, ne)
        hit = iota_e == ei
        cur = jnp.where(hit, -jnp.inf, cur)
        chosen = jnp.where(hit, 1.0, chosen)
    picked = chosen > 0.5
    top_scores = jnp.where(picked, scores, 0.0)
    wsum = jnp.sum(top_scores, axis=0, keepdims=True)
    w_ref[...] = top_scores / wsum * ROUTED_SCALE
    before = jnp.dot(chosen.astype(BF16), tri_ref[...], preferred_element_type=F32)
    n = jnp.sum(chosen, axis=1, keepdims=True)
    n_chunks = jnp.floor((n + (CHUNK - 1)) * (1.0 / CHUNK))
    run_off = jnp.dot(ltri_ref[...], jnp.broadcast_to(n_chunks, (ne, 128)).astype(BF16),
                      preferred_element_type=F32)[:, 0:1] * CHUNK
    pos_ref[...] = jnp.where(picked, before + run_off, NO_SLOT)

    @pl.when(tile == 0)
    def _():
        cnt_ref[...] = jnp.zeros_like(cnt_ref)

    lane = lax.broadcasted_iota(I32, cnt_ref.shape, 1)
    cnt_ref[...] = jnp.where(lane == tile, jnp.broadcast_to(n, cnt_ref.shape).astype(I32),
                             cnt_ref[...])


def _topk_route(x2, mod3, w_router, router_b, seq):
    t, d = x2.shape
    ne = w_router.shape[1]
    tr = TOK_TILE
    tpb = seq // tr
    assert t // tr <= 128
    w_t = w_router.T
    w_hi = w_t.astype(BF16)
    w_split = jnp.concatenate([w_hi, (w_t - w_hi.astype(F32)).astype(BF16)], axis=0)
    tri = (jnp.arange(tr)[:, None] < jnp.arange(tr)[None, :]).astype(BF16)
    ltri = (jnp.arange(ne)[None, :] < jnp.arange(ne)[:, None]).astype(BF16)
    out = lambda dt: jax.ShapeDtypeStruct((ne, t), dt)
    row = pl.BlockSpec((ne, tr), lambda i: (0, i))
    return pl.pallas_call(
        _topk_kernel,
        grid=(t // tr,),
        in_specs=[pl.BlockSpec((tr, d), lambda i: (i, 0)),
                  _mod_spec(3, d, tpb), _mod_spec(4, d, tpb),
                  pl.BlockSpec((2 * ne, d), lambda i: (0, 0)),
                  pl.BlockSpec((ne, 1), lambda i: (0, 0)),
                  pl.BlockSpec((tr, tr), lambda i: (0, 0)),
                  pl.BlockSpec((ne, ne), lambda i: (0, 0))],
        out_specs=[row, row, pl.BlockSpec((ne, 128), lambda i: (0, 0))],
        out_shape=[out(F32), out(F32), jax.ShapeDtypeStruct((ne, 128), I32)],
        compiler_params=_cparams(("arbitrary",)),
    )(x2, mod3, mod3, w_split, router_b.reshape(ne, 1), tri, ltri)


def _slot_plan(counts, n_tiles, n_blocks):
    n = counts[:, :n_tiles]
    nch = (n + (CHUNK - 1)) // CHUNK
    rows = jnp.sum(nch, axis=1) * CHUNK
    region = ((rows + EXPERT_ROWS - 1) // EXPERT_ROWS) * EXPERT_ROWS
    region_end = jnp.cumsum(region)
    region_start = region_end - region
    run_chunk = region_start[:, None] // CHUNK + jnp.cumsum(nch, axis=1) - nch
    local_end = jnp.cumsum(nch, axis=0)
    local_chunk = local_end - nch
    experts = jnp.arange(N_EXPERTS, dtype=I32)
    copies = []
    done = jnp.zeros_like(nch)
    for size in COPY_SIZES:
        k = (nch - done) // size
        end = jnp.cumsum(k, axis=0)
        cap = _copy_cap(size)
        m = jnp.arange(cap, dtype=I32)
        run_of_m = jnp.sum((end[:, :, None] <= m[None, None, :]).astype(I32), axis=0)
        owner = run_of_m[None] == experts[:, None, None]
        pick = lambda a: jnp.sum(jnp.where(owner, a[:, :, None], 0), axis=0)
        within = (m[None, :] - pick(end - k)) * size
        copies += [(pick(local_chunk + done) + within).reshape(-1).astype(I32),
                   (pick(run_chunk + done) + within).reshape(-1).astype(I32),
                   end[-1].astype(I32)]
        done = done + k * size
    block_row = jnp.arange(n_blocks, dtype=I32) * EXPERT_ROWS
    group_e = jnp.minimum(jnp.sum((region_end[None, :] <= block_row[:, None]).astype(I32), axis=1),
                          N_EXPERTS - 1)
    n_used = region_end[-1] // EXPERT_ROWS
    nonempty = region > 0
    run_of_expert = jnp.cumsum(nonempty.astype(I32)) - 1
    run_expert = jnp.sum(jnp.where((run_of_expert[None, :] == experts[:, None]) & nonempty[None, :],
                                   experts[None, :], 0), axis=1)
    blocks = jnp.arange(n_blocks, dtype=I32)
    first = ((blocks == 0) | (group_e != jnp.roll(group_e, 1))) & (blocks < n_used)
    per_tile = lambda a: (a * CHUNK).T.astype(I32)
    return dict(
        run_begin_rows=per_tile(local_chunk)[:, None, :], run_end_rows=per_tile(local_end)[:, None, :],
        run_begin_cols=per_tile(local_chunk)[:, :, None], run_end_cols=per_tile(local_end)[:, :, None],
        block_run=(jnp.cumsum(first.astype(I32)) - 1).astype(I32), block_first=first.astype(I32),
        run_expert=run_expert.astype(I32), n_runs=jnp.sum(nonempty.astype(I32)).reshape(1),
        copies=copies,
        tile_chunks=jnp.sum(nch, axis=0).astype(I32),
        tail_chunk=((region_start + rows) // CHUNK).astype(I32),
        tail_chunks=((region - rows) // CHUNK).astype(I32),
        group_e=group_e, n_used=(region_end[-1:] // EXPERT_ROWS).astype(I32))


def _chunk_rows(ref, chunk_index, n_chunks=1):
    start = chunk_index * CHUNK
    if not isinstance(start, int):
        start = pl.multiple_of(start, CHUNK)
    return ref.at[pl.ds(start, n_chunks * CHUNK), :]


def _copy_cap(size):
    return LOCAL_ROWS // CHUNK // size if size == COPY_SIZES[0] else N_EXPERTS


def _start_copies(copy_refs, tile, make_copy):
    for n, size in enumerate(COPY_SIZES):
        src_ref, dst_ref, cnt_ref = copy_refs[3 * n:3 * n + 3]
        base = tile * _copy_cap(size)

        def start(m, priority, src_ref=src_ref, dst_ref=dst_ref, base=base, size=size):
            make_copy(src_ref[base + m], dst_ref[base + m], size).start(priority=priority)

        _for_chunks_two_queues(cnt_ref[tile], start)


def _pos_digits(pos):
    hi = jnp.floor(pos * (1.0 / POS_RADIX))
    return hi.astype(BF16), (pos - POS_RADIX * hi).astype(BF16)


def _for_chunks_two_queues(count, start):
    def pair(p, carry):
        start(2 * p, 0)
        start(2 * p + 1, 1)
        return carry

    lax.fori_loop(0, lax.shift_right_logical(count, 1), pair, 0)

    @pl.when((count & 1) == 1)
    def _():
        start(count - 1, 0)


def _wait_chunks(count, src_ref, dst_ref, sem):
    group_rows = pl.ds(0, WAIT_GROUP * CHUNK)

    def wait_group(c, carry):
        pltpu.make_async_copy(src_ref.at[group_rows, :], dst_ref.at[group_rows, :], sem).wait()
        return carry

    def wait_one(c, carry):
        pltpu.make_async_copy(_chunk_rows(src_ref, 0), _chunk_rows(dst_ref, 0), sem).wait()
        return carry

    lax.fori_loop(0, lax.shift_right_logical(count, WAIT_GROUP.bit_length() - 1), wait_group, 0)
    lax.fori_loop(0, count & (WAIT_GROUP - 1), wait_one, 0)


def _dispatch_kernel(*refs):
    copy_refs, refs = refs[:3 * len(COPY_SIZES)], refs[3 * len(COPY_SIZES):]
    _dispatch_body(copy_refs, *refs)


def _dispatch_body(copy_refs, tile_ref, tail_ref, tailn_ref, used_ref,
                     x_ref, sh_ref, sc_ref, pos_ref, begin_ref, end_ref,
                     xg_hbm, stage_ref, zero_ref, sems, sem_blk):
    tr = x_ref.shape[0]
    i = pl.program_id(0)
    n_blocks = xg_hbm.shape[0] // EXPERT_ROWS

    def drain(count, sem):
        _wait_chunks(count, stage_ref.at[0], xg_hbm, sem)

    def block_copy(j):
        rows = pl.ds(pl.multiple_of(j * EXPERT_ROWS, EXPERT_ROWS), EXPERT_ROWS)
        return pltpu.make_async_copy(zero_ref, xg_hbm.at[rows, :], sem_blk)

    def wait_block(j, carry):
        block_copy(j).wait()
        return carry

    @pl.when(i == 0)
    def _():
        zero_ref[...] = jnp.zeros_like(zero_ref)

        def start_block(j, carry):
            block_copy(j).start()
            return carry

        lax.fori_loop(used_ref[0], n_blocks, start_block, 0)

        def per_expert(e, total):
            def per_chunk(c, carry):
                pltpu.make_async_copy(_chunk_rows(zero_ref, 0),
                                      _chunk_rows(xg_hbm, tail_ref[e] + c), sems.at[0]).start()
                return carry
            lax.fori_loop(0, tailn_ref[e], per_chunk, 0)
            return total + tailn_ref[e]

        drain(lax.fori_loop(0, N_EXPERTS, per_expert, 0), sems.at[0])

    hf = (x_ref[...] * (1.0 + sc_ref[...]) + sh_ref[...]).astype(BF16)
    row_e = lax.broadcasted_iota(I32, (LOCAL_ROWS, N_EXPERTS), 0)
    owner = jnp.where((row_e >= begin_ref[...]) & (row_e < end_ref[...]), 1.0, 0.0).astype(BF16)
    hi_digit, lo_digit = _pos_digits(pos_ref[...])
    slot_pos = (POS_RADIX * jnp.dot(owner, hi_digit, preferred_element_type=F32)
                + jnp.dot(owner, lo_digit, preferred_element_type=F32))
    row_t = lax.broadcasted_iota(I32, (LOCAL_ROWS, tr), 0).astype(F32)
    perm = jnp.where(slot_pos == row_t, 1.0, 0.0).astype(BF16)
    rows = jnp.dot(perm, hf, preferred_element_type=F32)
    half = rows.shape[1] // 2
    buf = stage_ref.at[i % 2]
    buf[...] = (lax.bitcast_convert_type(rows[:, :half], U32)
                | (lax.bitcast_convert_type(rows[:, half:], U32) >> 16))

    _start_copies(copy_refs, i, lambda src, dst, n: pltpu.make_async_copy(
        _chunk_rows(buf, src, n), _chunk_rows(xg_hbm, dst, n), sems.at[i % 2]))

    @pl.when(i > 0)
    def _():
        drain(tile_ref[jnp.maximum(i - 1, 0)], sems.at[(i + 1) % 2])

    @pl.when(i == pl.num_programs(0) - 1)
    def _():
        drain(tile_ref[i], sems.at[i % 2])
        lax.fori_loop(used_ref[0], n_blocks, wait_block, 0)


def _dispatch(plan, x2, mod3, pos, n_slots, seq):
    t, d = x2.shape
    tr = TOK_TILE
    tpb = seq // tr
    return pl.pallas_call(
        _dispatch_kernel,
        grid_spec=pltpu.PrefetchScalarGridSpec(
            num_scalar_prefetch=4 + len(plan["copies"]),
            grid=(t // tr,),
            in_specs=[pl.BlockSpec((tr, d), lambda i, *_: (i, 0)),
                      _mod_spec(3, d, tpb), _mod_spec(4, d, tpb),
                      pl.BlockSpec((N_EXPERTS, tr), lambda i, *_: (0, i)),
                      pl.BlockSpec((None, 1, N_EXPERTS), lambda i, *_: (i, 0, 0)),
                      pl.BlockSpec((None, 1, N_EXPERTS), lambda i, *_: (i, 0, 0))],
            out_specs=pl.BlockSpec(memory_space=pl.ANY),
            scratch_shapes=[pltpu.VMEM((2, LOCAL_ROWS, d // 2), U32),
                            pltpu.VMEM((EXPERT_ROWS, d // 2), U32),
                            pltpu.SemaphoreType.DMA((2,)), pltpu.SemaphoreType.DMA],
        ),
        out_shape=jax.ShapeDtypeStruct((n_slots, d // 2), U32),
        compiler_params=_cparams(("arbitrary",)),
    )(*plan["copies"], plan["tile_chunks"], plan["tail_chunk"], plan["tail_chunks"],
      plan["n_used"], x2, mod3, mod3, pos, plan["run_begin_rows"], plan["run_end_rows"])


def _expert_kernel(run_ref, first_ref, rexp_ref, nrun_ref, nu_ref,
                   xg_hbm, wgu_hbm, wdn_hbm, yg_hbm,
                   xbuf, ybuf, zbuf, wgu_f32, wdn_f32, wgu_bf, wdn_bf, x_sem, y_sem, w_sem, z_sem,
                   *, layer):
    n_used = nu_ref[0]
    n_blocks = yg_hbm.shape[0] // EXPERT_ROWS
    depth, _, half = xbuf.shape
    f = wdn_bf.shape[0]

    def rows(j):
        return pl.ds(pl.multiple_of(j * EXPERT_ROWS, EXPERT_ROWS), EXPERT_ROWS)

    def x_copy(j):
        return pltpu.make_async_copy(xg_hbm.at[rows(j), :], xbuf.at[j % depth],
                                     x_sem.at[j % depth])

    def y_copy(j):
        return pltpu.make_async_copy(ybuf.at[j % depth], yg_hbm.at[rows(j), :],
                                     y_sem.at[j % depth])

    def w_copies(r):
        e = rexp_ref[r]
        return (pltpu.make_async_copy(wgu_hbm.at[layer, e], wgu_f32.at[r % 2], w_sem.at[r % 2]),
                pltpu.make_async_copy(wdn_hbm.at[layer, e], wdn_f32.at[r % 2], w_sem.at[r % 2]))

    def zero_copy(j):
        return pltpu.make_async_copy(zbuf, yg_hbm.at[rows(j), :], z_sem)

    zbuf[...] = jnp.zeros_like(zbuf)

    def zero_start(j, carry):
        zero_copy(j).start()
        return carry

    def zero_wait(j, carry):
        zero_copy(j).wait()
        return carry

    lax.fori_loop(n_used, n_blocks, zero_start, 0)

    for cp in w_copies(0):
        cp.start()
    for a in range(depth - 1):
        @pl.when(a < n_used)
        def _():
            x_copy(a).start()

    def block(j, carry):
        r = run_ref[j]

        @pl.when(first_ref[j] == 1)
        def _():
            for cp in w_copies(r):
                cp.wait()

            @pl.when(r + 1 < nrun_ref[0])
            def _():
                for cp in w_copies(r + 1):
                    cp.start()

            wgu_bf[...] = wgu_f32[r % 2].astype(BF16)
            wdn_bf[...] = wdn_f32[r % 2].astype(BF16)

        x_copy(j).wait()

        @pl.when(j + (depth - 1) < n_used)
        def _():
            x_copy(j + (depth - 1)).start()

        @pl.when(j >= depth)
        def _():
            y_copy(j - depth).wait()

        for g in range(EXPERT_ROWS // GEMM_SUB_ROWS):
            rws = slice(g * GEMM_SUB_ROWS, (g + 1) * GEMM_SUB_ROWS)
            hi, lo = _unpack_bf16_pairs(xbuf[j % depth, rws, :])
            gu = (jnp.dot(hi.astype(BF16), wgu_bf[0:half, :], preferred_element_type=F32)
                  + jnp.dot(lo.astype(BF16), wgu_bf[half:, :], preferred_element_type=F32))
            act = (_silu(gu[:, :f]) * gu[:, f:]).astype(BF16)
            ybuf[j % depth, rws, :] = _pack_bf16_pairs(
                jnp.dot(act, wdn_bf[...], preferred_element_type=F32))
        y_copy(j).start()
        return carry

    lax.fori_loop(0, n_used, block, 0)

    for a in range(depth, 0, -1):
        @pl.when(n_used >= a)
        def _():
            y_copy(n_used - a).wait()

    lax.fori_loop(n_used, n_blocks, zero_wait, 0)


def _expert_gemm(plan, xg, w_gu, w_down, layer):
    n_slots, half = xg.shape
    _, ne, d, f2 = w_gu.shape
    f = w_down.shape[2]
    hbm = pl.BlockSpec(memory_space=pl.ANY)
    return pl.pallas_call(
        functools.partial(_expert_kernel, layer=layer),
        grid_spec=pltpu.PrefetchScalarGridSpec(
            num_scalar_prefetch=5,
            grid=(1,),
            in_specs=[hbm, hbm, hbm],
            out_specs=hbm,
            scratch_shapes=[pltpu.VMEM((GEMM_DEPTH, EXPERT_ROWS, half), U32),
                            pltpu.VMEM((GEMM_DEPTH, EXPERT_ROWS, half), U32),
                            pltpu.VMEM((EXPERT_ROWS, half), U32),
                            pltpu.VMEM((2, d, f2), F32), pltpu.VMEM((2, f, d), F32),
                            pltpu.VMEM((d, f2), BF16), pltpu.VMEM((f, d), BF16),
                            pltpu.SemaphoreType.DMA((GEMM_DEPTH,)),
                            pltpu.SemaphoreType.DMA((GEMM_DEPTH,)),
                            pltpu.SemaphoreType.DMA((2,)), pltpu.SemaphoreType.DMA],
        ),
        out_shape=jax.ShapeDtypeStruct((n_slots, half), U32),
        compiler_params=_cparams(("arbitrary",)),
    )(plan["block_run"], plan["block_first"], plan["run_expert"], plan["n_runs"], plan["n_used"],
      xg, w_gu, w_down)


def _combine_kernel(*refs):
    copy_refs, refs = refs[:3 * len(COPY_SIZES)], refs[3 * len(COPY_SIZES):]
    _combine_body(copy_refs, *refs)


def _combine_body(copy_refs, tile_ref,
                    x_ref, sh_ref, sc_ref, g_ref, pos_ref, wt_ref, begin_ref, end_ref, brow_ref,
                    wsgu_ref, wsdn_ref, lng_ref, lnb_ref, yg_hbm, o_ref, stage_ref, sems):
    tr, d = x_ref.shape
    i = pl.program_id(0)
    n_tiles = pl.num_programs(0)

    def fetch(tile, buffer):
        _start_copies(copy_refs, tile, lambda src, dst, n: pltpu.make_async_copy(
            _chunk_rows(yg_hbm, dst, n), _chunk_rows(stage_ref.at[buffer], src, n),
            sems.at[buffer]))

    @pl.when(i == 0)
    def _():
        stage_ref[...] = jnp.zeros_like(stage_ref)
        fetch(0, 0)

    nxt = jnp.minimum(i + 1, n_tiles - 1)
    for buffer in range(2):
        @pl.when((i + 1 < n_tiles) & (nxt % 2 == buffer))
        def _():
            fetch(nxt, buffer)

    x = x_ref[...]
    hf = (x * (1.0 + sc_ref[...]) + sh_ref[...]).astype(BF16)
    f = wsdn_ref.shape[0]
    su = jnp.dot(hf, wsgu_ref[...], preferred_element_type=F32)
    act = (_silu(su[:, :f]) * su[:, f:]).astype(BF16)
    shared = jnp.dot(act, wsdn_ref[...], preferred_element_type=F32)

    col_e = lax.broadcasted_iota(I32, (N_EXPERTS, LOCAL_ROWS), 1)
    owner = jnp.where((col_e >= begin_ref[...]) & (col_e < end_ref[...]), 1.0, 0.0).astype(BF16)
    begin_row = brow_ref[...].astype(F32)
    pos = pos_ref[...]
    rank1 = jnp.where(pos >= 0.0, pos - begin_row + 1.0, 0.0).astype(BF16)
    slot_rank = jnp.dot(rank1, owner, preferred_element_type=F32)
    b_hi, b_lo = _pos_digits(jnp.broadcast_to(begin_row, (8, N_EXPERTS)))
    run_begin = (POS_RADIX * jnp.dot(b_hi, owner, preferred_element_type=F32)
                 + jnp.dot(b_lo, owner, preferred_element_type=F32))[0:1, :]
    offset1 = lax.broadcasted_iota(I32, (1, LOCAL_ROWS), 1).astype(F32) - run_begin + 1.0
    holds = slot_rank == offset1
    weights = jnp.where(holds, jnp.dot(wt_ref[...].astype(BF16), owner,
                                       preferred_element_type=F32), 0.0).astype(BF16)

    buf = stage_ref.at[i % 2]
    _wait_chunks(tile_ref[i], yg_hbm, buf, sems.at[i % 2])

    y_hi, y_lo = _unpack_bf16_pairs(buf[...])
    y = jnp.concatenate([y_hi.astype(BF16), y_lo.astype(BF16)], axis=1)
    routed = jnp.dot(weights, y, preferred_element_type=F32)
    v = ALPHA * x + (1.0 + g_ref[...]) * (routed + shared)
    o_ref[...] = _layer_norm(v, lng_ref[...], lnb_ref[...])


def _combine(plan, x2, mod3, pos_tok, w_tok, w_sh_gu, w_sh_down, ln_g, ln_b, yg, seq):
    t, d = x2.shape
    tr = TOK_TILE
    tpb = seq // tr
    full = lambda shape: pl.BlockSpec(shape, lambda i, *_: (0,) * len(shape))
    return pl.pallas_call(
        _combine_kernel,
        grid_spec=pltpu.PrefetchScalarGridSpec(
            num_scalar_prefetch=1 + len(plan["copies"]),
            grid=(t // tr,),
            in_specs=[pl.BlockSpec((tr, d), lambda i, *_: (i, 0)),
                      _mod_spec(3, d, tpb), _mod_spec(4, d, tpb), _mod_spec(5, d, tpb),
                      pl.BlockSpec((tr, N_EXPERTS), lambda i, *_: (i, 0)),
                      pl.BlockSpec((tr, N_EXPERTS), lambda i, *_: (i, 0)),
                      pl.BlockSpec((None, N_EXPERTS, 1), lambda i, *_: (i, 0, 0)),
                      pl.BlockSpec((None, N_EXPERTS, 1), lambda i, *_: (i, 0, 0)),
                      pl.BlockSpec((None, 1, N_EXPERTS), lambda i, *_: (i, 0, 0)),
                      full(w_sh_gu.shape), full(w_sh_down.shape), full((1, d)), full((1, d)),
                      pl.BlockSpec(memory_space=pl.ANY)],
            out_specs=pl.BlockSpec((tr, d), lambda i, *_: (i, 0)),
            scratch_shapes=[pltpu.VMEM((2, LOCAL_ROWS, d // 2), U32),
                            pltpu.SemaphoreType.DMA((2,))],
        ),
        out_shape=jax.ShapeDtypeStruct((t, d), F32),
        compiler_params=_cparams(("arbitrary",)),
    )(*plan["copies"], plan["tile_chunks"],
      x2, mod3, mod3, mod3, pos_tok, w_tok, plan["run_begin_cols"], plan["run_end_cols"],
      plan["run_begin_rows"],
      w_sh_gu, w_sh_down, ln_g.reshape(1, d), ln_b.reshape(1, d), yg)


def _moe_layer(x2, mod3, w_router, router_b, w_gu, w_down, layer, w_sh_gu, w_sh_down,
               ln_g, ln_b, seq):
    t, d = x2.shape
    n_tiles = t // TOK_TILE
    w_top, pos, counts = _topk_route(x2, mod3, w_router, router_b, seq)
    bound = t * TOP_K + n_tiles * N_EXPERTS * (CHUNK - 1) + N_EXPERTS * (EXPERT_ROWS - 1)
    n_blocks = -(-bound // EXPERT_ROWS)
    plan = _slot_plan(counts, n_tiles, n_blocks)
    xg = _dispatch(plan, x2, mod3, pos, n_blocks * EXPERT_ROWS, seq)
    yg = _expert_gemm(plan, xg, w_gu, w_down, layer)
    return _combine(plan, x2, mod3, pos.T, w_top.T, w_sh_gu, w_sh_down, ln_g, ln_b, yg, seq)


def kernel(x, c, ada_w, ada_b, pool_w_in, pool_w_grp, pool_scale, pool_w_out, attn_w_in, attn_w_out, ln1_g, ln1_b, router_w, router_b, exp_w_gu, exp_w_down, sh_w_gu, sh_w_down, ln2_g, ln2_b):
    batch, seq, d = x.shape
    depth = ada_w.shape[0]
    t = batch * seq
    mod = _modulation(c, ada_w, ada_b).reshape(depth, batch, 1, 6 * d)
    perms = jnp.stack([_perm_matrix(dil) for _, dil in ATTN_PATTERNS[1:]])
    perms_t = jnp.swapaxes(perms, 1, 2)
    expand = (jnp.arange(LSE_LANES)[:, None] == (jnp.arange(d)[None, :] // HEAD_DIM)).astype(BF16)
    part = (jnp.arange(attn_w_in.shape[2]) // d) % 3
    q_scale = jnp.where(part == 0, HEAD_DIM ** -0.5, 1.0).astype(F32)
    x2 = x.reshape(t, d)
    for i in range(depth):
        mod3 = mod[i]
        j = i // 2
        if i % 2 == 0:
            x2 = _pool_layer(x2, mod3, pool_w_in[j].astype(BF16), pool_w_grp[j].astype(BF16),
                             pool_scale[j], pool_w_out[j].astype(BF16), ln1_g[i], ln1_b[i], seq)
        else:
            qkv = _qkv_proj(x2, mod3, perms, (attn_w_in[j] * q_scale).astype(BF16), seq)
            res = [_attention_group(qkv, g, batch, seq) for g in range(len(ATTN_PATTERNS))]
            x2 = _attn_out(x2, mod3, [r[0] for r in res], [r[1] for r in res], perms_t, expand,
                           attn_w_out[j].astype(BF16), ln1_g[i], ln1_b[i], seq)
        x2 = _moe_layer(x2, mod3, router_w[i], router_b[i], exp_w_gu, exp_w_down, i,
                        sh_w_gu[i].astype(BF16), sh_w_down[i].astype(BF16),
                        ln2_g[i], ln2_b[i], seq)
    return x2.reshape(batch, seq, d)
```

```python
import functools

import jax
import jax.numpy as jnp
from jax import lax
from jax.experimental import pallas as pl
from jax.experimental.pallas import tpu as pltpu

F32 = jnp.float32
BF16 = jnp.bfloat16
U32 = jnp.uint32
I32 = jnp.int32

POOL_WINDOWS = (2, 4, 8, 16)
ATTN_PATTERNS = ((128, 1), (512, 4), (2048, 16))
HEAD_DIM = 64
N_HEADS = 16
Q_BLOCK = 128
N_EXPERTS = 64
TOP_K = 8
N_EXPERT_GROUPS = 8
TOPK_GROUPS = 4
ROUTED_SCALE = 2.5
DEPTH = 4
ALPHA = (2 * DEPTH) ** 0.25
LN_EPS = 1e-5

PERM_TILE = 256
POOL_HALO = 16
TOK_TILE = 256
CHUNK = 8
EXPERT_ROWS = 512
LOCAL_ROWS = -(-(TOK_TILE * TOP_K + N_EXPERTS * (CHUNK - 1)) // 256) * 256
NO_SLOT = -64.0
POS_RADIX = 64.0
COPY_SIZES = (4, 2, 1)
WAIT_GROUP = 32
GEMM_SUB_ROWS = 256
GEMM_DEPTH = 4
LSE_LANES = 128
VMEM_LIMIT = 56 * 1024 * 1024
NEG_BIG = -1e30


def _cparams(sem):
    return pltpu.CompilerParams(dimension_semantics=sem, vmem_limit_bytes=VMEM_LIMIT)


def _layer_norm(v, g, b):
    mu = jnp.mean(v, axis=-1, keepdims=True)
    c = v - mu
    var = jnp.mean(c * c, axis=-1, keepdims=True)
    return c * lax.rsqrt(var + LN_EPS) * g + b


def _silu(v):
    return v * (1.0 / (1.0 + jnp.exp(-v)))


def _pack_bf16_pairs(v):
    n = v.shape[1] // 2
    hi = lax.bitcast_convert_type(v[:, :n].astype(BF16).astype(F32), U32)
    lo = lax.bitcast_convert_type(v[:, n:].astype(BF16).astype(F32), U32)
    return hi | (lo >> 16)


def _unpack_bf16_pairs(p):
    hi = lax.bitcast_convert_type(p & jnp.uint32(0xFFFF0000), F32)
    lo = lax.bitcast_convert_type(p << 16, F32)
    return hi, lo


def _mod_kernel(c_ref, w_ref, b_ref, o_ref):
    cs = _silu(c_ref[...])
    o_ref[...] = jnp.dot(cs, w_ref[...], preferred_element_type=F32) + b_ref[...]


def _modulation(c, ada_w, ada_b):
    depth, d, n6 = ada_w.shape
    b = c.shape[0]
    tn = 1536
    return pl.pallas_call(
        _mod_kernel,
        grid=(depth, n6 // tn),
        in_specs=[
            pl.BlockSpec((b, d), lambda i, n: (0, 0)),
            pl.BlockSpec((None, d, tn), lambda i, n: (i, 0, n)),
            pl.BlockSpec((None, 1, tn), lambda i, n: (i, 0, n)),
        ],
        out_specs=pl.BlockSpec((None, b, tn), lambda i, n: (i, 0, n)),
        out_shape=jax.ShapeDtypeStruct((depth, b, n6), F32),
        compiler_params=_cparams(("arbitrary", "arbitrary")),
    )(c, ada_w, ada_b.reshape(depth, 1, n6))


def _mod_spec(chunk, d, tiles_per_batch):
    return pl.BlockSpec((None, 1, d), lambda *idx: (idx[0] // tiles_per_batch, 0, chunk))


def _pool_kernel(x_ref, sh_ref, sc_ref, g_ref, win_ref, wgrp_ref, cs_ref, wout_ref,
                 lng_ref, lnb_ref, o_ref, ext_ref, *, tiles_per_batch):
    tm, d = x_ref.shape
    s_idx = pl.program_id(0) % tiles_per_batch
    x = x_ref[...]
    h = (x * (1.0 + sc_ref[...]) + sh_ref[...]).astype(BF16)
    u = jnp.dot(h, win_ref[...], preferred_element_type=F32)

    @pl.when(s_idx == 0)
    def _():
        ext_ref[0:POOL_HALO, :] = jnp.zeros((POOL_HALO, d), F32)

    @pl.when(s_idx != 0)
    def _():
        ext_ref[0:POOL_HALO, :] = ext_ref[tm:tm + POOL_HALO, :]

    ext_ref[POOL_HALO:POOL_HALO + tm, :] = u

    pos = s_idx * tm + lax.broadcasted_iota(I32, (tm, 1), 0) + 1
    gc = d // len(POOL_WINDOWS)
    ys = []
    for g, w in enumerate(POOL_WINDOWS):
        cols = slice(g * gc, (g + 1) * gc)
        acc = u[:, cols]
        for j in range(1, w):
            acc = acc + ext_ref[POOL_HALO - j:POOL_HALO - j + tm, cols]
        cnt = jnp.minimum(pos, w).astype(F32)
        z = (acc / cnt - u[:, cols]).astype(BF16)
        ys.append(jnp.dot(z, wgrp_ref[g], preferred_element_type=F32))
    y = (jnp.concatenate(ys, axis=1) * cs_ref[...]).astype(BF16)
    out = jnp.dot(y, wout_ref[...], preferred_element_type=F32)
    v = ALPHA * x + (1.0 + g_ref[...]) * out
    o_ref[...] = _layer_norm(v, lng_ref[...], lnb_ref[...])


def _pool_layer(x2, mod3, w_in, w_grp, ch_scale, w_out, ln_g, ln_b, seq):
    t, d = x2.shape
    tm = 512
    tpb = seq // tm
    full = lambda shape: pl.BlockSpec(shape, lambda i: (0,) * len(shape))
    return pl.pallas_call(
        functools.partial(_pool_kernel, tiles_per_batch=tpb),
        grid=(t // tm,),
        in_specs=[
            pl.BlockSpec((tm, d), lambda i: (i, 0)),
            _mod_spec(0, d, tpb), _mod_spec(1, d, tpb), _mod_spec(2, d, tpb),
            full((d, d)), full(w_grp.shape), full((1, d)), full((d, d)),
            full((1, d)), full((1, d)),
        ],
        out_specs=pl.BlockSpec((tm, d), lambda i: (i, 0)),
        out_shape=jax.ShapeDtypeStruct((t, d), F32),
        scratch_shapes=[pltpu.VMEM((tm + POOL_HALO, d), F32)],
        compiler_params=_cparams(("arbitrary",)),
    )(x2, mod3, mod3, mod3, w_in, w_grp, ch_scale.reshape(1, d), w_out,
      ln_g.reshape(1, d), ln_b.reshape(1, d))


def _perm_matrix(dil):
    p = jnp.arange(PERM_TILE)
    chunk = PERM_TILE // dil
    src = (p % chunk) * dil + p // chunk
    return (src[:, None] == jnp.arange(PERM_TILE)[None, :]).astype(BF16)


def _qkv_kernel(x_ref, sh_ref, sc_ref, p_ref, w_ref, o_ref, h_ref):
    tm = x_ref.shape[0]
    g = pl.program_id(1)

    @pl.when(g == 0)
    def _():
        h = (x_ref[...] * (1.0 + sc_ref[...]) + sh_ref[...]).astype(BF16)
        h_ref[0] = h
        for gi in range(1, len(ATTN_PATTERNS)):
            for s in range(tm // PERM_TILE):
                rows = slice(s * PERM_TILE, (s + 1) * PERM_TILE)
                h_ref[gi, rows, :] = jnp.dot(
                    p_ref[gi - 1], h[rows, :], preferred_element_type=F32).astype(BF16)

    o_ref[...] = jnp.dot(h_ref[g], w_ref[...], preferred_element_type=F32).astype(BF16)


def _qkv_proj(x2, mod3, perms, w_in, seq):
    t, d = x2.shape
    ng = len(ATTN_PATTERNS)
    tm = 1024
    tpb = seq // tm
    return pl.pallas_call(
        _qkv_kernel,
        grid=(t // tm, ng),
        in_specs=[
            pl.BlockSpec((tm, d), lambda m, g: (m, 0)),
            _mod_spec(0, d, tpb), _mod_spec(1, d, tpb),
            pl.BlockSpec(perms.shape, lambda m, g: (0, 0, 0)),
            pl.BlockSpec((d, 3 * d), lambda m, g: (0, g)),
        ],
        out_specs=pl.BlockSpec((tm, 3 * d), lambda m, g: (m, g)),
        out_shape=jax.ShapeDtypeStruct((t, ng * 3 * d), BF16),
        scratch_shapes=[pltpu.VMEM((ng, tm, d), BF16)],
        compiler_params=_cparams(("arbitrary", "arbitrary")),
    )(x2, mod3, mod3, perms, w_in)


def _attn_kernel(q_ref, kp_ref, kc_ref, vp_ref, vc_ref, o_ref, lse_ref, *, group, dil):
    bq = Q_BLOCK
    d = N_HEADS * HEAD_DIM
    j = pl.program_id(2)
    q = q_ref[...].reshape(bq, d)
    kp = kp_ref[...].reshape(bq, d)
    kc = kc_ref[...].reshape(bq, d)
    vp = vp_ref[...].reshape(bq, d)
    vc = vc_ref[...].reshape(bq, d)

    qi = lax.broadcasted_iota(I32, (bq, 2 * bq), 0)
    kj = lax.broadcasted_iota(I32, (bq, 2 * bq), 1)
    dist = qi + bq - kj
    steps = ATTN_PATTERNS[group][0] // dil
    valid = (dist >= 0) & (dist <= steps) & ((kj >= bq) | (j > 0))
    neg_dist = jnp.where(valid, (dist * -dil).astype(F32), NEG_BIG)
    lane = lax.broadcasted_iota(I32, (bq, LSE_LANES), 1)
    lse_tile = jnp.zeros((bq, LSE_LANES), F32)
    n_tot = len(ATTN_PATTERNS) * N_HEADS
    pair = 2 * HEAD_DIM
    first_half = lax.broadcasted_iota(I32, (bq, pair), 1) < HEAD_DIM
    head_a_rows = lax.broadcasted_iota(I32, (2 * bq, 1), 0) < bq
    neg_dist2 = jnp.concatenate([neg_dist, neg_dist], axis=0)
    slope_of = lambda h: 2.0 ** (-8.0 * (group * N_HEADS + h + 1) / n_tot)
    outs = []
    for hp in range(N_HEADS // 2):
        cols = slice(hp * pair, (hp + 1) * pair)
        q2 = q[:, cols]
        k2 = jnp.concatenate([kp[:, cols], kc[:, cols]], axis=0)
        v2 = jnp.concatenate([vp[:, cols], vc[:, cols]], axis=0)
        zero = jnp.zeros_like(q2)
        q4 = jnp.concatenate([jnp.where(first_half, q2, zero), jnp.where(first_half, zero, q2)],
                             axis=0)
        slopes = jnp.where(head_a_rows, slope_of(2 * hp), slope_of(2 * hp + 1))
        s = lax.dot_general(q4, k2, (((1,), (1,)), ((), ())), preferred_element_type=F32)
        s = s + slopes * neg_dist2
        m = jnp.max(s, axis=-1, keepdims=True)
        p = jnp.exp(s - m)
        den = jnp.sum(p, axis=-1, keepdims=True)
        o4 = jnp.dot(p.astype(BF16), v2, preferred_element_type=F32) / den
        lse4 = m + jnp.log(den)
        outs.append(jnp.where(first_half, o4[:bq, :], o4[bq:, :]))
        lse_tile = jnp.where(lane == 2 * hp, lse4[:bq, :],
                             jnp.where(lane == 2 * hp + 1, lse4[bq:, :], lse_tile))
    o_ref[...] = jnp.concatenate(outs, axis=1).astype(BF16).reshape(o_ref.shape)
    lse_ref[...] = lse_tile.reshape(lse_ref.shape)


def _attention_group(qkv, group, batch, seq):
    dil = ATTN_PATTERNS[group][1]
    d = N_HEADS * HEAD_DIM
    t = qkv.shape[0]
    sub = seq // dil
    nb = sub // Q_BLOCK
    rows = Q_BLOCK if dil == 1 else PERM_TILE // dil
    chunks = Q_BLOCK // rows
    u = seq // (rows * dil)
    view = lambda a, c: a.reshape(batch, u, dil, rows, c)
    blk = lambda c: (None, chunks, None, rows, c)
    col0 = group * 3
    q_spec = pl.BlockSpec(blk(d), lambda b, r, j: (b, j, r, 0, col0))
    kc_spec = pl.BlockSpec(blk(d), lambda b, r, j: (b, j, r, 0, col0 + 1))
    kp_spec = pl.BlockSpec(blk(d), lambda b, r, j: (b, jnp.maximum(j - 1, 0), r, 0, col0 + 1))
    vc_spec = pl.BlockSpec(blk(d), lambda b, r, j: (b, j, r, 0, col0 + 2))
    vp_spec = pl.BlockSpec(blk(d), lambda b, r, j: (b, jnp.maximum(j - 1, 0), r, 0, col0 + 2))
    qkv5 = view(qkv, qkv.shape[1])
    o, lse = pl.pallas_call(
        functools.partial(_attn_kernel, group=group, dil=dil),
        grid=(batch, dil, nb),
        in_specs=[q_spec, kp_spec, kc_spec, vp_spec, vc_spec],
        out_specs=[
            pl.BlockSpec(blk(d), lambda b, r, j: (b, j, r, 0, 0)),
            pl.BlockSpec(blk(LSE_LANES), lambda b, r, j: (b, j, r, 0, 0)),
        ],
        out_shape=[
            jax.ShapeDtypeStruct((batch, u, dil, rows, d), BF16),
            jax.ShapeDtypeStruct((batch, u, dil, rows, LSE_LANES), F32),
        ],
        compiler_params=_cparams(("arbitrary", "arbitrary", "arbitrary")),
    )(qkv5, qkv5, qkv5, qkv5, qkv5)
    return o.reshape(t, d), lse.reshape(t, LSE_LANES)


def _split3(v):
    a = v.astype(BF16)
    r = v - a.astype(F32)
    b = r.astype(BF16)
    c = (r - b.astype(F32)).astype(BF16)
    return a, b, c


def _attn_out_kernel(x_ref, g_ref, o0_ref, o1_ref, o2_ref, l0_ref, l1_ref, l2_ref,
                     pt_ref, e_ref, wout_ref, lng_ref, lnb_ref, out_ref):
    tm, d = x_ref.shape
    o_refs = (o0_ref, o1_ref, o2_ref)
    l_refs = (l0_ref, l1_ref, l2_ref)
    n_sub = tm // PERM_TILE

    def unperm(gi, val_bf16):
        if gi == 0:
            return val_bf16.astype(F32)
        parts = [jnp.dot(pt_ref[gi - 1], val_bf16[s * PERM_TILE:(s + 1) * PERM_TILE, :],
                         preferred_element_type=F32) for s in range(n_sub)]
        return jnp.concatenate(parts, axis=0)

    lses = []
    for gi in range(3):
        l = l_refs[gi][...]
        if gi == 0:
            lses.append(l)
        else:
            a, b, c = _split3(l)
            lses.append(unperm(gi, a) + unperm(gi, b) + unperm(gi, c))
    mx = jnp.maximum(jnp.maximum(lses[0], lses[1]), lses[2])
    es = [jnp.exp(l - mx) for l in lses]
    tot = es[0] + es[1] + es[2]
    mixed = jnp.zeros((tm, d), F32)
    for gi in range(3):
        w = es[gi] / tot
        a, b, _ = _split3(w)
        wide = (jnp.dot(a, e_ref[...], preferred_element_type=F32)
                + jnp.dot(b, e_ref[...], preferred_element_type=F32))
        mixed = mixed + wide * unperm(gi, o_refs[gi][...])
    y = jnp.dot(mixed.astype(BF16), wout_ref[...], preferred_element_type=F32)
    v = ALPHA * x_ref[...] + (1.0 + g_ref[...]) * y
    out_ref[...] = _layer_norm(v, lng_ref[...], lnb_ref[...])


def _attn_out(x2, mod3, os_, lses, perms_t, expand, w_out, ln_g, ln_b, seq):
    t, d = x2.shape
    tm = 512
    tpb = seq // tm
    row = lambda c: pl.BlockSpec((tm, c), lambda i: (i, 0))
    full = lambda shape: pl.BlockSpec(shape, lambda i: (0,) * len(shape))
    return pl.pallas_call(
        _attn_out_kernel,
        grid=(t // tm,),
        in_specs=[row(d), _mod_spec(2, d, tpb), row(d), row(d), row(d),
                  row(LSE_LANES), row(LSE_LANES), row(LSE_LANES),
                  full(perms_t.shape), full(expand.shape), full((d, d)),
                  full((1, d)), full((1, d))],
        out_specs=row(d),
        out_shape=jax.ShapeDtypeStruct((t, d), F32),
        compiler_params=_cparams(("arbitrary",)),
    )(x2, mod3, *os_, *lses, perms_t, expand, w_out, ln_g.reshape(1, d), ln_b.reshape(1, d))


def _first_index_of_max(v, iota, size):
    m = jnp.max(v, axis=0, keepdims=True)
    idx = jnp.min(jnp.where(v == m, iota, float(size)), axis=0, keepdims=True)
    return m, idx


def _topk_kernel(x_ref, sh_ref, sc_ref, wr_ref, b_ref, tri_ref, ltri_ref,
                 w_ref, pos_ref, cnt_ref):
    tr = x_ref.shape[0]
    ne = b_ref.shape[0]
    gsz = ne // N_EXPERT_GROUPS
    tile = pl.program_id(0)

    hf = x_ref[...] * (1.0 + sc_ref[...]) + sh_ref[...]
    hf_hi = hf.astype(BF16)
    hf_lo = (hf - hf_hi.astype(F32)).astype(BF16)
    nt = (((1,), (1,)), ((), ()))
    by_hi = lax.dot_general(wr_ref[...], hf_hi, nt, preferred_element_type=F32)
    logits = (by_hi[:ne, :] + by_hi[ne:, :]
              + lax.dot_general(wr_ref[0:ne, :], hf_lo, nt, preferred_element_type=F32))
    scores = 1.0 / (1.0 + jnp.exp(-logits))
    sel = scores + b_ref[...]
    iota_g = lax.broadcasted_iota(I32, (gsz, tr), 0).astype(F32)
    iota_n = lax.broadcasted_iota(I32, (N_EXPERT_GROUPS, tr), 0).astype(F32)
    gs = jnp.zeros((N_EXPERT_GROUPS, tr), F32)
    for g in range(N_EXPERT_GROUPS):
        blk = sel[g * gsz:(g + 1) * gsz, :]
        m1, i1 = _first_index_of_max(blk, iota_g, gsz)
        m2 = jnp.max(jnp.where(iota_g == i1, -jnp.inf, blk), axis=0, keepdims=True)
        gs = jnp.where(iota_n == float(g), m1 + m2, gs)
    gmask = jnp.zeros((N_EXPERT_GROUPS, tr), F32)
    for _ in range(TOPK_GROUPS):
        _, gi = _first_index_of_max(gs, iota_n, N_EXPERT_GROUPS)
        hit = iota_n == gi
        gmask = jnp.where(hit, 1.0, gmask)
        gs = jnp.where(hit, -jnp.inf, gs)
    masked_rows = []
    for g in range(N_EXPERT_GROUPS):
        keep = jnp.broadcast_to(gmask[g:g + 1, :], (gsz, tr)) > 0.5
        masked_rows.append(jnp.where(keep, sel[g * gsz:(g + 1) * gsz, :], -jnp.inf))
    cur = jnp.concatenate(masked_rows, axis=0)
    iota_e = lax.broadcasted_iota(I32, (ne, tr), 0).astype(F32)
    chosen = jnp.zeros((ne, tr), F32)
    for _ in range(TOP_K):
        _, ei = _first_index_of_max(cur, iota_e, ne)
        hit = iota_e == ei
        cur = jnp.where(hit, -jnp.inf, cur)
        chosen = jnp.where(hit, 1.0, chosen)
    picked = chosen > 0.5
    top_scores = jnp.where(picked, scores, 0.0)
    wsum = jnp.sum(top_scores, axis=0, keepdims=True)
    w_ref[...] = top_scores / wsum * ROUTED_SCALE
    before = jnp.dot(chosen.astype(BF16), tri_ref[...], preferred_element_type=F32)
    n = jnp.sum(chosen, axis=1, keepdims=True)
    n_chunks = jnp.floor((n + (CHUNK - 1)) * (1.0 / CHUNK))
    run_off = jnp.dot(ltri_ref[...], jnp.broadcast_to(n_chunks, (ne, 128)).astype(BF16),
                      preferred_element_type=F32)[:, 0:1] * CHUNK
    pos_ref[...] = jnp.where(picked, before + run_off, NO_SLOT)

    @pl.when(tile == 0)
    def _():
        cnt_ref[...] = jnp.zeros_like(cnt_ref)

    lane = lax.broadcasted_iota(I32, cnt_ref.shape, 1)
    cnt_ref[...] = jnp.where(lane == tile, jnp.broadcast_to(n, cnt_ref.shape).astype(I32),
                             cnt_ref[...])


def _topk_route(x2, mod3, w_router, router_b, seq):
    t, d = x2.shape
    ne = w_router.shape[1]
    tr = TOK_TILE
    tpb = seq // tr
    assert t // tr <= 128
    w_t = w_router.T
    w_hi = w_t.astype(BF16)
    w_split = jnp.concatenate([w_hi, (w_t - w_hi.astype(F32)).astype(BF16)], axis=0)
    tri = (jnp.arange(tr)[:, None] < jnp.arange(tr)[None, :]).astype(BF16)
    ltri = (jnp.arange(ne)[None, :] < jnp.arange(ne)[:, None]).astype(BF16)
    out = lambda dt: jax.ShapeDtypeStruct((ne, t), dt)
    row = pl.BlockSpec((ne, tr), lambda i: (0, i))
    return pl.pallas_call(
        _topk_kernel,
        grid=(t // tr,),
        in_specs=[pl.BlockSpec((tr, d), lambda i: (i, 0)),
                  _mod_spec(3, d, tpb), _mod_spec(4, d, tpb),
                  pl.BlockSpec((2 * ne, d), lambda i: (0, 0)),
                  pl.BlockSpec((ne, 1), lambda i: (0, 0)),
                  pl.BlockSpec((tr, tr), lambda i: (0, 0)),
                  pl.BlockSpec((ne, ne), lambda i: (0, 0))],
        out_specs=[row, row, pl.BlockSpec((ne, 128), lambda i: (0, 0))],
        out_shape=[out(F32), out(F32), jax.ShapeDtypeStruct((ne, 128), I32)],
        compiler_params=_cparams(("arbitrary",)),
    )(x2, mod3, mod3, w_split, router_b.reshape(ne, 1), tri, ltri)


def _slot_plan(counts, n_tiles, n_blocks):
    n = counts[:, :n_tiles]
    nch = (n + (CHUNK - 1)) // CHUNK
    rows = jnp.sum(nch, axis=1) * CHUNK
    region = ((rows + EXPERT_ROWS - 1) // EXPERT_ROWS) * EXPERT_ROWS
    region_end = jnp.cumsum(region)
    region_start = region_end - region
    run_chunk = region_start[:, None] // CHUNK + jnp.cumsum(nch, axis=1) - nch
    local_end = jnp.cumsum(nch, axis=0)
    local_chunk = local_end - nch
    experts = jnp.arange(N_EXPERTS, dtype=I32)
    copies = []
    done = jnp.zeros_like(nch)
    for size in COPY_SIZES:
        k = (nch - done) // size
        end = jnp.cumsum(k, axis=0)
        cap = _copy_cap(size)
        m = jnp.arange(cap, dtype=I32)
        run_of_m = jnp.sum((end[:, :, None] <= m[None, None, :]).astype(I32), axis=0)
        owner = run_of_m[None] == experts[:, None, None]
        pick = lambda a: jnp.sum(jnp.where(owner, a[:, :, None], 0), axis=0)
        within = (m[None, :] - pick(end - k)) * size
        copies += [(pick(local_chunk + done) + within).reshape(-1).astype(I32),
                   (pick(run_chunk + done) + within).reshape(-1).astype(I32),
                   end[-1].astype(I32)]
        done = done + k * size
    block_row = jnp.arange(n_blocks, dtype=I32) * EXPERT_ROWS
    group_e = jnp.minimum(jnp.sum((region_end[None, :] <= block_row[:, None]).astype(I32), axis=1),
                          N_EXPERTS - 1)
    n_used = region_end[-1] // EXPERT_ROWS
    nonempty = region > 0
    run_of_expert = jnp.cumsum(nonempty.astype(I32)) - 1
    run_expert = jnp.sum(jnp.where((run_of_expert[None, :] == experts[:, None]) & nonempty[None, :],
                                   experts[None, :], 0), axis=1)
    blocks = jnp.arange(n_blocks, dtype=I32)
    first = ((blocks == 0) | (group_e != jnp.roll(group_e, 1))) & (blocks < n_used)
    per_tile = lambda a: (a * CHUNK).T.astype(I32)
    return dict(
        run_begin_rows=per_tile(local_chunk)[:, None, :], run_end_rows=per_tile(local_end)[:, None, :],
        run_begin_cols=per_tile(local_chunk)[:, :, None], run_end_cols=per_tile(local_end)[:, :, None],
        block_run=(jnp.cumsum(first.astype(I32)) - 1).astype(I32), block_first=first.astype(I32),
        run_expert=run_expert.astype(I32), n_runs=jnp.sum(nonempty.astype(I32)).reshape(1),
        copies=copies,
        tile_chunks=jnp.sum(nch, axis=0).astype(I32),
        tail_chunk=((region_start + rows) // CHUNK).astype(I32),
        tail_chunks=((region - rows) // CHUNK).astype(I32),
        group_e=group_e, n_used=(region_end[-1:] // EXPERT_ROWS).astype(I32))


def _chunk_rows(ref, chunk_index, n_chunks=1):
    start = chunk_index * CHUNK
    if not isinstance(start, int):
        start = pl.multiple_of(start, CHUNK)
    return ref.at[pl.ds(start, n_chunks * CHUNK), :]


def _copy_cap(size):
    return LOCAL_ROWS // CHUNK // size if size == COPY_SIZES[0] else N_EXPERTS


def _start_copies(copy_refs, tile, make_copy):
    for n, size in enumerate(COPY_SIZES):
        src_ref, dst_ref, cnt_ref = copy_refs[3 * n:3 * n + 3]
        base = tile * _copy_cap(size)

        def start(m, priority, src_ref=src_ref, dst_ref=dst_ref, base=base, size=size):
            make_copy(src_ref[base + m], dst_ref[base + m], size).start(priority=priority)

        _for_chunks_two_queues(cnt_ref[tile], start)


def _pos_digits(pos):
    hi = jnp.floor(pos * (1.0 / POS_RADIX))
    return hi.astype(BF16), (pos - POS_RADIX * hi).astype(BF16)


def _for_chunks_two_queues(count, start):
    def pair(p, carry):
        start(2 * p, 0)
        start(2 * p + 1, 1)
        return carry

    lax.fori_loop(0, lax.shift_right_logical(count, 1), pair, 0)

    @pl.when((count & 1) == 1)
    def _():
        start(count - 1, 0)


def _wait_chunks(count, src_ref, dst_ref, sem):
    group_rows = pl.ds(0, WAIT_GROUP * CHUNK)

    def wait_group(c, carry):
        pltpu.make_async_copy(src_ref.at[group_rows, :], dst_ref.at[group_rows, :], sem).wait()
        return carry

    def wait_one(c, carry):
        pltpu.make_async_copy(_chunk_rows(src_ref, 0), _chunk_rows(dst_ref, 0), sem).wait()
        return carry

    lax.fori_loop(0, lax.shift_right_logical(count, WAIT_GROUP.bit_length() - 1), wait_group, 0)
    lax.fori_loop(0, count & (WAIT_GROUP - 1), wait_one, 0)


def _dispatch_kernel(*refs):
    copy_refs, refs = refs[:3 * len(COPY_SIZES)], refs[3 * len(COPY_SIZES):]
    _dispatch_body(copy_refs, *refs)


def _dispatch_body(copy_refs, tile_ref, tail_ref, tailn_ref, used_ref,
                     x_ref, sh_ref, sc_ref, pos_ref, begin_ref, end_ref,
                     xg_hbm, stage_ref, zero_ref, sems, sem_blk):
    tr = x_ref.shape[0]
    i = pl.program_id(0)
    n_blocks = xg_hbm.shape[0] // EXPERT_ROWS

    def drain(count, sem):
        _wait_chunks(count, stage_ref.at[0], xg_hbm, sem)

    def block_copy(j):
        rows = pl.ds(pl.multiple_of(j * EXPERT_ROWS, EXPERT_ROWS), EXPERT_ROWS)
        return pltpu.make_async_copy(zero_ref, xg_hbm.at[rows, :], sem_blk)

    def wait_block(j, carry):
        block_copy(j).wait()
        return carry

    @pl.when(i == 0)
    def _():
        zero_ref[...] = jnp.zeros_like(zero_ref)

        def start_block(j, carry):
            block_copy(j).start()
            return carry

        lax.fori_loop(used_ref[0], n_blocks, start_block, 0)

        def per_expert(e, total):
            def per_chunk(c, carry):
                pltpu.make_async_copy(_chunk_rows(zero_ref, 0),
                                      _chunk_rows(xg_hbm, tail_ref[e] + c), sems.at[0]).start()
                return carry
            lax.fori_loop(0, tailn_ref[e], per_chunk, 0)
            return total + tailn_ref[e]

        drain(lax.fori_loop(0, N_EXPERTS, per_expert, 0), sems.at[0])

    hf = (x_ref[...] * (1.0 + sc_ref[...]) + sh_ref[...]).astype(BF16)
    row_e = lax.broadcasted_iota(I32, (LOCAL_ROWS, N_EXPERTS), 0)
    owner = jnp.where((row_e >= begin_ref[...]) & (row_e < end_ref[...]), 1.0, 0.0).astype(BF16)
    hi_digit, lo_digit = _pos_digits(pos_ref[...])
    slot_pos = (POS_RADIX * jnp.dot(owner, hi_digit, preferred_element_type=F32)
                + jnp.dot(owner, lo_digit, preferred_element_type=F32))
    row_t = lax.broadcasted_iota(I32, (LOCAL_ROWS, tr), 0).astype(F32)
    perm = jnp.where(slot_pos == row_t, 1.0, 0.0).astype(BF16)
    rows = jnp.dot(perm, hf, preferred_element_type=F32)
    half = rows.shape[1] // 2
    buf = stage_ref.at[i % 2]
    buf[...] = (lax.bitcast_convert_type(rows[:, :half], U32)
                | (lax.bitcast_convert_type(rows[:, half:], U32) >> 16))

    _start_copies(copy_refs, i, lambda src, dst, n: pltpu.make_async_copy(
        _chunk_rows(buf, src, n), _chunk_rows(xg_hbm, dst, n), sems.at[i % 2]))

    @pl.when(i > 0)
    def _():
        drain(tile_ref[jnp.maximum(i - 1, 0)], sems.at[(i + 1) % 2])

    @pl.when(i == pl.num_programs(0) - 1)
    def _():
        drain(tile_ref[i], sems.at[i % 2])
        lax.fori_loop(used_ref[0], n_blocks, wait_block, 0)


def _dispatch(plan, x2, mod3, pos, n_slots, seq):
    t, d = x2.shape
    tr = TOK_TILE
    tpb = seq // tr
    return pl.pallas_call(
        _dispatch_kernel,
        grid_spec=pltpu.PrefetchScalarGridSpec(
            num_scalar_prefetch=4 + len(plan["copies"]),
            grid=(t // tr,),
            in_specs=[pl.BlockSpec((tr, d), lambda i, *_: (i, 0)),
                      _mod_spec(3, d, tpb), _mod_spec(4, d, tpb),
                      pl.BlockSpec((N_EXPERTS, tr), lambda i, *_: (0, i)),
                      pl.BlockSpec((None, 1, N_EXPERTS), lambda i, *_: (i, 0, 0)),
                      pl.BlockSpec((None, 1, N_EXPERTS), lambda i, *_: (i, 0, 0))],
            out_specs=pl.BlockSpec(memory_space=pl.ANY),
            scratch_shapes=[pltpu.VMEM((2, LOCAL_ROWS, d // 2), U32),
                            pltpu.VMEM((EXPERT_ROWS, d // 2), U32),
                            pltpu.SemaphoreType.DMA((2,)), pltpu.SemaphoreType.DMA],
        ),
        out_shape=jax.ShapeDtypeStruct((n_slots, d // 2), U32),
        compiler_params=_cparams(("arbitrary",)),
    )(*plan["copies"], plan["tile_chunks"], plan["tail_chunk"], plan["tail_chunks"],
      plan["n_used"], x2, mod3, mod3, pos, plan["run_begin_rows"], plan["run_end_rows"])


def _expert_kernel(run_ref, first_ref, rexp_ref, nrun_ref, nu_ref,
                   xg_hbm, wgu_hbm, wdn_hbm, yg_hbm,
                   xbuf, ybuf, zbuf, wgu_f32, wdn_f32, wgu_bf, wdn_bf, x_sem, y_sem, w_sem, z_sem,
                   *, layer):
    n_used = nu_ref[0]
    n_blocks = yg_hbm.shape[0] // EXPERT_ROWS
    depth, _, half = xbuf.shape
    f = wdn_bf.shape[0]

    def rows(j):
        return pl.ds(pl.multiple_of(j * EXPERT_ROWS, EXPERT_ROWS), EXPERT_ROWS)

    def x_copy(j):
        return pltpu.make_async_copy(xg_hbm.at[rows(j), :], xbuf.at[j % depth],
                                     x_sem.at[j % depth])

    def y_copy(j):
        return pltpu.make_async_copy(ybuf.at[j % depth], yg_hbm.at[rows(j), :],
                                     y_sem.at[j % depth])

    def w_copies(r):
        e = rexp_ref[r]
        return (pltpu.make_async_copy(wgu_hbm.at[layer, e], wgu_f32.at[r % 2], w_sem.at[r % 2]),
                pltpu.make_async_copy(wdn_hbm.at[layer, e], wdn_f32.at[r % 2], w_sem.at[r % 2]))

    def zero_copy(j):
        return pltpu.make_async_copy(zbuf, yg_hbm.at[rows(j), :], z_sem)

    zbuf[...] = jnp.zeros_like(zbuf)

    def zero_start(j, carry):
        zero_copy(j).start()
        return carry

    def zero_wait(j, carry):
        zero_copy(j).wait()
        return carry

    lax.fori_loop(n_used, n_blocks, zero_start, 0)

    for cp in w_copies(0):
        cp.start()
    for a in range(depth - 1):
        @pl.when(a < n_used)
        def _():
            x_copy(a).start()

    def block(j, carry):
        r = run_ref[j]

        @pl.when(first_ref[j] == 1)
        def _():
            for cp in w_copies(r):
                cp.wait()

            @pl.when(r + 1 < nrun_ref[0])
            def _():
                for cp in w_copies(r + 1):
                    cp.start()

            wgu_bf[...] = wgu_f32[r % 2].astype(BF16)
            wdn_bf[...] = wdn_f32[r % 2].astype(BF16)

        x_copy(j).wait()

        @pl.when(j + (depth - 1) < n_used)
        def _():
            x_copy(j + (depth - 1)).start()

        @pl.when(j >= depth)
        def _():
            y_copy(j - depth).wait()

        for g in range(EXPERT_ROWS // GEMM_SUB_ROWS):
            rws = slice(g * GEMM_SUB_ROWS, (g + 1) * GEMM_SUB_ROWS)
            hi, lo = _unpack_bf16_pairs(xbuf[j % depth, rws, :])
            gu = (jnp.dot(hi.astype(BF16), wgu_bf[0:half, :], preferred_element_type=F32)
                  + jnp.dot(lo.astype(BF16), wgu_bf[half:, :], preferred_element_type=F32))
            act = (_silu(gu[:, :f]) * gu[:, f:]).astype(BF16)
            ybuf[j % depth, rws, :] = _pack_bf16_pairs(
                jnp.dot(act, wdn_bf[...], preferred_element_type=F32))
        y_copy(j).start()
        return carry

    lax.fori_loop(0, n_used, block, 0)

    for a in range(depth, 0, -1):
        @pl.when(n_used >= a)
        def _():
            y_copy(n_used - a).wait()

    lax.fori_loop(n_used, n_blocks, zero_wait, 0)


def _expert_gemm(plan, xg, w_gu, w_down, layer):
    n_slots, half = xg.shape
    _, ne, d, f2 = w_gu.shape
    f = w_down.shape[2]
    hbm = pl.BlockSpec(memory_space=pl.ANY)
    return pl.pallas_call(
        functools.partial(_expert_kernel, layer=layer),
        grid_spec=pltpu.PrefetchScalarGridSpec(
            num_scalar_prefetch=5,
            grid=(1,),
            in_specs=[hbm, hbm, hbm],
            out_specs=hbm,
            scratch_shapes=[pltpu.VMEM((GEMM_DEPTH, EXPERT_ROWS, half), U32),
                            pltpu.VMEM((GEMM_DEPTH, EXPERT_ROWS, half), U32),
                            pltpu.VMEM((EXPERT_ROWS, half), U32),
                            pltpu.VMEM((2, d, f2), F32), pltpu.VMEM((2, f, d), F32),
                            pltpu.VMEM((d, f2), BF16), pltpu.VMEM((f, d), BF16),
                            pltpu.SemaphoreType.DMA((GEMM_DEPTH,)),
                            pltpu.SemaphoreType.DMA((GEMM_DEPTH,)),
                            pltpu.SemaphoreType.DMA((2,)), pltpu.SemaphoreType.DMA],
        ),
        out_shape=jax.ShapeDtypeStruct((n_slots, half), U32),
        compiler_params=_cparams(("arbitrary",)),
    )(plan["block_run"], plan["block_first"], plan["run_expert"], plan["n_runs"], plan["n_used"],
      xg, w_gu, w_down)


def _combine_kernel(*refs):
    copy_refs, refs = refs[:3 * len(COPY_SIZES)], refs[3 * len(COPY_SIZES):]
    _combine_body(copy_refs, *refs)


def _combine_body(copy_refs, tile_ref,
                    x_ref, sh_ref, sc_ref, g_ref, pos_ref, wt_ref, begin_ref, end_ref, brow_ref,
                    wsgu_ref, wsdn_ref, lng_ref, lnb_ref, yg_hbm, o_ref, stage_ref, sems):
    tr, d = x_ref.shape
    i = pl.program_id(0)
    n_tiles = pl.num_programs(0)

    def fetch(tile, buffer):
        _start_copies(copy_refs, tile, lambda src, dst, n: pltpu.make_async_copy(
            _chunk_rows(yg_hbm, dst, n), _chunk_rows(stage_ref.at[buffer], src, n),
            sems.at[buffer]))

    @pl.when(i == 0)
    def _():
        stage_ref[...] = jnp.zeros_like(stage_ref)
        fetch(0, 0)

    nxt = jnp.minimum(i + 1, n_tiles - 1)
    for buffer in range(2):
        @pl.when((i + 1 < n_tiles) & (nxt % 2 == buffer))
        def _():
            fetch(nxt, buffer)

    x = x_ref[...]
    hf = (x * (1.0 + sc_ref[...]) + sh_ref[...]).astype(BF16)
    f = wsdn_ref.shape[0]
    su = jnp.dot(hf, wsgu_ref[...], preferred_element_type=F32)
    act = (_silu(su[:, :f]) * su[:, f:]).astype(BF16)
    shared = jnp.dot(act, wsdn_ref[...], preferred_element_type=F32)

    col_e = lax.broadcasted_iota(I32, (N_EXPERTS, LOCAL_ROWS), 1)
    owner = jnp.where((col_e >= begin_ref[...]) & (col_e < end_ref[...]), 1.0, 0.0).astype(BF16)
    begin_row = brow_ref[...].astype(F32)
    pos = pos_ref[...]
    rank1 = jnp.where(pos >= 0.0, pos - begin_row + 1.0, 0.0).astype(BF16)
    slot_rank = jnp.dot(rank1, owner, preferred_element_type=F32)
    b_hi, b_lo = _pos_digits(jnp.broadcast_to(begin_row, (8, N_EXPERTS)))
    run_begin = (POS_RADIX * jnp.dot(b_hi, owner, preferred_element_type=F32)
                 + jnp.dot(b_lo, owner, preferred_element_type=F32))[0:1, :]
    offset1 = lax.broadcasted_iota(I32, (1, LOCAL_ROWS), 1).astype(F32) - run_begin + 1.0
    holds = slot_rank == offset1
    weights = jnp.where(holds, jnp.dot(wt_ref[...].astype(BF16), owner,
                                       preferred_element_type=F32), 0.0).astype(BF16)

    buf = stage_ref.at[i % 2]
    _wait_chunks(tile_ref[i], yg_hbm, buf, sems.at[i % 2])

    y_hi, y_lo = _unpack_bf16_pairs(buf[...])
    y = jnp.concatenate([y_hi.astype(BF16), y_lo.astype(BF16)], axis=1)
    routed = jnp.dot(weights, y, preferred_element_type=F32)
    v = ALPHA * x + (1.0 + g_ref[...]) * (routed + shared)
    o_ref[...] = _layer_norm(v, lng_ref[...], lnb_ref[...])


def _combine(plan, x2, mod3, pos_tok, w_tok, w_sh_gu, w_sh_down, ln_g, ln_b, yg, seq):
    t, d = x2.shape
    tr = TOK_TILE
    tpb = seq // tr
    full = lambda shape: pl.BlockSpec(shape, lambda i, *_: (0,) * len(shape))
    return pl.pallas_call(
        _combine_kernel,
        grid_spec=pltpu.PrefetchScalarGridSpec(
            num_scalar_prefetch=1 + len(plan["copies"]),
            grid=(t // tr,),
            in_specs=[pl.BlockSpec((tr, d), lambda i, *_: (i, 0)),
                      _mod_spec(3, d, tpb), _mod_spec(4, d, tpb), _mod_spec(5, d, tpb),
                      pl.BlockSpec((tr, N_EXPERTS), lambda i, *_: (i, 0)),
                      pl.BlockSpec((tr, N_EXPERTS), lambda i, *_: (i, 0)),
                      pl.BlockSpec((None, N_EXPERTS, 1), lambda i, *_: (i, 0, 0)),
                      pl.BlockSpec((None, N_EXPERTS, 1), lambda i, *_: (i, 0, 0)),
                      pl.BlockSpec((None, 1, N_EXPERTS), lambda i, *_: (i, 0, 0)),
                      full(w_sh_gu.shape), full(w_sh_down.shape), full((1, d)), full((1, d)),
                      pl.BlockSpec(memory_space=pl.ANY)],
            out_specs=pl.BlockSpec((tr, d), lambda i, *_: (i, 0)),
            scratch_shapes=[pltpu.VMEM((2, LOCAL_ROWS, d // 2), U32),
                            pltpu.SemaphoreType.DMA((2,))],
        ),
        out_shape=jax.ShapeDtypeStruct((t, d), F32),
        compiler_params=_cparams(("arbitrary",)),
    )(*plan["copies"], plan["tile_chunks"],
      x2, mod3, mod3, mod3, pos_tok, w_tok, plan["run_begin_cols"], plan["run_end_cols"],
      plan["run_begin_rows"],
      w_sh_gu, w_sh_down, ln_g.reshape(1, d), ln_b.reshape(1, d), yg)


def _moe_layer(x2, mod3, w_router, router_b, w_gu, w_down, layer, w_sh_gu, w_sh_down,
               ln_g, ln_b, seq):
    t, d = x2.shape
    n_tiles = t // TOK_TILE
    w_top, pos, counts = _topk_route(x2, mod3, w_router, router_b, seq)
    bound = t * TOP_K + n_tiles * N_EXPERTS * (CHUNK - 1) + N_EXPERTS * (EXPERT_ROWS - 1)
    n_blocks = -(-bound // EXPERT_ROWS)
    plan = _slot_plan(counts, n_tiles, n_blocks)
    xg = _dispatch(plan, x2, mod3, pos, n_blocks * EXPERT_ROWS, seq)
    yg = _expert_gemm(plan, xg, w_gu, w_down, layer)
    return _combine(plan, x2, mod3, pos.T, w_top.T, w_sh_gu, w_sh_down, ln_g, ln_b, yg, seq)


def kernel(x, c, ada_w, ada_b, pool_w_in, pool_w_grp, pool_scale, pool_w_out, attn_w_in, attn_w_out, ln1_g, ln1_b, router_w, router_b, exp_w_gu, exp_w_down, sh_w_gu, sh_w_down, ln2_g, ln2_b):
    batch, seq, d = x.shape
    depth = ada_w.shape[0]
    t = batch * seq
    mod = _modulation(c, ada_w, ada_b).reshape(depth, batch, 1, 6 * d)
    perms = jnp.stack([_perm_matrix(dil) for _, dil in ATTN_PATTERNS[1:]])
    perms_t = jnp.swapaxes(perms, 1, 2)
    expand = (jnp.arange(LSE_LANES)[:, None] == (jnp.arange(d)[None, :] // HEAD_DIM)).astype(BF16)
    part = (jnp.arange(attn_w_in.shape[2]) // d) % 3
    q_scale = jnp.where(part == 0, HEAD_DIM ** -0.5, 1.0).astype(F32)
    x2 = x.reshape(t, d)
    for i in range(depth):
        mod3 = mod[i]
        j = i // 2
        if i % 2 == 0:
            x2 = _pool_layer(x2, mod3, pool_w_in[j].astype(BF16), pool_w_grp[j].astype(BF16),
                             pool_scale[j], pool_w_out[j].astype(BF16), ln1_g[i], ln1_b[i], seq)
        else:
            qkv = _qkv_proj(x2, mod3, perms, (attn_w_in[j] * q_scale).astype(BF16), seq)
            res = [_attention_group(qkv, g, batch, seq) for g in range(len(ATTN_PATTERNS))]
            x2 = _attn_out(x2, mod3, [r[0] for r in res], [r[1] for r in res], perms_t, expand,
                           attn_w_out[j].astype(BF16), ln1_g[i], ln1_b[i], seq)
        x2 = _moe_layer(x2, mod3, router_w[i], router_b[i], exp_w_gu, exp_w_down, i,
                        sh_w_gu[i].astype(BF16), sh_w_down[i].astype(BF16),
                        ln2_g[i], ln2_b[i], seq)
    return x2.reshape(batch, seq, d)
```
